```python
import jax, jax.numpy as jnp
from jax import lax
import numpy as np

D_MODEL = 1024
BATCH = 8
SEQ = 2048
DEPTH = 2

N_META = 16
BLOCK = 128
META_PAD = BLOCK - N_META
HEAD_DIM = 64
ROPE_THETA = 10000.0
NORM_EPS = 1e-6
NEG_INF = -1e30
SWA_HEADS = D_MODEL // (2 * HEAD_DIM)
SWA_KV_HEADS = SWA_HEADS // 4
SWA_GROUP = SWA_HEADS // SWA_KV_HEADS
SWA_WINDOW = 128
SWA_WIDTH = SWA_HEADS * HEAD_DIM
SWA_KV_WIDTH = SWA_KV_HEADS * HEAD_DIM
CONV_CHANNELS = D_MODEL // 2
CONV_WIDTH = 31
CONV_LN_EPS = 1e-5
SB_HEADS = D_MODEL // HEAD_DIM
SB_WIDTH = SB_HEADS * HEAD_DIM
AB_SPLITS = (SWA_WIDTH, SWA_KV_WIDTH, SWA_KV_WIDTH, SWA_WIDTH, 2 * CONV_CHANNELS, CONV_CHANNELS)
AB_IN = sum(AB_SPLITS)
AB_MIX = SWA_WIDTH + CONV_CHANNELS
SB_SPLITS = (SB_WIDTH, SB_WIDTH, SB_WIDTH, SB_WIDTH)
SB_IN = sum(SB_SPLITS)
N_EVEN = (DEPTH + 1) // 2
N_ODD = DEPTH // 2

kernel_name = "hybrid_swa_conformer_stickbreaking_trunk"


def _split(x, sizes):
    idx = [int(i) for i in np.cumsum(sizes)[:-1]]
    return jnp.split(x, idx, axis=-1)


def rms_norm(x, g):
    xf = x.astype(jnp.float32)
    y = xf * lax.rsqrt(jnp.mean(xf * xf, axis=-1, keepdims=True) + NORM_EPS)
    return (y * g.astype(jnp.float32)).astype(x.dtype)


def layer_norm(x, g, b):
    xf = x.astype(jnp.float32)
    mu = jnp.mean(xf, axis=-1, keepdims=True)
    xc = xf - mu
    y = xc * lax.rsqrt(jnp.mean(xc * xc, axis=-1, keepdims=True) + CONV_LN_EPS)
    return (y * g.astype(jnp.float32) + b.astype(jnp.float32)).astype(x.dtype)


def apply_rope(x, pos):
    half = x.shape[-1] // 2
    inv = ROPE_THETA ** (-jnp.arange(half, dtype=jnp.float32) / half)
    ang = pos.astype(jnp.float32)[:, None] * inv[None, :]
    cos = jnp.cos(ang)[None, :, None, :]
    sin = jnp.sin(ang)[None, :, None, :]
    xf = x.astype(jnp.float32)
    x1, x2 = xf[..., :half], xf[..., half:]
    return jnp.concatenate([x1 * cos - x2 * sin, x2 * cos + x1 * sin], axis=-1).astype(x.dtype)


def sliding_window_sink_attention(q, k, v, sinks):
    b, l = q.shape[0], q.shape[1]
    lp = l + META_PAD
    nb = lp // BLOCK
    padw = ((0, 0), (META_PAD, 0), (0, 0), (0, 0))
    qb = jnp.pad(q, padw).reshape(b, nb, BLOCK, SWA_KV_HEADS, SWA_GROUP, HEAD_DIM)
    kb = jnp.pad(k, padw).reshape(b, nb, BLOCK, SWA_KV_HEADS, HEAD_DIM)
    vb = jnp.pad(v, padw).reshape(b, nb, BLOCK, SWA_KV_HEADS, HEAD_DIM)

    def band(t):
        prev = jnp.concatenate([jnp.zeros_like(t[:, :1]), t[:, :-1]], axis=1)
        meta = jnp.broadcast_to(t[:, :1], t.shape)
        return jnp.concatenate([meta, prev, t], axis=2)

    kk, vv = band(kb), band(vb)
    scale = HEAD_DIM ** -0.5
    s = jnp.einsum('bnqgrd,bnkgd->bngrqk', qb, kk).astype(jnp.float32) * scale
    blk = jnp.arange(nb)[:, None, None]
    r = jnp.arange(BLOCK)
    qpos = blk * BLOCK + r[None, :, None]
    mpos = r[None, None, :]
    bpos = (blk - 1) * BLOCK + jnp.arange(2 * BLOCK)[None, None, :]
    meta_ok = (mpos >= META_PAD) & (qpos - mpos >= SWA_WINDOW)
    band_ok = (bpos >= META_PAD) & (qpos >= bpos) & (qpos - bpos < SWA_WINDOW)
    mask = jnp.concatenate([jnp.broadcast_to(meta_ok, (nb, BLOCK, BLOCK)), band_ok], axis=-1)
    s = jnp.where(mask[None, :, None, None], s, NEG_INF)
    sink = jnp.broadcast_to(sinks.astype(jnp.float32).reshape(1, 1, SWA_KV_HEADS, SWA_GROUP, 1, 1),
                            s.shape[:-1] + (1,))
    p = jax.nn.softmax(jnp.concatenate([s, sink], axis=-1), axis=-1)[..., :-1]
    o = jnp.einsum('bngrqk,bnkgd->bnqgrd', p.astype(v.dtype), vv)
    return o.reshape(b, lp, SWA_HEADS, HEAD_DIM)[:, META_PAD:]


def causal_depthwise_conv(u, w, bias):
    y = lax.conv_general_dilated(u, w[:, None, :].astype(u.dtype), window_strides=(1,),
                                 padding=((CONV_WIDTH - 1, 0),),
                                 dimension_numbers=('NWC', 'WIO', 'NWC'),
                                 feature_group_count=u.shape[-1])
    return y + bias


def stick_breaking_attention(q, k, v):
    b, l, h, d = q.shape
    lp = l + META_PAD
    nb = lp // BLOCK
    padw = ((0, 0), (META_PAD, 0), (0, 0), (0, 0))
    qp, kp, vp = jnp.pad(q, padw), jnp.pad(k, padw), jnp.pad(v, padw)
    scale = d ** -0.5
    outs = []
    for i in range(nb):
        kend = (i + 1) * BLOCK
        z = jnp.einsum('bqhd,bkhd->bhqk', qp[:, i * BLOCK:kend], kp[:, :kend]).astype(jnp.float32) * scale
        qpos = i * BLOCK + jnp.arange(BLOCK)[:, None]
        kpos = jnp.arange(kend)[None, :]
        valid = (kpos >= META_PAD) & (kpos < qpos)
        log_beta = jax.nn.log_sigmoid(z)
        log_1m = jnp.where(valid, jax.nn.log_sigmoid(-z), 0.0)
        after = lax.cumsum(log_1m, axis=3, reverse=True) - log_1m
        a = jnp.where(valid, jnp.exp(log_beta + after), 0.0)
        outs.append(jnp.einsum('bhqk,bkhd->bqhd', a.astype(v.dtype), vp[:, :kend]))
    return jnp.concatenate(outs, axis=1)[:, META_PAD:]


def swa_conv_mixer(h, pos, w_in, sinks, conv_w, conv_b, ln_g, ln_b, w_pw2, w_out):
    b, l, _ = h.shape
    q, k, v, g_a, glu_in, g_b = _split(h @ w_in, AB_SPLITS)
    q = apply_rope(q.reshape(b, l, SWA_HEADS, HEAD_DIM), pos)
    k = apply_rope(k.reshape(b, l, SWA_KV_HEADS, HEAD_DIM), pos)
    v = v.reshape(b, l, SWA_KV_HEADS, HEAD_DIM)
    a = sliding_window_sink_attention(q, k, v, sinks).reshape(b, l, SWA_WIDTH) * jax.nn.silu(g_a)
    u = glu_in[..., :CONV_CHANNELS] * jax.nn.sigmoid(glu_in[..., CONV_CHANNELS:])
    c = jax.nn.silu(layer_norm(causal_depthwise_conv(u, conv_w, conv_b), ln_g, ln_b))
    c = (c @ w_pw2) * jax.nn.silu(g_b)
    return jnp.concatenate([a, c], axis=-1) @ w_out


def stick_breaking_mixer(h, w_in, w_out):
    b, l, _ = h.shape
    q, k, v, g = _split(h @ w_in, SB_SPLITS)
    shp = (b, l, SB_HEADS, HEAD_DIM)
    o = stick_breaking_attention(q.reshape(shp), k.reshape(shp), v.reshape(shp))
    return (o.reshape(b, l, SB_WIDTH) * jax.nn.silu(g)) @ w_out


def _fwd_setup_inputs(seed: int = 0) -> dict:
    key = jax.random.key(seed)
    ks = jax.random.split(key, 16)
    f32 = jnp.float32
    nrm = lambda k, s: jax.random.normal(k, s, dtype=f32)
    return {
        "x": nrm(ks[0], (BATCH, SEQ, D_MODEL)),
        "meta_tokens": nrm(ks[1], (N_META, D_MODEL)),
        "ab_pre_norm": 1.0 + 0.05 * nrm(ks[2], (N_EVEN, D_MODEL)),
        "ab_w_in": nrm(ks[3], (N_EVEN, D_MODEL, AB_IN)) * D_MODEL ** -0.5,
        "ab_sinks": nrm(ks[4], (N_EVEN, SWA_HEADS)),
        "ab_conv_w": nrm(ks[5], (N_EVEN, CONV_WIDTH, CONV_CHANNELS)) * CONV_WIDTH ** -0.5,
        "ab_conv_b": 0.02 * nrm(ks[6], (N_EVEN, CONV_CHANNELS)),
        "ab_conv_ln_g": 1.0 + 0.05 * nrm(ks[7], (N_EVEN, CONV_CHANNELS)),
        "ab_conv_ln_b": 0.02 * nrm(ks[8], (N_EVEN, CONV_CHANNELS)),
        "ab_w_pw2": nrm(ks[9], (N_EVEN, CONV_CHANNELS, CONV_CHANNELS)) * CONV_CHANNELS ** -0.5,
        "ab_w_out": nrm(ks[10], (N_EVEN, AB_MIX, D_MODEL)) * AB_MIX ** -0.5,
        "ab_post_norm": 1.0 + 0.05 * nrm(ks[11], (N_EVEN, D_MODEL)),
        "sb_pre_norm": 1.0 + 0.05 * nrm(ks[12], (N_ODD, D_MODEL)),
        "sb_w_in": nrm(ks[13], (N_ODD, D_MODEL, SB_IN)) * D_MODEL ** -0.5,
        "sb_w_out": nrm(ks[14], (N_ODD, SB_WIDTH, D_MODEL)) * SB_WIDTH ** -0.5,
        "sb_post_norm": 1.0 + 0.05 * nrm(ks[15], (N_ODD, D_MODEL)),
    }


def _fwd_reference(x, meta_tokens, ab_pre_norm, ab_w_in, ab_sinks, ab_conv_w, ab_conv_b, ab_conv_ln_g,
              ab_conv_ln_b, ab_w_pw2, ab_w_out, ab_post_norm, sb_pre_norm, sb_w_in, sb_w_out,
              sb_post_norm):
    b = x.shape[0]
    meta = jnp.broadcast_to(meta_tokens[None].astype(x.dtype), (b, N_META, D_MODEL))
    h = jnp.concatenate([meta, x], axis=1)
    pos = jnp.arange(h.shape[1])
    for layer in range(DEPTH):
        i = layer // 2
        if layer % 2 == 0:
            y = swa_conv_mixer(rms_norm(h, ab_pre_norm[i]), pos, ab_w_in[i], ab_sinks[i],
                               ab_conv_w[i], ab_conv_b[i], ab_conv_ln_g[i], ab_conv_ln_b[i],
                               ab_w_pw2[i], ab_w_out[i])
            h = h + rms_norm(y, ab_post_norm[i])
        else:
            y = stick_breaking_mixer(rms_norm(h, sb_pre_norm[i]), sb_w_in[i], sb_w_out[i])
            h = h + rms_norm(y, sb_post_norm[i])
    return h[:, N_META:]


import jax as _jax
import jax.numpy as _jnp

TWIN_FORMAT = 'train_step'
FWD_PARAMS = ['x', 'meta_tokens', 'ab_pre_norm', 'ab_w_in', 'ab_sinks', 'ab_conv_w', 'ab_conv_b', 'ab_conv_ln_g', 'ab_conv_ln_b', 'ab_w_pw2', 'ab_w_out', 'ab_post_norm', 'sb_pre_norm', 'sb_w_in', 'sb_w_out', 'sb_post_norm']
TWIN_WEIGHTS = ['meta_tokens', 'ab_pre_norm', 'ab_w_in', 'ab_sinks', 'ab_conv_w', 'ab_conv_b', 'ab_conv_ln_g', 'ab_conv_ln_b', 'ab_w_pw2', 'ab_w_out', 'ab_post_norm', 'sb_pre_norm', 'sb_w_in', 'sb_w_out', 'sb_post_norm']
TWIN_DIFF_INPUT = 'x'
TWIN_INPUTS = ['x', 'meta_tokens', 'ab_pre_norm', 'ab_w_in', 'ab_sinks', 'ab_conv_w', 'ab_conv_b', 'ab_conv_ln_g', 'ab_conv_ln_b', 'ab_w_pw2', 'ab_w_out', 'ab_post_norm', 'sb_pre_norm', 'sb_w_in', 'sb_w_out', 'sb_post_norm', 'loss_target', 'm_meta_tokens', 'm_ab_pre_norm', 'm_ab_w_in', 'm_ab_sinks', 'm_ab_conv_w', 'm_ab_conv_b', 'm_ab_conv_ln_g', 'm_ab_conv_ln_b', 'm_ab_w_pw2', 'm_ab_w_out', 'm_ab_post_norm', 'm_sb_pre_norm', 'm_sb_w_in', 'm_sb_w_out', 'm_sb_post_norm', 'v_meta_tokens', 'v_ab_pre_norm', 'v_ab_w_in', 'v_ab_sinks', 'v_ab_conv_w', 'v_ab_conv_b', 'v_ab_conv_ln_g', 'v_ab_conv_ln_b', 'v_ab_w_pw2', 'v_ab_w_out', 'v_ab_post_norm', 'v_sb_pre_norm', 'v_sb_w_in', 'v_sb_w_out', 'v_sb_post_norm']
TWIN_OUTPUTS = ['loss', 'grad_x', 'grad_meta_tokens', 'grad_ab_pre_norm', 'grad_ab_w_in', 'grad_ab_sinks', 'grad_ab_conv_w', 'grad_ab_conv_b', 'grad_ab_conv_ln_g', 'grad_ab_conv_ln_b', 'grad_ab_w_pw2', 'grad_ab_w_out', 'grad_ab_post_norm', 'grad_sb_pre_norm', 'grad_sb_w_in', 'grad_sb_w_out', 'grad_sb_post_norm', 'delta_meta_tokens', 'delta_ab_pre_norm', 'delta_ab_w_in', 'delta_ab_sinks', 'delta_ab_conv_w', 'delta_ab_conv_b', 'delta_ab_conv_ln_g', 'delta_ab_conv_ln_b', 'delta_ab_w_pw2', 'delta_ab_w_out', 'delta_ab_post_norm', 'delta_sb_pre_norm', 'delta_sb_w_in', 'delta_sb_w_out', 'delta_sb_post_norm', 'new_m_meta_tokens', 'new_m_ab_pre_norm', 'new_m_ab_w_in', 'new_m_ab_sinks', 'new_m_ab_conv_w', 'new_m_ab_conv_b', 'new_m_ab_conv_ln_g', 'new_m_ab_conv_ln_b', 'new_m_ab_w_pw2', 'new_m_ab_w_out', 'new_m_ab_post_norm', 'new_m_sb_pre_norm', 'new_m_sb_w_in', 'new_m_sb_w_out', 'new_m_sb_post_norm', 'new_v_meta_tokens', 'new_v_ab_pre_norm', 'new_v_ab_w_in', 'new_v_ab_sinks', 'new_v_ab_conv_w', 'new_v_ab_conv_b', 'new_v_ab_conv_ln_g', 'new_v_ab_conv_ln_b', 'new_v_ab_w_pw2', 'new_v_ab_w_out', 'new_v_ab_post_norm', 'new_v_sb_pre_norm', 'new_v_sb_w_in', 'new_v_sb_w_out', 'new_v_sb_post_norm']
TWIN_LEAF_KINDS = {'loss': 'loss', 'grad_x': 'grad_x', 'grad_meta_tokens': 'grad_w', 'grad_ab_pre_norm': 'grad_w', 'grad_ab_w_in': 'grad_w', 'grad_ab_sinks': 'grad_w', 'grad_ab_conv_w': 'grad_w', 'grad_ab_conv_b': 'grad_w', 'grad_ab_conv_ln_g': 'grad_w', 'grad_ab_conv_ln_b': 'grad_w', 'grad_ab_w_pw2': 'grad_w', 'grad_ab_w_out': 'grad_w', 'grad_ab_post_norm': 'grad_w', 'grad_sb_pre_norm': 'grad_w', 'grad_sb_w_in': 'grad_w', 'grad_sb_w_out': 'grad_w', 'grad_sb_post_norm': 'grad_w', 'delta_meta_tokens': 'delta_w', 'delta_ab_pre_norm': 'delta_w', 'delta_ab_w_in': 'delta_w', 'delta_ab_sinks': 'delta_w', 'delta_ab_conv_w': 'delta_w', 'delta_ab_conv_b': 'delta_w', 'delta_ab_conv_ln_g': 'delta_w', 'delta_ab_conv_ln_b': 'delta_w', 'delta_ab_w_pw2': 'delta_w', 'delta_ab_w_out': 'delta_w', 'delta_ab_post_norm': 'delta_w', 'delta_sb_pre_norm': 'delta_w', 'delta_sb_w_in': 'delta_w', 'delta_sb_w_out': 'delta_w', 'delta_sb_post_norm': 'delta_w', 'new_m_meta_tokens': 'new_m', 'new_m_ab_pre_norm': 'new_m', 'new_m_ab_w_in': 'new_m', 'new_m_ab_sinks': 'new_m', 'new_m_ab_conv_w': 'new_m', 'new_m_ab_conv_b': 'new_m', 'new_m_ab_conv_ln_g': 'new_m', 'new_m_ab_conv_ln_b': 'new_m', 'new_m_ab_w_pw2': 'new_m', 'new_m_ab_w_out': 'new_m', 'new_m_ab_post_norm': 'new_m', 'new_m_sb_pre_norm': 'new_m', 'new_m_sb_w_in': 'new_m', 'new_m_sb_w_out': 'new_m', 'new_m_sb_post_norm': 'new_m', 'new_v_meta_tokens': 'new_v', 'new_v_ab_pre_norm': 'new_v', 'new_v_ab_w_in': 'new_v', 'new_v_ab_sinks': 'new_v', 'new_v_ab_conv_w': 'new_v', 'new_v_ab_conv_b': 'new_v', 'new_v_ab_conv_ln_g': 'new_v', 'new_v_ab_conv_ln_b': 'new_v', 'new_v_ab_w_pw2': 'new_v', 'new_v_ab_w_out': 'new_v', 'new_v_ab_post_norm': 'new_v', 'new_v_sb_pre_norm': 'new_v', 'new_v_sb_w_in': 'new_v', 'new_v_sb_w_out': 'new_v', 'new_v_sb_post_norm': 'new_v'}


def _forward(args):
    return _fwd_reference(*[args[k] for k in FWD_PARAMS])


def _output_shape():
    out = _jax.eval_shape(lambda: _forward(_fwd_setup_inputs(0)))
    return out.shape, out.dtype

N_MICROBATCH = 1
ADAM_LR = 0.001
ADAM_B1 = 0.9
ADAM_B2 = 0.999
ADAM_EPS = 1e-08
ADAM_WD = 0.01
ADAM_STEP = 10
PER_EXAMPLE_BATCH_AXIS = {'x': 0, 'loss_target': 0}
SHARED_INPUTS = []
_WEIGHT_DTYPES = {'meta_tokens': _jnp.float32, 'ab_pre_norm': _jnp.float32, 'ab_w_in': _jnp.float32, 'ab_sinks': _jnp.float32, 'ab_conv_w': _jnp.float32, 'ab_conv_b': _jnp.float32, 'ab_conv_ln_g': _jnp.float32, 'ab_conv_ln_b': _jnp.float32, 'ab_w_pw2': _jnp.float32, 'ab_w_out': _jnp.float32, 'ab_post_norm': _jnp.float32, 'sb_pre_norm': _jnp.float32, 'sb_w_in': _jnp.float32, 'sb_w_out': _jnp.float32, 'sb_post_norm': _jnp.float32}
MOMENT_SCALE = {'meta_tokens': 2.563273e-02, 'ab_pre_norm': 5.101297e-01, 'ab_w_in': 3.115606e-01, 'ab_sinks': 2.503547e-02, 'ab_conv_w': 4.650976e-01, 'ab_conv_b': 1.318337e+00, 'ab_conv_ln_g': 6.215069e-01, 'ab_conv_ln_b': 7.234851e-01, 'ab_w_pw2': 5.044326e-01, 'ab_w_out': 3.684235e-01, 'ab_post_norm': 1.602901e+01, 'sb_pre_norm': 3.719669e-01, 'sb_w_in': 1.812472e-01, 'sb_w_out': 2.390592e-01, 'sb_post_norm': 1.602351e+01}


def _to_microbatches(a, axis):
    t = _jnp.moveaxis(a, axis, 0)
    t = t.reshape((N_MICROBATCH, t.shape[0] // N_MICROBATCH) + t.shape[1:])
    return _jnp.moveaxis(t, 1, axis + 1)


def setup_inputs(seed: int = 0) -> dict:
    inp = _fwd_setup_inputs(seed)
    key = _jax.random.fold_in(_jax.random.key(seed), 7919)
    shape, _ = _output_shape()
    out = dict(inp)
    out["loss_target"] = _jax.random.normal(_jax.random.fold_in(key, 0), shape, _jnp.float32)
    for i, name in enumerate(TWIN_WEIGHTS):
        w = inp[name].astype(_jnp.float32)
        if MOMENT_SCALE is None:
            s = _jnp.sqrt(_jnp.mean(_jnp.square(w)) + 1e-30)
        else:
            s = MOMENT_SCALE[name]
        km, kv = _jax.random.split(_jax.random.fold_in(key, i + 1))
        out[name] = w
        out["m_" + name] = s * _jax.random.normal(km, w.shape, _jnp.float32)
        out["v_" + name] = (s * s) * _jax.random.uniform(kv, w.shape, _jnp.float32, 0.5, 1.5)
    if N_MICROBATCH > 1:
        for name, axis in PER_EXAMPLE_BATCH_AXIS.items():
            out[name] = _to_microbatches(out[name], axis)
    return {'x': out['x'], 'meta_tokens': out['meta_tokens'], 'ab_pre_norm': out['ab_pre_norm'], 'ab_w_in': out['ab_w_in'], 'ab_sinks': out['ab_sinks'], 'ab_conv_w': out['ab_conv_w'], 'ab_conv_b': out['ab_conv_b'], 'ab_conv_ln_g': out['ab_conv_ln_g'], 'ab_conv_ln_b': out['ab_conv_ln_b'], 'ab_w_pw2': out['ab_w_pw2'], 'ab_w_out': out['ab_w_out'], 'ab_post_norm': out['ab_post_norm'], 'sb_pre_norm': out['sb_pre_norm'], 'sb_w_in': out['sb_w_in'], 'sb_w_out': out['sb_w_out'], 'sb_post_norm': out['sb_post_norm'], 'loss_target': out['loss_target'], 'm_meta_tokens': out['m_meta_tokens'], 'm_ab_pre_norm': out['m_ab_pre_norm'], 'm_ab_w_in': out['m_ab_w_in'], 'm_ab_sinks': out['m_ab_sinks'], 'm_ab_conv_w': out['m_ab_conv_w'], 'm_ab_conv_b': out['m_ab_conv_b'], 'm_ab_conv_ln_g': out['m_ab_conv_ln_g'], 'm_ab_conv_ln_b': out['m_ab_conv_ln_b'], 'm_ab_w_pw2': out['m_ab_w_pw2'], 'm_ab_w_out': out['m_ab_w_out'], 'm_ab_post_norm': out['m_ab_post_norm'], 'm_sb_pre_norm': out['m_sb_pre_norm'], 'm_sb_w_in': out['m_sb_w_in'], 'm_sb_w_out': out['m_sb_w_out'], 'm_sb_post_norm': out['m_sb_post_norm'], 'v_meta_tokens': out['v_meta_tokens'], 'v_ab_pre_norm': out['v_ab_pre_norm'], 'v_ab_w_in': out['v_ab_w_in'], 'v_ab_sinks': out['v_ab_sinks'], 'v_ab_conv_w': out['v_ab_conv_w'], 'v_ab_conv_b': out['v_ab_conv_b'], 'v_ab_conv_ln_g': out['v_ab_conv_ln_g'], 'v_ab_conv_ln_b': out['v_ab_conv_ln_b'], 'v_ab_w_pw2': out['v_ab_w_pw2'], 'v_ab_w_out': out['v_ab_w_out'], 'v_ab_post_norm': out['v_ab_post_norm'], 'v_sb_pre_norm': out['v_sb_pre_norm'], 'v_sb_w_in': out['v_sb_w_in'], 'v_sb_w_out': out['v_sb_w_out'], 'v_sb_post_norm': out['v_sb_post_norm']}


def _loss(weights, diff, rest, loss_target):
    with _jax.named_scope("forward"):
        args = {**rest, TWIN_DIFF_INPUT: diff, **{k: w.astype(_WEIGHT_DTYPES[k]) for k, w in weights.items()}}
        y = _forward(args)
    with _jax.named_scope("loss_head"):
        err = _jnp.square(y.astype(_jnp.float32) - loss_target)
        return 0.5 * _jnp.sum(_jnp.mean(err, axis=-1)) if err.ndim else 0.5 * err


def _adamw(w, g, m, v):
    m = ADAM_B1 * m + (1.0 - ADAM_B1) * g
    v = ADAM_B2 * v + (1.0 - ADAM_B2) * _jnp.square(g)
    m_hat = m / (1.0 - ADAM_B1 ** ADAM_STEP)
    v_hat = v / (1.0 - ADAM_B2 ** ADAM_STEP)
    delta = -ADAM_LR * (m_hat / (_jnp.sqrt(v_hat) + ADAM_EPS) + ADAM_WD * w)
    return delta, m, v


def reference(x, meta_tokens, ab_pre_norm, ab_w_in, ab_sinks, ab_conv_w, ab_conv_b, ab_conv_ln_g, ab_conv_ln_b, ab_w_pw2, ab_w_out, ab_post_norm, sb_pre_norm, sb_w_in, sb_w_out, sb_post_norm, loss_target, m_meta_tokens, m_ab_pre_norm, m_ab_w_in, m_ab_sinks, m_ab_conv_w, m_ab_conv_b, m_ab_conv_ln_g, m_ab_conv_ln_b, m_ab_w_pw2, m_ab_w_out, m_ab_post_norm, m_sb_pre_norm, m_sb_w_in, m_sb_w_out, m_sb_post_norm, v_meta_tokens, v_ab_pre_norm, v_ab_w_in, v_ab_sinks, v_ab_conv_w, v_ab_conv_b, v_ab_conv_ln_g, v_ab_conv_ln_b, v_ab_w_pw2, v_ab_w_out, v_ab_post_norm, v_sb_pre_norm, v_sb_w_in, v_sb_w_out, v_sb_post_norm):
    given = dict(x=x, meta_tokens=meta_tokens, ab_pre_norm=ab_pre_norm, ab_w_in=ab_w_in, ab_sinks=ab_sinks, ab_conv_w=ab_conv_w, ab_conv_b=ab_conv_b, ab_conv_ln_g=ab_conv_ln_g, ab_conv_ln_b=ab_conv_ln_b, ab_w_pw2=ab_w_pw2, ab_w_out=ab_w_out, ab_post_norm=ab_post_norm, sb_pre_norm=sb_pre_norm, sb_w_in=sb_w_in, sb_w_out=sb_w_out, sb_post_norm=sb_post_norm, loss_target=loss_target, m_meta_tokens=m_meta_tokens, m_ab_pre_norm=m_ab_pre_norm, m_ab_w_in=m_ab_w_in, m_ab_sinks=m_ab_sinks, m_ab_conv_w=m_ab_conv_w, m_ab_conv_b=m_ab_conv_b, m_ab_conv_ln_g=m_ab_conv_ln_g, m_ab_conv_ln_b=m_ab_conv_ln_b, m_ab_w_pw2=m_ab_w_pw2, m_ab_w_out=m_ab_w_out, m_ab_post_norm=m_ab_post_norm, m_sb_pre_norm=m_sb_pre_norm, m_sb_w_in=m_sb_w_in, m_sb_w_out=m_sb_w_out, m_sb_post_norm=m_sb_post_norm, v_meta_tokens=v_meta_tokens, v_ab_pre_norm=v_ab_pre_norm, v_ab_w_in=v_ab_w_in, v_ab_sinks=v_ab_sinks, v_ab_conv_w=v_ab_conv_w, v_ab_conv_b=v_ab_conv_b, v_ab_conv_ln_g=v_ab_conv_ln_g, v_ab_conv_ln_b=v_ab_conv_ln_b, v_ab_w_pw2=v_ab_w_pw2, v_ab_w_out=v_ab_w_out, v_ab_post_norm=v_ab_post_norm, v_sb_pre_norm=v_sb_pre_norm, v_sb_w_in=v_sb_w_in, v_sb_w_out=v_sb_w_out, v_sb_post_norm=v_sb_post_norm)
    weights = {n: given[n] for n in TWIN_WEIGHTS}
    shared = {n: given[n] for n in SHARED_INPUTS}
    per_example = {n: given[n] for n in ['x']}
    grad_fn = _jax.value_and_grad(_loss, argnums=(0, 1))

    def one_microbatch(ex, loss_target):
        ex = dict(ex)
        diff = ex.pop(TWIN_DIFF_INPUT)
        return grad_fn(weights, diff, {**shared, **ex}, loss_target)

    if N_MICROBATCH == 1:
        loss, (grad_w, grad_x) = one_microbatch(per_example, given["loss_target"])
    else:
        def body(carry, xs):
            loss_sum, grad_sum = carry
            l_k, (gw_k, gx_k) = one_microbatch(xs[0], xs[1])
            with _jax.named_scope("update"):
                return (loss_sum + l_k, _jax.tree.map(_jnp.add, grad_sum, gw_k)), gx_k

        init = (_jnp.zeros((), _jnp.float32), _jax.tree.map(_jnp.zeros_like, weights))
        (loss, grad_w), grad_x = _jax.lax.scan(body, init, (per_example, given["loss_target"]))
    with _jax.named_scope("update"):
        delta_w, new_m, new_v = {}, {}, {}
        for n in TWIN_WEIGHTS:
            delta_w[n], new_m[n], new_v[n] = _adamw(weights[n], grad_w[n], given["m_" + n], given["v_" + n])
    return (loss, grad_x, *[grad_w[n] for n in TWIN_WEIGHTS], *[delta_w[n] for n in TWIN_WEIGHTS],
            *[new_m[n] for n in TWIN_WEIGHTS], *[new_v[n] for n in TWIN_WEIGHTS])
```

```python
import functools

import numpy as np
import jax
import jax.numpy as jnp
from jax import lax
from jax.experimental import pallas as pl
from jax.experimental.pallas import tpu as pltpu

F32 = jnp.float32
MXU = jnp.bfloat16
ACT = jnp.bfloat16
WIRE = jnp.bfloat16

D = 1024
N_META = 16
BLK = 128
PAD = BLK - N_META
HEAD = 64
NEG = -1e30
EPS = 1e-6
LN_EPS = 1e-5
ROPE_THETA = 10000.0
SCALE = HEAD ** -0.5
CONV_W = 31
HALO = 32
LR, B1, B2, ADAM_EPS, WD, STEP = 0.001, 0.9, 0.999, 1e-08, 0.01, 10
VMEM_LIMIT = 56 * 1024 * 1024
MESH = pl.DeviceIdType.MESH

P0_SRC = (5, 6, 7, 8, 0, 1, 3, 4, 9, 10, 2)


def _cparams(sem=None):
    return pltpu.CompilerParams(dimension_semantics=sem, vmem_limit_bytes=VMEM_LIMIT)


def _tile(t, pref):
    for cand in (pref, 544, 272, 128):
        if cand <= pref and t % cand == 0:
            return cand
    raise ValueError(t)


def _sigmoid(x):
    return 1.0 / (1.0 + jnp.exp(-x))


def _silu_and_grad(x):
    s = _sigmoid(x)
    return x * s, s * (1.0 + x * (1.0 - s))


def _dot(a, b):
    return jnp.dot(a, b, preferred_element_type=F32)


def _dot_nt(a, b):
    return lax.dot_general(a, b, (((1,), (1,)), ((), ())), preferred_element_type=F32)


def _dot_tn(a, b):
    return lax.dot_general(a, b, (((0,), (0,)), ((), ())), preferred_element_type=F32)


def _rows(shape, base):
    return base + lax.broadcasted_iota(jnp.int32, shape, 0)


def _rope_tables(t):
    half = HEAD // 2
    inv = ROPE_THETA ** (-np.arange(half, dtype=np.float32) / half)
    pos = (np.arange(t) - PAD).astype(np.float32)
    ang = pos[:, None] * inv[None, :]
    lane = np.arange(BLK)
    cos = np.cos(ang)[:, lane % half].astype(np.float32)
    sin = np.sin(ang)[:, lane % half].astype(np.float32)
    first = (lane % HEAD) < half
    sin_a = np.where(first[None, :], -sin, 0.0).astype(np.float32)
    sin_b = np.where(first[None, :], 0.0, sin).astype(np.float32)
    return jnp.asarray(cos), jnp.asarray(sin_a), jnp.asarray(sin_b)


def _rope(v, cos, sin_a, sin_b):
    return v * cos + pltpu.roll(v, 96, 1) * sin_a + pltpu.roll(v, 32, 1) * sin_b


def _unrope(v, cos, sin_a, sin_b):
    return v * cos - pltpu.roll(v, 96, 1) * sin_a - pltpu.roll(v, 32, 1) * sin_b


def _coords():
    return lax.axis_index("x"), lax.axis_index("y"), lax.axis_index("c")


def _all_gather(arrs, name):
    n = len(arrs)

    def body(*refs):
        ins, outs = refs[:n], refs[n:2 * n]
        send_sems, recv_sems, local_sems = refs[2 * n:]
        x, y, c = _coords()
        me, sibling = (x, y, c), (x, y, 1 - c)
        chips = [(1 - x, y), (x, 1 - y), (1 - x, 1 - y)]

        def copy(a, k, block, to, src=None):
            dst = outs[a].at[4 * block[0] + 2 * block[1] + block[2]]
            return pltpu.make_async_remote_copy(
                src_ref=dst if src is None else src, dst_ref=dst,
                send_sem=send_sems.at[a, k], recv_sem=recv_sems.at[a, k],
                device_id=to, device_id_type=MESH)

        mine = [pltpu.make_async_copy(ins[a], outs[a].at[4 * x + 2 * y + c], local_sems.at[a])
                for a in range(n)]
        for cp in mine:
            cp.start()
        first = []
        for a in range(n):
            first.append(copy(a, 0, me, sibling, src=ins[a]))
            for j, chip in enumerate(chips):
                first.append(copy(a, 1 + j, me, (*chip, c), src=ins[a]))
        for cp in first:
            cp.start()
        passed = []
        for j, chip in enumerate(chips):
            for a in range(n):
                copy(a, 1 + j, (*chip, c), me).wait_recv()
                cp = copy(a, 4 + j, (*chip, c), sibling)
                cp.start()
                passed.append(cp)
        for a in range(n):
            copy(a, 0, sibling, me).wait_recv()
            for j, chip in enumerate(chips):
                copy(a, 4 + j, (*chip, 1 - c), me).wait_recv()
        for cp in first + passed:
            cp.wait_send()
        for cp in mine:
            cp.wait()

    any_spec = pl.BlockSpec(memory_space=pl.ANY)
    return pl.pallas_call(
        body, name=name,
        out_shape=[jax.ShapeDtypeStruct((8,) + a.shape, a.dtype) for a in arrs],
        in_specs=[any_spec] * n, out_specs=[any_spec] * n,
        scratch_shapes=[pltpu.SemaphoreType.DMA((n, 7)), pltpu.SemaphoreType.DMA((n, 7)),
                        pltpu.SemaphoreType.DMA((n,))],
    )(*arrs)


def _exchange_sibling(parts, name):
    n = len(parts)

    def body(*refs):
        ins, outs = refs[:n], refs[n:2 * n]
        send_sems, recv_sems = refs[2 * n:]
        x, y, c = _coords()
        copies = [pltpu.make_async_remote_copy(
            src_ref=ins[a].at[:, 1 - c], dst_ref=outs[a],
            send_sem=send_sems.at[a], recv_sem=recv_sems.at[a],
            device_id=(x, y, 1 - c), device_id_type=MESH) for a in range(n)]
        for cp in copies:
            cp.start()
        for cp in copies:
            cp.wait()

    any_spec = pl.BlockSpec(memory_space=pl.ANY)
    return pl.pallas_call(
        body, name=name,
        out_shape=[jax.ShapeDtypeStruct((4,) + p.shape[2:], p.dtype) for p in parts],
        in_specs=[any_spec] * n, out_specs=[any_spec] * n,
        scratch_shapes=[pltpu.SemaphoreType.DMA((n,)), pltpu.SemaphoreType.DMA((n,))],
    )(*parts)


def _exchange_chips(sums, name):
    n = len(sums)

    def body(*refs):
        ins, outs = refs[:n], refs[n:2 * n]
        send_sems, recv_sems = refs[2 * n:]
        x, y, c = _coords()
        chips = [(1 - x, y), (x, 1 - y), (1 - x, 1 - y)]
        copies = []
        for a in range(n):
            for k, chip in enumerate(chips):
                copies.append(pltpu.make_async_remote_copy(
                    src_ref=ins[a].at[2 * chip[0] + chip[1]], dst_ref=outs[a].at[k],
                    send_sem=send_sems.at[a, k], recv_sem=recv_sems.at[a, k],
                    device_id=(*chip, c), device_id_type=MESH))
        for cp in copies:
            cp.start()
        for cp in copies:
            cp.wait()

    any_spec = pl.BlockSpec(memory_space=pl.ANY)
    return pl.pallas_call(
        body, name=name,
        out_shape=[jax.ShapeDtypeStruct((3,) + s.shape[1:], s.dtype) for s in sums],
        in_specs=[any_spec] * n, out_specs=[any_spec] * n,
        scratch_shapes=[pltpu.SemaphoreType.DMA((n, 3)), pltpu.SemaphoreType.DMA((n, 3))],
    )(*sums)


def _add_sibling(pos, part, recv, name):
    _, _, r, c = part.shape

    def body(pos_ref, p_ref, r_ref, o_ref):
        o_ref[...] = (p_ref[...].astype(F32) + r_ref[...].astype(F32)).astype(o_ref.dtype)

    return pl.pallas_call(
        body, name=name,
        grid_spec=pltpu.PrefetchScalarGridSpec(
            num_scalar_prefetch=1, grid=(4,),
            in_specs=[pl.BlockSpec((None, None, r, c), lambda q, pos: (q, pos[2], 0, 0)),
                      pl.BlockSpec((None, r, c), lambda q, pos: (q, 0, 0))],
            out_specs=pl.BlockSpec((None, r, c), lambda q, pos: (q, 0, 0))),
        out_shape=jax.ShapeDtypeStruct((4, r, c), part.dtype),
        compiler_params=_cparams(("arbitrary",)),
    )(pos, part, recv)


def _sum_chips(pos, sums, recv, name):
    _, r, c = sums.shape

    def body(pos_ref, s_ref, r_ref, o_ref):
        g = s_ref[...].astype(F32)
        for k in range(3):
            g = g + r_ref[k].astype(F32)
        o_ref[...] = g

    return pl.pallas_call(
        body, name=name,
        grid_spec=pltpu.PrefetchScalarGridSpec(
            num_scalar_prefetch=1, grid=(1,),
            in_specs=[pl.BlockSpec((None, r, c), lambda i, pos: (2 * pos[0] + pos[1], 0, 0)),
                      pl.BlockSpec((3, r, c), lambda i, pos: (0, 0, 0))],
            out_specs=pl.BlockSpec((r, c), lambda i, pos: (0, 0))),
        out_shape=jax.ShapeDtypeStruct((r, c), F32),
        compiler_params=_cparams(("arbitrary",)),
    )(pos, sums, recv)


def _adamw(ws, gs, ms, vs, name):
    n = len(ws)
    c1 = 1.0 / (1.0 - B1 ** STEP)
    c2 = 1.0 / (1.0 - B2 ** STEP)

    def body(*refs):
        w_r, g_r, m_r, v_r = refs[:n], refs[n:2 * n], refs[2 * n:3 * n], refs[3 * n:4 * n]
        d_o, m_o, v_o = refs[4 * n:5 * n], refs[5 * n:6 * n], refs[6 * n:7 * n]
        for a in range(n):
            g = g_r[a][...]
            m = B1 * m_r[a][...] + (1.0 - B1) * g
            v = B2 * v_r[a][...] + (1.0 - B2) * (g * g)
            d_o[a][...] = -LR * ((m * c1) / (jnp.sqrt(v * c2) + ADAM_EPS) + WD * w_r[a][...])
            m_o[a][...] = m
            v_o[a][...] = v

    shapes = [jax.ShapeDtypeStruct(w.shape, F32) for w in ws]
    outs = pl.pallas_call(body, name=name, out_shape=shapes * 3,
                          compiler_params=_cparams())(*ws, *gs, *ms, *vs)
    return outs[:n], outs[n:2 * n], outs[2 * n:]


def _reduce_small(gathered, name):
    n = len(gathered)

    def body(*refs):
        for a in range(n):
            acc = refs[a][0]
            for k in range(1, 8):
                acc = acc + refs[a][k]
            refs[n + a][...] = acc

    return pl.pallas_call(
        body, name=name,
        out_shape=[jax.ShapeDtypeStruct(g.shape[1:], F32) for g in gathered],
        compiler_params=_cparams())(*gathered)


class _Cols:
    def __init__(self, pieces, tw):
        self.pieces, self.tw = pieces, tw
        self.arrays = [p[0] for p in pieces]
        self.n_tiles = sum(p[2] for p in pieces)

    def specs(self, tm, row_of, tile_of):
        out = []
        for _, first, cnt in self.pieces:
            def imap(*g, first=first, cnt=cnt):
                return (row_of(*g), jnp.clip(tile_of(*g) - first, 0, cnt - 1))
            out.append(pl.BlockSpec((tm, self.tw), imap))
        return out

    def apply(self, t, refs, fn):
        for ref, (_, first, cnt) in zip(refs, self.pieces):
            pl.when((t >= first) & (t < first + cnt))(functools.partial(fn, ref))


def _ab_in(h, g, w_t, tables):
    t = h.shape[0]
    tm = _tile(t, 544)
    src = jnp.asarray(np.array(P0_SRC, np.int32))

    def body(src_ref, h_ref, g_ref, w_ref, cos_ref, sa_ref, sb_ref, o_ref, hn_ref, hn_s):
        j = pl.program_id(1)

        @pl.when(j == 0)
        def _():
            x = h_ref[...]
            hn = (x * lax.rsqrt(jnp.mean(x * x, -1, keepdims=True) + EPS) * g_ref[...]).astype(MXU)
            hn_s[...] = hn
            hn_ref[...] = hn.astype(ACT)

        acc = _dot_nt(hn_s[...], w_ref[...])
        rope = lambda v: _rope(v, cos_ref[...], sa_ref[...], sb_ref[...])

        @pl.when((j == 4) | (j == 5))
        def _():
            o_ref[:, :BLK] = rope(acc[:, :BLK])
            o_ref[:, BLK:] = rope(acc[:, BLK:])

        @pl.when(j == 10)
        def _():
            o_ref[:, :BLK] = rope(acc[:, :BLK])
            o_ref[:, BLK:] = acc[:, BLK:]

        @pl.when((j < 4) | ((j > 5) & (j < 10)))
        def _():
            o_ref[...] = acc

    tab = pl.BlockSpec((tm, BLK), lambda i, j, s: (i, 0))
    return pl.pallas_call(
        body, name="ab_in",
        grid_spec=pltpu.PrefetchScalarGridSpec(
            num_scalar_prefetch=1, grid=(t // tm, 11),
            in_specs=[pl.BlockSpec((tm, D), lambda i, j, s: (i, 0)),
                      pl.BlockSpec((1, D), lambda i, j, s: (0, 0)),
                      pl.BlockSpec((256, D), lambda i, j, s: (s[j], 0)),
                      tab, tab, tab],
            out_specs=[pl.BlockSpec((tm, 256), lambda i, j, s: (i, j)),
                       pl.BlockSpec((tm, D), lambda i, j, s: (i, 0))],
            scratch_shapes=[pltpu.VMEM((tm, D), MXU)]),
        out_shape=[jax.ShapeDtypeStruct((t, 2816), F32), jax.ShapeDtypeStruct((t, D), ACT)],
        compiler_params=_cparams(("arbitrary", "arbitrary")),
    )(src, h, g, w_t, *tables)


def _swa_mask(n):
    r = lax.broadcasted_iota(jnp.int32, (BLK, 3 * BLK), 0)
    c = lax.broadcasted_iota(jnp.int32, (BLK, 3 * BLK), 1)
    qpos = n * BLK + r
    bpos = (n - 2) * BLK + c
    meta_ok = (c >= PAD) & (c < BLK) & (qpos - c >= BLK)
    band_ok = (c >= BLK) & (bpos >= PAD) & (qpos >= bpos) & (qpos - bpos < BLK)
    return meta_ok | band_ok


def _swa_keys(kv_ref, n):
    def blk(b):
        return kv_ref[pl.ds(pl.multiple_of(b * BLK, BLK), BLK), :]
    kv = jnp.concatenate([kv_ref[0:BLK, :], blk(jnp.maximum(n - 1, 0)), blk(n)], axis=0)
    lo = lax.broadcasted_iota(jnp.int32, (1, BLK), 1) < HEAD
    out = []
    for part in (kv[:, :BLK], kv[:, BLK:]):
        rolled = pltpu.roll(part, HEAD, 1)
        out.append((jnp.where(lo, part, rolled).astype(MXU), jnp.where(lo, rolled, part).astype(MXU)))
    return out[0], out[1], lo


def _swa_probs(qm, kd, mask, sink):
    s = jnp.where(mask, _dot_nt(qm, kd) * SCALE, NEG)
    m = jnp.maximum(jnp.max(s, -1, keepdims=True), sink)
    e = jnp.exp(s - m)
    inv = 1.0 / (jnp.sum(e, -1, keepdims=True) + jnp.exp(sink - m))
    return e * inv, jnp.exp(sink - m) * inv


def _swa_fwd(p0, sinks):
    t = p0.shape[0]

    def body(sink_ref, q_ref, kv_ref, o_ref):
        n = pl.program_id(0)
        kd, vd, lo = _swa_keys(kv_ref, n)
        mask = _swa_mask(n)
        for p in range(4):
            qp = q_ref[:, p * BLK:(p + 1) * BLK]
            outs = []
            for j in range(2):
                qm = jnp.where(lo if j == 0 else ~lo, qp, 0.0).astype(MXU)
                pr, _ = _swa_probs(qm, kd[p // 2], mask, sink_ref[2 * p + j])
                outs.append(_dot(pr.astype(MXU), vd[p // 2]))
            o_ref[:, p * BLK:(p + 1) * BLK] = jnp.where(lo, outs[0], outs[1])

    return pl.pallas_call(
        body, name="swa_fwd", grid=(t // BLK,),
        in_specs=[pl.BlockSpec(memory_space=pltpu.SMEM),
                  pl.BlockSpec((BLK, 512), lambda n: (n, 2)),
                  pl.BlockSpec((t, 256), lambda n: (0, 10))],
        out_specs=pl.BlockSpec((BLK, 512), lambda n: (n, 0)),
        out_shape=jax.ShapeDtypeStruct((t, 512), F32),
        compiler_params=_cparams(("arbitrary",)),
    )(sinks, p0, p0)


def _conv_window(u_w, w_ref, n_out, first):
    rows = u_w.shape[0]
    acc = None
    for j in range(CONV_W):
        shifted = pltpu.roll(u_w, (rows - (first + j)) % rows, 0)[:n_out]
        term = shifted * w_ref[j:j + 1, :]
        acc = term if acc is None else acc + term
    return acc


def _conv_fwd(p0, conv_w, conv_b, ln_g, ln_b):
    t = p0.shape[0]
    tm = _tile(t, 544)
    hb = tm // HALO

    def body(cur_ref, prev_ref, w_ref, b_ref, g_ref, bb_ref, o_ref):
        i = pl.program_id(0)
        glu = jnp.concatenate([prev_ref[...], cur_ref[...]], axis=0)
        rw = _rows((tm + HALO, 1), i * tm - HALO)
        u_w = jnp.where(rw >= PAD, glu[:, :512] * _sigmoid(glu[:, 512:]), 0.0)
        cv = _conv_window(u_w, w_ref, tm, HALO - (CONV_W - 1)) + b_ref[...]
        xc = cv - jnp.mean(cv, -1, keepdims=True)
        ln = xc * lax.rsqrt(jnp.mean(xc * xc, -1, keepdims=True) + LN_EPS) * g_ref[...] + bb_ref[...]
        o_ref[...] = (ln * _sigmoid(ln)).astype(ACT)

    vec = pl.BlockSpec((1, 512), lambda i: (0, 0))
    return pl.pallas_call(
        body, name="conv_fwd", grid=(t // tm,),
        in_specs=[pl.BlockSpec((tm, D), lambda i: (i, 0)),
                  pl.BlockSpec((HALO, D), lambda i: (jnp.maximum(i * hb - 1, 0), 0)),
                  pl.BlockSpec((CONV_W, 512), lambda i: (0, 0)), vec, vec, vec],
        out_specs=pl.BlockSpec((tm, 512), lambda i: (i, 0)),
        out_shape=jax.ShapeDtypeStruct((t, 512), ACT),
        compiler_params=_cparams(("arbitrary",)),
    )(p0, p0, conv_w, conv_b, ln_g, ln_b)


def _ab_out(h, p0, att, c1, w_pw2, w_out, g_post):
    t = h.shape[0]
    tm = _tile(t, 272)

    def body(h_ref, ga_ref, gb_ref, att_ref, c1_ref, pw_ref, wo_ref, g_ref, h1_ref, y_ref, mix_ref):
        i = pl.program_id(0)
        sga, _ = _silu_and_grad(ga_ref[...])
        sgb, _ = _silu_and_grad(gb_ref[...])
        a = att_ref[...] * sga
        c = _dot(c1_ref[...].astype(MXU), pw_ref[...]) * sgb
        mix = jnp.concatenate([a, c], axis=1).astype(MXU)
        y = _dot(mix, wo_ref[...])
        yn = y * lax.rsqrt(jnp.mean(y * y, -1, keepdims=True) + EPS) * g_ref[...]
        h1_ref[...] = jnp.where(_rows((tm, 1), i * tm) >= PAD, h_ref[...] + yn, 0.0)
        y_ref[...] = y
        mix_ref[...] = mix.astype(ACT)

    row = lambda w, idx: pl.BlockSpec((tm, w), lambda i: (i, idx))
    full = lambda a: pl.BlockSpec(a.shape, lambda i: (0, 0))
    return pl.pallas_call(
        body, name="ab_out", grid=(t // tm,),
        in_specs=[row(D, 0), row(512, 3), row(512, 4), row(512, 0), row(512, 0),
                  full(w_pw2), full(w_out), full(g_post)],
        out_specs=[row(D, 0), row(D, 0), row(D, 0)],
        out_shape=[jax.ShapeDtypeStruct((t, D), F32), jax.ShapeDtypeStruct((t, D), F32),
                   jax.ShapeDtypeStruct((t, D), ACT)],
        compiler_params=_cparams(("arbitrary",)),
    )(h, p0, p0, att, c1, w_pw2, w_out, g_post)


def _sb_in(h, g, w):
    t = h.shape[0]
    tm = _tile(t, 544)

    def body(h_ref, g_ref, w_ref, o_ref, hn_ref, hn_s):
        @pl.when(pl.program_id(1) == 0)
        def _():
            x = h_ref[...]
            hn = (x * lax.rsqrt(jnp.mean(x * x, -1, keepdims=True) + EPS) * g_ref[...]).astype(MXU)
            hn_s[...] = hn
            hn_ref[...] = hn.astype(ACT)

        o_ref[...] = _dot(hn_s[...], w_ref[...])

    return pl.pallas_call(
        body, name="sb_in", grid=(t // tm, 8),
        in_specs=[pl.BlockSpec((tm, D), lambda i, j: (i, 0)),
                  pl.BlockSpec((1, D), lambda i, j: (0, 0)),
                  pl.BlockSpec((None, D, 512), lambda i, j: (j, 0, 0))],
        out_specs=[pl.BlockSpec((tm, 512), lambda i, j: (i, j)),
                   pl.BlockSpec((tm, D), lambda i, j: (i, 0))],
        out_shape=[jax.ShapeDtypeStruct((t, 4096), F32), jax.ShapeDtypeStruct((t, D), ACT)],
        scratch_shapes=[pltpu.VMEM((tm, D), MXU)],
        compiler_params=_cparams(("arbitrary", "arbitrary")),
    )(h, g, w)


def _split_hi_lo(x):
    hi = x.astype(MXU)
    lo = (x - hi.astype(F32)).astype(MXU)
    return hi, lo


def _scan_matrix(suffix):
    j = lax.broadcasted_iota(jnp.int32, (BLK, 2 * BLK), 0)
    s = lax.broadcasted_iota(jnp.int32, (BLK, 2 * BLK), 1)
    keep = (s >= BLK) | ((j > s) if suffix else (j < s))
    return jnp.where(keep, 1.0, 0.0).astype(MXU)


def _scan_sums(x, mat):
    hi, lo = _split_hi_lo(x)
    both = _dot(hi, mat) + _dot(lo, mat)
    return both[:, :BLK], both[:, BLK:]


def _sb_tile(qm, k2, i, kb):
    z = _dot_nt(qm, k2) * SCALE
    soft = jnp.log(1.0 + jnp.exp(-jnp.abs(z)))
    log_beta = jnp.minimum(z, 0.0) - soft
    r = lax.broadcasted_iota(jnp.int32, (BLK, BLK), 0)
    c = lax.broadcasted_iota(jnp.int32, (BLK, BLK), 1)
    kpos = kb * BLK + c
    valid = (kpos >= PAD) & (kpos < i * BLK + r)
    log_1m = jnp.where(valid, log_beta - z, 0.0)
    return log_beta, log_1m, valid


def _sb_fwd(p1):
    t = p1.shape[0]

    def body(q_ref, k_ref, v_ref, o_ref, lt_ref):
        i = pl.program_id(1)
        lo = lax.broadcasted_iota(jnp.int32, (1, BLK), 1) < HEAD
        q = q_ref[...]
        qm = [jnp.where(lo, q, 0.0).astype(MXU), jnp.where(lo, 0.0, q).astype(MXU)]
        mat = _scan_matrix(True)

        def step(s, carry):
            kb = i - s
            off = pl.multiple_of(kb * BLK, BLK)
            k2 = k_ref[pl.ds(off, BLK), :].astype(MXU)
            v2 = v_ref[pl.ds(off, BLK), :].astype(MXU)
            new = []
            for j in range(2):
                later, acc = carry[2 * j], carry[2 * j + 1]
                log_beta, log_1m, valid = _sb_tile(qm[j], k2, i, kb)
                after, total = _scan_sums(log_1m, mat)
                a = jnp.where(valid, jnp.exp(log_beta + after + later), 0.0)
                new += [later + total, acc + _dot(a.astype(MXU), v2)]
            return tuple(new)

        zero = jnp.zeros((BLK, BLK), F32)
        res = lax.fori_loop(0, i + 1, step, (zero, zero, zero, zero))
        o_ref[...] = jnp.where(lo, res[1], res[3])
        lt_ref[...] = jnp.where(lo, res[0], res[2])

    blk = pl.BlockSpec((BLK, BLK), lambda hp, i: (i, hp))
    return pl.pallas_call(
        body, name="sb_fwd", grid=(8, t // BLK),
        in_specs=[blk,
                  pl.BlockSpec((t, BLK), lambda hp, i: (0, 8 + hp)),
                  pl.BlockSpec((t, BLK), lambda hp, i: (0, 16 + hp))],
        out_specs=[blk, blk],
        out_shape=[jax.ShapeDtypeStruct((t, D), F32)] * 2,
        compiler_params=_cparams(("arbitrary", "arbitrary")),
    )(p1, p1, p1)


def _sb_out(o, p1, w_out, h1, g_post, tgt):
    t = o.shape[0]
    tm = _tile(t, 272)

    def body(o_ref, g_ref, w_ref, h_ref, gp_ref, t_ref,
             loss_ref, dh_ref, dy_ref, m_ref, do_ref, dg_ref, dgp_ref):
        i = pl.program_id(0)

        @pl.when(i == 0)
        def _():
            loss_ref[...] = jnp.zeros_like(loss_ref)
            dgp_ref[...] = jnp.zeros_like(dgp_ref)

        gate = g_ref[...]
        sg, dsg = _silu_and_grad(gate)
        ov = o_ref[...]
        m = (ov * sg).astype(MXU)
        y = _dot(m, w_ref[...])
        r = lax.rsqrt(jnp.mean(y * y, -1, keepdims=True) + EPS)
        yhat = y * r
        h2 = h_ref[...] + yhat * gp_ref[...]
        diff = jnp.where(_rows((tm, 1), i * tm) >= BLK, h2 - t_ref[...], 0.0)
        loss_ref[...] += jnp.full(loss_ref.shape, 0.5 / D, F32) * jnp.sum(diff * diff)
        dh = diff * (1.0 / D)
        dgp_ref[...] += jnp.sum(dh * yhat, 0, keepdims=True)
        dyn = dh * gp_ref[...]
        dy = (r * (dyn - yhat * jnp.mean(dyn * yhat, -1, keepdims=True))).astype(MXU)
        dm = _dot_nt(dy, w_ref[...])
        dh_ref[...] = dh
        dy_ref[...] = dy.astype(ACT)
        m_ref[...] = m.astype(ACT)
        do_ref[...] = dm * sg
        dg_ref[...] = (dm * ov * dsg).astype(ACT)

    row = lambda idx: pl.BlockSpec((tm, D), lambda i: (i, idx))
    full = lambda a: pl.BlockSpec(a.shape, lambda i: (0, 0))
    acc = lambda s: pl.BlockSpec(s, lambda i: (0, 0))
    return pl.pallas_call(
        body, name="sb_out", grid=(t // tm,),
        in_specs=[row(0), row(3), full(w_out), row(0), full(g_post), row(0)],
        out_specs=[acc((8, BLK)), row(0), row(0), row(0), row(0), row(0), acc((1, D))],
        out_shape=[jax.ShapeDtypeStruct((8, BLK), F32), jax.ShapeDtypeStruct((t, D), F32),
                   jax.ShapeDtypeStruct((t, D), ACT), jax.ShapeDtypeStruct((t, D), ACT),
                   jax.ShapeDtypeStruct((t, D), F32), jax.ShapeDtypeStruct((t, D), ACT),
                   jax.ShapeDtypeStruct((1, D), F32)],
        compiler_params=_cparams(("arbitrary",)),
    )(o, p1, w_out, h1, g_post, tgt)


def _sb_bwd(p1, ltot, do):
    t = p1.shape[0]
    nb = t // BLK

    def body(q_ref, k_ref, v_ref, lt_ref, do_ref, dq_ref, dk_ref, dv_ref, dk_s, dv_s):
        i = pl.program_id(1)
        lo = lax.broadcasted_iota(jnp.int32, (1, BLK), 1) < HEAD

        @pl.when(i == 0)
        def _():
            dk_s[...] = jnp.zeros_like(dk_s)
            dv_s[...] = jnp.zeros_like(dv_s)

        q, dout, lt = q_ref[...], do_ref[...], lt_ref[...]
        halves = [lo, ~lo]
        qm = [jnp.where(h, q, 0.0).astype(MXU) for h in halves]
        dom = [jnp.where(h, dout, 0.0).astype(MXU) for h in halves]
        lt_r = pltpu.roll(lt, HEAD, 1)
        row_total = [jnp.where(lo, lt, lt_r), jnp.where(lo, lt_r, lt)]
        mat_l = _scan_matrix(True)
        mat_g = _scan_matrix(False)

        def step(kb, carry):
            off = pl.multiple_of(kb * BLK, BLK)
            k2 = k_ref[pl.ds(off, BLK), :].astype(MXU)
            v2 = v_ref[pl.ds(off, BLK), :].astype(MXU)
            new = []
            dk_t = jnp.zeros((BLK, BLK), F32)
            dv_t = jnp.zeros((BLK, BLK), F32)
            for j in range(2):
                passed, passed_g, dq = carry[3 * j], carry[3 * j + 1], carry[3 * j + 2]
                log_beta, log_1m, valid = _sb_tile(qm[j], k2, i, kb)
                after, total = _scan_sums(log_1m, mat_l)
                later = row_total[j] - (passed + total)
                a = jnp.where(valid, jnp.exp(log_beta + after + later), 0.0)
                g = _dot_nt(dom[j], v2) * a
                before, total_g = _scan_sums(g, mat_g)
                before = before + passed_g
                sig = jnp.exp(log_beta)
                dz = jnp.where(valid, g * (1.0 - sig) - sig * before, 0.0) * SCALE
                dzm = dz.astype(MXU)
                dk_t = dk_t + _dot_tn(dzm, qm[j])
                dv_t = dv_t + _dot_tn(a.astype(MXU), dom[j])
                new += [passed + total, passed_g + total_g, dq + _dot(dzm, k2)]
            dk_s[pl.ds(off, BLK), :] += dk_t
            dv_s[pl.ds(off, BLK), :] += dv_t
            return tuple(new)

        zero = jnp.zeros((BLK, BLK), F32)
        res = lax.fori_loop(0, i + 1, step, (zero,) * 6)
        dq_ref[...] = jnp.where(lo, res[2], res[5]).astype(ACT)

        @pl.when(i == nb - 1)
        def _():
            dk_ref[...] = dk_s[...].astype(ACT)
            dv_ref[...] = dv_s[...].astype(ACT)

    blk = lambda off: pl.BlockSpec((BLK, BLK), lambda hp, i: (i, off + hp))
    col = lambda off: pl.BlockSpec((t, BLK), lambda hp, i: (0, off + hp))
    return pl.pallas_call(
        body, name="sb_bwd", grid=(8, nb),
        in_specs=[blk(0), col(8), col(16), blk(0), blk(0)],
        out_specs=[blk(0), col(0), col(0)],
        out_shape=[jax.ShapeDtypeStruct((t, D), ACT)] * 3,
        scratch_shapes=[pltpu.VMEM((t, BLK), F32), pltpu.VMEM((t, BLK), F32)],
        compiler_params=_cparams(("arbitrary", "arbitrary")),
    )(p1, p1, p1, ltot, do)


def _mid_bwd(dp1, w_sb, h1, g_pre1, dh2, y0, g_post0, w_out, p0, att, c1, w_pw2):
    t = h1.shape[0]
    tm = _tile(t, 272)

    def body(*refs):
        d_refs = refs[:4]
        (w_ref, h_ref, g1_ref, dh2_ref, y_ref, g0_ref, wo_ref, ga_ref, gb_ref, att_ref, c1_ref,
         pw_ref, dh1_ref, dy_ref, dga_ref, dgb_ref, datt_ref, dc1_ref, dc2_ref, dg1_ref, dg0_ref,
         acc) = refs[4:]
        i, j = pl.program_id(0), pl.program_id(1)

        @pl.when((i == 0) & (j == 0))
        def _():
            dg1_ref[...] = jnp.zeros_like(dg1_ref)
            dg0_ref[...] = jnp.zeros_like(dg0_ref)

        @pl.when(j == 0)
        def _():
            acc[...] = jnp.zeros_like(acc)

        def add(ref):
            acc[...] += _dot_nt(ref[...].astype(MXU), w_ref[...])
        dp1.apply(j, d_refs, add)

        @pl.when(j == 7)
        def _():
            dhn = acc[...]
            x = h_ref[...]
            r = lax.rsqrt(jnp.mean(x * x, -1, keepdims=True) + EPS)
            xhat = x * r
            dg1_ref[...] += jnp.sum(dhn * xhat, 0, keepdims=True)
            dxn = dhn * g1_ref[...]
            dh1 = dh2_ref[...] + r * (dxn - xhat * jnp.mean(dxn * xhat, -1, keepdims=True))
            dh1_ref[...] = dh1
            y = y_ref[...]
            ry = lax.rsqrt(jnp.mean(y * y, -1, keepdims=True) + EPS)
            yhat = y * ry
            dg0_ref[...] += jnp.sum(dh1 * yhat, 0, keepdims=True)
            dyn = dh1 * g0_ref[...]
            dy = (ry * (dyn - yhat * jnp.mean(dyn * yhat, -1, keepdims=True))).astype(MXU)
            dy_ref[...] = dy.astype(ACT)
            dmix = _dot_nt(dy, wo_ref[...])
            da, dc = dmix[:, :512], dmix[:, 512:]
            sga, dsga = _silu_and_grad(ga_ref[...])
            sgb, dsgb = _silu_and_grad(gb_ref[...])
            datt_ref[...] = da * sga
            dga_ref[...] = (da * att_ref[...] * dsga).astype(ACT)
            c2 = _dot(c1_ref[...].astype(MXU), pw_ref[...])
            dc2 = (dc * sgb).astype(MXU)
            dgb_ref[...] = (dc * c2 * dsgb).astype(ACT)
            dc2_ref[...] = dc2.astype(ACT)
            dc1_ref[...] = _dot_nt(dc2, pw_ref[...])

    row = lambda w, idx: pl.BlockSpec((tm, w), lambda i, j: (i, idx))
    full = lambda a: pl.BlockSpec(a.shape, lambda i, j: (0, 0))
    acc_spec = pl.BlockSpec((1, D), lambda i, j: (0, 0))
    sd = jax.ShapeDtypeStruct
    return pl.pallas_call(
        body, name="mid_bwd", grid=(t // tm, 8),
        in_specs=dp1.specs(tm, lambda i, j: i, lambda i, j: j) + [
            pl.BlockSpec((None, D, 512), lambda i, j: (j, 0, 0)),
            row(D, 0), full(g_pre1), row(D, 0), row(D, 0), full(g_post0), full(w_out),
            row(512, 3), row(512, 4), row(512, 0), row(512, 0), full(w_pw2)],
        out_specs=[row(D, 0), row(D, 0), row(512, 0), row(512, 0), row(512, 0), row(512, 0),
                   row(512, 0), acc_spec, acc_spec],
        out_shape=[sd((t, D), F32), sd((t, D), ACT), sd((t, 512), ACT), sd((t, 512), ACT),
                   sd((t, 512), F32), sd((t, 512), F32), sd((t, 512), ACT),
                   sd((1, D), F32), sd((1, D), F32)],
        scratch_shapes=[pltpu.VMEM((tm, D), F32)],
        compiler_params=_cparams(("arbitrary", "arbitrary")),
    )(*dp1.arrays, w_sb, h1, g_pre1, dh2, y0, g_post0, w_out, p0, p0, att, c1, w_pw2)


def _conv_bwd(p0, dc1, conv_w, conv_b, ln_g, ln_b):
    t = p0.shape[0]
    tm = _tile(t, 544)
    hb = tm // HALO
    last = t // HALO - 1

    def body(cur_ref, prev_ref, next_ref, d_ref, dn_ref, w_ref, b_ref, g_ref, bb_ref,
             dglu_ref, dw_ref, db_ref, dlg_ref, dlb_ref):
        i = pl.program_id(0)

        @pl.when(i == 0)
        def _():
            for ref in (dw_ref, db_ref, dlg_ref, dlb_ref):
                ref[...] = jnp.zeros_like(ref)

        glu = jnp.concatenate([prev_ref[...], cur_ref[...], next_ref[...]], axis=0)
        rw = _rows((tm + 2 * HALO, 1), i * tm - HALO)
        ga, sg = glu[:, :512], _sigmoid(glu[:, 512:])
        u_w = jnp.where((rw >= PAD) & (rw < t), ga * sg, 0.0)
        n_cv = tm + HALO
        cv = _conv_window(u_w, w_ref, n_cv, HALO - (CONV_W - 1)) + b_ref[...]
        xc = cv - jnp.mean(cv, -1, keepdims=True)
        rstd = lax.rsqrt(jnp.mean(xc * xc, -1, keepdims=True) + LN_EPS)
        cvhat = xc * rstd
        ln = cvhat * g_ref[...] + bb_ref[...]
        _, dsl = _silu_and_grad(ln)
        rc = _rows((n_cv, 1), i * tm)
        dc = jnp.concatenate([d_ref[...], dn_ref[...]], axis=0)
        dln = jnp.where(rc < t, dc * dsl, 0.0)
        dhat = dln * g_ref[...]
        dcv = rstd * (dhat - jnp.mean(dhat, -1, keepdims=True)
                      - cvhat * jnp.mean(dhat * cvhat, -1, keepdims=True))
        own = dcv[:tm]
        dlg_ref[...] += jnp.sum((dln * cvhat)[:tm], 0, keepdims=True)
        dlb_ref[...] += jnp.sum(dln[:tm], 0, keepdims=True)
        db_ref[...] += jnp.sum(own, 0, keepdims=True)
        rows = tm + 2 * HALO
        du = None
        for j in range(CONV_W):
            first = HALO - (CONV_W - 1) + j
            shifted = pltpu.roll(u_w, (rows - first) % rows, 0)[:tm]
            dw_ref[j:j + 1, :] += jnp.sum(own * shifted, 0, keepdims=True)
            back = pltpu.roll(dcv, (n_cv - (CONV_W - 1 - j)) % n_cv, 0)[:tm]
            term = back * w_ref[j:j + 1, :]
            du = term if du is None else du + term
        du = jnp.where(_rows((tm, 1), i * tm) >= PAD, du, 0.0)
        ga_c, sg_c = ga[HALO:HALO + tm], sg[HALO:HALO + tm]
        dglu_ref[:, :512] = (du * sg_c).astype(ACT)
        dglu_ref[:, 512:] = (du * ga_c * sg_c * (1.0 - sg_c)).astype(ACT)

    vec = pl.BlockSpec((1, 512), lambda i: (0, 0))
    nxt = lambda i: (jnp.minimum((i + 1) * hb, last), 0)
    return pl.pallas_call(
        body, name="conv_bwd", grid=(t // tm,),
        in_specs=[pl.BlockSpec((tm, D), lambda i: (i, 0)),
                  pl.BlockSpec((HALO, D), lambda i: (jnp.maximum(i * hb - 1, 0), 0)),
                  pl.BlockSpec((HALO, D), nxt),
                  pl.BlockSpec((tm, 512), lambda i: (i, 0)),
                  pl.BlockSpec((HALO, 512), nxt),
                  pl.BlockSpec((CONV_W, 512), lambda i: (0, 0)), vec, vec, vec],
        out_specs=[pl.BlockSpec((tm, D), lambda i: (i, 0)),
                   pl.BlockSpec((HALO, 512), lambda i: (0, 0)), vec, vec, vec],
        out_shape=[jax.ShapeDtypeStruct((t, D), ACT), jax.ShapeDtypeStruct((HALO, 512), F32)]
        + [jax.ShapeDtypeStruct((1, 512), F32)] * 3,
        compiler_params=_cparams(("arbitrary",)),
    )(p0, p0, p0, dc1, dc1, conv_w, conv_b, ln_g, ln_b)


def _swa_bwd(p0, datt, sinks, tables):
    t = p0.shape[0]
    nb = t // BLK

    def body(sink_ref, q_ref, kv_ref, d_ref, cos_ref, sa_ref, sb_ref,
             dq_ref, dkv_ref, ds_ref, acc):
        n = pl.program_id(0)

        @pl.when(n == 0)
        def _():
            acc[...] = jnp.zeros_like(acc)
            ds_ref[...] = jnp.zeros_like(ds_ref)

        kd, vd, lo = _swa_keys(kv_ref, n)
        mask = _swa_mask(n)
        row0 = pl.multiple_of(n * BLK, BLK)
        tabs = [r[pl.ds(row0, BLK), :] for r in (cos_ref, sa_ref, sb_ref)]
        dk_g = [jnp.zeros((3 * BLK, BLK), F32), jnp.zeros((3 * BLK, BLK), F32)]
        dv_g = [jnp.zeros((3 * BLK, BLK), F32), jnp.zeros((3 * BLK, BLK), F32)]
        for p in range(4):
            g = p // 2
            qp = q_ref[:, p * BLK:(p + 1) * BLK]
            dp_ = d_ref[:, p * BLK:(p + 1) * BLK]
            dqs = []
            for j in range(2):
                half = lo if j == 0 else ~lo
                qm = jnp.where(half, qp, 0.0).astype(MXU)
                dom = jnp.where(half, dp_, 0.0).astype(MXU)
                pr, p_sink = _swa_probs(qm, kd[g], mask, sink_ref[2 * p + j])
                dpr = _dot_nt(dom, vd[g])
                delta = jnp.sum(pr * dpr, -1, keepdims=True)
                dsc = (pr * (dpr - delta) * SCALE).astype(MXU)
                ds_ref[2 * p + j:2 * p + j + 1, :] += jnp.full((1, BLK), -1.0, F32) * jnp.sum(p_sink * delta)
                dqs.append(_dot(dsc, kd[g]))
                dk_g[g] = dk_g[g] + _dot_tn(dsc, qm)
                dv_g[g] = dv_g[g] + _dot_tn(pr.astype(MXU), dom)
            dq_ref[:, p * BLK:(p + 1) * BLK] = _unrope(jnp.where(lo, dqs[0], dqs[1]), *tabs).astype(ACT)
        fold = lambda a: a + pltpu.roll(a, HEAD, 1)
        dk = jnp.where(lo, fold(dk_g[0]), fold(dk_g[1]))
        dv = jnp.where(lo, fold(dv_g[0]), fold(dv_g[1]))
        dkv = jnp.concatenate([dk, dv], axis=1)
        prev = pl.multiple_of(jnp.maximum(n - 1, 0) * BLK, BLK)
        acc[0:BLK, :] += dkv[0:BLK]
        acc[pl.ds(prev, BLK), :] += dkv[BLK:2 * BLK]
        acc[pl.ds(row0, BLK), :] += dkv[2 * BLK:]

        @pl.when(n == nb - 1)
        def _():
            dkv_ref[:, :BLK] = _unrope(acc[:, :BLK], cos_ref[...], sa_ref[...], sb_ref[...]).astype(ACT)
            dkv_ref[:, BLK:] = acc[:, BLK:].astype(ACT)

    tab = pl.BlockSpec((t, BLK), lambda n: (0, 0))
    return pl.pallas_call(
        body, name="swa_bwd", grid=(nb,),
        in_specs=[pl.BlockSpec(memory_space=pltpu.SMEM),
                  pl.BlockSpec((BLK, 512), lambda n: (n, 2)),
                  pl.BlockSpec((t, 256), lambda n: (0, 10)),
                  pl.BlockSpec((BLK, 512), lambda n: (n, 0)), tab, tab, tab],
        out_specs=[pl.BlockSpec((BLK, 512), lambda n: (n, 0)),
                   pl.BlockSpec((t, 256), lambda n: (0, 0)),
                   pl.BlockSpec((8, BLK), lambda n: (0, 0))],
        out_shape=[jax.ShapeDtypeStruct((t, 512), ACT), jax.ShapeDtypeStruct((t, 256), ACT),
                   jax.ShapeDtypeStruct((8, BLK), F32)],
        scratch_shapes=[pltpu.VMEM((t, 256), F32)],
        compiler_params=_cparams(("arbitrary",)),
    )(sinks, p0, p0, datt, *tables)


def _ab_in_bwd(dp0, w_t, h0, g_pre, dh1):
    t = h0.shape[0]
    tm = _tile(t, 544)

    def body(*refs):
        d_refs = refs[:5]
        w_ref, h_ref, g_ref, dh1_ref, dh0_ref, dg_ref, acc = refs[5:]
        i, j = pl.program_id(0), pl.program_id(1)

        @pl.when((i == 0) & (j == 0))
        def _():
            dg_ref[...] = jnp.zeros_like(dg_ref)

        @pl.when(j == 0)
        def _():
            acc[...] = jnp.zeros_like(acc)

        def add(ref):
            acc[...] += _dot(ref[...].astype(MXU), w_ref[...])
        dp0.apply(j, d_refs, add)

        @pl.when(j == 10)
        def _():
            dhn = acc[...]
            x = h_ref[...]
            r = lax.rsqrt(jnp.mean(x * x, -1, keepdims=True) + EPS)
            xhat = x * r
            dg_ref[...] += jnp.sum(dhn * xhat, 0, keepdims=True)
            dxn = dhn * g_ref[...]
            dh0_ref[...] = dh1_ref[...] + r * (dxn - xhat * jnp.mean(dxn * xhat, -1, keepdims=True))

    row = pl.BlockSpec((tm, D), lambda i, j: (i, 0))
    vec = pl.BlockSpec((1, D), lambda i, j: (0, 0))
    return pl.pallas_call(
        body, name="ab_in_bwd", grid=(t // tm, 11),
        in_specs=dp0.specs(tm, lambda i, j: i, lambda i, j: j) + [
            pl.BlockSpec((256, D), lambda i, j: (j, 0)), row, vec, row],
        out_specs=[row, vec],
        out_shape=[jax.ShapeDtypeStruct((t, D), F32), jax.ShapeDtypeStruct((1, D), F32)],
        scratch_shapes=[pltpu.VMEM((tm, D), F32)],
        compiler_params=_cparams(("arbitrary", "arbitrary")),
    )(*dp0.arrays, w_t, h0, g_pre, dh1)


def _dw_plain(a, b, name):
    t, m = a.shape
    n = b.shape[1]
    tm = _tile(t, 544)
    tn = min(n, 512)
    nk = t // tm

    def body(a_ref, b_ref, o_ref, acc):
        k = pl.program_id(1)

        @pl.when(k == 0)
        def _():
            acc[...] = jnp.zeros_like(acc)

        acc[...] += _dot_tn(a_ref[...].astype(MXU), b_ref[...].astype(MXU))

        @pl.when(k == nk - 1)
        def _():
            o_ref[...] = acc[...].astype(WIRE)

    return pl.pallas_call(
        body, name=name, grid=(n // tn, nk),
        in_specs=[pl.BlockSpec((tm, m), lambda j, k: (k, 0)),
                  pl.BlockSpec((tm, tn), lambda j, k: (k, j))],
        out_specs=pl.BlockSpec((m, tn), lambda j, k: (0, j)),
        out_shape=jax.ShapeDtypeStruct((m, n), WIRE),
        scratch_shapes=[pltpu.VMEM((m, tn), F32)],
        compiler_params=_cparams(("arbitrary", "arbitrary")),
    )(a, b)


def _dw_chunks(hn, dp, name):
    t = hn.shape[0]
    tm = _tile(t, 544)
    nk = t // tm
    nt, tw = dp.n_tiles, dp.tw
    n_in = len(dp.arrays)

    def body(*refs):
        d_refs = refs[:n_in]
        h_ref, o_ref, acc = refs[n_in:]
        j, k = pl.program_id(0), pl.program_id(1)

        @pl.when(k == 0)
        def _():
            acc[...] = jnp.zeros_like(acc)

        def add(ref):
            acc[...] += _dot_tn(h_ref[...].astype(MXU), ref[...].astype(MXU))
        dp.apply(j, d_refs, add)

        @pl.when(k == nk - 1)
        def _():
            o_ref[...] = acc[...].astype(WIRE)

    return pl.pallas_call(
        body, name=name, grid=(nt, nk),
        in_specs=dp.specs(tm, lambda j, k: k, lambda j, k: j) + [
            pl.BlockSpec((tm, D), lambda j, k: (k, 0))],
        out_specs=pl.BlockSpec((None, D, tw), lambda j, k: (j, 0, 0)),
        out_shape=jax.ShapeDtypeStruct((nt, D, tw), WIRE),
        scratch_shapes=[pltpu.VMEM((D, tw), F32)],
        compiler_params=_cparams(("arbitrary", "arbitrary")),
    )(*dp.arrays, hn)


def _dw_transposed(dp, hn, name):
    t = hn.shape[0]
    tm = _tile(t, 544)
    nk = t // tm
    nt, tw = dp.n_tiles, dp.tw
    n_in = len(dp.arrays)

    def body(*refs):
        d_refs = refs[:n_in]
        h_ref, o_ref, acc = refs[n_in:]
        j, k = pl.program_id(0), pl.program_id(1)

        @pl.when(k == 0)
        def _():
            acc[...] = jnp.zeros_like(acc)

        def add(ref):
            acc[...] += _dot_tn(ref[...].astype(MXU), h_ref[...].astype(MXU))
        dp.apply(j, d_refs, add)

        @pl.when(k == nk - 1)
        def _():
            o_ref[...] = acc[...].astype(WIRE)

    return pl.pallas_call(
        body, name=name, grid=(nt, nk),
        in_specs=dp.specs(tm, lambda j, k: k, lambda j, k: j) + [
            pl.BlockSpec((tm, D), lambda j, k: (k, 0))],
        out_specs=pl.BlockSpec((tw, D), lambda j, k: (j, 0)),
        out_shape=jax.ShapeDtypeStruct((nt * tw, D), WIRE),
        scratch_shapes=[pltpu.VMEM((tw, D), F32)],
        compiler_params=_cparams(("arbitrary", "arbitrary")),
    )(*dp.arrays, hn)


def kernel(x, meta_tokens, ab_pre_norm, ab_w_in, ab_sinks, ab_conv_w, ab_conv_b, ab_conv_ln_g, ab_conv_ln_b, ab_w_pw2, ab_w_out, ab_post_norm, sb_pre_norm, sb_w_in, sb_w_out, sb_post_norm, loss_target, m_meta_tokens, m_ab_pre_norm, m_ab_w_in, m_ab_sinks, m_ab_conv_w, m_ab_conv_b, m_ab_conv_ln_g, m_ab_conv_ln_b, m_ab_w_pw2, m_ab_w_out, m_ab_post_norm, m_sb_pre_norm, m_sb_w_in, m_sb_w_out, m_sb_post_norm, v_meta_tokens, v_ab_pre_norm, v_ab_w_in, v_ab_sinks, v_ab_conv_w, v_ab_conv_b, v_ab_conv_ln_g, v_ab_conv_ln_b, v_ab_w_pw2, v_ab_w_out, v_ab_post_norm, v_sb_pre_norm, v_sb_w_in, v_sb_w_out, v_sb_post_norm):
    seq = x.shape[1]
    t = seq + BLK
    mx, my, mc = _coords()
    me = 4 * mx + 2 * my + mc
    pos = jnp.stack([mx, my, mc, me]).astype(jnp.int32)

    w_ab_t, w_sb, w_oa, w_os, w_pw = _all_gather(
        [ab_w_in[0].T.astype(WIRE), sb_w_in[0].astype(WIRE), ab_w_out[0].astype(WIRE),
         sb_w_out[0].astype(WIRE), ab_w_pw2[0].astype(WIRE)], "gather_weights")
    w_ab_t = w_ab_t.reshape(2816, D)
    w_oa = w_oa.reshape(D, D)
    w_os = w_os.reshape(D, D)
    w_pw = w_pw.reshape(512, 512)
    small = _all_gather([meta_tokens, ab_conv_w[0], sb_pre_norm, sb_post_norm], "gather_small")
    meta_full = jnp.moveaxis(small[0], 0, 1).reshape(N_META, D)
    conv_w = jnp.moveaxis(small[1], 0, 1).reshape(CONV_W, 512)
    sb_pre = jnp.moveaxis(small[2], 0, 1).reshape(1, D)
    sb_post = jnp.moveaxis(small[3], 0, 1).reshape(1, D)

    h0 = jnp.concatenate([jnp.zeros((PAD, D), F32), meta_full, x[0]], axis=0)
    tgt = jnp.concatenate([jnp.zeros((BLK, D), F32), loss_target[0]], axis=0)
    tables = _rope_tables(t)
    sinks = ab_sinks[0]

    p0, hn0 = _ab_in(h0, ab_pre_norm, w_ab_t, tables)
    att = _swa_fwd(p0, sinks)
    c1 = _conv_fwd(p0, conv_w, ab_conv_b, ab_conv_ln_g, ab_conv_ln_b)
    h1, y0, mix = _ab_out(h0, p0, att, c1, w_pw, w_oa, ab_post_norm)
    p1, hn1 = _sb_in(h1, sb_pre, w_sb)
    o, ltot = _sb_fwd(p1)
    loss_part, dh2, dy1, m1, do, dgate, dg_sb_post = _sb_out(o, p1, w_os, h1, sb_post, tgt)

    dq1, dk1, dv1 = _sb_bwd(p1, ltot, do)
    dp1 = _Cols([(dq1, 0, 2), (dk1, 2, 2), (dv1, 4, 2), (dgate, 6, 2)], 512)
    dh1, dy0, dga, dgb, datt, dc1, dc2, dg_sb_pre, dg_ab_post = _mid_bwd(
        dp1, w_sb, h1, sb_pre, dh2, y0, ab_post_norm, w_oa, p0, att, c1, w_pw)
    dglu, dconv_w, dconv_b, dln_g, dln_b = _conv_bwd(p0, dc1, conv_w, ab_conv_b, ab_conv_ln_g, ab_conv_ln_b)
    dq0, dkv0, dsinks = _swa_bwd(p0, datt, sinks, tables)
    dp0 = _Cols([(dq0, 0, 2), (dkv0, 2, 1), (dga, 3, 2), (dglu, 5, 4), (dgb, 9, 2)], 256)
    dh0, dg_ab_pre = _ab_in_bwd(dp0, w_ab_t, h0, ab_pre_norm, dh1)

    parts = [
        _dw_transposed(dp0, hn0, "dw_ab_in").reshape(4, 2, 352, D),
        _dw_chunks(hn1, dp1, "dw_sb_in").reshape(4, 2, D, 512),
        _dw_plain(mix, dy0, "dw_ab_out").reshape(4, 2, BLK, D),
        _dw_plain(m1, dy1, "dw_sb_out").reshape(4, 2, BLK, D),
        _dw_plain(c1, dc2, "dw_pw2").reshape(4, 2, 64, 512),
    ]
    names = ["ab_in", "sb_in", "ab_out", "sb_out", "pw2"]
    from_sibling = _exchange_sibling(parts, "reduce_sibling")
    chip_sums = [_add_sibling(pos, p, r, "add_sibling_" + nm) for p, r, nm in zip(parts, from_sibling, names)]
    from_chips = _exchange_chips(chip_sums, "reduce_chips")
    big = [_sum_chips(pos, s, r, "sum_chips_" + nm) for s, r, nm in zip(chip_sums, from_chips, names)]
    g_ab_w_in = big[0].T
    g_sb_w_in, g_ab_w_out, g_sb_w_out, g_ab_w_pw2 = big[1:]

    small_parts = [dh0[PAD:BLK], dg_ab_pre, dsinks, dconv_w, dconv_b, dln_g, dln_b,
                   dg_ab_post, dg_sb_pre, dg_sb_post]
    red = _reduce_small(_all_gather(small_parts, "gather_small_grads"), "reduce_small")
    col = lambda a, w: lax.dynamic_slice_in_dim(a, me * w, w, axis=1)
    g_meta = col(red[0], BLK)
    g_ab_pre = red[1]
    g_sinks = red[2][:, 0].reshape(1, 8)
    g_conv_w = col(red[3][:CONV_W], 64)
    g_conv_b, g_ln_g, g_ln_b, g_ab_post = red[4], red[5], red[6], red[7]
    g_sb_pre, g_sb_post = col(red[8], BLK), col(red[9], BLK)

    loss = lax.psum(loss_part[0, 0], ("x", "y", "c"))
    grad_x = dh0[BLK:][None]

    weights = [meta_tokens, ab_pre_norm, ab_w_in[0], ab_sinks, ab_conv_w[0], ab_conv_b, ab_conv_ln_g,
               ab_conv_ln_b, ab_w_pw2[0], ab_w_out[0], ab_post_norm, sb_pre_norm, sb_w_in[0],
               sb_w_out[0], sb_post_norm]
    grads = [g_meta, g_ab_pre, g_ab_w_in, g_sinks, g_conv_w, g_conv_b, g_ln_g, g_ln_b, g_ab_w_pw2,
             g_ab_w_out, g_ab_post, g_sb_pre, g_sb_w_in, g_sb_w_out, g_sb_post]
    ms = [m_meta_tokens, m_ab_pre_norm, m_ab_w_in[0], m_ab_sinks, m_ab_conv_w[0], m_ab_conv_b,
          m_ab_conv_ln_g, m_ab_conv_ln_b, m_ab_w_pw2[0], m_ab_w_out[0], m_ab_post_norm,
          m_sb_pre_norm, m_sb_w_in[0], m_sb_w_out[0], m_sb_post_norm]
    vs = [v_meta_tokens, v_ab_pre_norm, v_ab_w_in[0], v_ab_sinks, v_ab_conv_w[0], v_ab_conv_b,
          v_ab_conv_ln_g, v_ab_conv_ln_b, v_ab_w_pw2[0], v_ab_w_out[0], v_ab_post_norm,
          v_sb_pre_norm, v_sb_w_in[0], v_sb_w_out[0], v_sb_post_norm]
    lead = [w.ndim == 3 for w in (meta_tokens, ab_pre_norm, ab_w_in, ab_sinks, ab_conv_w, ab_conv_b,
                                   ab_conv_ln_g, ab_conv_ln_b, ab_w_pw2, ab_w_out, ab_post_norm,
                                   sb_pre_norm, sb_w_in, sb_w_out, sb_post_norm)]
    big_ids = [2, 8, 9, 12, 13]
    small_ids = [i for i in range(15) if i not in big_ids]
    deltas, new_m, new_v = [None] * 15, [None] * 15, [None] * 15
    for ids, nm in ((small_ids, "adamw_small"), (big_ids, "adamw_big")):
        d_, m_, v_ = _adamw([weights[i] for i in ids], [grads[i] for i in ids],
                            [ms[i] for i in ids], [vs[i] for i in ids], nm)
        for k, i in enumerate(ids):
            deltas[i], new_m[i], new_v[i] = d_[k], m_[k], v_[k]
    fix = lambda arrs: [a[None] if l else a for a, l in zip(arrs, lead)]
    return (loss, grad_x, *fix(grads), *fix(deltas), *fix(new_m), *fix(new_v))
```

```python
import functools

import numpy as np
import jax
import jax.numpy as jnp
from jax import lax
from jax.experimental import pallas as pl
from jax.experimental.pallas import tpu as pltpu

F32 = jnp.float32
MXU = jnp.bfloat16
ACT = jnp.bfloat16
WIRE = jnp.bfloat16

D = 1024
N_META = 16
BLK = 128
PAD = BLK - N_META
HEAD = 64
NEG = -1e30
EPS = 1e-6
LN_EPS = 1e-5
ROPE_THETA = 10000.0
SCALE = HEAD ** -0.5
CONV_W = 31
HALO = 32
LR, B1, B2, ADAM_EPS, WD, STEP = 0.001, 0.9, 0.999, 1e-08, 0.01, 10
VMEM_LIMIT = 56 * 1024 * 1024
MESH = pl.DeviceIdType.MESH

P0_SRC = (5, 6, 7, 8, 0, 1, 3, 4, 9, 10, 2)


def _cparams(sem=None):
    return pltpu.CompilerParams(dimension_semantics=sem, vmem_limit_bytes=VMEM_LIMIT)


def _tile(t, pref):
    for cand in (pref, 544, 272, 128):
        if cand <= pref and t % cand == 0:
            return cand
    raise ValueError(t)


def _sigmoid(x):
    return 1.0 / (1.0 + jnp.exp(-x))


def _silu_and_grad(x):
    s = _sigmoid(x)
    return x * s, s * (1.0 + x * (1.0 - s))


def _dot(a, b):
    return jnp.dot(a, b, preferred_element_type=F32)


def _dot_nt(a, b):
    return lax.dot_general(a, b, (((1,), (1,)), ((), ())), preferred_element_type=F32)


def _dot_tn(a, b):
    return lax.dot_general(a, b, (((0,), (0,)), ((), ())), preferred_element_type=F32)


def _rows(shape, base):
    return base + lax.broadcasted_iota(jnp.int32, shape, 0)


def _rope_tables(t):
    half = HEAD // 2
    inv = ROPE_THETA ** (-np.arange(half, dtype=np.float32) / half)
    pos = (np.arange(t) - PAD).astype(np.float32)
    ang = pos[:, None] * inv[None, :]
    lane = np.arange(BLK)
    cos = np.cos(ang)[:, lane % half].astype(np.float32)
    sin = np.sin(ang)[:, lane % half].astype(np.float32)
    first = (lane % HEAD) < half
    sin_a = np.where(first[None, :], -sin, 0.0).astype(np.float32)
    sin_b = np.where(first[None, :], 0.0, sin).astype(np.float32)
    return jnp.asarray(cos), jnp.asarray(sin_a), jnp.asarray(sin_b)


def _rope(v, cos, sin_a, sin_b):
    return v * cos + pltpu.roll(v, 96, 1) * sin_a + pltpu.roll(v, 32, 1) * sin_b


def _unrope(v, cos, sin_a, sin_b):
    return v * cos - pltpu.roll(v, 96, 1) * sin_a - pltpu.roll(v, 32, 1) * sin_b


def _coords():
    return lax.axis_index("x"), lax.axis_index("y"), lax.axis_index("c")


def _all_gather(arrs, name):
    n = len(arrs)

    def body(*refs):
        ins, outs = refs[:n], refs[n:2 * n]
        send_sems, recv_sems, local_sems = refs[2 * n:]
        x, y, c = _coords()
        me, sibling = (x, y, c), (x, y, 1 - c)
        chips = [(1 - x, y), (x, 1 - y), (1 - x, 1 - y)]

        def copy(a, k, block, to, src=None):
            dst = outs[a].at[4 * block[0] + 2 * block[1] + block[2]]
            return pltpu.make_async_remote_copy(
                src_ref=dst if src is None else src, dst_ref=dst,
                send_sem=send_sems.at[a, k], recv_sem=recv_sems.at[a, k],
                device_id=to, device_id_type=MESH)

        mine = [pltpu.make_async_copy(ins[a], outs[a].at[4 * x + 2 * y + c], local_sems.at[a])
                for a in range(n)]
        for cp in mine:
            cp.start()
        first = []
        for a in range(n):
            first.append(copy(a, 0, me, sibling, src=ins[a]))
            for j, chip in enumerate(chips):
                first.append(copy(a, 1 + j, me, (*chip, c), src=ins[a]))
        for cp in first:
            cp.start()
        passed = []
        for j, chip in enumerate(chips):
            for a in range(n):
                copy(a, 1 + j, (*chip, c), me).wait_recv()
                cp = copy(a, 4 + j, (*chip, c), sibling)
                cp.start()
                passed.append(cp)
        for a in range(n):
            copy(a, 0, sibling, me).wait_recv()
            for j, chip in enumerate(chips):
                copy(a, 4 + j, (*chip, 1 - c), me).wait_recv()
        for cp in first + passed:
            cp.wait_send()
        for cp in mine:
            cp.wait()

    any_spec = pl.BlockSpec(memory_space=pl.ANY)
    return pl.pallas_call(
        body, name=name,
        out_shape=[jax.ShapeDtypeStruct((8,) + a.shape, a.dtype) for a in arrs],
        in_specs=[any_spec] * n, out_specs=[any_spec] * n,
        scratch_shapes=[pltpu.SemaphoreType.DMA((n, 7)), pltpu.SemaphoreType.DMA((n, 7)),
                        pltpu.SemaphoreType.DMA((n,))],
    )(*arrs)


def _exchange_sibling(parts, name):
    n = len(parts)

    def body(*refs):
        ins, outs = refs[:n], refs[n:2 * n]
        send_sems, recv_sems = refs[2 * n:]
        x, y, c = _coords()
        copies = [pltpu.make_async_remote_copy(
            src_ref=ins[a].at[:, 1 - c], dst_ref=outs[a],
            send_sem=send_sems.at[a], recv_sem=recv_sems.at[a],
            device_id=(x, y, 1 - c), device_id_type=MESH) for a in range(n)]
        for cp in copies:
            cp.start()
        for cp in copies:
            cp.wait()

    any_spec = pl.BlockSpec(memory_space=pl.ANY)
    return pl.pallas_call(
        body, name=name,
        out_shape=[jax.ShapeDtypeStruct((4,) + p.shape[2:], p.dtype) for p in parts],
        in_specs=[any_spec] * n, out_specs=[any_spec] * n,
        scratch_shapes=[pltpu.SemaphoreType.DMA((n,)), pltpu.SemaphoreType.DMA((n,))],
    )(*parts)


def _exchange_chips(sums, name):
    n = len(sums)

    def body(*refs):
        ins, outs = refs[:n], refs[n:2 * n]
        send_sems, recv_sems = refs[2 * n:]
        x, y, c = _coords()
        chips = [(1 - x, y), (x, 1 - y), (1 - x, 1 - y)]
        copies = []
        for a in range(n):
            for k, chip in enumerate(chips):
                copies.append(pltpu.make_async_remote_copy(
                    src_ref=ins[a].at[2 * chip[0] + chip[1]], dst_ref=outs[a].at[k],
                    send_sem=send_sems.at[a, k], recv_sem=recv_sems.at[a, k],
                    device_id=(*chip, c), device_id_type=MESH))
        for cp in copies:
            cp.start()
        for cp in copies:
            cp.wait()

    any_spec = pl.BlockSpec(memory_space=pl.ANY)
    return pl.pallas_call(
        body, name=name,
        out_shape=[jax.ShapeDtypeStruct((3,) + s.shape[1:], s.dtype) for s in sums],
        in_specs=[any_spec] * n, out_specs=[any_spec] * n,
        scratch_shapes=[pltpu.SemaphoreType.DMA((n, 3)), pltpu.SemaphoreType.DMA((n, 3))],
    )(*sums)


def _add_sibling(pos, part, recv, name):
    _, _, r, c = part.shape

    def body(pos_ref, p_ref, r_ref, o_ref):
        o_ref[...] = (p_ref[...].astype(F32) + r_ref[...].astype(F32)).astype(o_ref.dtype)

    return pl.pallas_call(
        body, name=name,
        grid_spec=pltpu.PrefetchScalarGridSpec(
            num_scalar_prefetch=1, grid=(4,),
            in_specs=[pl.BlockSpec((None, None, r, c), lambda q, pos: (q, pos[2], 0, 0)),
                      pl.BlockSpec((None, r, c), lambda q, pos: (q, 0, 0))],
            out_specs=pl.BlockSpec((None, r, c), lambda q, pos: (q, 0, 0))),
        out_shape=jax.ShapeDtypeStruct((4, r, c), part.dtype),
        compiler_params=_cparams(("arbitrary",)),
    )(pos, part, recv)


def _sum_chips(pos, sums, recv, name):
    _, r, c = sums.shape

    def body(pos_ref, s_ref, r_ref, o_ref):
        g = s_ref[...].astype(F32)
        for k in range(3):
            g = g + r_ref[k].astype(F32)
        o_ref[...] = g

    return pl.pallas_call(
        body, name=name,
        grid_spec=pltpu.PrefetchScalarGridSpec(
            num_scalar_prefetch=1, grid=(1,),
            in_specs=[pl.BlockSpec((None, r, c), lambda i, pos: (2 * pos[0] + pos[1], 0, 0)),
                      pl.BlockSpec((3, r, c), lambda i, pos: (0, 0, 0))],
            out_specs=pl.BlockSpec((r, c), lambda i, pos: (0, 0))),
        out_shape=jax.ShapeDtypeStruct((r, c), F32),
        compiler_params=_cparams(("arbitrary",)),
    )(pos, sums, recv)


def _adamw(ws, gs, ms, vs, name):
    n = len(ws)
    c1 = 1.0 / (1.0 - B1 ** STEP)
    c2 = 1.0 / (1.0 - B2 ** STEP)

    def body(*refs):
        w_r, g_r, m_r, v_r = refs[:n], refs[n:2 * n], refs[2 * n:3 * n], refs[3 * n:4 * n]
        d_o, m_o, v_o = refs[4 * n:5 * n], refs[5 * n:6 * n], refs[6 * n:7 * n]
        for a in range(n):
            g = g_r[a][...]
            m = B1 * m_r[a][...] + (1.0 - B1) * g
            v = B2 * v_r[a][...] + (1.0 - B2) * (g * g)
            d_o[a][...] = -LR * ((m * c1) / (jnp.sqrt(v * c2) + ADAM_EPS) + WD * w_r[a][...])
            m_o[a][...] = m
            v_o[a][...] = v

    shapes = [jax.ShapeDtypeStruct(w.shape, F32) for w in ws]
    outs = pl.pallas_call(body, name=name, out_shape=shapes * 3,
                          compiler_params=_cparams())(*ws, *gs, *ms, *vs)
    return outs[:n], outs[n:2 * n], outs[2 * n:]


def _reduce_small(gathered, name):
    n = len(gathered)

    def body(*refs):
        for a in range(n):
            acc = refs[a][0]
            for k in range(1, 8):
                acc = acc + refs[a][k]
            refs[n + a][...] = acc

    return pl.pallas_call(
        body, name=name,
        out_shape=[jax.ShapeDtypeStruct(g.shape[1:], F32) for g in gathered],
        compiler_params=_cparams())(*gathered)


class _Cols:
    def __init__(self, pieces, tw):
        self.pieces, self.tw = pieces, tw
        self.arrays = [p[0] for p in pieces]
        self.n_tiles = sum(p[2] for p in pieces)

    def specs(self, tm, row_of, tile_of):
        out = []
        for _, first, cnt in self.pieces:
            def imap(*g, first=first, cnt=cnt):
                return (row_of(*g), jnp.clip(tile_of(*g) - first, 0, cnt - 1))
            out.append(pl.BlockSpec((tm, self.tw), imap))
        return out

    def apply(self, t, refs, fn):
        for ref, (_, first, cnt) in zip(refs, self.pieces):
            pl.when((t >= first) & (t < first + cnt))(functools.partial(fn, ref))


def _ab_in(h, g, w_t, tables):
    t = h.shape[0]
    tm = _tile(t, 544)
    src = jnp.asarray(np.array(P0_SRC, np.int32))

    def body(src_ref, h_ref, g_ref, w_ref, cos_ref, sa_ref, sb_ref, o_ref, hn_ref, hn_s):
        j = pl.program_id(1)

        @pl.when(j == 0)
        def _():
            x = h_ref[...]
            hn = (x * lax.rsqrt(jnp.mean(x * x, -1, keepdims=True) + EPS) * g_ref[...]).astype(MXU)
            hn_s[...] = hn
            hn_ref[...] = hn.astype(ACT)

        acc = _dot_nt(hn_s[...], w_ref[...])
        rope = lambda v: _rope(v, cos_ref[...], sa_ref[...], sb_ref[...])

        @pl.when((j == 4) | (j == 5))
        def _():
            o_ref[:, :BLK] = rope(acc[:, :BLK])
            o_ref[:, BLK:] = rope(acc[:, BLK:])

        @pl.when(j == 10)
        def _():
            o_ref[:, :BLK] = rope(acc[:, :BLK])
            o_ref[:, BLK:] = acc[:, BLK:]

        @pl.when((j < 4) | ((j > 5) & (j < 10)))
        def _():
            o_ref[...] = acc

    tab = pl.BlockSpec((tm, BLK), lambda i, j, s: (i, 0))
    return pl.pallas_call(
        body, name="ab_in",
        grid_spec=pltpu.PrefetchScalarGridSpec(
            num_scalar_prefetch=1, grid=(t // tm, 11),
            in_specs=[pl.BlockSpec((tm, D), lambda i, j, s: (i, 0)),
                      pl.BlockSpec((1, D), lambda i, j, s: (0, 0)),
                      pl.BlockSpec((256, D), lambda i, j, s: (s[j], 0)),
                      tab, tab, tab],
            out_specs=[pl.BlockSpec((tm, 256), lambda i, j, s: (i, j)),
                       pl.BlockSpec((tm, D), lambda i, j, s: (i, 0))],
            scratch_shapes=[pltpu.VMEM((tm, D), MXU)]),
        out_shape=[jax.ShapeDtypeStruct((t, 2816), F32), jax.ShapeDtypeStruct((t, D), ACT)],
        compiler_params=_cparams(("arbitrary", "arbitrary")),
    )(src, h, g, w_t, *tables)


def _swa_mask(n):
    r = lax.broadcasted_iota(jnp.int32, (BLK, 3 * BLK), 0)
    c = lax.broadcasted_iota(jnp.int32, (BLK, 3 * BLK), 1)
    qpos = n * BLK + r
    bpos = (n - 2) * BLK + c
    meta_ok = (c >= PAD) & (c < BLK) & (qpos - c >= BLK)
    band_ok = (c >= BLK) & (bpos >= PAD) & (qpos >= bpos) & (qpos - bpos < BLK)
    return meta_ok | band_ok


def _swa_keys(kv_ref, n):
    def blk(b):
        return kv_ref[pl.ds(pl.multiple_of(b * BLK, BLK), BLK), :]
    kv = jnp.concatenate([kv_ref[0:BLK, :], blk(jnp.maximum(n - 1, 0)), blk(n)], axis=0)
    lo = lax.broadcasted_iota(jnp.int32, (1, BLK), 1) < HEAD
    out = []
    for part in (kv[:, :BLK], kv[:, BLK:]):
        rolled = pltpu.roll(part, HEAD, 1)
        out.append((jnp.where(lo, part, rolled).astype(MXU), jnp.where(lo, rolled, part).astype(MXU)))
    return out[0], out[1], lo


def _swa_probs(qm, kd, mask, sink):
    s = jnp.where(mask, _dot_nt(qm, kd) * SCALE, NEG)
    m = jnp.maximum(jnp.max(s, -1, keepdims=True), sink)
    e = jnp.exp(s - m)
    inv = 1.0 / (jnp.sum(e, -1, keepdims=True) + jnp.exp(sink - m))
    return e * inv, jnp.exp(sink - m) * inv


def _swa_fwd(p0, sinks):
    t = p0.shape[0]

    def body(sink_ref, q_ref, kv_ref, o_ref):
        n = pl.program_id(0)
        kd, vd, lo = _swa_keys(kv_ref, n)
        mask = _swa_mask(n)
        for p in range(4):
            qp = q_ref[:, p * BLK:(p + 1) * BLK]
            outs = []
            for j in range(2):
                qm = jnp.where(lo if j == 0 else ~lo, qp, 0.0).astype(MXU)
                pr, _ = _swa_probs(qm, kd[p // 2], mask, sink_ref[2 * p + j])
                outs.append(_dot(pr.astype(MXU), vd[p // 2]))
            o_ref[:, p * BLK:(p + 1) * BLK] = jnp.where(lo, outs[0], outs[1])

    return pl.pallas_call(
        body, name="swa_fwd", grid=(t // BLK,),
        in_specs=[pl.BlockSpec(memory_space=pltpu.SMEM),
                  pl.BlockSpec((BLK, 512), lambda n: (n, 2)),
                  pl.BlockSpec((t, 256), lambda n: (0, 10))],
        out_specs=pl.BlockSpec((BLK, 512), lambda n: (n, 0)),
        out_shape=jax.ShapeDtypeStruct((t, 512), F32),
        compiler_params=_cparams(("arbitrary",)),
    )(sinks, p0, p0)


def _conv_window(u_w, w_ref, n_out, first):
    rows = u_w.shape[0]
    acc = None
    for j in range(CONV_W):
        shifted = pltpu.roll(u_w, (rows - (first + j)) % rows, 0)[:n_out]
        term = shifted * w_ref[j:j + 1, :]
        acc = term if acc is None else acc + term
    return acc


def _conv_fwd(p0, conv_w, conv_b, ln_g, ln_b):
    t = p0.shape[0]
    tm = _tile(t, 544)
    hb = tm // HALO

    def body(cur_ref, prev_ref, w_ref, b_ref, g_ref, bb_ref, o_ref):
        i = pl.program_id(0)
        glu = jnp.concatenate([prev_ref[...], cur_ref[...]], axis=0)
        rw = _rows((tm + HALO, 1), i * tm - HALO)
        u_w = jnp.where(rw >= PAD, glu[:, :512] * _sigmoid(glu[:, 512:]), 0.0)
        cv = _conv_window(u_w, w_ref, tm, HALO - (CONV_W - 1)) + b_ref[...]
        xc = cv - jnp.mean(cv, -1, keepdims=True)
        ln = xc * lax.rsqrt(jnp.mean(xc * xc, -1, keepdims=True) + LN_EPS) * g_ref[...] + bb_ref[...]
        o_ref[...] = (ln * _sigmoid(ln)).astype(ACT)

    vec = pl.BlockSpec((1, 512), lambda i: (0, 0))
    return pl.pallas_call(
        body, name="conv_fwd", grid=(t // tm,),
        in_specs=[pl.BlockSpec((tm, D), lambda i: (i, 0)),
                  pl.BlockSpec((HALO, D), lambda i: (jnp.maximum(i * hb - 1, 0), 0)),
                  pl.BlockSpec((CONV_W, 512), lambda i: (0, 0)), vec, vec, vec],
        out_specs=pl.BlockSpec((tm, 512), lambda i: (i, 0)),
        out_shape=jax.ShapeDtypeStruct((t, 512), ACT),
        compiler_params=_cparams(("arbitrary",)),
    )(p0, p0, conv_w, conv_b, ln_g, ln_b)


def _ab_out(h, p0, att, c1, w_pw2, w_out, g_post):
    t = h.shape[0]
    tm = _tile(t, 272)

    def body(h_ref, ga_ref, gb_ref, att_ref, c1_ref, pw_ref, wo_ref, g_ref, h1_ref, y_ref, mix_ref):
        i = pl.program_id(0)
        sga, _ = _silu_and_grad(ga_ref[...])
        sgb, _ = _silu_and_grad(gb_ref[...])
        a = att_ref[...] * sga
        c = _dot(c1_ref[...].astype(MXU), pw_ref[...]) * sgb
        mix = jnp.concatenate([a, c], axis=1).astype(MXU)
        y = _dot(mix, wo_ref[...])
        yn = y * lax.rsqrt(jnp.mean(y * y, -1, keepdims=True) + EPS) * g_ref[...]
        h1_ref[...] = jnp.where(_rows((tm, 1), i * tm) >= PAD, h_ref[...] + yn, 0.0)
        y_ref[...] = y
        mix_ref[...] = mix.astype(ACT)

    row = lambda w, idx: pl.BlockSpec((tm, w), lambda i: (i, idx))
    full = lambda a: pl.BlockSpec(a.shape, lambda i: (0, 0))
    return pl.pallas_call(
        body, name="ab_out", grid=(t // tm,),
        in_specs=[row(D, 0), row(512, 3), row(512, 4), row(512, 0), row(512, 0),
                  full(w_pw2), full(w_out), full(g_post)],
        out_specs=[row(D, 0), row(D, 0), row(D, 0)],
        out_shape=[jax.ShapeDtypeStruct((t, D), F32), jax.ShapeDtypeStruct((t, D), F32),
                   jax.ShapeDtypeStruct((t, D), ACT)],
        compiler_params=_cparams(("arbitrary",)),
    )(h, p0, p0, att, c1, w_pw2, w_out, g_post)


def _sb_in(h, g, w):
    t = h.shape[0]
    tm = _tile(t, 544)

    def body(h_ref, g_ref, w_ref, o_ref, hn_ref, hn_s):
        @pl.when(pl.program_id(1) == 0)
        def _():
            x = h_ref[...]
            hn = (x * lax.rsqrt(jnp.mean(x * x, -1, keepdims=True) + EPS) * g_ref[...]).astype(MXU)
            hn_s[...] = hn
            hn_ref[...] = hn.astype(ACT)

        o_ref[...] = _dot(hn_s[...], w_ref[...])

    return pl.pallas_call(
        body, name="sb_in", grid=(t // tm, 8),
        in_specs=[pl.BlockSpec((tm, D), lambda i, j: (i, 0)),
                  pl.BlockSpec((1, D), lambda i, j: (0, 0)),
                  pl.BlockSpec((None, D, 512), lambda i, j: (j, 0, 0))],
        out_specs=[pl.BlockSpec((tm, 512), lambda i, j: (i, j)),
                   pl.BlockSpec((tm, D), lambda i, j: (i, 0))],
        out_shape=[jax.ShapeDtypeStruct((t, 4096), F32), jax.ShapeDtypeStruct((t, D), ACT)],
        scratch_shapes=[pltpu.VMEM((tm, D), MXU)],
        compiler_params=_cparams(("arbitrary", "arbitrary")),
    )(h, g, w)


def _split_hi_lo(x):
    hi = x.astype(MXU)
    lo = (x - hi.astype(F32)).astype(MXU)
    return hi, lo


def _scan_matrix(suffix):
    j = lax.broadcasted_iota(jnp.int32, (BLK, 2 * BLK), 0)
    s = lax.broadcasted_iota(jnp.int32, (BLK, 2 * BLK), 1)
    keep = (s >= BLK) | ((j > s) if suffix else (j < s))
    return jnp.where(keep, 1.0, 0.0).astype(MXU)


def _scan_sums(x, mat):
    hi, lo = _split_hi_lo(x)
    both = _dot(hi, mat) + _dot(lo, mat)
    return both[:, :BLK], both[:, BLK:]


KC = 4
CHUNK = KC * BLK
SLACK = CHUNK - BLK


def _sb_logits(qm, kc, valid):
    z = _dot_nt(qm, kc)
    log_beta = jnp.minimum(z, 0.0) - jnp.log(1.0 + jnp.exp(-jnp.abs(z)))
    return log_beta, jnp.where(valid, log_beta - z, 0.0)


def _sb_valid(i, first_key):
    r = lax.broadcasted_iota(jnp.int32, (BLK, CHUNK), 0)
    c = lax.broadcasted_iota(jnp.int32, (BLK, CHUNK), 1)
    kpos = first_key + c
    return (kpos >= PAD) & (kpos < i * BLK + r)


def _sb_fwd(p1):
    t = p1.shape[0]

    def body(q_ref, k_ref, v_ref, o_ref, lt_ref, k_s, v_s):
        i = pl.program_id(1)

        @pl.when(i == 0)
        def _():
            for src, dst in ((k_ref, k_s), (v_ref, v_s)):
                dst[0:SLACK, :] = jnp.zeros((SLACK, BLK), MXU)
                dst[SLACK:, :] = src[...].astype(MXU)

        lo = lax.broadcasted_iota(jnp.int32, (1, BLK), 1) < HEAD
        q = q_ref[...] * SCALE
        qm = [jnp.where(lo, q, 0.0).astype(MXU), jnp.where(lo, 0.0, q).astype(MXU)]
        mat = _scan_matrix(True)

        def step(s, carry):
            start = pl.multiple_of((i - KC * s) * BLK, BLK)
            kc, vc = k_s[pl.ds(start, CHUNK), :], v_s[pl.ds(start, CHUNK), :]
            valid = _sb_valid(i, start - SLACK)
            new = []
            for j in range(2):
                run, acc = carry[2 * j], carry[2 * j + 1]
                log_beta, log_1m = _sb_logits(qm[j], kc, valid)
                parts = [None] * KC
                for b in reversed(range(KC)):
                    after, total = _scan_sums(log_1m[:, b * BLK:(b + 1) * BLK], mat)
                    parts[b] = after + run
                    run = run + total
                a = jnp.where(valid, jnp.exp(log_beta + jnp.concatenate(parts, axis=1)), 0.0)
                new += [run, acc + _dot(a.astype(MXU), vc)]
            return tuple(new)

        zero = jnp.zeros((BLK, BLK), F32)
        res = lax.fori_loop(0, (i + KC) // KC, step, (zero, zero, zero, zero))
        o_ref[...] = jnp.where(lo, res[1], res[3])
        lt_ref[...] = jnp.where(lo, res[0], res[2])

    blk = pl.BlockSpec((BLK, BLK), lambda hp, i: (i, hp))
    return pl.pallas_call(
        body, name="sb_fwd", grid=(8, t // BLK),
        in_specs=[blk,
                  pl.BlockSpec((t, BLK), lambda hp, i: (0, 8 + hp)),
                  pl.BlockSpec((t, BLK), lambda hp, i: (0, 16 + hp))],
        out_specs=[blk, blk],
        out_shape=[jax.ShapeDtypeStruct((t, D), F32)] * 2,
        scratch_shapes=[pltpu.VMEM((t + SLACK, BLK), MXU), pltpu.VMEM((t + SLACK, BLK), MXU)],
        compiler_params=_cparams(("arbitrary", "arbitrary")),
    )(p1, p1, p1)


def _sb_out(o, p1, w_out, h1, g_post, tgt):
    t = o.shape[0]
    tm = _tile(t, 272)

    def body(o_ref, g_ref, w_ref, h_ref, gp_ref, t_ref,
             loss_ref, dh_ref, dy_ref, m_ref, do_ref, dg_ref, dgp_ref):
        i = pl.program_id(0)

        @pl.when(i == 0)
        def _():
            loss_ref[...] = jnp.zeros_like(loss_ref)
            dgp_ref[...] = jnp.zeros_like(dgp_ref)

        gate = g_ref[...]
        sg, dsg = _silu_and_grad(gate)
        ov = o_ref[...]
        m = (ov * sg).astype(MXU)
        y = _dot(m, w_ref[...])
        r = lax.rsqrt(jnp.mean(y * y, -1, keepdims=True) + EPS)
        yhat = y * r
        h2 = h_ref[...] + yhat * gp_ref[...]
        diff = jnp.where(_rows((tm, 1), i * tm) >= BLK, h2 - t_ref[...], 0.0)
        loss_ref[...] += jnp.full(loss_ref.shape, 0.5 / D, F32) * jnp.sum(diff * diff)
        dh = diff * (1.0 / D)
        dgp_ref[...] += jnp.sum(dh * yhat, 0, keepdims=True)
        dyn = dh * gp_ref[...]
        dy = (r * (dyn - yhat * jnp.mean(dyn * yhat, -1, keepdims=True))).astype(MXU)
        dm = _dot_nt(dy, w_ref[...])
        dh_ref[...] = dh
        dy_ref[...] = dy.astype(ACT)
        m_ref[...] = m.astype(ACT)
        do_ref[...] = dm * sg
        dg_ref[...] = (dm * ov * dsg).astype(ACT)

    row = lambda idx: pl.BlockSpec((tm, D), lambda i: (i, idx))
    full = lambda a: pl.BlockSpec(a.shape, lambda i: (0, 0))
    acc = lambda s: pl.BlockSpec(s, lambda i: (0, 0))
    return pl.pallas_call(
        body, name="sb_out", grid=(t // tm,),
        in_specs=[row(0), row(3), full(w_out), row(0), full(g_post), row(0)],
        out_specs=[acc((8, BLK)), row(0), row(0), row(0), row(0), row(0), acc((1, D))],
        out_shape=[jax.ShapeDtypeStruct((8, BLK), F32), jax.ShapeDtypeStruct((t, D), F32),
                   jax.ShapeDtypeStruct((t, D), ACT), jax.ShapeDtypeStruct((t, D), ACT),
                   jax.ShapeDtypeStruct((t, D), F32), jax.ShapeDtypeStruct((t, D), ACT),
                   jax.ShapeDtypeStruct((1, D), F32)],
        compiler_params=_cparams(("arbitrary",)),
    )(o, p1, w_out, h1, g_post, tgt)


def _sb_bwd(p1, ltot, do):
    t = p1.shape[0]
    nb = t // BLK

    def body(q_ref, k_ref, v_ref, lt_ref, do_ref, dq_ref, dk_ref, dv_ref, k_s, v_s, dk_s, dv_s):
        i = pl.program_id(1)
        lo = lax.broadcasted_iota(jnp.int32, (1, BLK), 1) < HEAD

        @pl.when(i == 0)
        def _():
            for src, dst in ((k_ref, k_s), (v_ref, v_s)):
                dst[0:t, :] = src[...].astype(MXU)
                dst[t:, :] = jnp.zeros((SLACK, BLK), MXU)
            dk_s[...] = jnp.zeros_like(dk_s)
            dv_s[...] = jnp.zeros_like(dv_s)

        q, dout, lt = q_ref[...] * SCALE, do_ref[...], lt_ref[...]
        halves = [lo, ~lo]
        qm = [jnp.where(h, q, 0.0).astype(MXU) for h in halves]
        dom = [jnp.where(h, dout, 0.0).astype(MXU) for h in halves]
        lt_r = pltpu.roll(lt, HEAD, 1)
        row_total = [jnp.where(lo, lt, lt_r), jnp.where(lo, lt_r, lt)]
        mat_l = _scan_matrix(True)
        mat_g = _scan_matrix(False)

        def step(s, carry):
            start = pl.multiple_of(s * CHUNK, BLK)
            kc, vc = k_s[pl.ds(start, CHUNK), :], v_s[pl.ds(start, CHUNK), :]
            valid = _sb_valid(i, start)
            new = []
            dk_c = jnp.zeros((CHUNK, BLK), F32)
            dv_c = jnp.zeros((CHUNK, BLK), F32)
            for j in range(2):
                run, run_g, dq = carry[3 * j], carry[3 * j + 1], carry[3 * j + 2]
                log_beta, log_1m = _sb_logits(qm[j], kc, valid)
                parts = []
                for b in range(KC):
                    after, total = _scan_sums(log_1m[:, b * BLK:(b + 1) * BLK], mat_l)
                    run = run + total
                    parts.append(after + (row_total[j] - run))
                a = jnp.where(valid, jnp.exp(log_beta + jnp.concatenate(parts, axis=1)), 0.0)
                g = _dot_nt(dom[j], vc) * a
                parts = []
                for b in range(KC):
                    before, total_g = _scan_sums(g[:, b * BLK:(b + 1) * BLK], mat_g)
                    parts.append(before + run_g)
                    run_g = run_g + total_g
                sig = jnp.exp(log_beta)
                dz = jnp.where(valid, g * (1.0 - sig) - sig * jnp.concatenate(parts, axis=1), 0.0)
                dzm = dz.astype(MXU)
                dk_c = dk_c + _dot_tn(dzm, qm[j])
                dv_c = dv_c + _dot_tn(a.astype(MXU), dom[j])
                new += [run, run_g, dq + _dot(dzm, kc)]
            dk_s[pl.ds(start, CHUNK), :] += dk_c
            dv_s[pl.ds(start, CHUNK), :] += dv_c
            return tuple(new)

        zero = jnp.zeros((BLK, BLK), F32)
        res = lax.fori_loop(0, (i + KC) // KC, step, (zero,) * 6)
        dq_ref[...] = (jnp.where(lo, res[2], res[5]) * SCALE).astype(ACT)

        @pl.when(i == nb - 1)
        def _():
            dk_ref[...] = dk_s[0:t, :].astype(ACT)
            dv_ref[...] = dv_s[0:t, :].astype(ACT)

    blk = lambda off: pl.BlockSpec((BLK, BLK), lambda hp, i: (i, off + hp))
    col = lambda off: pl.BlockSpec((t, BLK), lambda hp, i: (0, off + hp))
    return pl.pallas_call(
        body, name="sb_bwd", grid=(8, nb),
        in_specs=[blk(0), col(8), col(16), blk(0), blk(0)],
        out_specs=[blk(0), col(0), col(0)],
        out_shape=[jax.ShapeDtypeStruct((t, D), ACT)] * 3,
        scratch_shapes=[pltpu.VMEM((t + SLACK, BLK), MXU), pltpu.VMEM((t + SLACK, BLK), MXU),
                        pltpu.VMEM((t + SLACK, BLK), F32), pltpu.VMEM((t + SLACK, BLK), F32)],
        compiler_params=_cparams(("arbitrary", "arbitrary")),
    )(p1, p1, p1, ltot, do)


def _mid_bwd(dp1, w_sb, h1, g_pre1, dh2, y0, g_post0, w_out, p0, att, c1, w_pw2):
    t = h1.shape[0]
    tm = _tile(t, 272)

    def body(*refs):
        d_refs = refs[:4]
        (w_ref, h_ref, g1_ref, dh2_ref, y_ref, g0_ref, wo_ref, ga_ref, gb_ref, att_ref, c1_ref,
         pw_ref, dh1_ref, dy_ref, dga_ref, dgb_ref, datt_ref, dc1_ref, dc2_ref, dg1_ref, dg0_ref,
         acc) = refs[4:]
        i, j = pl.program_id(0), pl.program_id(1)

        @pl.when((i == 0) & (j == 0))
        def _():
            dg1_ref[...] = jnp.zeros_like(dg1_ref)
            dg0_ref[...] = jnp.zeros_like(dg0_ref)

        @pl.when(j == 0)
        def _():
            acc[...] = jnp.zeros_like(acc)

        def add(ref):
            acc[...] += _dot_nt(ref[...].astype(MXU), w_ref[...])
        dp1.apply(j, d_refs, add)

        @pl.when(j == 7)
        def _():
            dhn = acc[...]
            x = h_ref[...]
            r = lax.rsqrt(jnp.mean(x * x, -1, keepdims=True) + EPS)
            xhat = x * r
            dg1_ref[...] += jnp.sum(dhn * xhat, 0, keepdims=True)
            dxn = dhn * g1_ref[...]
            dh1 = dh2_ref[...] + r * (dxn - xhat * jnp.mean(dxn * xhat, -1, keepdims=True))
            dh1_ref[...] = dh1
            y = y_ref[...]
            ry = lax.rsqrt(jnp.mean(y * y, -1, keepdims=True) + EPS)
            yhat = y * ry
            dg0_ref[...] += jnp.sum(dh1 * yhat, 0, keepdims=True)
            dyn = dh1 * g0_ref[...]
            dy = (ry * (dyn - yhat * jnp.mean(dyn * yhat, -1, keepdims=True))).astype(MXU)
            dy_ref[...] = dy.astype(ACT)
            dmix = _dot_nt(dy, wo_ref[...])
            da, dc = dmix[:, :512], dmix[:, 512:]
            sga, dsga = _silu_and_grad(ga_ref[...])
            sgb, dsgb = _silu_and_grad(gb_ref[...])
            datt_ref[...] = da * sga
            dga_ref[...] = (da * att_ref[...] * dsga).astype(ACT)
            c2 = _dot(c1_ref[...].astype(MXU), pw_ref[...])
            dc2 = (dc * sgb).astype(MXU)
            dgb_ref[...] = (dc * c2 * dsgb).astype(ACT)
            dc2_ref[...] = dc2.astype(ACT)
            dc1_ref[...] = _dot_nt(dc2, pw_ref[...])

    row = lambda w, idx: pl.BlockSpec((tm, w), lambda i, j: (i, idx))
    full = lambda a: pl.BlockSpec(a.shape, lambda i, j: (0, 0))
    acc_spec = pl.BlockSpec((1, D), lambda i, j: (0, 0))
    sd = jax.ShapeDtypeStruct
    return pl.pallas_call(
        body, name="mid_bwd", grid=(t // tm, 8),
        in_specs=dp1.specs(tm, lambda i, j: i, lambda i, j: j) + [
            pl.BlockSpec((None, D, 512), lambda i, j: (j, 0, 0)),
            row(D, 0), full(g_pre1), row(D, 0), row(D, 0), full(g_post0), full(w_out),
            row(512, 3), row(512, 4), row(512, 0), row(512, 0), full(w_pw2)],
        out_specs=[row(D, 0), row(D, 0), row(512, 0), row(512, 0), row(512, 0), row(512, 0),
                   row(512, 0), acc_spec, acc_spec],
        out_shape=[sd((t, D), F32), sd((t, D), ACT), sd((t, 512), ACT), sd((t, 512), ACT),
                   sd((t, 512), F32), sd((t, 512), F32), sd((t, 512), ACT),
                   sd((1, D), F32), sd((1, D), F32)],
        scratch_shapes=[pltpu.VMEM((tm, D), F32)],
        compiler_params=_cparams(("arbitrary", "arbitrary")),
    )(*dp1.arrays, w_sb, h1, g_pre1, dh2, y0, g_post0, w_out, p0, p0, att, c1, w_pw2)


def _conv_bwd(p0, dc1, conv_w, conv_b, ln_g, ln_b):
    t = p0.shape[0]
    tm = _tile(t, 544)
    hb = tm // HALO
    last = t // HALO - 1

    def body(cur_ref, prev_ref, next_ref, d_ref, dn_ref, w_ref, b_ref, g_ref, bb_ref,
             dglu_ref, dw_ref, db_ref, dlg_ref, dlb_ref):
        i = pl.program_id(0)

        @pl.when(i == 0)
        def _():
            for ref in (dw_ref, db_ref, dlg_ref, dlb_ref):
                ref[...] = jnp.zeros_like(ref)

        glu = jnp.concatenate([prev_ref[...], cur_ref[...], next_ref[...]], axis=0)
        rw = _rows((tm + 2 * HALO, 1), i * tm - HALO)
        ga, sg = glu[:, :512], _sigmoid(glu[:, 512:])
        u_w = jnp.where((rw >= PAD) & (rw < t), ga * sg, 0.0)
        n_cv = tm + HALO
        cv = _conv_window(u_w, w_ref, n_cv, HALO - (CONV_W - 1)) + b_ref[...]
        xc = cv - jnp.mean(cv, -1, keepdims=True)
        rstd = lax.rsqrt(jnp.mean(xc * xc, -1, keepdims=True) + LN_EPS)
        cvhat = xc * rstd
        ln = cvhat * g_ref[...] + bb_ref[...]
        _, dsl = _silu_and_grad(ln)
        rc = _rows((n_cv, 1), i * tm)
        dc = jnp.concatenate([d_ref[...], dn_ref[...]], axis=0)
        dln = jnp.where(rc < t, dc * dsl, 0.0)
        dhat = dln * g_ref[...]
        dcv = rstd * (dhat - jnp.mean(dhat, -1, keepdims=True)
                      - cvhat * jnp.mean(dhat * cvhat, -1, keepdims=True))
        own = dcv[:tm]
        dlg_ref[...] += jnp.sum((dln * cvhat)[:tm], 0, keepdims=True)
        dlb_ref[...] += jnp.sum(dln[:tm], 0, keepdims=True)
        db_ref[...] += jnp.sum(own, 0, keepdims=True)
        rows = tm + 2 * HALO
        du = None
        for j in range(CONV_W):
            first = HALO - (CONV_W - 1) + j
            shifted = pltpu.roll(u_w, (rows - first) % rows, 0)[:tm]
            dw_ref[j:j + 1, :] += jnp.sum(own * shifted, 0, keepdims=True)
            back = pltpu.roll(dcv, (n_cv - (CONV_W - 1 - j)) % n_cv, 0)[:tm]
            term = back * w_ref[j:j + 1, :]
            du = term if du is None else du + term
        du = jnp.where(_rows((tm, 1), i * tm) >= PAD, du, 0.0)
        ga_c, sg_c = ga[HALO:HALO + tm], sg[HALO:HALO + tm]
        dglu_ref[:, :512] = (du * sg_c).astype(ACT)
        dglu_ref[:, 512:] = (du * ga_c * sg_c * (1.0 - sg_c)).astype(ACT)

    vec = pl.BlockSpec((1, 512), lambda i: (0, 0))
    nxt = lambda i: (jnp.minimum((i + 1) * hb, last), 0)
    return pl.pallas_call(
        body, name="conv_bwd", grid=(t // tm,),
        in_specs=[pl.BlockSpec((tm, D), lambda i: (i, 0)),
                  pl.BlockSpec((HALO, D), lambda i: (jnp.maximum(i * hb - 1, 0), 0)),
                  pl.BlockSpec((HALO, D), nxt),
                  pl.BlockSpec((tm, 512), lambda i: (i, 0)),
                  pl.BlockSpec((HALO, 512), nxt),
                  pl.BlockSpec((CONV_W, 512), lambda i: (0, 0)), vec, vec, vec],
        out_specs=[pl.BlockSpec((tm, D), lambda i: (i, 0)),
                   pl.BlockSpec((HALO, 512), lambda i: (0, 0)), vec, vec, vec],
        out_shape=[jax.ShapeDtypeStruct((t, D), ACT), jax.ShapeDtypeStruct((HALO, 512), F32)]
        + [jax.ShapeDtypeStruct((1, 512), F32)] * 3,
        compiler_params=_cparams(("arbitrary",)),
    )(p0, p0, p0, dc1, dc1, conv_w, conv_b, ln_g, ln_b)


def _swa_bwd(p0, datt, sinks, tables):
    t = p0.shape[0]
    nb = t // BLK

    def body(sink_ref, q_ref, kv_ref, d_ref, cos_ref, sa_ref, sb_ref,
             dq_ref, dkv_ref, ds_ref, acc):
        n = pl.program_id(0)

        @pl.when(n == 0)
        def _():
            acc[...] = jnp.zeros_like(acc)
            ds_ref[...] = jnp.zeros_like(ds_ref)

        kd, vd, lo = _swa_keys(kv_ref, n)
        mask = _swa_mask(n)
        row0 = pl.multiple_of(n * BLK, BLK)
        tabs = [r[pl.ds(row0, BLK), :] for r in (cos_ref, sa_ref, sb_ref)]
        dk_g = [jnp.zeros((3 * BLK, BLK), F32), jnp.zeros((3 * BLK, BLK), F32)]
        dv_g = [jnp.zeros((3 * BLK, BLK), F32), jnp.zeros((3 * BLK, BLK), F32)]
        for p in range(4):
            g = p // 2
            qp = q_ref[:, p * BLK:(p + 1) * BLK]
            dp_ = d_ref[:, p * BLK:(p + 1) * BLK]
            dqs = []
            for j in range(2):
                half = lo if j == 0 else ~lo
                qm = jnp.where(half, qp, 0.0).astype(MXU)
                dom = jnp.where(half, dp_, 0.0).astype(MXU)
                pr, p_sink = _swa_probs(qm, kd[g], mask, sink_ref[2 * p + j])
                dpr = _dot_nt(dom, vd[g])
                delta = jnp.sum(pr * dpr, -1, keepdims=True)
                dsc = (pr * (dpr - delta) * SCALE).astype(MXU)
                ds_ref[2 * p + j:2 * p + j + 1, :] += jnp.full((1, BLK), -1.0, F32) * jnp.sum(p_sink * delta)
                dqs.append(_dot(dsc, kd[g]))
                dk_g[g] = dk_g[g] + _dot_tn(dsc, qm)
                dv_g[g] = dv_g[g] + _dot_tn(pr.astype(MXU), dom)
            dq_ref[:, p * BLK:(p + 1) * BLK] = _unrope(jnp.where(lo, dqs[0], dqs[1]), *tabs).astype(ACT)
        fold = lambda a: a + pltpu.roll(a, HEAD, 1)
        dk = jnp.where(lo, fold(dk_g[0]), fold(dk_g[1]))
        dv = jnp.where(lo, fold(dv_g[0]), fold(dv_g[1]))
        dkv = jnp.concatenate([dk, dv], axis=1)
        prev = pl.multiple_of(jnp.maximum(n - 1, 0) * BLK, BLK)
        acc[0:BLK, :] += dkv[0:BLK]
        acc[pl.ds(prev, BLK), :] += dkv[BLK:2 * BLK]
        acc[pl.ds(row0, BLK), :] += dkv[2 * BLK:]

        @pl.when(n == nb - 1)
        def _():
            dkv_ref[:, :BLK] = _unrope(acc[:, :BLK], cos_ref[...], sa_ref[...], sb_ref[...]).astype(ACT)
            dkv_ref[:, BLK:] = acc[:, BLK:].astype(ACT)

    tab = pl.BlockSpec((t, BLK), lambda n: (0, 0))
    return pl.pallas_call(
        body, name="swa_bwd", grid=(nb,),
        in_specs=[pl.BlockSpec(memory_space=pltpu.SMEM),
                  pl.BlockSpec((BLK, 512), lambda n: (n, 2)),
                  pl.BlockSpec((t, 256), lambda n: (0, 10)),
                  pl.BlockSpec((BLK, 512), lambda n: (n, 0)), tab, tab, tab],
        out_specs=[pl.BlockSpec((BLK, 512), lambda n: (n, 0)),
                   pl.BlockSpec((t, 256), lambda n: (0, 0)),
                   pl.BlockSpec((8, BLK), lambda n: (0, 0))],
        out_shape=[jax.ShapeDtypeStruct((t, 512), ACT), jax.ShapeDtypeStruct((t, 256), ACT),
                   jax.ShapeDtypeStruct((8, BLK), F32)],
        scratch_shapes=[pltpu.VMEM((t, 256), F32)],
        compiler_params=_cparams(("arbitrary",)),
    )(sinks, p0, p0, datt, *tables)


def _ab_in_bwd(dp0, w_t, h0, g_pre, dh1):
    t = h0.shape[0]
    tm = _tile(t, 544)

    def body(*refs):
        d_refs = refs[:5]
        w_ref, h_ref, g_ref, dh1_ref, dh0_ref, dg_ref, acc = refs[5:]
        i, j = pl.program_id(0), pl.program_id(1)

        @pl.when((i == 0) & (j == 0))
        def _():
            dg_ref[...] = jnp.zeros_like(dg_ref)

        @pl.when(j == 0)
        def _():
            acc[...] = jnp.zeros_like(acc)

        def add(ref):
            acc[...] += _dot(ref[...].astype(MXU), w_ref[...])
        dp0.apply(j, d_refs, add)

        @pl.when(j == 10)
        def _():
            dhn = acc[...]
            x = h_ref[...]
            r = lax.rsqrt(jnp.mean(x * x, -1, keepdims=True) + EPS)
            xhat = x * r
            dg_ref[...] += jnp.sum(dhn * xhat, 0, keepdims=True)
            dxn = dhn * g_ref[...]
            dh0_ref[...] = dh1_ref[...] + r * (dxn - xhat * jnp.mean(dxn * xhat, -1, keepdims=True))

    row = pl.BlockSpec((tm, D), lambda i, j: (i, 0))
    vec = pl.BlockSpec((1, D), lambda i, j: (0, 0))
    return pl.pallas_call(
        body, name="ab_in_bwd", grid=(t // tm, 11),
        in_specs=dp0.specs(tm, lambda i, j: i, lambda i, j: j) + [
            pl.BlockSpec((256, D), lambda i, j: (j, 0)), row, vec, row],
        out_specs=[row, vec],
        out_shape=[jax.ShapeDtypeStruct((t, D), F32), jax.ShapeDtypeStruct((1, D), F32)],
        scratch_shapes=[pltpu.VMEM((tm, D), F32)],
        compiler_params=_cparams(("arbitrary", "arbitrary")),
    )(*dp0.arrays, w_t, h0, g_pre, dh1)


def _dw_plain(a, b, name):
    t, m = a.shape
    n = b.shape[1]
    tm = _tile(t, 544)
    tn = min(n, 512)
    nk = t // tm

    def body(a_ref, b_ref, o_ref, acc):
        k = pl.program_id(1)

        @pl.when(k == 0)
        def _():
            acc[...] = jnp.zeros_like(acc)

        acc[...] += _dot_tn(a_ref[...].astype(MXU), b_ref[...].astype(MXU))

        @pl.when(k == nk - 1)
        def _():
            o_ref[...] = acc[...].astype(WIRE)

    return pl.pallas_call(
        body, name=name, grid=(n // tn, nk),
        in_specs=[pl.BlockSpec((tm, m), lambda j, k: (k, 0)),
                  pl.BlockSpec((tm, tn), lambda j, k: (k, j))],
        out_specs=pl.BlockSpec((m, tn), lambda j, k: (0, j)),
        out_shape=jax.ShapeDtypeStruct((m, n), WIRE),
        scratch_shapes=[pltpu.VMEM((m, tn), F32)],
        compiler_params=_cparams(("arbitrary", "arbitrary")),
    )(a, b)


def _dw_chunks(hn, dp, name):
    t = hn.shape[0]
    tm = _tile(t, 544)
    nk = t // tm
    nt, tw = dp.n_tiles, dp.tw
    n_in = len(dp.arrays)

    def body(*refs):
        d_refs = refs[:n_in]
        h_ref, o_ref, acc = refs[n_in:]
        j, k = pl.program_id(0), pl.program_id(1)

        @pl.when(k == 0)
        def _():
            acc[...] = jnp.zeros_like(acc)

        def add(ref):
            acc[...] += _dot_tn(h_ref[...].astype(MXU), ref[...].astype(MXU))
        dp.apply(j, d_refs, add)

        @pl.when(k == nk - 1)
        def _():
            o_ref[...] = acc[...].astype(WIRE)

    return pl.pallas_call(
        body, name=name, grid=(nt, nk),
        in_specs=dp.specs(tm, lambda j, k: k, lambda j, k: j) + [
            pl.BlockSpec((tm, D), lambda j, k: (k, 0))],
        out_specs=pl.BlockSpec((None, D, tw), lambda j, k: (j, 0, 0)),
        out_shape=jax.ShapeDtypeStruct((nt, D, tw), WIRE),
        scratch_shapes=[pltpu.VMEM((D, tw), F32)],
        compiler_params=_cparams(("arbitrary", "arbitrary")),
    )(*dp.arrays, hn)


def _dw_transposed(dp, hn, name):
    t = hn.shape[0]
    tm = _tile(t, 544)
    nk = t // tm
    nt, tw = dp.n_tiles, dp.tw
    n_in = len(dp.arrays)

    def body(*refs):
        d_refs = refs[:n_in]
        h_ref, o_ref, acc = refs[n_in:]
        j, k = pl.program_id(0), pl.program_id(1)

        @pl.when(k == 0)
        def _():
            acc[...] = jnp.zeros_like(acc)

        def add(ref):
            acc[...] += _dot_tn(ref[...].astype(MXU), h_ref[...].astype(MXU))
        dp.apply(j, d_refs, add)

        @pl.when(k == nk - 1)
        def _():
            o_ref[...] = acc[...].astype(WIRE)

    return pl.pallas_call(
        body, name=name, grid=(nt, nk),
        in_specs=dp.specs(tm, lambda j, k: k, lambda j, k: j) + [
            pl.BlockSpec((tm, D), lambda j, k: (k, 0))],
        out_specs=pl.BlockSpec((tw, D), lambda j, k: (j, 0)),
        out_shape=jax.ShapeDtypeStruct((nt * tw, D), WIRE),
        scratch_shapes=[pltpu.VMEM((tw, D), F32)],
        compiler_params=_cparams(("arbitrary", "arbitrary")),
    )(*dp.arrays, hn)


def kernel(x, meta_tokens, ab_pre_norm, ab_w_in, ab_sinks, ab_conv_w, ab_conv_b, ab_conv_ln_g, ab_conv_ln_b, ab_w_pw2, ab_w_out, ab_post_norm, sb_pre_norm, sb_w_in, sb_w_out, sb_post_norm, loss_target, m_meta_tokens, m_ab_pre_norm, m_ab_w_in, m_ab_sinks, m_ab_conv_w, m_ab_conv_b, m_ab_conv_ln_g, m_ab_conv_ln_b, m_ab_w_pw2, m_ab_w_out, m_ab_post_norm, m_sb_pre_norm, m_sb_w_in, m_sb_w_out, m_sb_post_norm, v_meta_tokens, v_ab_pre_norm, v_ab_w_in, v_ab_sinks, v_ab_conv_w, v_ab_conv_b, v_ab_conv_ln_g, v_ab_conv_ln_b, v_ab_w_pw2, v_ab_w_out, v_ab_post_norm, v_sb_pre_norm, v_sb_w_in, v_sb_w_out, v_sb_post_norm):
    seq = x.shape[1]
    t = seq + BLK
    mx, my, mc = _coords()
    me = 4 * mx + 2 * my + mc
    pos = jnp.stack([mx, my, mc, me]).astype(jnp.int32)

    w_ab_t, w_sb, w_oa, w_os, w_pw = _all_gather(
        [ab_w_in[0].T.astype(WIRE), sb_w_in[0].astype(WIRE), ab_w_out[0].astype(WIRE),
         sb_w_out[0].astype(WIRE), ab_w_pw2[0].astype(WIRE)], "gather_weights")
    w_ab_t = w_ab_t.reshape(2816, D)
    w_oa = w_oa.reshape(D, D)
    w_os = w_os.reshape(D, D)
    w_pw = w_pw.reshape(512, 512)
    small = _all_gather([meta_tokens, ab_conv_w[0], sb_pre_norm, sb_post_norm], "gather_small")
    meta_full = jnp.moveaxis(small[0], 0, 1).reshape(N_META, D)
    conv_w = jnp.moveaxis(small[1], 0, 1).reshape(CONV_W, 512)
    sb_pre = jnp.moveaxis(small[2], 0, 1).reshape(1, D)
    sb_post = jnp.moveaxis(small[3], 0, 1).reshape(1, D)

    h0 = jnp.concatenate([jnp.zeros((PAD, D), F32), meta_full, x[0]], axis=0)
    tgt = jnp.concatenate([jnp.zeros((BLK, D), F32), loss_target[0]], axis=0)
    tables = _rope_tables(t)
    sinks = ab_sinks[0]

    p0, hn0 = _ab_in(h0, ab_pre_norm, w_ab_t, tables)
    att = _swa_fwd(p0, sinks)
    c1 = _conv_fwd(p0, conv_w, ab_conv_b, ab_conv_ln_g, ab_conv_ln_b)
    h1, y0, mix = _ab_out(h0, p0, att, c1, w_pw, w_oa, ab_post_norm)
    p1, hn1 = _sb_in(h1, sb_pre, w_sb)
    o, ltot = _sb_fwd(p1)
    loss_part, dh2, dy1, m1, do, dgate, dg_sb_post = _sb_out(o, p1, w_os, h1, sb_post, tgt)

    dq1, dk1, dv1 = _sb_bwd(p1, ltot, do)
    dp1 = _Cols([(dq1, 0, 2), (dk1, 2, 2), (dv1, 4, 2), (dgate, 6, 2)], 512)
    dh1, dy0, dga, dgb, datt, dc1, dc2, dg_sb_pre, dg_ab_post = _mid_bwd(
        dp1, w_sb, h1, sb_pre, dh2, y0, ab_post_norm, w_oa, p0, att, c1, w_pw)
    dglu, dconv_w, dconv_b, dln_g, dln_b = _conv_bwd(p0, dc1, conv_w, ab_conv_b, ab_conv_ln_g, ab_conv_ln_b)
    dq0, dkv0, dsinks = _swa_bwd(p0, datt, sinks, tables)
    dp0 = _Cols([(dq0, 0, 2), (dkv0, 2, 1), (dga, 3, 2), (dglu, 5, 4), (dgb, 9, 2)], 256)
    dh0, dg_ab_pre = _ab_in_bwd(dp0, w_ab_t, h0, ab_pre_norm, dh1)

    parts = [
        _dw_transposed(dp0, hn0, "dw_ab_in").reshape(4, 2, 352, D),
        _dw_chunks(hn1, dp1, "dw_sb_in").reshape(4, 2, D, 512),
        _dw_plain(mix, dy0, "dw_ab_out").reshape(4, 2, BLK, D),
        _dw_plain(m1, dy1, "dw_sb_out").reshape(4, 2, BLK, D),
        _dw_plain(c1, dc2, "dw_pw2").reshape(4, 2, 64, 512),
    ]
    names = ["ab_in", "sb_in", "ab_out", "sb_out", "pw2"]
    from_sibling = _exchange_sibling(parts, "reduce_sibling")
    chip_sums = [_add_sibling(pos, p, r, "add_sibling_" + nm) for p, r, nm in zip(parts, from_sibling, names)]
    from_chips = _exchange_chips(chip_sums, "reduce_chips")
    big = [_sum_chips(pos, s, r, "sum_chips_" + nm) for s, r, nm in zip(chip_sums, from_chips, names)]
    g_ab_w_in = big[0].T
    g_sb_w_in, g_ab_w_out, g_sb_w_out, g_ab_w_pw2 = big[1:]

    small_parts = [dh0[PAD:BLK], dg_ab_pre, dsinks, dconv_w, dconv_b, dln_g, dln_b,
                   dg_ab_post, dg_sb_pre, dg_sb_post]
    red = _reduce_small(_all_gather(small_parts, "gather_small_grads"), "reduce_small")
    col = lambda a, w: lax.dynamic_slice_in_dim(a, me * w, w, axis=1)
    g_meta = col(red[0], BLK)
    g_ab_pre = red[1]
    g_sinks = red[2][:, 0].reshape(1, 8)
    g_conv_w = col(red[3][:CONV_W], 64)
    g_conv_b, g_ln_g, g_ln_b, g_ab_post = red[4], red[5], red[6], red[7]
    g_sb_pre, g_sb_post = col(red[8], BLK), col(red[9], BLK)

    loss = lax.psum(loss_part[0, 0], ("x", "y", "c"))
    grad_x = dh0[BLK:][None]

    weights = [meta_tokens, ab_pre_norm, ab_w_in[0], ab_sinks, ab_conv_w[0], ab_conv_b, ab_conv_ln_g,
               ab_conv_ln_b, ab_w_pw2[0], ab_w_out[0], ab_post_norm, sb_pre_norm, sb_w_in[0],
               sb_w_out[0], sb_post_norm]
    grads = [g_meta, g_ab_pre, g_ab_w_in, g_sinks, g_conv_w, g_conv_b, g_ln_g, g_ln_b, g_ab_w_pw2,
             g_ab_w_out, g_ab_post, g_sb_pre, g_sb_w_in, g_sb_w_out, g_sb_post]
    ms = [m_meta_tokens, m_ab_pre_norm, m_ab_w_in[0], m_ab_sinks, m_ab_conv_w[0], m_ab_conv_b,
          m_ab_conv_ln_g, m_ab_conv_ln_b, m_ab_w_pw2[0], m_ab_w_out[0], m_ab_post_norm,
          m_sb_pre_norm, m_sb_w_in[0], m_sb_w_out[0], m_sb_post_norm]
    vs = [v_meta_tokens, v_ab_pre_norm, v_ab_w_in[0], v_ab_sinks, v_ab_conv_w[0], v_ab_conv_b,
          v_ab_conv_ln_g, v_ab_conv_ln_b, v_ab_w_pw2[0], v_ab_w_out[0], v_ab_post_norm,
          v_sb_pre_norm, v_sb_w_in[0], v_sb_w_out[0], v_sb_post_norm]
    lead = [w.ndim == 3 for w in (meta_tokens, ab_pre_norm, ab_w_in, ab_sinks, ab_conv_w, ab_conv_b,
                                   ab_conv_ln_g, ab_conv_ln_b, ab_w_pw2, ab_w_out, ab_post_norm,
                                   sb_pre_norm, sb_w_in, sb_w_out, sb_post_norm)]
    big_ids = [2, 8, 9, 12, 13]
    small_ids = [i for i in range(15) if i not in big_ids]
    deltas, new_m, new_v = [None] * 15, [None] * 15, [None] * 15
    for ids, nm in ((small_ids, "adamw_small"), (big_ids, "adamw_big")):
        d_, m_, v_ = _adamw([weights[i] for i in ids], [grads[i] for i in ids],
                            [ms[i] for i in ids], [vs[i] for i in ids], nm)
        for k, i in enumerate(ids):
            deltas[i], new_m[i], new_v[i] = d_[k], m_[k], v_[k]
    fix = lambda arrs: [a[None] if l else a for a, l in zip(arrs, lead)]
    return (loss, grad_x, *fix(grads), *fix(deltas), *fix(new_m), *fix(new_v))
```

```python
import functools

import numpy as np
import jax
import jax.numpy as jnp
from jax import lax
from jax.experimental import pallas as pl
from jax.experimental.pallas import tpu as pltpu

F32 = jnp.float32
MXU = jnp.bfloat16
ACT = jnp.bfloat16
WIRE = jnp.bfloat16

D = 1024
N_META = 16
BLK = 128
PAD = BLK - N_META
HEAD = 64
NEG = -1e30
EPS = 1e-6
LN_EPS = 1e-5
ROPE_THETA = 10000.0
SCALE = HEAD ** -0.5
CONV_W = 31
HALO = 32
LR, B1, B2, ADAM_EPS, WD, STEP = 0.001, 0.9, 0.999, 1e-08, 0.01, 10
VMEM_LIMIT = 56 * 1024 * 1024
MESH = pl.DeviceIdType.MESH

P0_SRC = (5, 6, 7, 8, 0, 1, 3, 4, 9, 10, 2)


def _cparams(sem=None):
    return pltpu.CompilerParams(dimension_semantics=sem, vmem_limit_bytes=VMEM_LIMIT)


def _tile(t, pref):
    for cand in (pref, 544, 272, 128):
        if cand <= pref and t % cand == 0:
            return cand
    raise ValueError(t)


def _sigmoid(x):
    return 1.0 / (1.0 + jnp.exp(-x))


def _silu_and_grad(x):
    s = _sigmoid(x)
    return x * s, s * (1.0 + x * (1.0 - s))


def _dot(a, b):
    return jnp.dot(a, b, preferred_element_type=F32)


def _dot_nt(a, b):
    return lax.dot_general(a, b, (((1,), (1,)), ((), ())), preferred_element_type=F32)


def _dot_tn(a, b):
    return lax.dot_general(a, b, (((0,), (0,)), ((), ())), preferred_element_type=F32)


def _rows(shape, base):
    return base + lax.broadcasted_iota(jnp.int32, shape, 0)


def _rope_tables(t):
    half = HEAD // 2
    inv = ROPE_THETA ** (-np.arange(half, dtype=np.float32) / half)
    pos = (np.arange(t) - PAD).astype(np.float32)
    ang = pos[:, None] * inv[None, :]
    lane = np.arange(BLK)
    cos = np.cos(ang)[:, lane % half].astype(np.float32)
    sin = np.sin(ang)[:, lane % half].astype(np.float32)
    first = (lane % HEAD) < half
    sin_a = np.where(first[None, :], -sin, 0.0).astype(np.float32)
    sin_b = np.where(first[None, :], 0.0, sin).astype(np.float32)
    return jnp.asarray(cos), jnp.asarray(sin_a), jnp.asarray(sin_b)


def _rope(v, cos, sin_a, sin_b):
    return v * cos + pltpu.roll(v, 96, 1) * sin_a + pltpu.roll(v, 32, 1) * sin_b


def _unrope(v, cos, sin_a, sin_b):
    return v * cos - pltpu.roll(v, 96, 1) * sin_a - pltpu.roll(v, 32, 1) * sin_b


def _coords():
    return lax.axis_index("x"), lax.axis_index("y"), lax.axis_index("c")


def _all_gather(arrs, name):
    n = len(arrs)

    def body(*refs):
        ins, outs = refs[:n], refs[n:2 * n]
        send_sems, recv_sems, local_sems = refs[2 * n:]
        x, y, c = _coords()
        me, sibling = (x, y, c), (x, y, 1 - c)
        chips = [(1 - x, y), (x, 1 - y), (1 - x, 1 - y)]

        def copy(a, k, block, to, src=None):
            dst = outs[a].at[4 * block[0] + 2 * block[1] + block[2]]
            return pltpu.make_async_remote_copy(
                src_ref=dst if src is None else src, dst_ref=dst,
                send_sem=send_sems.at[a, k], recv_sem=recv_sems.at[a, k],
                device_id=to, device_id_type=MESH)

        mine = [pltpu.make_async_copy(ins[a], outs[a].at[4 * x + 2 * y + c], local_sems.at[a])
                for a in range(n)]
        for cp in mine:
            cp.start()
        first = []
        for a in range(n):
            first.append(copy(a, 0, me, sibling, src=ins[a]))
            for j, chip in enumerate(chips):
                first.append(copy(a, 1 + j, me, (*chip, c), src=ins[a]))
        for cp in first:
            cp.start()
        passed = []
        for j, chip in enumerate(chips):
            for a in range(n):
                copy(a, 1 + j, (*chip, c), me).wait_recv()
                cp = copy(a, 4 + j, (*chip, c), sibling)
                cp.start()
                passed.append(cp)
        for a in range(n):
            copy(a, 0, sibling, me).wait_recv()
            for j, chip in enumerate(chips):
                copy(a, 4 + j, (*chip, 1 - c), me).wait_recv()
        for cp in first + passed:
            cp.wait_send()
        for cp in mine:
            cp.wait()

    any_spec = pl.BlockSpec(memory_space=pl.ANY)
    return pl.pallas_call(
        body, name=name,
        out_shape=[jax.ShapeDtypeStruct((8,) + a.shape, a.dtype) for a in arrs],
        in_specs=[any_spec] * n, out_specs=[any_spec] * n,
        scratch_shapes=[pltpu.SemaphoreType.DMA((n, 7)), pltpu.SemaphoreType.DMA((n, 7)),
                        pltpu.SemaphoreType.DMA((n,))],
    )(*arrs)


def _exchange_sibling(parts, name):
    n = len(parts)

    def body(*refs):
        ins, outs = refs[:n], refs[n:2 * n]
        send_sems, recv_sems = refs[2 * n:]
        x, y, c = _coords()
        copies = [pltpu.make_async_remote_copy(
            src_ref=ins[a].at[:, 1 - c], dst_ref=outs[a],
            send_sem=send_sems.at[a], recv_sem=recv_sems.at[a],
            device_id=(x, y, 1 - c), device_id_type=MESH) for a in range(n)]
        for cp in copies:
            cp.start()
        for cp in copies:
            cp.wait()

    any_spec = pl.BlockSpec(memory_space=pl.ANY)
    return pl.pallas_call(
        body, name=name,
        out_shape=[jax.ShapeDtypeStruct((4,) + p.shape[2:], p.dtype) for p in parts],
        in_specs=[any_spec] * n, out_specs=[any_spec] * n,
        scratch_shapes=[pltpu.SemaphoreType.DMA((n,)), pltpu.SemaphoreType.DMA((n,))],
    )(*parts)


def _exchange_chips(sums, name):
    n = len(sums)

    def body(*refs):
        ins, outs = refs[:n], refs[n:2 * n]
        send_sems, recv_sems = refs[2 * n:]
        x, y, c = _coords()
        chips = [(1 - x, y), (x, 1 - y), (1 - x, 1 - y)]
        copies = []
        for a in range(n):
            for k, chip in enumerate(chips):
                copies.append(pltpu.make_async_remote_copy(
                    src_ref=ins[a].at[2 * chip[0] + chip[1]], dst_ref=outs[a].at[k],
                    send_sem=send_sems.at[a, k], recv_sem=recv_sems.at[a, k],
                    device_id=(*chip, c), device_id_type=MESH))
        for cp in copies:
            cp.start()
        for cp in copies:
            cp.wait()

    any_spec = pl.BlockSpec(memory_space=pl.ANY)
    return pl.pallas_call(
        body, name=name,
        out_shape=[jax.ShapeDtypeStruct((3,) + s.shape[1:], s.dtype) for s in sums],
        in_specs=[any_spec] * n, out_specs=[any_spec] * n,
        scratch_shapes=[pltpu.SemaphoreType.DMA((n, 3)), pltpu.SemaphoreType.DMA((n, 3))],
    )(*sums)


def _add_sibling(pos, part, recv, name):
    _, _, r, c = part.shape

    def body(pos_ref, p_ref, r_ref, o_ref):
        o_ref[...] = (p_ref[...].astype(F32) + r_ref[...].astype(F32)).astype(o_ref.dtype)

    return pl.pallas_call(
        body, name=name,
        grid_spec=pltpu.PrefetchScalarGridSpec(
            num_scalar_prefetch=1, grid=(4,),
            in_specs=[pl.BlockSpec((None, None, r, c), lambda q, pos: (q, pos[2], 0, 0)),
                      pl.BlockSpec((None, r, c), lambda q, pos: (q, 0, 0))],
            out_specs=pl.BlockSpec((None, r, c), lambda q, pos: (q, 0, 0))),
        out_shape=jax.ShapeDtypeStruct((4, r, c), part.dtype),
        compiler_params=_cparams(("arbitrary",)),
    )(pos, part, recv)


def _sum_chips(pos, sums, recv, name):
    _, r, c = sums.shape

    def body(pos_ref, s_ref, r_ref, o_ref):
        g = s_ref[...].astype(F32)
        for k in range(3):
            g = g + r_ref[k].astype(F32)
        o_ref[...] = g

    return pl.pallas_call(
        body, name=name,
        grid_spec=pltpu.PrefetchScalarGridSpec(
            num_scalar_prefetch=1, grid=(1,),
            in_specs=[pl.BlockSpec((None, r, c), lambda i, pos: (2 * pos[0] + pos[1], 0, 0)),
                      pl.BlockSpec((3, r, c), lambda i, pos: (0, 0, 0))],
            out_specs=pl.BlockSpec((r, c), lambda i, pos: (0, 0))),
        out_shape=jax.ShapeDtypeStruct((r, c), F32),
        compiler_params=_cparams(("arbitrary",)),
    )(pos, sums, recv)


def _adamw(ws, gs, ms, vs, name):
    n = len(ws)
    c1 = 1.0 / (1.0 - B1 ** STEP)
    c2 = 1.0 / (1.0 - B2 ** STEP)

    def body(*refs):
        w_r, g_r, m_r, v_r = refs[:n], refs[n:2 * n], refs[2 * n:3 * n], refs[3 * n:4 * n]
        d_o, m_o, v_o = refs[4 * n:5 * n], refs[5 * n:6 * n], refs[6 * n:7 * n]
        for a in range(n):
            g = g_r[a][...]
            m = B1 * m_r[a][...] + (1.0 - B1) * g
            v = B2 * v_r[a][...] + (1.0 - B2) * (g * g)
            d_o[a][...] = -LR * ((m * c1) / (jnp.sqrt(v * c2) + ADAM_EPS) + WD * w_r[a][...])
            m_o[a][...] = m
            v_o[a][...] = v

    shapes = [jax.ShapeDtypeStruct(w.shape, F32) for w in ws]
    outs = pl.pallas_call(body, name=name, out_shape=shapes * 3,
                          compiler_params=_cparams())(*ws, *gs, *ms, *vs)
    return outs[:n], outs[n:2 * n], outs[2 * n:]


def _reduce_small(gathered, name):
    n = len(gathered)

    def body(*refs):
        for a in range(n):
            acc = refs[a][0]
            for k in range(1, 8):
                acc = acc + refs[a][k]
            refs[n + a][...] = acc

    return pl.pallas_call(
        body, name=name,
        out_shape=[jax.ShapeDtypeStruct(g.shape[1:], F32) for g in gathered],
        compiler_params=_cparams())(*gathered)


class _Cols:
    def __init__(self, pieces, tw):
        self.pieces, self.tw = pieces, tw
        self.arrays = [p[0] for p in pieces]
        self.n_tiles = sum(p[2] for p in pieces)

    def specs(self, tm, row_of, tile_of):
        out = []
        for _, first, cnt in self.pieces:
            def imap(*g, first=first, cnt=cnt):
                return (row_of(*g), jnp.clip(tile_of(*g) - first, 0, cnt - 1))
            out.append(pl.BlockSpec((tm, self.tw), imap))
        return out

    def apply(self, t, refs, fn):
        for ref, (_, first, cnt) in zip(refs, self.pieces):
            pl.when((t >= first) & (t < first + cnt))(functools.partial(fn, ref))


def _ab_in(h, g, w_t, tables):
    t = h.shape[0]
    tm = _tile(t, 544)
    src = jnp.asarray(np.array(P0_SRC, np.int32))

    def body(src_ref, h_ref, g_ref, w_ref, cos_ref, sa_ref, sb_ref, o_ref, hn_ref, hn_s):
        j = pl.program_id(1)

        @pl.when(j == 0)
        def _():
            x = h_ref[...]
            hn = (x * lax.rsqrt(jnp.mean(x * x, -1, keepdims=True) + EPS) * g_ref[...]).astype(MXU)
            hn_s[...] = hn
            hn_ref[...] = hn.astype(ACT)

        acc = _dot_nt(hn_s[...], w_ref[...])
        rope = lambda v: _rope(v, cos_ref[...], sa_ref[...], sb_ref[...])

        @pl.when((j == 4) | (j == 5))
        def _():
            o_ref[:, :BLK] = rope(acc[:, :BLK])
            o_ref[:, BLK:] = rope(acc[:, BLK:])

        @pl.when(j == 10)
        def _():
            o_ref[:, :BLK] = rope(acc[:, :BLK])
            o_ref[:, BLK:] = acc[:, BLK:]

        @pl.when((j < 4) | ((j > 5) & (j < 10)))
        def _():
            o_ref[...] = acc

    tab = pl.BlockSpec((tm, BLK), lambda i, j, s: (i, 0))
    return pl.pallas_call(
        body, name="ab_in",
        grid_spec=pltpu.PrefetchScalarGridSpec(
            num_scalar_prefetch=1, grid=(t // tm, 11),
            in_specs=[pl.BlockSpec((tm, D), lambda i, j, s: (i, 0)),
                      pl.BlockSpec((1, D), lambda i, j, s: (0, 0)),
                      pl.BlockSpec((256, D), lambda i, j, s: (s[j], 0)),
                      tab, tab, tab],
            out_specs=[pl.BlockSpec((tm, 256), lambda i, j, s: (i, j)),
                       pl.BlockSpec((tm, D), lambda i, j, s: (i, 0))],
            scratch_shapes=[pltpu.VMEM((tm, D), MXU)]),
        out_shape=[jax.ShapeDtypeStruct((t, 2816), F32), jax.ShapeDtypeStruct((t, D), ACT)],
        compiler_params=_cparams(("arbitrary", "arbitrary")),
    )(src, h, g, w_t, *tables)


def _swa_mask(n):
    r = lax.broadcasted_iota(jnp.int32, (BLK, 3 * BLK), 0)
    c = lax.broadcasted_iota(jnp.int32, (BLK, 3 * BLK), 1)
    qpos = n * BLK + r
    bpos = (n - 2) * BLK + c
    meta_ok = (c >= PAD) & (c < BLK) & (qpos - c >= BLK)
    band_ok = (c >= BLK) & (bpos >= PAD) & (qpos >= bpos) & (qpos - bpos < BLK)
    return meta_ok | band_ok


def _swa_keys(kv_ref, n):
    def blk(b):
        return kv_ref[pl.ds(pl.multiple_of(b * BLK, BLK), BLK), :]
    kv = jnp.concatenate([kv_ref[0:BLK, :], blk(jnp.maximum(n - 1, 0)), blk(n)], axis=0)
    lo = lax.broadcasted_iota(jnp.int32, (1, BLK), 1) < HEAD
    out = []
    for part in (kv[:, :BLK], kv[:, BLK:]):
        rolled = pltpu.roll(part, HEAD, 1)
        out.append((jnp.where(lo, part, rolled).astype(MXU), jnp.where(lo, rolled, part).astype(MXU)))
    return out[0], out[1], lo


def _swa_probs(qm, kd, mask, sink):
    s = jnp.where(mask, _dot_nt(qm, kd) * SCALE, NEG)
    m = jnp.maximum(jnp.max(s, -1, keepdims=True), sink)
    e = jnp.exp(s - m)
    inv = 1.0 / (jnp.sum(e, -1, keepdims=True) + jnp.exp(sink - m))
    return e * inv, jnp.exp(sink - m) * inv


def _swa_fwd(p0, sinks):
    t = p0.shape[0]

    def body(sink_ref, q_ref, kv_ref, o_ref):
        n = pl.program_id(0)
        kd, vd, lo = _swa_keys(kv_ref, n)
        mask = _swa_mask(n)
        for p in range(4):
            qp = q_ref[:, p * BLK:(p + 1) * BLK]
            outs = []
            for j in range(2):
                qm = jnp.where(lo if j == 0 else ~lo, qp, 0.0).astype(MXU)
                pr, _ = _swa_probs(qm, kd[p // 2], mask, sink_ref[2 * p + j])
                outs.append(_dot(pr.astype(MXU), vd[p // 2]))
            o_ref[:, p * BLK:(p + 1) * BLK] = jnp.where(lo, outs[0], outs[1])

    return pl.pallas_call(
        body, name="swa_fwd", grid=(t // BLK,),
        in_specs=[pl.BlockSpec(memory_space=pltpu.SMEM),
                  pl.BlockSpec((BLK, 512), lambda n: (n, 2)),
                  pl.BlockSpec((t, 256), lambda n: (0, 10))],
        out_specs=pl.BlockSpec((BLK, 512), lambda n: (n, 0)),
        out_shape=jax.ShapeDtypeStruct((t, 512), F32),
        compiler_params=_cparams(("arbitrary",)),
    )(sinks, p0, p0)


def _conv_window(u_w, w_ref, n_out, first):
    rows = u_w.shape[0]
    acc = None
    for j in range(CONV_W):
        shifted = pltpu.roll(u_w, (rows - (first + j)) % rows, 0)[:n_out]
        term = shifted * w_ref[j:j + 1, :]
        acc = term if acc is None else acc + term
    return acc


def _conv_fwd(p0, conv_w, conv_b, ln_g, ln_b):
    t = p0.shape[0]
    tm = _tile(t, 544)
    hb = tm // HALO

    def body(cur_ref, prev_ref, w_ref, b_ref, g_ref, bb_ref, o_ref):
        i = pl.program_id(0)
        glu = jnp.concatenate([prev_ref[...], cur_ref[...]], axis=0)
        rw = _rows((tm + HALO, 1), i * tm - HALO)
        u_w = jnp.where(rw >= PAD, glu[:, :512] * _sigmoid(glu[:, 512:]), 0.0)
        cv = _conv_window(u_w, w_ref, tm, HALO - (CONV_W - 1)) + b_ref[...]
        xc = cv - jnp.mean(cv, -1, keepdims=True)
        ln = xc * lax.rsqrt(jnp.mean(xc * xc, -1, keepdims=True) + LN_EPS) * g_ref[...] + bb_ref[...]
        o_ref[...] = (ln * _sigmoid(ln)).astype(ACT)

    vec = pl.BlockSpec((1, 512), lambda i: (0, 0))
    return pl.pallas_call(
        body, name="conv_fwd", grid=(t // tm,),
        in_specs=[pl.BlockSpec((tm, D), lambda i: (i, 0)),
                  pl.BlockSpec((HALO, D), lambda i: (jnp.maximum(i * hb - 1, 0), 0)),
                  pl.BlockSpec((CONV_W, 512), lambda i: (0, 0)), vec, vec, vec],
        out_specs=pl.BlockSpec((tm, 512), lambda i: (i, 0)),
        out_shape=jax.ShapeDtypeStruct((t, 512), ACT),
        compiler_params=_cparams(("arbitrary",)),
    )(p0, p0, conv_w, conv_b, ln_g, ln_b)


def _ab_out(h, p0, att, c1, w_pw2, w_out, g_post):
    t = h.shape[0]
    tm = _tile(t, 272)

    def body(h_ref, ga_ref, gb_ref, att_ref, c1_ref, pw_ref, wo_ref, g_ref, h1_ref, y_ref, mix_ref):
        i = pl.program_id(0)
        sga, _ = _silu_and_grad(ga_ref[...])
        sgb, _ = _silu_and_grad(gb_ref[...])
        a = att_ref[...] * sga
        c = _dot(c1_ref[...].astype(MXU), pw_ref[...]) * sgb
        mix = jnp.concatenate([a, c], axis=1).astype(MXU)
        y = _dot(mix, wo_ref[...])
        yn = y * lax.rsqrt(jnp.mean(y * y, -1, keepdims=True) + EPS) * g_ref[...]
        h1_ref[...] = jnp.where(_rows((tm, 1), i * tm) >= PAD, h_ref[...] + yn, 0.0)
        y_ref[...] = y
        mix_ref[...] = mix.astype(ACT)

    row = lambda w, idx: pl.BlockSpec((tm, w), lambda i: (i, idx))
    full = lambda a: pl.BlockSpec(a.shape, lambda i: (0, 0))
    return pl.pallas_call(
        body, name="ab_out", grid=(t // tm,),
        in_specs=[row(D, 0), row(512, 3), row(512, 4), row(512, 0), row(512, 0),
                  full(w_pw2), full(w_out), full(g_post)],
        out_specs=[row(D, 0), row(D, 0), row(D, 0)],
        out_shape=[jax.ShapeDtypeStruct((t, D), F32), jax.ShapeDtypeStruct((t, D), F32),
                   jax.ShapeDtypeStruct((t, D), ACT)],
        compiler_params=_cparams(("arbitrary",)),
    )(h, p0, p0, att, c1, w_pw2, w_out, g_post)


def _sb_in(h, g, w):
    t = h.shape[0]
    tm = _tile(t, 544)

    def body(h_ref, g_ref, w_ref, o_ref, hn_ref, hn_s):
        @pl.when(pl.program_id(1) == 0)
        def _():
            x = h_ref[...]
            hn = (x * lax.rsqrt(jnp.mean(x * x, -1, keepdims=True) + EPS) * g_ref[...]).astype(MXU)
            hn_s[...] = hn
            hn_ref[...] = hn.astype(ACT)

        o_ref[...] = _dot(hn_s[...], w_ref[...])

    return pl.pallas_call(
        body, name="sb_in", grid=(t // tm, 8),
        in_specs=[pl.BlockSpec((tm, D), lambda i, j: (i, 0)),
                  pl.BlockSpec((1, D), lambda i, j: (0, 0)),
                  pl.BlockSpec((None, D, 512), lambda i, j: (j, 0, 0))],
        out_specs=[pl.BlockSpec((tm, 512), lambda i, j: (i, j)),
                   pl.BlockSpec((tm, D), lambda i, j: (i, 0))],
        out_shape=[jax.ShapeDtypeStruct((t, 4096), F32), jax.ShapeDtypeStruct((t, D), ACT)],
        scratch_shapes=[pltpu.VMEM((tm, D), MXU)],
        compiler_params=_cparams(("arbitrary", "arbitrary")),
    )(h, g, w)


def _split_hi_lo(x):
    hi = x.astype(MXU)
    lo = (x - hi.astype(F32)).astype(MXU)
    return hi, lo


def _scan_matrix(suffix):
    j = lax.broadcasted_iota(jnp.int32, (2 * BLK, 2 * BLK), 0) % BLK
    s = lax.broadcasted_iota(jnp.int32, (2 * BLK, 2 * BLK), 1)
    keep = (s >= BLK) | ((j > s) if suffix else (j < s))
    return jnp.where(keep, 1.0, 0.0).astype(MXU)


def _scan_packed(hi_lo, b, mat):
    cols = slice(b * BLK, (b + 1) * BLK)
    both = _dot(jnp.concatenate([hi_lo[:BLK, cols], hi_lo[BLK:, cols]], axis=1), mat)
    return both[:, :BLK], both[:, BLK:]


KC = 4
CHUNK = KC * BLK
SLACK = CHUNK - BLK
GROUPS = 2


def _sb_logits(qm, kc, valid):
    z = _dot_nt(qm, kc)
    log_beta = jnp.minimum(z, 0.0) - jnp.log(1.0 + jnp.exp(-jnp.abs(z)))
    return log_beta, jnp.where(valid, log_beta - z, 0.0)


def _sb_valid(i, first_key):
    r = lax.broadcasted_iota(jnp.int32, (BLK, CHUNK), 0)
    c = lax.broadcasted_iota(jnp.int32, (BLK, CHUNK), 1)
    kpos = first_key + c
    return (kpos >= PAD) & (kpos < i * BLK + r)


def _sb_fwd(p1):
    t = p1.shape[0]

    w = GROUPS * BLK

    def body(q_ref, k_ref, v_ref, o_ref, lt_ref, k_s, v_s):
        i = pl.program_id(1)

        @pl.when(i == 0)
        def _():
            for src, dst in ((k_ref, k_s), (v_ref, v_s)):
                dst[0:SLACK, :] = jnp.zeros((SLACK, w), MXU)
                dst[SLACK:, :] = src[...].astype(MXU)

        lo = lax.broadcasted_iota(jnp.int32, (1, BLK), 1) < HEAD
        qm = []
        for g in range(GROUPS):
            q = q_ref[:, g * BLK:(g + 1) * BLK] * SCALE
            qm += [jnp.where(lo, q, 0.0).astype(MXU), jnp.where(lo, 0.0, q).astype(MXU)]
        mat = _scan_matrix(True)

        def step(s, carry):
            start = pl.multiple_of((i - KC * s) * BLK, BLK)
            valid = _sb_valid(i, start - SLACK)
            new, staged = [], []
            for h in range(2 * GROUPS):
                lanes = slice((h // 2) * BLK, (h // 2 + 1) * BLK)
                log_beta, log_1m = _sb_logits(qm[h], k_s[pl.ds(start, CHUNK), lanes], valid)
                staged.append((log_beta, jnp.concatenate(_split_hi_lo(log_1m), axis=0)))
            for h in range(2 * GROUPS):
                lanes = slice((h // 2) * BLK, (h // 2 + 1) * BLK)
                log_beta, hi_lo = staged[h]
                run, acc = carry[2 * h], carry[2 * h + 1]
                parts = [None] * KC
                for b in reversed(range(KC)):
                    after, total = _scan_packed(hi_lo, b, mat)
                    parts[b] = after + run
                    run = run + total
                a = jnp.where(valid, jnp.exp(log_beta + jnp.concatenate(parts, axis=1)), 0.0)
                new += [run, acc + _dot(a.astype(MXU), v_s[pl.ds(start, CHUNK), lanes])]
            return tuple(new)

        zero = jnp.zeros((BLK, BLK), F32)
        res = lax.fori_loop(0, (i + KC) // KC, step, (zero,) * (4 * GROUPS))
        for g in range(GROUPS):
            lanes = slice(g * BLK, (g + 1) * BLK)
            o_ref[:, lanes] = jnp.where(lo, res[4 * g + 1], res[4 * g + 3])
            lt_ref[:, lanes] = jnp.where(lo, res[4 * g], res[4 * g + 2])

    ng = D // w
    blk = pl.BlockSpec((BLK, w), lambda hp, i: (i, hp))
    return pl.pallas_call(
        body, name="sb_fwd", grid=(ng, t // BLK),
        in_specs=[blk,
                  pl.BlockSpec((t, w), lambda hp, i: (0, ng + hp)),
                  pl.BlockSpec((t, w), lambda hp, i: (0, 2 * ng + hp))],
        out_specs=[blk, blk],
        out_shape=[jax.ShapeDtypeStruct((t, D), F32)] * 2,
        scratch_shapes=[pltpu.VMEM((t + SLACK, w), MXU), pltpu.VMEM((t + SLACK, w), MXU)],
        compiler_params=_cparams(("arbitrary", "arbitrary")),
    )(p1, p1, p1)


def _sb_out(o, p1, w_out, h1, g_post, tgt):
    t = o.shape[0]
    tm = _tile(t, 272)

    def body(o_ref, g_ref, w_ref, h_ref, gp_ref, t_ref,
             loss_ref, dh_ref, dy_ref, m_ref, do_ref, dg_ref, dgp_ref):
        i = pl.program_id(0)

        @pl.when(i == 0)
        def _():
            loss_ref[...] = jnp.zeros_like(loss_ref)
            dgp_ref[...] = jnp.zeros_like(dgp_ref)

        gate = g_ref[...]
        sg, dsg = _silu_and_grad(gate)
        ov = o_ref[...]
        m = (ov * sg).astype(MXU)
        y = _dot(m, w_ref[...])
        r = lax.rsqrt(jnp.mean(y * y, -1, keepdims=True) + EPS)
        yhat = y * r
        h2 = h_ref[...] + yhat * gp_ref[...]
        diff = jnp.where(_rows((tm, 1), i * tm) >= BLK, h2 - t_ref[...], 0.0)
        loss_ref[...] += jnp.full(loss_ref.shape, 0.5 / D, F32) * jnp.sum(diff * diff)
        dh = diff * (1.0 / D)
        dgp_ref[...] += jnp.sum(dh * yhat, 0, keepdims=True)
        dyn = dh * gp_ref[...]
        dy = (r * (dyn - yhat * jnp.mean(dyn * yhat, -1, keepdims=True))).astype(MXU)
        dm = _dot_nt(dy, w_ref[...])
        dh_ref[...] = dh
        dy_ref[...] = dy.astype(ACT)
        m_ref[...] = m.astype(ACT)
        do_ref[...] = dm * sg
        dg_ref[...] = (dm * ov * dsg).astype(ACT)

    row = lambda idx: pl.BlockSpec((tm, D), lambda i: (i, idx))
    full = lambda a: pl.BlockSpec(a.shape, lambda i: (0, 0))
    acc = lambda s: pl.BlockSpec(s, lambda i: (0, 0))
    return pl.pallas_call(
        body, name="sb_out", grid=(t // tm,),
        in_specs=[row(0), row(3), full(w_out), row(0), full(g_post), row(0)],
        out_specs=[acc((8, BLK)), row(0), row(0), row(0), row(0), row(0), acc((1, D))],
        out_shape=[jax.ShapeDtypeStruct((8, BLK), F32), jax.ShapeDtypeStruct((t, D), F32),
                   jax.ShapeDtypeStruct((t, D), ACT), jax.ShapeDtypeStruct((t, D), ACT),
                   jax.ShapeDtypeStruct((t, D), F32), jax.ShapeDtypeStruct((t, D), ACT),
                   jax.ShapeDtypeStruct((1, D), F32)],
        compiler_params=_cparams(("arbitrary",)),
    )(o, p1, w_out, h1, g_post, tgt)


def _sb_bwd(p1, ltot, do):
    t = p1.shape[0]
    nb = t // BLK

    def body(q_ref, k_ref, v_ref, lt_ref, do_ref, dq_ref, dk_ref, dv_ref, k_s, v_s, dk_s, dv_s):
        i = pl.program_id(1)
        lo = lax.broadcasted_iota(jnp.int32, (1, BLK), 1) < HEAD

        @pl.when(i == 0)
        def _():
            for src, dst in ((k_ref, k_s), (v_ref, v_s)):
                dst[0:t, :] = src[...].astype(MXU)
                dst[t:, :] = jnp.zeros((SLACK, BLK), MXU)
            dk_s[...] = jnp.zeros_like(dk_s)
            dv_s[...] = jnp.zeros_like(dv_s)

        q, dout, lt = q_ref[...] * SCALE, do_ref[...], lt_ref[...]
        halves = [lo, ~lo]
        qm = [jnp.where(h, q, 0.0).astype(MXU) for h in halves]
        dom = [jnp.where(h, dout, 0.0).astype(MXU) for h in halves]
        q2, do2 = jnp.concatenate(qm, axis=0), jnp.concatenate(dom, axis=0)
        lt_r = pltpu.roll(lt, HEAD, 1)
        row_total = [jnp.where(lo, lt, lt_r), jnp.where(lo, lt_r, lt)]
        mat_l = _scan_matrix(True)
        mat_g = _scan_matrix(False)

        def step(s, carry):
            start = pl.multiple_of(s * CHUNK, BLK)
            kc, vc = k_s[pl.ds(start, CHUNK), :], v_s[pl.ds(start, CHUNK), :]
            valid = _sb_valid(i, start)
            new, dzs, probs, st1, st2 = [], [], [], [], []
            for j in range(2):
                log_beta, log_1m = _sb_logits(qm[j], kc, valid)
                st1.append((log_beta, jnp.concatenate(_split_hi_lo(log_1m), axis=0)))
            for j in range(2):
                log_beta, hi_lo = st1[j]
                run = carry[3 * j]
                parts = []
                for b in range(KC):
                    after, total = _scan_packed(hi_lo, b, mat_l)
                    run = run + total
                    parts.append(after + (row_total[j] - run))
                a = jnp.where(valid, jnp.exp(log_beta + jnp.concatenate(parts, axis=1)), 0.0)
                g = _dot_nt(dom[j], vc) * a
                probs.append(a.astype(MXU))
                st2.append((run, g, jnp.concatenate(_split_hi_lo(g), axis=0)))
            for j in range(2):
                run, g, hi_lo = st2[j]
                run_g, dq = carry[3 * j + 1], carry[3 * j + 2]
                parts = []
                for b in range(KC):
                    before, total_g = _scan_packed(hi_lo, b, mat_g)
                    parts.append(before + run_g)
                    run_g = run_g + total_g
                sig = jnp.exp(st1[j][0])
                dz = jnp.where(valid, g * (1.0 - sig) - sig * jnp.concatenate(parts, axis=1), 0.0)
                dzm = dz.astype(MXU)
                dzs.append(dzm)
                new += [run, run_g, dq + _dot(dzm, kc)]
            dk_s[pl.ds(start, CHUNK), :] += _dot_tn(jnp.concatenate(dzs, axis=0), q2)
            dv_s[pl.ds(start, CHUNK), :] += _dot_tn(jnp.concatenate(probs, axis=0), do2)
            return tuple(new)

        zero = jnp.zeros((BLK, BLK), F32)
        res = lax.fori_loop(0, (i + KC) // KC, step, (zero,) * 6)
        dq_ref[...] = (jnp.where(lo, res[2], res[5]) * SCALE).astype(ACT)

        @pl.when(i == nb - 1)
        def _():
            dk_ref[...] = dk_s[0:t, :].astype(ACT)
            dv_ref[...] = dv_s[0:t, :].astype(ACT)

    blk = lambda off: pl.BlockSpec((BLK, BLK), lambda hp, i: (i, off + hp))
    col = lambda off: pl.BlockSpec((t, BLK), lambda hp, i: (0, off + hp))
    return pl.pallas_call(
        body, name="sb_bwd", grid=(8, nb),
        in_specs=[blk(0), col(8), col(16), blk(0), blk(0)],
        out_specs=[blk(0), col(0), col(0)],
        out_shape=[jax.ShapeDtypeStruct((t, D), ACT)] * 3,
        scratch_shapes=[pltpu.VMEM((t + SLACK, BLK), MXU), pltpu.VMEM((t + SLACK, BLK), MXU),
                        pltpu.VMEM((t + SLACK, BLK), F32), pltpu.VMEM((t + SLACK, BLK), F32)],
        compiler_params=_cparams(("arbitrary", "arbitrary")),
    )(p1, p1, p1, ltot, do)


def _mid_bwd(dp1, w_sb, h1, g_pre1, dh2, y0, g_post0, w_out, p0, att, c1, w_pw2):
    t = h1.shape[0]
    tm = _tile(t, 272)

    def body(*refs):
        d_refs = refs[:4]
        (w_ref, h_ref, g1_ref, dh2_ref, y_ref, g0_ref, wo_ref, ga_ref, gb_ref, att_ref, c1_ref,
         pw_ref, dh1_ref, dy_ref, dga_ref, dgb_ref, datt_ref, dc1_ref, dc2_ref, dg1_ref, dg0_ref,
         acc) = refs[4:]
        i, j = pl.program_id(0), pl.program_id(1)

        @pl.when((i == 0) & (j == 0))
        def _():
            dg1_ref[...] = jnp.zeros_like(dg1_ref)
            dg0_ref[...] = jnp.zeros_like(dg0_ref)

        @pl.when(j == 0)
        def _():
            acc[...] = jnp.zeros_like(acc)

        def add(ref):
            acc[...] += _dot_nt(ref[...].astype(MXU), w_ref[...])
        dp1.apply(j, d_refs, add)

        @pl.when(j == 7)
        def _():
            dhn = acc[...]
            x = h_ref[...]
            r = lax.rsqrt(jnp.mean(x * x, -1, keepdims=True) + EPS)
            xhat = x * r
            dg1_ref[...] += jnp.sum(dhn * xhat, 0, keepdims=True)
            dxn = dhn * g1_ref[...]
            dh1 = dh2_ref[...] + r * (dxn - xhat * jnp.mean(dxn * xhat, -1, keepdims=True))
            dh1_ref[...] = dh1
            y = y_ref[...]
            ry = lax.rsqrt(jnp.mean(y * y, -1, keepdims=True) + EPS)
            yhat = y * ry
            dg0_ref[...] += jnp.sum(dh1 * yhat, 0, keepdims=True)
            dyn = dh1 * g0_ref[...]
            dy = (ry * (dyn - yhat * jnp.mean(dyn * yhat, -1, keepdims=True))).astype(MXU)
            dy_ref[...] = dy.astype(ACT)
            dmix = _dot_nt(dy, wo_ref[...])
            da, dc = dmix[:, :512], dmix[:, 512:]
            sga, dsga = _silu_and_grad(ga_ref[...])
            sgb, dsgb = _silu_and_grad(gb_ref[...])
            datt_ref[...] = da * sga
            dga_ref[...] = (da * att_ref[...] * dsga).astype(ACT)
            c2 = _dot(c1_ref[...].astype(MXU), pw_ref[...])
            dc2 = (dc * sgb).astype(MXU)
            dgb_ref[...] = (dc * c2 * dsgb).astype(ACT)
            dc2_ref[...] = dc2.astype(ACT)
            dc1_ref[...] = _dot_nt(dc2, pw_ref[...])

    row = lambda w, idx: pl.BlockSpec((tm, w), lambda i, j: (i, idx))
    full = lambda a: pl.BlockSpec(a.shape, lambda i, j: (0, 0))
    acc_spec = pl.BlockSpec((1, D), lambda i, j: (0, 0))
    sd = jax.ShapeDtypeStruct
    return pl.pallas_call(
        body, name="mid_bwd", grid=(t // tm, 8),
        in_specs=dp1.specs(tm, lambda i, j: i, lambda i, j: j) + [
            pl.BlockSpec((None, D, 512), lambda i, j: (j, 0, 0)),
            row(D, 0), full(g_pre1), row(D, 0), row(D, 0), full(g_post0), full(w_out),
            row(512, 3), row(512, 4), row(512, 0), row(512, 0), full(w_pw2)],
        out_specs=[row(D, 0), row(D, 0), row(512, 0), row(512, 0), row(512, 0), row(512, 0),
                   row(512, 0), acc_spec, acc_spec],
        out_shape=[sd((t, D), F32), sd((t, D), ACT), sd((t, 512), ACT), sd((t, 512), ACT),
                   sd((t, 512), F32), sd((t, 512), F32), sd((t, 512), ACT),
                   sd((1, D), F32), sd((1, D), F32)],
        scratch_shapes=[pltpu.VMEM((tm, D), F32)],
        compiler_params=_cparams(("arbitrary", "arbitrary")),
    )(*dp1.arrays, w_sb, h1, g_pre1, dh2, y0, g_post0, w_out, p0, p0, att, c1, w_pw2)


def _conv_bwd(p0, dc1, conv_w, conv_b, ln_g, ln_b):
    t = p0.shape[0]
    tm = _tile(t, 544)
    hb = tm // HALO
    last = t // HALO - 1

    def body(cur_ref, prev_ref, next_ref, d_ref, dn_ref, w_ref, b_ref, g_ref, bb_ref,
             dglu_ref, dw_ref, db_ref, dlg_ref, dlb_ref):
        i = pl.program_id(0)

        @pl.when(i == 0)
        def _():
            for ref in (dw_ref, db_ref, dlg_ref, dlb_ref):
                ref[...] = jnp.zeros_like(ref)

        glu = jnp.concatenate([prev_ref[...], cur_ref[...], next_ref[...]], axis=0)
        rw = _rows((tm + 2 * HALO, 1), i * tm - HALO)
        ga, sg = glu[:, :512], _sigmoid(glu[:, 512:])
        u_w = jnp.where((rw >= PAD) & (rw < t), ga * sg, 0.0)
        n_cv = tm + HALO
        cv = _conv_window(u_w, w_ref, n_cv, HALO - (CONV_W - 1)) + b_ref[...]
        xc = cv - jnp.mean(cv, -1, keepdims=True)
        rstd = lax.rsqrt(jnp.mean(xc * xc, -1, keepdims=True) + LN_EPS)
        cvhat = xc * rstd
        ln = cvhat * g_ref[...] + bb_ref[...]
        _, dsl = _silu_and_grad(ln)
        rc = _rows((n_cv, 1), i * tm)
        dc = jnp.concatenate([d_ref[...], dn_ref[...]], axis=0)
        dln = jnp.where(rc < t, dc * dsl, 0.0)
        dhat = dln * g_ref[...]
        dcv = rstd * (dhat - jnp.mean(dhat, -1, keepdims=True)
                      - cvhat * jnp.mean(dhat * cvhat, -1, keepdims=True))
        own = dcv[:tm]
        dlg_ref[...] += jnp.sum((dln * cvhat)[:tm], 0, keepdims=True)
        dlb_ref[...] += jnp.sum(dln[:tm], 0, keepdims=True)
        db_ref[...] += jnp.sum(own, 0, keepdims=True)
        rows = tm + 2 * HALO
        du = None
        for j in range(CONV_W):
            first = HALO - (CONV_W - 1) + j
            shifted = pltpu.roll(u_w, (rows - first) % rows, 0)[:tm]
            dw_ref[j:j + 1, :] += jnp.sum(own * shifted, 0, keepdims=True)
            back = pltpu.roll(dcv, (n_cv - (CONV_W - 1 - j)) % n_cv, 0)[:tm]
            term = back * w_ref[j:j + 1, :]
            du = term if du is None else du + term
        du = jnp.where(_rows((tm, 1), i * tm) >= PAD, du, 0.0)
        ga_c, sg_c = ga[HALO:HALO + tm], sg[HALO:HALO + tm]
        dglu_ref[:, :512] = (du * sg_c).astype(ACT)
        dglu_ref[:, 512:] = (du * ga_c * sg_c * (1.0 - sg_c)).astype(ACT)

    vec = pl.BlockSpec((1, 512), lambda i: (0, 0))
    nxt = lambda i: (jnp.minimum((i + 1) * hb, last), 0)
    return pl.pallas_call(
        body, name="conv_bwd", grid=(t // tm,),
        in_specs=[pl.BlockSpec((tm, D), lambda i: (i, 0)),
                  pl.BlockSpec((HALO, D), lambda i: (jnp.maximum(i * hb - 1, 0), 0)),
                  pl.BlockSpec((HALO, D), nxt),
                  pl.BlockSpec((tm, 512), lambda i: (i, 0)),
                  pl.BlockSpec((HALO, 512), nxt),
                  pl.BlockSpec((CONV_W, 512), lambda i: (0, 0)), vec, vec, vec],
        out_specs=[pl.BlockSpec((tm, D), lambda i: (i, 0)),
                   pl.BlockSpec((HALO, 512), lambda i: (0, 0)), vec, vec, vec],
        out_shape=[jax.ShapeDtypeStruct((t, D), ACT), jax.ShapeDtypeStruct((HALO, 512), F32)]
        + [jax.ShapeDtypeStruct((1, 512), F32)] * 3,
        compiler_params=_cparams(("arbitrary",)),
    )(p0, p0, p0, dc1, dc1, conv_w, conv_b, ln_g, ln_b)


def _swa_bwd(p0, datt, sinks, tables):
    t = p0.shape[0]
    nb = t // BLK

    def body(sink_ref, q_ref, kv_ref, d_ref, cos_ref, sa_ref, sb_ref,
             dq_ref, dkv_ref, ds_ref, acc):
        n = pl.program_id(0)

        @pl.when(n == 0)
        def _():
            acc[...] = jnp.zeros_like(acc)
            ds_ref[...] = jnp.zeros_like(ds_ref)

        kd, vd, lo = _swa_keys(kv_ref, n)
        mask = _swa_mask(n)
        row0 = pl.multiple_of(n * BLK, BLK)
        tabs = [r[pl.ds(row0, BLK), :] for r in (cos_ref, sa_ref, sb_ref)]
        dk_g = [jnp.zeros((3 * BLK, BLK), F32), jnp.zeros((3 * BLK, BLK), F32)]
        dv_g = [jnp.zeros((3 * BLK, BLK), F32), jnp.zeros((3 * BLK, BLK), F32)]
        for p in range(4):
            g = p // 2
            qp = q_ref[:, p * BLK:(p + 1) * BLK]
            dp_ = d_ref[:, p * BLK:(p + 1) * BLK]
            dqs = []
            for j in range(2):
                half = lo if j == 0 else ~lo
                qm = jnp.where(half, qp, 0.0).astype(MXU)
                dom = jnp.where(half, dp_, 0.0).astype(MXU)
                pr, p_sink = _swa_probs(qm, kd[g], mask, sink_ref[2 * p + j])
                dpr = _dot_nt(dom, vd[g])
                delta = jnp.sum(pr * dpr, -1, keepdims=True)
                dsc = (pr * (dpr - delta) * SCALE).astype(MXU)
                ds_ref[2 * p + j:2 * p + j + 1, :] += jnp.full((1, BLK), -1.0, F32) * jnp.sum(p_sink * delta)
                dqs.append(_dot(dsc, kd[g]))
                dk_g[g] = dk_g[g] + _dot_tn(dsc, qm)
                dv_g[g] = dv_g[g] + _dot_tn(pr.astype(MXU), dom)
            dq_ref[:, p * BLK:(p + 1) * BLK] = _unrope(jnp.where(lo, dqs[0], dqs[1]), *tabs).astype(ACT)
        fold = lambda a: a + pltpu.roll(a, HEAD, 1)
        dk = jnp.where(lo, fold(dk_g[0]), fold(dk_g[1]))
        dv = jnp.where(lo, fold(dv_g[0]), fold(dv_g[1]))
        dkv = jnp.concatenate([dk, dv], axis=1)
        prev = pl.multiple_of(jnp.maximum(n - 1, 0) * BLK, BLK)
        acc[0:BLK, :] += dkv[0:BLK]
        acc[pl.ds(prev, BLK), :] += dkv[BLK:2 * BLK]
        acc[pl.ds(row0, BLK), :] += dkv[2 * BLK:]

        @pl.when(n == nb - 1)
        def _():
            dkv_ref[:, :BLK] = _unrope(acc[:, :BLK], cos_ref[...], sa_ref[...], sb_ref[...]).astype(ACT)
            dkv_ref[:, BLK:] = acc[:, BLK:].astype(ACT)

    tab = pl.BlockSpec((t, BLK), lambda n: (0, 0))
    return pl.pallas_call(
        body, name="swa_bwd", grid=(nb,),
        in_specs=[pl.BlockSpec(memory_space=pltpu.SMEM),
                  pl.BlockSpec((BLK, 512), lambda n: (n, 2)),
                  pl.BlockSpec((t, 256), lambda n: (0, 10)),
                  pl.BlockSpec((BLK, 512), lambda n: (n, 0)), tab, tab, tab],
        out_specs=[pl.BlockSpec((BLK, 512), lambda n: (n, 0)),
                   pl.BlockSpec((t, 256), lambda n: (0, 0)),
                   pl.BlockSpec((8, BLK), lambda n: (0, 0))],
        out_shape=[jax.ShapeDtypeStruct((t, 512), ACT), jax.ShapeDtypeStruct((t, 256), ACT),
                   jax.ShapeDtypeStruct((8, BLK), F32)],
        scratch_shapes=[pltpu.VMEM((t, 256), F32)],
        compiler_params=_cparams(("arbitrary",)),
    )(sinks, p0, p0, datt, *tables)


def _ab_in_bwd(dp0, w_t, h0, g_pre, dh1):
    t = h0.shape[0]
    tm = _tile(t, 544)

    def body(*refs):
        d_refs = refs[:5]
        w_ref, h_ref, g_ref, dh1_ref, dh0_ref, dg_ref, acc = refs[5:]
        i, j = pl.program_id(0), pl.program_id(1)

        @pl.when((i == 0) & (j == 0))
        def _():
            dg_ref[...] = jnp.zeros_like(dg_ref)

        @pl.when(j == 0)
        def _():
            acc[...] = jnp.zeros_like(acc)

        def add(ref):
            acc[...] += _dot(ref[...].astype(MXU), w_ref[...])
        dp0.apply(j, d_refs, add)

        @pl.when(j == 10)
        def _():
            dhn = acc[...]
            x = h_ref[...]
            r = lax.rsqrt(jnp.mean(x * x, -1, keepdims=True) + EPS)
            xhat = x * r
            dg_ref[...] += jnp.sum(dhn * xhat, 0, keepdims=True)
            dxn = dhn * g_ref[...]
            dh0_ref[...] = dh1_ref[...] + r * (dxn - xhat * jnp.mean(dxn * xhat, -1, keepdims=True))

    row = pl.BlockSpec((tm, D), lambda i, j: (i, 0))
    vec = pl.BlockSpec((1, D), lambda i, j: (0, 0))
    return pl.pallas_call(
        body, name="ab_in_bwd", grid=(t // tm, 11),
        in_specs=dp0.specs(tm, lambda i, j: i, lambda i, j: j) + [
            pl.BlockSpec((256, D), lambda i, j: (j, 0)), row, vec, row],
        out_specs=[row, vec],
        out_shape=[jax.ShapeDtypeStruct((t, D), F32), jax.ShapeDtypeStruct((1, D), F32)],
        scratch_shapes=[pltpu.VMEM((tm, D), F32)],
        compiler_params=_cparams(("arbitrary", "arbitrary")),
    )(*dp0.arrays, w_t, h0, g_pre, dh1)


def _dw_plain(a, b, name):
    t, m = a.shape
    n = b.shape[1]
    tm = _tile(t, 544)
    tn = min(n, 512)
    nk = t // tm

    def body(a_ref, b_ref, o_ref, acc):
        k = pl.program_id(1)

        @pl.when(k == 0)
        def _():
            acc[...] = jnp.zeros_like(acc)

        acc[...] += _dot_tn(a_ref[...].astype(MXU), b_ref[...].astype(MXU))

        @pl.when(k == nk - 1)
        def _():
            o_ref[...] = acc[...].astype(WIRE)

    return pl.pallas_call(
        body, name=name, grid=(n // tn, nk),
        in_specs=[pl.BlockSpec((tm, m), lambda j, k: (k, 0)),
                  pl.BlockSpec((tm, tn), lambda j, k: (k, j))],
        out_specs=pl.BlockSpec((m, tn), lambda j, k: (0, j)),
        out_shape=jax.ShapeDtypeStruct((m, n), WIRE),
        scratch_shapes=[pltpu.VMEM((m, tn), F32)],
        compiler_params=_cparams(("arbitrary", "arbitrary")),
    )(a, b)


def _dw_chunks(hn, dp, name):
    t = hn.shape[0]
    tm = _tile(t, 544)
    nk = t // tm
    nt, tw = dp.n_tiles, dp.tw
    n_in = len(dp.arrays)

    def body(*refs):
        d_refs = refs[:n_in]
        h_ref, o_ref, acc = refs[n_in:]
        j, k = pl.program_id(0), pl.program_id(1)

        @pl.when(k == 0)
        def _():
            acc[...] = jnp.zeros_like(acc)

        def add(ref):
            acc[...] += _dot_tn(h_ref[...].astype(MXU), ref[...].astype(MXU))
        dp.apply(j, d_refs, add)

        @pl.when(k == nk - 1)
        def _():
            o_ref[...] = acc[...].astype(WIRE)

    return pl.pallas_call(
        body, name=name, grid=(nt, nk),
        in_specs=dp.specs(tm, lambda j, k: k, lambda j, k: j) + [
            pl.BlockSpec((tm, D), lambda j, k: (k, 0))],
        out_specs=pl.BlockSpec((None, D, tw), lambda j, k: (j, 0, 0)),
        out_shape=jax.ShapeDtypeStruct((nt, D, tw), WIRE),
        scratch_shapes=[pltpu.VMEM((D, tw), F32)],
        compiler_params=_cparams(("arbitrary", "arbitrary")),
    )(*dp.arrays, hn)


def _dw_transposed(dp, hn, name):
    t = hn.shape[0]
    tm = _tile(t, 544)
    nk = t // tm
    nt, tw = dp.n_tiles, dp.tw
    n_in = len(dp.arrays)

    def body(*refs):
        d_refs = refs[:n_in]
        h_ref, o_ref, acc = refs[n_in:]
        j, k = pl.program_id(0), pl.program_id(1)

        @pl.when(k == 0)
        def _():
            acc[...] = jnp.zeros_like(acc)

        def add(ref):
            acc[...] += _dot_tn(ref[...].astype(MXU), h_ref[...].astype(MXU))
        dp.apply(j, d_refs, add)

        @pl.when(k == nk - 1)
        def _():
            o_ref[...] = acc[...].astype(WIRE)

    return pl.pallas_call(
        body, name=name, grid=(nt, nk),
        in_specs=dp.specs(tm, lambda j, k: k, lambda j, k: j) + [
            pl.BlockSpec((tm, D), lambda j, k: (k, 0))],
        out_specs=pl.BlockSpec((tw, D), lambda j, k: (j, 0)),
        out_shape=jax.ShapeDtypeStruct((nt * tw, D), WIRE),
        scratch_shapes=[pltpu.VMEM((tw, D), F32)],
        compiler_params=_cparams(("arbitrary", "arbitrary")),
    )(*dp.arrays, hn)


def kernel(x, meta_tokens, ab_pre_norm, ab_w_in, ab_sinks, ab_conv_w, ab_conv_b, ab_conv_ln_g, ab_conv_ln_b, ab_w_pw2, ab_w_out, ab_post_norm, sb_pre_norm, sb_w_in, sb_w_out, sb_post_norm, loss_target, m_meta_tokens, m_ab_pre_norm, m_ab_w_in, m_ab_sinks, m_ab_conv_w, m_ab_conv_b, m_ab_conv_ln_g, m_ab_conv_ln_b, m_ab_w_pw2, m_ab_w_out, m_ab_post_norm, m_sb_pre_norm, m_sb_w_in, m_sb_w_out, m_sb_post_norm, v_meta_tokens, v_ab_pre_norm, v_ab_w_in, v_ab_sinks, v_ab_conv_w, v_ab_conv_b, v_ab_conv_ln_g, v_ab_conv_ln_b, v_ab_w_pw2, v_ab_w_out, v_ab_post_norm, v_sb_pre_norm, v_sb_w_in, v_sb_w_out, v_sb_post_norm):
    seq = x.shape[1]
    t = seq + BLK
    mx, my, mc = _coords()
    me = 4 * mx + 2 * my + mc
    pos = jnp.stack([mx, my, mc, me]).astype(jnp.int32)

    w_ab_t, w_sb, w_oa, w_os, w_pw = _all_gather(
        [ab_w_in[0].T.astype(WIRE), sb_w_in[0].astype(WIRE), ab_w_out[0].astype(WIRE),
         sb_w_out[0].astype(WIRE), ab_w_pw2[0].astype(WIRE)], "gather_weights")
    w_ab_t = w_ab_t.reshape(2816, D)
    w_oa = w_oa.reshape(D, D)
    w_os = w_os.reshape(D, D)
    w_pw = w_pw.reshape(512, 512)
    small = _all_gather([meta_tokens, ab_conv_w[0], sb_pre_norm, sb_post_norm], "gather_small")
    meta_full = jnp.moveaxis(small[0], 0, 1).reshape(N_META, D)
    conv_w = jnp.moveaxis(small[1], 0, 1).reshape(CONV_W, 512)
    sb_pre = jnp.moveaxis(small[2], 0, 1).reshape(1, D)
    sb_post = jnp.moveaxis(small[3], 0, 1).reshape(1, D)

    h0 = jnp.concatenate([jnp.zeros((PAD, D), F32), meta_full, x[0]], axis=0)
    tgt = jnp.concatenate([jnp.zeros((BLK, D), F32), loss_target[0]], axis=0)
    tables = _rope_tables(t)
    sinks = ab_sinks[0]

    p0, hn0 = _ab_in(h0, ab_pre_norm, w_ab_t, tables)
    att = _swa_fwd(p0, sinks)
    c1 = _conv_fwd(p0, conv_w, ab_conv_b, ab_conv_ln_g, ab_conv_ln_b)
    h1, y0, mix = _ab_out(h0, p0, att, c1, w_pw, w_oa, ab_post_norm)
    p1, hn1 = _sb_in(h1, sb_pre, w_sb)
    o, ltot = _sb_fwd(p1)
    loss_part, dh2, dy1, m1, do, dgate, dg_sb_post = _sb_out(o, p1, w_os, h1, sb_post, tgt)

    dq1, dk1, dv1 = _sb_bwd(p1, ltot, do)
    dp1 = _Cols([(dq1, 0, 2), (dk1, 2, 2), (dv1, 4, 2), (dgate, 6, 2)], 512)
    dh1, dy0, dga, dgb, datt, dc1, dc2, dg_sb_pre, dg_ab_post = _mid_bwd(
        dp1, w_sb, h1, sb_pre, dh2, y0, ab_post_norm, w_oa, p0, att, c1, w_pw)
    dglu, dconv_w, dconv_b, dln_g, dln_b = _conv_bwd(p0, dc1, conv_w, ab_conv_b, ab_conv_ln_g, ab_conv_ln_b)
    dq0, dkv0, dsinks = _swa_bwd(p0, datt, sinks, tables)
    dp0 = _Cols([(dq0, 0, 2), (dkv0, 2, 1), (dga, 3, 2), (dglu, 5, 4), (dgb, 9, 2)], 256)
    dh0, dg_ab_pre = _ab_in_bwd(dp0, w_ab_t, h0, ab_pre_norm, dh1)

    parts = [
        _dw_transposed(dp0, hn0, "dw_ab_in").reshape(4, 2, 352, D),
        _dw_chunks(hn1, dp1, "dw_sb_in").reshape(4, 2, D, 512),
        _dw_plain(mix, dy0, "dw_ab_out").reshape(4, 2, BLK, D),
        _dw_plain(m1, dy1, "dw_sb_out").reshape(4, 2, BLK, D),
        _dw_plain(c1, dc2, "dw_pw2").reshape(4, 2, 64, 512),
    ]
    names = ["ab_in", "sb_in", "ab_out", "sb_out", "pw2"]
    from_sibling = _exchange_sibling(parts, "reduce_sibling")
    chip_sums = [_add_sibling(pos, p, r, "add_sibling_" + nm) for p, r, nm in zip(parts, from_sibling, names)]
    from_chips = _exchange_chips(chip_sums, "reduce_chips")
    big = [_sum_chips(pos, s, r, "sum_chips_" + nm) for s, r, nm in zip(chip_sums, from_chips, names)]
    g_ab_w_in = big[0].T
    g_sb_w_in, g_ab_w_out, g_sb_w_out, g_ab_w_pw2 = big[1:]

    small_parts = [dh0[PAD:BLK], dg_ab_pre, dsinks, dconv_w, dconv_b, dln_g, dln_b,
                   dg_ab_post, dg_sb_pre, dg_sb_post]
    red = _reduce_small(_all_gather(small_parts, "gather_small_grads"), "reduce_small")
    col = lambda a, w: lax.dynamic_slice_in_dim(a, me * w, w, axis=1)
    g_meta = col(red[0], BLK)
    g_ab_pre = red[1]
    g_sinks = red[2][:, 0].reshape(1, 8)
    g_conv_w = col(red[3][:CONV_W], 64)
    g_conv_b, g_ln_g, g_ln_b, g_ab_post = red[4], red[5], red[6], red[7]
    g_sb_pre, g_sb_post = col(red[8], BLK), col(red[9], BLK)

    loss = lax.psum(loss_part[0, 0], ("x", "y", "c"))
    grad_x = dh0[BLK:][None]

    weights = [meta_tokens, ab_pre_norm, ab_w_in[0], ab_sinks, ab_conv_w[0], ab_conv_b, ab_conv_ln_g,
               ab_conv_ln_b, ab_w_pw2[0], ab_w_out[0], ab_post_norm, sb_pre_norm, sb_w_in[0],
               sb_w_out[0], sb_post_norm]
    grads = [g_meta, g_ab_pre, g_ab_w_in, g_sinks, g_conv_w, g_conv_b, g_ln_g, g_ln_b, g_ab_w_pw2,
             g_ab_w_out, g_ab_post, g_sb_pre, g_sb_w_in, g_sb_w_out, g_sb_post]
    ms = [m_meta_tokens, m_ab_pre_norm, m_ab_w_in[0], m_ab_sinks, m_ab_conv_w[0], m_ab_conv_b,
          m_ab_conv_ln_g, m_ab_conv_ln_b, m_ab_w_pw2[0], m_ab_w_out[0], m_ab_post_norm,
          m_sb_pre_norm, m_sb_w_in[0], m_sb_w_out[0], m_sb_post_norm]
    vs = [v_meta_tokens, v_ab_pre_norm, v_ab_w_in[0], v_ab_sinks, v_ab_conv_w[0], v_ab_conv_b,
          v_ab_conv_ln_g, v_ab_conv_ln_b, v_ab_w_pw2[0], v_ab_w_out[0], v_ab_post_norm,
          v_sb_pre_norm, v_sb_w_in[0], v_sb_w_out[0], v_sb_post_norm]
    lead = [w.ndim == 3 for w in (meta_tokens, ab_pre_norm, ab_w_in, ab_sinks, ab_conv_w, ab_conv_b,
                                   ab_conv_ln_g, ab_conv_ln_b, ab_w_pw2, ab_w_out, ab_post_norm,
                                   sb_pre_norm, sb_w_in, sb_w_out, sb_post_norm)]
    big_ids = [2, 8, 9, 12, 13]
    small_ids = [i for i in range(15) if i not in big_ids]
    deltas, new_m, new_v = [None] * 15, [None] * 15, [None] * 15
    for ids, nm in ((small_ids, "adamw_small"), (big_ids, "adamw_big")):
        d_, m_, v_ = _adamw([weights[i] for i in ids], [grads[i] for i in ids],
                            [ms[i] for i in ids], [vs[i] for i in ids], nm)
        for k, i in enumerate(ids):
            deltas[i], new_m[i], new_v[i] = d_[k], m_[k], v_[k]
    fix = lambda arrs: [a[None] if l else a for a, l in zip(arrs, lead)]
    return (loss, grad_x, *fix(grads), *fix(deltas), *fix(new_m), *fix(new_v))
```

```python
import functools

import numpy as np
import jax
import jax.numpy as jnp
from jax import lax
from jax.experimental import pallas as pl
from jax.experimental.pallas import tpu as pltpu

F32 = jnp.float32
MXU = jnp.bfloat16
ACT = jnp.bfloat16
WIRE = jnp.bfloat16

D = 1024
N_META = 16
BLK = 128
PAD = BLK - N_META
HEAD = 64
NEG = -1e30
EPS = 1e-6
LN_EPS = 1e-5
ROPE_THETA = 10000.0
SCALE = HEAD ** -0.5
CONV_W = 31
HALO = 32
LR, B1, B2, ADAM_EPS, WD, STEP = 0.001, 0.9, 0.999, 1e-08, 0.01, 10
VMEM_LIMIT = 56 * 1024 * 1024
MESH = pl.DeviceIdType.MESH

P0_SRC = (5, 6, 7, 8, 0, 1, 3, 4, 9, 10, 2)


def _cparams(sem=None):
    return pltpu.CompilerParams(dimension_semantics=sem, vmem_limit_bytes=VMEM_LIMIT)


def _tile(t, pref):
    for cand in (pref, 544, 272, 128):
        if cand <= pref and t % cand == 0:
            return cand
    raise ValueError(t)


def _sigmoid(x):
    return 1.0 / (1.0 + jnp.exp(-x))


def _silu_and_grad(x):
    s = _sigmoid(x)
    return x * s, s * (1.0 + x * (1.0 - s))


def _dot(a, b):
    return jnp.dot(a, b, preferred_element_type=F32)


def _dot_nt(a, b):
    return lax.dot_general(a, b, (((1,), (1,)), ((), ())), preferred_element_type=F32)


def _dot_tn(a, b):
    return lax.dot_general(a, b, (((0,), (0,)), ((), ())), preferred_element_type=F32)


def _rows(shape, base):
    return base + lax.broadcasted_iota(jnp.int32, shape, 0)


def _rope_tables(t):
    half = HEAD // 2
    inv = ROPE_THETA ** (-np.arange(half, dtype=np.float32) / half)
    pos = (np.arange(t) - PAD).astype(np.float32)
    ang = pos[:, None] * inv[None, :]
    lane = np.arange(BLK)
    cos = np.cos(ang)[:, lane % half].astype(np.float32)
    sin = np.sin(ang)[:, lane % half].astype(np.float32)
    first = (lane % HEAD) < half
    sin_a = np.where(first[None, :], -sin, 0.0).astype(np.float32)
    sin_b = np.where(first[None, :], 0.0, sin).astype(np.float32)
    return jnp.asarray(cos), jnp.asarray(sin_a), jnp.asarray(sin_b)


def _rope(v, cos, sin_a, sin_b):
    return v * cos + pltpu.roll(v, 96, 1) * sin_a + pltpu.roll(v, 32, 1) * sin_b


def _unrope(v, cos, sin_a, sin_b):
    return v * cos - pltpu.roll(v, 96, 1) * sin_a - pltpu.roll(v, 32, 1) * sin_b


def _coords():
    return lax.axis_index("x"), lax.axis_index("y"), lax.axis_index("c")


def _all_gather(arrs, name):
    plan = _GatherPlan(arrs)

    def body(*refs):
        plan.begin(refs)
        plan.end(refs)

    return pl.pallas_call(
        body, name=name, out_shape=plan.out_shape,
        in_specs=plan.specs, out_specs=plan.specs, scratch_shapes=plan.scratch,
    )(*arrs)


class _GatherPlan:
    def __init__(self, arrs):
        n = self.n = len(arrs)
        self.out_shape = [jax.ShapeDtypeStruct((8,) + a.shape, a.dtype) for a in arrs]
        self.specs = [pl.BlockSpec(memory_space=pl.ANY)] * n
        self.scratch = [pltpu.SemaphoreType.DMA((n, 7)), pltpu.SemaphoreType.DMA((n, 7)),
                        pltpu.SemaphoreType.DMA((n,))]

    def _copies(self, refs):
        n = self.n
        ins, outs = refs[:n], refs[n:2 * n]
        send_sems, recv_sems, local_sems = refs[2 * n:]
        x, y, c = _coords()
        me, sibling = (x, y, c), (x, y, 1 - c)
        chips = [(1 - x, y), (x, 1 - y), (1 - x, 1 - y)]

        def copy(a, k, block, to, src=None):
            dst = outs[a].at[4 * block[0] + 2 * block[1] + block[2]]
            return pltpu.make_async_remote_copy(
                src_ref=dst if src is None else src, dst_ref=dst,
                send_sem=send_sems.at[a, k], recv_sem=recv_sems.at[a, k],
                device_id=to, device_id_type=MESH)

        mine = [pltpu.make_async_copy(ins[a], outs[a].at[4 * x + 2 * y + c], local_sems.at[a])
                for a in range(n)]
        first = []
        for a in range(n):
            first.append(copy(a, 0, me, sibling, src=ins[a]))
            for j, chip in enumerate(chips):
                first.append(copy(a, 1 + j, me, (*chip, c), src=ins[a]))
        return copy, mine, first, (me, sibling, chips, c)

    def begin(self, refs):
        _, mine, first, _ = self._copies(refs)
        for cp in mine + first:
            cp.start()

    def end(self, refs):
        copy, mine, first, (me, sibling, chips, c) = self._copies(refs)
        passed = []
        for j, chip in enumerate(chips):
            for a in range(self.n):
                copy(a, 1 + j, (*chip, c), me).wait_recv()
                cp = copy(a, 4 + j, (*chip, c), sibling)
                cp.start()
                passed.append(cp)
        for a in range(self.n):
            copy(a, 0, sibling, me).wait_recv()
            for j, chip in enumerate(chips):
                copy(a, 4 + j, (*chip, 1 - c), me).wait_recv()
        for cp in first + passed:
            cp.wait_send()
        for cp in mine:
            cp.wait()


class _ChipsPlan:
    def __init__(self, sums):
        n = self.n = len(sums)
        self.out_shape = [jax.ShapeDtypeStruct((3,) + s.shape[1:], s.dtype) for s in sums]
        self.specs = [pl.BlockSpec(memory_space=pl.ANY)] * n
        self.scratch = [pltpu.SemaphoreType.DMA((n, 3)), pltpu.SemaphoreType.DMA((n, 3))]

    def _copies(self, refs):
        n = self.n
        ins, outs = refs[:n], refs[n:2 * n]
        send_sems, recv_sems = refs[2 * n:]
        x, y, c = _coords()
        chips = [(1 - x, y), (x, 1 - y), (1 - x, 1 - y)]
        return [pltpu.make_async_remote_copy(
            src_ref=ins[a].at[2 * chip[0] + chip[1]], dst_ref=outs[a].at[k],
            send_sem=send_sems.at[a, k], recv_sem=recv_sems.at[a, k],
            device_id=(*chip, c), device_id_type=MESH)
            for a in range(n) for k, chip in enumerate(chips)]

    def begin(self, refs):
        for cp in self._copies(refs):
            cp.start()

    def end(self, refs):
        for cp in self._copies(refs):
            cp.wait()


def _exchange_sibling(parts, name):
    n = len(parts)

    def body(*refs):
        ins, outs = refs[:n], refs[n:2 * n]
        send_sems, recv_sems = refs[2 * n:]
        x, y, c = _coords()
        copies = [pltpu.make_async_remote_copy(
            src_ref=ins[a].at[:, 1 - c], dst_ref=outs[a],
            send_sem=send_sems.at[a], recv_sem=recv_sems.at[a],
            device_id=(x, y, 1 - c), device_id_type=MESH) for a in range(n)]
        for cp in copies:
            cp.start()
        for cp in copies:
            cp.wait()

    any_spec = pl.BlockSpec(memory_space=pl.ANY)
    return pl.pallas_call(
        body, name=name,
        out_shape=[jax.ShapeDtypeStruct((4,) + p.shape[2:], p.dtype) for p in parts],
        in_specs=[any_spec] * n, out_specs=[any_spec] * n,
        scratch_shapes=[pltpu.SemaphoreType.DMA((n,)), pltpu.SemaphoreType.DMA((n,))],
    )(*parts)


def _exchange_chips(sums, name):
    plan = _ChipsPlan(sums)

    def body(*refs):
        plan.begin(refs)
        plan.end(refs)

    return pl.pallas_call(
        body, name=name, out_shape=plan.out_shape,
        in_specs=plan.specs, out_specs=plan.specs, scratch_shapes=plan.scratch,
    )(*sums)


def _add_sibling(pos, part, recv, name):
    _, _, r, c = part.shape

    def body(pos_ref, p_ref, r_ref, o_ref):
        o_ref[...] = (p_ref[...].astype(F32) + r_ref[...].astype(F32)).astype(o_ref.dtype)

    return pl.pallas_call(
        body, name=name,
        grid_spec=pltpu.PrefetchScalarGridSpec(
            num_scalar_prefetch=1, grid=(4,),
            in_specs=[pl.BlockSpec((None, None, r, c), lambda q, pos: (q, pos[2], 0, 0)),
                      pl.BlockSpec((None, r, c), lambda q, pos: (q, 0, 0))],
            out_specs=pl.BlockSpec((None, r, c), lambda q, pos: (q, 0, 0))),
        out_shape=jax.ShapeDtypeStruct((4, r, c), part.dtype),
        compiler_params=_cparams(("arbitrary",)),
    )(pos, part, recv)


def _sum_chips(pos, sums, recv, name):
    _, r, c = sums.shape

    def body(pos_ref, s_ref, r_ref, o_ref):
        g = s_ref[...].astype(F32)
        for k in range(3):
            g = g + r_ref[k].astype(F32)
        o_ref[...] = g

    return pl.pallas_call(
        body, name=name,
        grid_spec=pltpu.PrefetchScalarGridSpec(
            num_scalar_prefetch=1, grid=(1,),
            in_specs=[pl.BlockSpec((None, r, c), lambda i, pos: (2 * pos[0] + pos[1], 0, 0)),
                      pl.BlockSpec((3, r, c), lambda i, pos: (0, 0, 0))],
            out_specs=pl.BlockSpec((r, c), lambda i, pos: (0, 0))),
        out_shape=jax.ShapeDtypeStruct((r, c), F32),
        compiler_params=_cparams(("arbitrary",)),
    )(pos, sums, recv)


def _adamw(ws, gs, ms, vs, name):
    n = len(ws)
    c1 = 1.0 / (1.0 - B1 ** STEP)
    c2 = 1.0 / (1.0 - B2 ** STEP)

    def body(*refs):
        w_r, g_r, m_r, v_r = refs[:n], refs[n:2 * n], refs[2 * n:3 * n], refs[3 * n:4 * n]
        d_o, m_o, v_o = refs[4 * n:5 * n], refs[5 * n:6 * n], refs[6 * n:7 * n]
        for a in range(n):
            g = g_r[a][...]
            m = B1 * m_r[a][...] + (1.0 - B1) * g
            v = B2 * v_r[a][...] + (1.0 - B2) * (g * g)
            d_o[a][...] = -LR * ((m * c1) / (jnp.sqrt(v * c2) + ADAM_EPS) + WD * w_r[a][...])
            m_o[a][...] = m
            v_o[a][...] = v

    shapes = [jax.ShapeDtypeStruct(w.shape, F32) for w in ws]
    outs = pl.pallas_call(body, name=name, out_shape=shapes * 3,
                          compiler_params=_cparams())(*ws, *gs, *ms, *vs)
    return outs[:n], outs[n:2 * n], outs[2 * n:]


def _reduce_small(gathered, name):
    n = len(gathered)

    def body(*refs):
        for a in range(n):
            acc = refs[a][0]
            for k in range(1, 8):
                acc = acc + refs[a][k]
            refs[n + a][...] = acc

    return pl.pallas_call(
        body, name=name,
        out_shape=[jax.ShapeDtypeStruct(g.shape[1:], F32) for g in gathered],
        compiler_params=_cparams())(*gathered)


class _Cols:
    def __init__(self, pieces, tw):
        self.pieces, self.tw = pieces, tw
        self.arrays = [p[0] for p in pieces]
        self.n_tiles = sum(p[2] for p in pieces)

    def specs(self, tm, row_of, tile_of):
        out = []
        for _, first, cnt in self.pieces:
            def imap(*g, first=first, cnt=cnt):
                return (row_of(*g), jnp.clip(tile_of(*g) - first, 0, cnt - 1))
            out.append(pl.BlockSpec((tm, self.tw), imap))
        return out

    def apply(self, t, refs, fn):
        for ref, (_, first, cnt) in zip(refs, self.pieces):
            pl.when((t >= first) & (t < first + cnt))(functools.partial(fn, ref))


def _ab_in(h, g, w_t, tables):
    t = h.shape[0]
    tm = _tile(t, 544)
    src = jnp.asarray(np.array(P0_SRC, np.int32))

    def body(src_ref, h_ref, g_ref, w_ref, cos_ref, sa_ref, sb_ref, o_ref, hn_ref, hn_s):
        j = pl.program_id(1)

        @pl.when(j == 0)
        def _():
            x = h_ref[...]
            hn = (x * lax.rsqrt(jnp.mean(x * x, -1, keepdims=True) + EPS) * g_ref[...]).astype(MXU)
            hn_s[...] = hn
            hn_ref[...] = hn.astype(ACT)

        acc = _dot_nt(hn_s[...], w_ref[...])
        rope = lambda v: _rope(v, cos_ref[...], sa_ref[...], sb_ref[...])

        @pl.when((j == 4) | (j == 5))
        def _():
            o_ref[:, :BLK] = rope(acc[:, :BLK])
            o_ref[:, BLK:] = rope(acc[:, BLK:])

        @pl.when(j == 10)
        def _():
            o_ref[:, :BLK] = rope(acc[:, :BLK])
            o_ref[:, BLK:] = acc[:, BLK:]

        @pl.when((j < 4) | ((j > 5) & (j < 10)))
        def _():
            o_ref[...] = acc

    tab = pl.BlockSpec((tm, BLK), lambda i, j, s: (i, 0))
    return pl.pallas_call(
        body, name="ab_in",
        grid_spec=pltpu.PrefetchScalarGridSpec(
            num_scalar_prefetch=1, grid=(t // tm, 11),
            in_specs=[pl.BlockSpec((tm, D), lambda i, j, s: (i, 0)),
                      pl.BlockSpec((1, D), lambda i, j, s: (0, 0)),
                      pl.BlockSpec((256, D), lambda i, j, s: (s[j], 0)),
                      tab, tab, tab],
            out_specs=[pl.BlockSpec((tm, 256), lambda i, j, s: (i, j)),
                       pl.BlockSpec((tm, D), lambda i, j, s: (i, 0))],
            scratch_shapes=[pltpu.VMEM((tm, D), MXU)]),
        out_shape=[jax.ShapeDtypeStruct((t, 2816), F32), jax.ShapeDtypeStruct((t, D), ACT)],
        compiler_params=_cparams(("arbitrary", "arbitrary")),
    )(src, h, g, w_t, *tables)


def _swa_mask(n):
    r = lax.broadcasted_iota(jnp.int32, (BLK, 3 * BLK), 0)
    c = lax.broadcasted_iota(jnp.int32, (BLK, 3 * BLK), 1)
    qpos = n * BLK + r
    bpos = (n - 2) * BLK + c
    meta_ok = (c >= PAD) & (c < BLK) & (qpos - c >= BLK)
    band_ok = (c >= BLK) & (bpos >= PAD) & (qpos >= bpos) & (qpos - bpos < BLK)
    return meta_ok | band_ok


def _swa_keys(kv_ref, n):
    def blk(b):
        return kv_ref[pl.ds(pl.multiple_of(b * BLK, BLK), BLK), :]
    kv = jnp.concatenate([kv_ref[0:BLK, :], blk(jnp.maximum(n - 1, 0)), blk(n)], axis=0)
    lo = lax.broadcasted_iota(jnp.int32, (1, BLK), 1) < HEAD
    out = []
    for part in (kv[:, :BLK], kv[:, BLK:]):
        rolled = pltpu.roll(part, HEAD, 1)
        out.append((jnp.where(lo, part, rolled).astype(MXU), jnp.where(lo, rolled, part).astype(MXU)))
    return out[0], out[1], lo


def _swa_stack(ref, g, lo):
    parts = []
    for p in (2 * g, 2 * g + 1):
        x = ref[:, p * BLK:(p + 1) * BLK]
        parts += [jnp.where(lo, x, 0.0), jnp.where(lo, 0.0, x)]
    return jnp.concatenate(parts, axis=0).astype(MXU)


def _swa_probs(qs, kd, mask4, sink_ref, g):
    sink = jnp.concatenate([jnp.full((BLK, 1), sink_ref[4 * g + h], F32) for h in range(4)], axis=0)
    s = jnp.where(mask4, _dot_nt(qs, kd) * SCALE, NEG)
    m = jnp.maximum(jnp.max(s, -1, keepdims=True), sink)
    e = jnp.exp(s - m)
    e_sink = jnp.exp(sink - m)
    inv = 1.0 / (jnp.sum(e, -1, keepdims=True) + e_sink)
    return e * inv, e_sink * inv


def _swa_unstack(x, lo):
    return [jnp.where(lo, x[0:BLK], x[BLK:2 * BLK]), jnp.where(lo, x[2 * BLK:3 * BLK], x[3 * BLK:])]


def _swa_fwd(p0, sinks, ride):
    t = p0.shape[0]
    plan = _GatherPlan(ride)
    nr = plan.n

    def body(sink_ref, q_ref, kv_ref, *rest):
        o_ref = rest[nr]
        comm = (*rest[:nr], *rest[nr + 1:])
        n = pl.program_id(0)
        pl.when(n == 0)(lambda: plan.begin(comm))
        kd, vd, lo = _swa_keys(kv_ref, n)
        mask4 = jnp.concatenate([_swa_mask(n)] * 4, axis=0)
        for g in range(2):
            pr, _ = _swa_probs(_swa_stack(q_ref, g, lo), kd[g], mask4, sink_ref, g)
            pairs = _swa_unstack(_dot(pr.astype(MXU), vd[g]), lo)
            for k in range(2):
                p = 2 * g + k
                o_ref[:, p * BLK:(p + 1) * BLK] = pairs[k]
        pl.when(n == t // BLK - 1)(lambda: plan.end(comm))

    outs = pl.pallas_call(
        body, name="swa_fwd", grid=(t // BLK,),
        in_specs=[pl.BlockSpec(memory_space=pltpu.SMEM),
                  pl.BlockSpec((BLK, 512), lambda n: (n, 2)),
                  pl.BlockSpec((t, 256), lambda n: (0, 10))] + plan.specs,
        out_specs=[pl.BlockSpec((BLK, 512), lambda n: (n, 0))] + plan.specs,
        out_shape=[jax.ShapeDtypeStruct((t, 512), F32)] + plan.out_shape,
        scratch_shapes=plan.scratch,
        compiler_params=_cparams(("arbitrary",)),
    )(sinks, p0, p0, *ride)
    return outs[0], outs[1:]


def _conv_window(u_w, w_ref, n_out, first):
    rows = u_w.shape[0]
    acc = None
    for j in range(CONV_W):
        shifted = pltpu.roll(u_w, (rows - (first + j)) % rows, 0)[:n_out]
        term = shifted * w_ref[j:j + 1, :]
        acc = term if acc is None else acc + term
    return acc


def _conv_fwd(p0, conv_w, conv_b, ln_g, ln_b):
    t = p0.shape[0]
    tm = _tile(t, 544)
    hb = tm // HALO

    def body(cur_ref, prev_ref, w_ref, b_ref, g_ref, bb_ref, o_ref):
        i = pl.program_id(0)
        glu = jnp.concatenate([prev_ref[...], cur_ref[...]], axis=0)
        rw = _rows((tm + HALO, 1), i * tm - HALO)
        u_w = jnp.where(rw >= PAD, glu[:, :512] * _sigmoid(glu[:, 512:]), 0.0)
        cv = _conv_window(u_w, w_ref, tm, HALO - (CONV_W - 1)) + b_ref[...]
        xc = cv - jnp.mean(cv, -1, keepdims=True)
        ln = xc * lax.rsqrt(jnp.mean(xc * xc, -1, keepdims=True) + LN_EPS) * g_ref[...] + bb_ref[...]
        o_ref[...] = (ln * _sigmoid(ln)).astype(ACT)

    vec = pl.BlockSpec((1, 512), lambda i: (0, 0))
    return pl.pallas_call(
        body, name="conv_fwd", grid=(t // tm,),
        in_specs=[pl.BlockSpec((tm, D), lambda i: (i, 0)),
                  pl.BlockSpec((HALO, D), lambda i: (jnp.maximum(i * hb - 1, 0), 0)),
                  pl.BlockSpec((CONV_W, 512), lambda i: (0, 0)), vec, vec, vec],
        out_specs=pl.BlockSpec((tm, 512), lambda i: (i, 0)),
        out_shape=jax.ShapeDtypeStruct((t, 512), ACT),
        compiler_params=_cparams(("arbitrary",)),
    )(p0, p0, conv_w, conv_b, ln_g, ln_b)


def _ab_out(h, p0, att, c1, w_pw2, w_out, g_post):
    t = h.shape[0]
    tm = _tile(t, 272)

    def body(h_ref, ga_ref, gb_ref, att_ref, c1_ref, pw_ref, wo_ref, g_ref, h1_ref, y_ref, mix_ref):
        i = pl.program_id(0)
        sga, _ = _silu_and_grad(ga_ref[...])
        sgb, _ = _silu_and_grad(gb_ref[...])
        a = att_ref[...] * sga
        c = _dot(c1_ref[...].astype(MXU), pw_ref[...]) * sgb
        mix = jnp.concatenate([a, c], axis=1).astype(MXU)
        y = _dot(mix, wo_ref[...])
        yn = y * lax.rsqrt(jnp.mean(y * y, -1, keepdims=True) + EPS) * g_ref[...]
        h1_ref[...] = jnp.where(_rows((tm, 1), i * tm) >= PAD, h_ref[...] + yn, 0.0)
        y_ref[...] = y
        mix_ref[...] = mix.astype(ACT)

    row = lambda w, idx: pl.BlockSpec((tm, w), lambda i: (i, idx))
    full = lambda a: pl.BlockSpec(a.shape, lambda i: (0, 0))
    return pl.pallas_call(
        body, name="ab_out", grid=(t // tm,),
        in_specs=[row(D, 0), row(512, 3), row(512, 4), row(512, 0), row(512, 0),
                  full(w_pw2), full(w_out), full(g_post)],
        out_specs=[row(D, 0), row(D, 0), row(D, 0)],
        out_shape=[jax.ShapeDtypeStruct((t, D), F32), jax.ShapeDtypeStruct((t, D), F32),
                   jax.ShapeDtypeStruct((t, D), ACT)],
        compiler_params=_cparams(("arbitrary",)),
    )(h, p0, p0, att, c1, w_pw2, w_out, g_post)


def _sb_in(h, g, w):
    t = h.shape[0]
    tm = _tile(t, 544)

    def body(h_ref, g_ref, w_ref, o_ref, hn_ref, hn_s):
        @pl.when(pl.program_id(1) == 0)
        def _():
            x = h_ref[...]
            hn = (x * lax.rsqrt(jnp.mean(x * x, -1, keepdims=True) + EPS) * g_ref[...]).astype(MXU)
            hn_s[...] = hn
            hn_ref[...] = hn.astype(ACT)

        o_ref[...] = _dot(hn_s[...], w_ref[...])

    return pl.pallas_call(
        body, name="sb_in", grid=(t // tm, 8),
        in_specs=[pl.BlockSpec((tm, D), lambda i, j: (i, 0)),
                  pl.BlockSpec((1, D), lambda i, j: (0, 0)),
                  pl.BlockSpec((None, D, 512), lambda i, j: (j, 0, 0))],
        out_specs=[pl.BlockSpec((tm, 512), lambda i, j: (i, j)),
                   pl.BlockSpec((tm, D), lambda i, j: (i, 0))],
        out_shape=[jax.ShapeDtypeStruct((t, 4096), F32), jax.ShapeDtypeStruct((t, D), ACT)],
        scratch_shapes=[pltpu.VMEM((tm, D), MXU)],
        compiler_params=_cparams(("arbitrary", "arbitrary")),
    )(h, g, w)


def _split_hi_lo(x):
    hi = x.astype(MXU)
    lo = (x - hi.astype(F32)).astype(MXU)
    return hi, lo


def _scan_matrix(suffix):
    j = lax.broadcasted_iota(jnp.int32, (2 * BLK, 2 * BLK), 0) % BLK
    s = lax.broadcasted_iota(jnp.int32, (2 * BLK, 2 * BLK), 1)
    keep = (s >= BLK) | ((j > s) if suffix else (j < s))
    return jnp.where(keep, 1.0, 0.0).astype(MXU)


def _scan_packed(hi_lo, b, mat):
    cols = slice(b * BLK, (b + 1) * BLK)
    both = _dot(jnp.concatenate([hi_lo[:BLK, cols], hi_lo[BLK:, cols]], axis=1), mat)
    return both[:, :BLK], both[:, BLK:]


KC = 4
CHUNK = KC * BLK
SLACK = CHUNK - BLK
GROUPS = 2


def _sb_logits(qm, kc, valid):
    z = _dot_nt(qm, kc)
    log_beta = jnp.minimum(z, 0.0) - jnp.log(1.0 + jnp.exp(-jnp.abs(z)))
    return log_beta, jnp.where(valid, log_beta - z, 0.0)


def _sb_valid(i, first_key):
    r = lax.broadcasted_iota(jnp.int32, (BLK, CHUNK), 0)
    c = lax.broadcasted_iota(jnp.int32, (BLK, CHUNK), 1)
    kpos = first_key + c
    return (kpos >= PAD) & (kpos < i * BLK + r)


def _sb_fwd(p1):
    t = p1.shape[0]

    w = GROUPS * BLK

    def body(q_ref, k_ref, v_ref, o_ref, lt_ref, k_s, v_s):
        i = pl.program_id(1)

        @pl.when(i == 0)
        def _():
            for src, dst in ((k_ref, k_s), (v_ref, v_s)):
                dst[0:SLACK, :] = jnp.zeros((SLACK, w), MXU)
                dst[SLACK:, :] = src[...].astype(MXU)

        lo = lax.broadcasted_iota(jnp.int32, (1, BLK), 1) < HEAD
        qm = []
        for g in range(GROUPS):
            q = q_ref[:, g * BLK:(g + 1) * BLK] * SCALE
            qm += [jnp.where(lo, q, 0.0).astype(MXU), jnp.where(lo, 0.0, q).astype(MXU)]
        mat = _scan_matrix(True)

        def step(s, carry):
            start = pl.multiple_of((i - KC * s) * BLK, BLK)
            valid = _sb_valid(i, start - SLACK)
            new, staged = [], []
            for h in range(2 * GROUPS):
                lanes = slice((h // 2) * BLK, (h // 2 + 1) * BLK)
                log_beta, log_1m = _sb_logits(qm[h], k_s[pl.ds(start, CHUNK), lanes], valid)
                staged.append((log_beta, jnp.concatenate(_split_hi_lo(log_1m), axis=0)))
            for h in range(2 * GROUPS):
                lanes = slice((h // 2) * BLK, (h // 2 + 1) * BLK)
                log_beta, hi_lo = staged[h]
                run, acc = carry[2 * h], carry[2 * h + 1]
                parts = [None] * KC
                for b in reversed(range(KC)):
                    after, total = _scan_packed(hi_lo, b, mat)
                    parts[b] = after + run
                    run = run + total
                a = jnp.where(valid, jnp.exp(log_beta + jnp.concatenate(parts, axis=1)), 0.0)
                new += [run, acc + _dot(a.astype(MXU), v_s[pl.ds(start, CHUNK), lanes])]
            return tuple(new)

        zero = jnp.zeros((BLK, BLK), F32)
        res = lax.fori_loop(0, (i + KC) // KC, step, (zero,) * (4 * GROUPS))
        for g in range(GROUPS):
            lanes = slice(g * BLK, (g + 1) * BLK)
            o_ref[:, lanes] = jnp.where(lo, res[4 * g + 1], res[4 * g + 3])
            lt_ref[:, lanes] = jnp.where(lo, res[4 * g], res[4 * g + 2])

    ng = D // w
    blk = pl.BlockSpec((BLK, w), lambda hp, i: (i, hp))
    return pl.pallas_call(
        body, name="sb_fwd", grid=(ng, t // BLK),
        in_specs=[blk,
                  pl.BlockSpec((t, w), lambda hp, i: (0, ng + hp)),
                  pl.BlockSpec((t, w), lambda hp, i: (0, 2 * ng + hp))],
        out_specs=[blk, blk],
        out_shape=[jax.ShapeDtypeStruct((t, D), F32)] * 2,
        scratch_shapes=[pltpu.VMEM((t + SLACK, w), MXU), pltpu.VMEM((t + SLACK, w), MXU)],
        compiler_params=_cparams(("arbitrary", "arbitrary")),
    )(p1, p1, p1)


def _sb_out(o, p1, w_out, h1, g_post, tgt):
    t = o.shape[0]
    tm = _tile(t, 272)

    def body(o_ref, g_ref, w_ref, h_ref, gp_ref, t_ref,
             loss_ref, dh_ref, dy_ref, m_ref, do_ref, dg_ref, dgp_ref):
        i = pl.program_id(0)

        @pl.when(i == 0)
        def _():
            loss_ref[...] = jnp.zeros_like(loss_ref)
            dgp_ref[...] = jnp.zeros_like(dgp_ref)

        gate = g_ref[...]
        sg, dsg = _silu_and_grad(gate)
        ov = o_ref[...]
        m = (ov * sg).astype(MXU)
        y = _dot(m, w_ref[...])
        r = lax.rsqrt(jnp.mean(y * y, -1, keepdims=True) + EPS)
        yhat = y * r
        h2 = h_ref[...] + yhat * gp_ref[...]
        diff = jnp.where(_rows((tm, 1), i * tm) >= BLK, h2 - t_ref[...], 0.0)
        loss_ref[...] += jnp.full(loss_ref.shape, 0.5 / D, F32) * jnp.sum(diff * diff)
        dh = diff * (1.0 / D)
        dgp_ref[...] += jnp.sum(dh * yhat, 0, keepdims=True)
        dyn = dh * gp_ref[...]
        dy = (r * (dyn - yhat * jnp.mean(dyn * yhat, -1, keepdims=True))).astype(MXU)
        dm = _dot_nt(dy, w_ref[...])
        dh_ref[...] = dh
        dy_ref[...] = dy.astype(ACT)
        m_ref[...] = m.astype(ACT)
        do_ref[...] = dm * sg
        dg_ref[...] = (dm * ov * dsg).astype(ACT)

    row = lambda idx: pl.BlockSpec((tm, D), lambda i: (i, idx))
    full = lambda a: pl.BlockSpec(a.shape, lambda i: (0, 0))
    acc = lambda s: pl.BlockSpec(s, lambda i: (0, 0))
    return pl.pallas_call(
        body, name="sb_out", grid=(t // tm,),
        in_specs=[row(0), row(3), full(w_out), row(0), full(g_post), row(0)],
        out_specs=[acc((8, BLK)), row(0), row(0), row(0), row(0), row(0), acc((1, D))],
        out_shape=[jax.ShapeDtypeStruct((8, BLK), F32), jax.ShapeDtypeStruct((t, D), F32),
                   jax.ShapeDtypeStruct((t, D), ACT), jax.ShapeDtypeStruct((t, D), ACT),
                   jax.ShapeDtypeStruct((t, D), F32), jax.ShapeDtypeStruct((t, D), ACT),
                   jax.ShapeDtypeStruct((1, D), F32)],
        compiler_params=_cparams(("arbitrary",)),
    )(o, p1, w_out, h1, g_post, tgt)


def _sb_bwd(p1, ltot, do):
    t = p1.shape[0]
    nb = t // BLK

    w = GROUPS * BLK

    def body(q_ref, k_ref, v_ref, lt_ref, do_ref, dq_ref, dk_ref, dv_ref, k_s, v_s, dk_s, dv_s):
        i = pl.program_id(1)
        lo = lax.broadcasted_iota(jnp.int32, (1, BLK), 1) < HEAD

        @pl.when(i == 0)
        def _():
            for src, dst in ((k_ref, k_s), (v_ref, v_s)):
                dst[0:t, :] = src[...].astype(MXU)
                dst[t:, :] = jnp.zeros((SLACK, w), MXU)
            dk_s[...] = jnp.zeros_like(dk_s)
            dv_s[...] = jnp.zeros_like(dv_s)

        qm, dom, row_total, q2, do2 = [], [], [], [], []
        for g in range(GROUPS):
            lanes = slice(g * BLK, (g + 1) * BLK)
            q, dout, lt = q_ref[:, lanes] * SCALE, do_ref[:, lanes], lt_ref[:, lanes]
            qm += [jnp.where(lo, q, 0.0).astype(MXU), jnp.where(lo, 0.0, q).astype(MXU)]
            dom += [jnp.where(lo, dout, 0.0).astype(MXU), jnp.where(lo, 0.0, dout).astype(MXU)]
            q2.append(jnp.concatenate(qm[-2:], axis=0))
            do2.append(jnp.concatenate(dom[-2:], axis=0))
            lt_r = pltpu.roll(lt, HEAD, 1)
            row_total += [jnp.where(lo, lt, lt_r), jnp.where(lo, lt_r, lt)]
        mat_l = _scan_matrix(True)
        mat_g = _scan_matrix(False)
        heads = range(2 * GROUPS)

        def step(s, carry):
            start = pl.multiple_of(s * CHUNK, BLK)
            keys = lambda ref, h: ref[pl.ds(start, CHUNK), (h // 2) * BLK:(h // 2 + 1) * BLK]
            valid = _sb_valid(i, start)
            new, dzs, probs, st1, st2 = [], [], [], [], []
            for h in heads:
                log_beta, log_1m = _sb_logits(qm[h], keys(k_s, h), valid)
                st1.append((log_beta, jnp.concatenate(_split_hi_lo(log_1m), axis=0)))
            for h in heads:
                log_beta, hi_lo = st1[h]
                run = carry[3 * h]
                parts = []
                for b in range(KC):
                    after, total = _scan_packed(hi_lo, b, mat_l)
                    run = run + total
                    parts.append(after + (row_total[h] - run))
                a = jnp.where(valid, jnp.exp(log_beta + jnp.concatenate(parts, axis=1)), 0.0)
                g = _dot_nt(dom[h], keys(v_s, h)) * a
                probs.append(a.astype(MXU))
                st2.append((run, g, jnp.concatenate(_split_hi_lo(g), axis=0)))
            for h in heads:
                run, g, hi_lo = st2[h]
                run_g, dq = carry[3 * h + 1], carry[3 * h + 2]
                parts = []
                for b in range(KC):
                    before, total_g = _scan_packed(hi_lo, b, mat_g)
                    parts.append(before + run_g)
                    run_g = run_g + total_g
                sig = jnp.exp(st1[h][0])
                dz = jnp.where(valid, g * (1.0 - sig) - sig * jnp.concatenate(parts, axis=1), 0.0)
                dzm = dz.astype(MXU)
                dzs.append(dzm)
                new += [run, run_g, dq + _dot(dzm, keys(k_s, h))]
            for g in range(GROUPS):
                lanes = slice(g * BLK, (g + 1) * BLK)
                dk_s[pl.ds(start, CHUNK), lanes] += _dot_tn(jnp.concatenate(dzs[2 * g:2 * g + 2], axis=0), q2[g])
                dv_s[pl.ds(start, CHUNK), lanes] += _dot_tn(jnp.concatenate(probs[2 * g:2 * g + 2], axis=0), do2[g])
            return tuple(new)

        zero = jnp.zeros((BLK, BLK), F32)
        res = lax.fori_loop(0, (i + KC) // KC, step, (zero,) * (6 * GROUPS))
        for g in range(GROUPS):
            dq = jnp.where(lo, res[6 * g + 2], res[6 * g + 5])
            dq_ref[:, g * BLK:(g + 1) * BLK] = (dq * SCALE).astype(ACT)

        @pl.when(i == nb - 1)
        def _():
            dk_ref[...] = dk_s[0:t, :].astype(ACT)
            dv_ref[...] = dv_s[0:t, :].astype(ACT)

    ng = D // w
    blk = pl.BlockSpec((BLK, w), lambda hp, i: (i, hp))
    col = lambda off: pl.BlockSpec((t, w), lambda hp, i: (0, off + hp))
    return pl.pallas_call(
        body, name="sb_bwd", grid=(ng, nb),
        in_specs=[blk, col(ng), col(2 * ng), blk, blk],
        out_specs=[blk, col(0), col(0)],
        out_shape=[jax.ShapeDtypeStruct((t, D), ACT)] * 3,
        scratch_shapes=[pltpu.VMEM((t + SLACK, w), MXU), pltpu.VMEM((t + SLACK, w), MXU),
                        pltpu.VMEM((t + SLACK, w), F32), pltpu.VMEM((t + SLACK, w), F32)],
        compiler_params=_cparams(("arbitrary", "arbitrary")),
    )(p1, p1, p1, ltot, do)


def _mid_bwd(dp1, w_sb, h1, g_pre1, dh2, y0, g_post0, w_out, p0, att, c1, w_pw2):
    t = h1.shape[0]
    tm = _tile(t, 272)

    def body(*refs):
        d_refs = refs[:4]
        (w_ref, h_ref, g1_ref, dh2_ref, y_ref, g0_ref, wo_ref, ga_ref, gb_ref, att_ref, c1_ref,
         pw_ref, dh1_ref, dy_ref, dga_ref, dgb_ref, datt_ref, dc1_ref, dc2_ref, dg1_ref, dg0_ref,
         acc) = refs[4:]
        i, j = pl.program_id(0), pl.program_id(1)

        @pl.when((i == 0) & (j == 0))
        def _():
            dg1_ref[...] = jnp.zeros_like(dg1_ref)
            dg0_ref[...] = jnp.zeros_like(dg0_ref)

        @pl.when(j == 0)
        def _():
            acc[...] = jnp.zeros_like(acc)

        def add(ref):
            acc[...] += _dot_nt(ref[...].astype(MXU), w_ref[...])
        dp1.apply(j, d_refs, add)

        @pl.when(j == 7)
        def _():
            dhn = acc[...]
            x = h_ref[...]
            r = lax.rsqrt(jnp.mean(x * x, -1, keepdims=True) + EPS)
            xhat = x * r
            dg1_ref[...] += jnp.sum(dhn * xhat, 0, keepdims=True)
            dxn = dhn * g1_ref[...]
            dh1 = dh2_ref[...] + r * (dxn - xhat * jnp.mean(dxn * xhat, -1, keepdims=True))
            dh1_ref[...] = dh1
            y = y_ref[...]
            ry = lax.rsqrt(jnp.mean(y * y, -1, keepdims=True) + EPS)
            yhat = y * ry
            dg0_ref[...] += jnp.sum(dh1 * yhat, 0, keepdims=True)
            dyn = dh1 * g0_ref[...]
            dy = (ry * (dyn - yhat * jnp.mean(dyn * yhat, -1, keepdims=True))).astype(MXU)
            dy_ref[...] = dy.astype(ACT)
            dmix = _dot_nt(dy, wo_ref[...])
            da, dc = dmix[:, :512], dmix[:, 512:]
            sga, dsga = _silu_and_grad(ga_ref[...])
            sgb, dsgb = _silu_and_grad(gb_ref[...])
            datt_ref[...] = da * sga
            dga_ref[...] = (da * att_ref[...] * dsga).astype(ACT)
            c2 = _dot(c1_ref[...].astype(MXU), pw_ref[...])
            dc2 = (dc * sgb).astype(MXU)
            dgb_ref[...] = (dc * c2 * dsgb).astype(ACT)
            dc2_ref[...] = dc2.astype(ACT)
            dc1_ref[...] = _dot_nt(dc2, pw_ref[...])

    row = lambda w, idx: pl.BlockSpec((tm, w), lambda i, j: (i, idx))
    full = lambda a: pl.BlockSpec(a.shape, lambda i, j: (0, 0))
    acc_spec = pl.BlockSpec((1, D), lambda i, j: (0, 0))
    sd = jax.ShapeDtypeStruct
    return pl.pallas_call(
        body, name="mid_bwd", grid=(t // tm, 8),
        in_specs=dp1.specs(tm, lambda i, j: i, lambda i, j: j) + [
            pl.BlockSpec((None, D, 512), lambda i, j: (j, 0, 0)),
            row(D, 0), full(g_pre1), row(D, 0), row(D, 0), full(g_post0), full(w_out),
            row(512, 3), row(512, 4), row(512, 0), row(512, 0), full(w_pw2)],
        out_specs=[row(D, 0), row(D, 0), row(512, 0), row(512, 0), row(512, 0), row(512, 0),
                   row(512, 0), acc_spec, acc_spec],
        out_shape=[sd((t, D), F32), sd((t, D), ACT), sd((t, 512), ACT), sd((t, 512), ACT),
                   sd((t, 512), F32), sd((t, 512), F32), sd((t, 512), ACT),
                   sd((1, D), F32), sd((1, D), F32)],
        scratch_shapes=[pltpu.VMEM((tm, D), F32)],
        compiler_params=_cparams(("arbitrary", "arbitrary")),
    )(*dp1.arrays, w_sb, h1, g_pre1, dh2, y0, g_post0, w_out, p0, p0, att, c1, w_pw2)


def _conv_bwd(p0, dc1, conv_w, conv_b, ln_g, ln_b):
    t = p0.shape[0]
    tm = _tile(t, 544)
    hb = tm // HALO
    last = t // HALO - 1

    def body(cur_ref, prev_ref, next_ref, d_ref, dn_ref, w_ref, b_ref, g_ref, bb_ref,
             dglu_ref, dw_ref, db_ref, dlg_ref, dlb_ref):
        i = pl.program_id(0)

        @pl.when(i == 0)
        def _():
            for ref in (dw_ref, db_ref, dlg_ref, dlb_ref):
                ref[...] = jnp.zeros_like(ref)

        glu = jnp.concatenate([prev_ref[...], cur_ref[...], next_ref[...]], axis=0)
        rw = _rows((tm + 2 * HALO, 1), i * tm - HALO)
        ga, sg = glu[:, :512], _sigmoid(glu[:, 512:])
        u_w = jnp.where((rw >= PAD) & (rw < t), ga * sg, 0.0)
        n_cv = tm + HALO
        cv = _conv_window(u_w, w_ref, n_cv, HALO - (CONV_W - 1)) + b_ref[...]
        xc = cv - jnp.mean(cv, -1, keepdims=True)
        rstd = lax.rsqrt(jnp.mean(xc * xc, -1, keepdims=True) + LN_EPS)
        cvhat = xc * rstd
        ln = cvhat * g_ref[...] + bb_ref[...]
        _, dsl = _silu_and_grad(ln)
        rc = _rows((n_cv, 1), i * tm)
        dc = jnp.concatenate([d_ref[...], dn_ref[...]], axis=0)
        dln = jnp.where(rc < t, dc * dsl, 0.0)
        dhat = dln * g_ref[...]
        dcv = rstd * (dhat - jnp.mean(dhat, -1, keepdims=True)
                      - cvhat * jnp.mean(dhat * cvhat, -1, keepdims=True))
        own = dcv[:tm]
        dlg_ref[...] += jnp.sum((dln * cvhat)[:tm], 0, keepdims=True)
        dlb_ref[...] += jnp.sum(dln[:tm], 0, keepdims=True)
        db_ref[...] += jnp.sum(own, 0, keepdims=True)
        rows = tm + 2 * HALO
        du = None
        for j in range(CONV_W):
            first = HALO - (CONV_W - 1) + j
            shifted = pltpu.roll(u_w, (rows - first) % rows, 0)[:tm]
            dw_ref[j:j + 1, :] += jnp.sum(own * shifted, 0, keepdims=True)
            back = pltpu.roll(dcv, (n_cv - (CONV_W - 1 - j)) % n_cv, 0)[:tm]
            term = back * w_ref[j:j + 1, :]
            du = term if du is None else du + term
        du = jnp.where(_rows((tm, 1), i * tm) >= PAD, du, 0.0)
        ga_c, sg_c = ga[HALO:HALO + tm], sg[HALO:HALO + tm]
        dglu_ref[:, :512] = (du * sg_c).astype(ACT)
        dglu_ref[:, 512:] = (du * ga_c * sg_c * (1.0 - sg_c)).astype(ACT)

    vec = pl.BlockSpec((1, 512), lambda i: (0, 0))
    nxt = lambda i: (jnp.minimum((i + 1) * hb, last), 0)
    return pl.pallas_call(
        body, name="conv_bwd", grid=(t // tm,),
        in_specs=[pl.BlockSpec((tm, D), lambda i: (i, 0)),
                  pl.BlockSpec((HALO, D), lambda i: (jnp.maximum(i * hb - 1, 0), 0)),
                  pl.BlockSpec((HALO, D), nxt),
                  pl.BlockSpec((tm, 512), lambda i: (i, 0)),
                  pl.BlockSpec((HALO, 512), nxt),
                  pl.BlockSpec((CONV_W, 512), lambda i: (0, 0)), vec, vec, vec],
        out_specs=[pl.BlockSpec((tm, D), lambda i: (i, 0)),
                   pl.BlockSpec((HALO, 512), lambda i: (0, 0)), vec, vec, vec],
        out_shape=[jax.ShapeDtypeStruct((t, D), ACT), jax.ShapeDtypeStruct((HALO, 512), F32)]
        + [jax.ShapeDtypeStruct((1, 512), F32)] * 3,
        compiler_params=_cparams(("arbitrary",)),
    )(p0, p0, p0, dc1, dc1, conv_w, conv_b, ln_g, ln_b)


def _swa_bwd(p0, datt, sinks, tables, ride):
    t = p0.shape[0]
    nb = t // BLK
    plan = _ChipsPlan(ride)
    nr = plan.n

    def body(sink_ref, q_ref, kv_ref, d_ref, cos_ref, sa_ref, sb_ref, *rest):
        dq_ref, dkv_ref, ds_ref = rest[nr:nr + 3]
        acc = rest[2 * nr + 3]
        comm = (*rest[:nr], *rest[nr + 3:2 * nr + 3], *rest[2 * nr + 4:])
        n = pl.program_id(0)
        pl.when(n == 0)(lambda: plan.begin(comm))

        @pl.when(n == 0)
        def _():
            acc[...] = jnp.zeros_like(acc)
            ds_ref[...] = jnp.zeros_like(ds_ref)

        kd, vd, lo = _swa_keys(kv_ref, n)
        mask4 = jnp.concatenate([_swa_mask(n)] * 4, axis=0)
        row0 = pl.multiple_of(n * BLK, BLK)
        tabs = [r[pl.ds(row0, BLK), :] for r in (cos_ref, sa_ref, sb_ref)]
        dk_g, dv_g = [], []
        for g in range(2):
            qs, dos = _swa_stack(q_ref, g, lo), _swa_stack(d_ref, g, lo)
            pr, p_sink = _swa_probs(qs, kd[g], mask4, sink_ref, g)
            dpr = _dot_nt(dos, vd[g])
            delta = jnp.sum(pr * dpr, -1, keepdims=True)
            dsc = (pr * (dpr - delta) * SCALE).astype(MXU)
            sunk = p_sink * delta
            for h in range(4):
                row = 4 * g + h
                ds_ref[row:row + 1, :] += jnp.full((1, BLK), -1.0, F32) * jnp.sum(sunk[h * BLK:(h + 1) * BLK])
            pairs = _swa_unstack(_dot(dsc, kd[g]), lo)
            for k in range(2):
                p = 2 * g + k
                dq_ref[:, p * BLK:(p + 1) * BLK] = _unrope(pairs[k], *tabs).astype(ACT)
            dk_g.append(_dot_tn(dsc, qs))
            dv_g.append(_dot_tn(pr.astype(MXU), dos))
        fold = lambda a: a + pltpu.roll(a, HEAD, 1)
        dk = jnp.where(lo, fold(dk_g[0]), fold(dk_g[1]))
        dv = jnp.where(lo, fold(dv_g[0]), fold(dv_g[1]))
        dkv = jnp.concatenate([dk, dv], axis=1)
        prev = pl.multiple_of(jnp.maximum(n - 1, 0) * BLK, BLK)
        acc[0:BLK, :] += dkv[0:BLK]
        acc[pl.ds(prev, BLK), :] += dkv[BLK:2 * BLK]
        acc[pl.ds(row0, BLK), :] += dkv[2 * BLK:]

        @pl.when(n == nb - 1)
        def _():
            dkv_ref[:, :BLK] = _unrope(acc[:, :BLK], cos_ref[...], sa_ref[...], sb_ref[...]).astype(ACT)
            dkv_ref[:, BLK:] = acc[:, BLK:].astype(ACT)

        pl.when(n == nb - 1)(lambda: plan.end(comm))

    tab = pl.BlockSpec((t, BLK), lambda n: (0, 0))
    outs = pl.pallas_call(
        body, name="swa_bwd", grid=(nb,),
        in_specs=[pl.BlockSpec(memory_space=pltpu.SMEM),
                  pl.BlockSpec((BLK, 512), lambda n: (n, 2)),
                  pl.BlockSpec((t, 256), lambda n: (0, 10)),
                  pl.BlockSpec((BLK, 512), lambda n: (n, 0)), tab, tab, tab] + plan.specs,
        out_specs=[pl.BlockSpec((BLK, 512), lambda n: (n, 0)),
                   pl.BlockSpec((t, 256), lambda n: (0, 0)),
                   pl.BlockSpec((8, BLK), lambda n: (0, 0))] + plan.specs,
        out_shape=[jax.ShapeDtypeStruct((t, 512), ACT), jax.ShapeDtypeStruct((t, 256), ACT),
                   jax.ShapeDtypeStruct((8, BLK), F32)] + plan.out_shape,
        scratch_shapes=[pltpu.VMEM((t, 256), F32)] + plan.scratch,
        compiler_params=_cparams(("arbitrary",)),
    )(sinks, p0, p0, datt, *tables, *ride)
    return outs[0], outs[1], outs[2], outs[3:]


def _ab_in_bwd(dp0, w_t, h0, g_pre, dh1):
    t = h0.shape[0]
    tm = _tile(t, 544)

    def body(*refs):
        d_refs = refs[:5]
        w_ref, h_ref, g_ref, dh1_ref, dh0_ref, dg_ref, acc = refs[5:]
        i, j = pl.program_id(0), pl.program_id(1)

        @pl.when((i == 0) & (j == 0))
        def _():
            dg_ref[...] = jnp.zeros_like(dg_ref)

        @pl.when(j == 0)
        def _():
            acc[...] = jnp.zeros_like(acc)

        def add(ref):
            acc[...] += _dot(ref[...].astype(MXU), w_ref[...])
        dp0.apply(j, d_refs, add)

        @pl.when(j == 10)
        def _():
            dhn = acc[...]
            x = h_ref[...]
            r = lax.rsqrt(jnp.mean(x * x, -1, keepdims=True) + EPS)
            xhat = x * r
            dg_ref[...] += jnp.sum(dhn * xhat, 0, keepdims=True)
            dxn = dhn * g_ref[...]
            dh0_ref[...] = dh1_ref[...] + r * (dxn - xhat * jnp.mean(dxn * xhat, -1, keepdims=True))

    row = pl.BlockSpec((tm, D), lambda i, j: (i, 0))
    vec = pl.BlockSpec((1, D), lambda i, j: (0, 0))
    return pl.pallas_call(
        body, name="ab_in_bwd", grid=(t // tm, 11),
        in_specs=dp0.specs(tm, lambda i, j: i, lambda i, j: j) + [
            pl.BlockSpec((256, D), lambda i, j: (j, 0)), row, vec, row],
        out_specs=[row, vec],
        out_shape=[jax.ShapeDtypeStruct((t, D), F32), jax.ShapeDtypeStruct((1, D), F32)],
        scratch_shapes=[pltpu.VMEM((tm, D), F32)],
        compiler_params=_cparams(("arbitrary", "arbitrary")),
    )(*dp0.arrays, w_t, h0, g_pre, dh1)


def _dw_plain(a, b, name):
    t, m = a.shape
    n = b.shape[1]
    tm = _tile(t, 544)
    tn = min(n, 512)
    nk = t // tm

    def body(a_ref, b_ref, o_ref, acc):
        k = pl.program_id(1)

        @pl.when(k == 0)
        def _():
            acc[...] = jnp.zeros_like(acc)

        acc[...] += _dot_tn(a_ref[...].astype(MXU), b_ref[...].astype(MXU))

        @pl.when(k == nk - 1)
        def _():
            o_ref[...] = acc[...].astype(WIRE)

    return pl.pallas_call(
        body, name=name, grid=(n // tn, nk),
        in_specs=[pl.BlockSpec((tm, m), lambda j, k: (k, 0)),
                  pl.BlockSpec((tm, tn), lambda j, k: (k, j))],
        out_specs=pl.BlockSpec((m, tn), lambda j, k: (0, j)),
        out_shape=jax.ShapeDtypeStruct((m, n), WIRE),
        scratch_shapes=[pltpu.VMEM((m, tn), F32)],
        compiler_params=_cparams(("arbitrary", "arbitrary")),
    )(a, b)


def _dw_chunks(hn, dp, name):
    t = hn.shape[0]
    tm = _tile(t, 544)
    nk = t // tm
    nt, tw = dp.n_tiles, dp.tw
    n_in = len(dp.arrays)

    def body(*refs):
        d_refs = refs[:n_in]
        h_ref, o_ref, acc = refs[n_in:]
        j, k = pl.program_id(0), pl.program_id(1)

        @pl.when(k == 0)
        def _():
            acc[...] = jnp.zeros_like(acc)

        def add(ref):
            acc[...] += _dot_tn(h_ref[...].astype(MXU), ref[...].astype(MXU))
        dp.apply(j, d_refs, add)

        @pl.when(k == nk - 1)
        def _():
            o_ref[...] = acc[...].astype(WIRE)

    return pl.pallas_call(
        body, name=name, grid=(nt, nk),
        in_specs=dp.specs(tm, lambda j, k: k, lambda j, k: j) + [
            pl.BlockSpec((tm, D), lambda j, k: (k, 0))],
        out_specs=pl.BlockSpec((None, D, tw), lambda j, k: (j, 0, 0)),
        out_shape=jax.ShapeDtypeStruct((nt, D, tw), WIRE),
        scratch_shapes=[pltpu.VMEM((D, tw), F32)],
        compiler_params=_cparams(("arbitrary", "arbitrary")),
    )(*dp.arrays, hn)


def _dw_transposed(dp, hn, name):
    t = hn.shape[0]
    tm = _tile(t, 544)
    nk = t // tm
    nt, tw = dp.n_tiles, dp.tw
    n_in = len(dp.arrays)

    def body(*refs):
        d_refs = refs[:n_in]
        h_ref, o_ref, acc = refs[n_in:]
        j, k = pl.program_id(0), pl.program_id(1)

        @pl.when(k == 0)
        def _():
            acc[...] = jnp.zeros_like(acc)

        def add(ref):
            acc[...] += _dot_tn(ref[...].astype(MXU), h_ref[...].astype(MXU))
        dp.apply(j, d_refs, add)

        @pl.when(k == nk - 1)
        def _():
            o_ref[...] = acc[...].astype(WIRE)

    return pl.pallas_call(
        body, name=name, grid=(nt, nk),
        in_specs=dp.specs(tm, lambda j, k: k, lambda j, k: j) + [
            pl.BlockSpec((tm, D), lambda j, k: (k, 0))],
        out_specs=pl.BlockSpec((tw, D), lambda j, k: (j, 0)),
        out_shape=jax.ShapeDtypeStruct((nt * tw, D), WIRE),
        scratch_shapes=[pltpu.VMEM((tw, D), F32)],
        compiler_params=_cparams(("arbitrary", "arbitrary")),
    )(*dp.arrays, hn)


def kernel(x, meta_tokens, ab_pre_norm, ab_w_in, ab_sinks, ab_conv_w, ab_conv_b, ab_conv_ln_g, ab_conv_ln_b, ab_w_pw2, ab_w_out, ab_post_norm, sb_pre_norm, sb_w_in, sb_w_out, sb_post_norm, loss_target, m_meta_tokens, m_ab_pre_norm, m_ab_w_in, m_ab_sinks, m_ab_conv_w, m_ab_conv_b, m_ab_conv_ln_g, m_ab_conv_ln_b, m_ab_w_pw2, m_ab_w_out, m_ab_post_norm, m_sb_pre_norm, m_sb_w_in, m_sb_w_out, m_sb_post_norm, v_meta_tokens, v_ab_pre_norm, v_ab_w_in, v_ab_sinks, v_ab_conv_w, v_ab_conv_b, v_ab_conv_ln_g, v_ab_conv_ln_b, v_ab_w_pw2, v_ab_w_out, v_ab_post_norm, v_sb_pre_norm, v_sb_w_in, v_sb_w_out, v_sb_post_norm):
    seq = x.shape[1]
    t = seq + BLK
    mx, my, mc = _coords()
    me = 4 * mx + 2 * my + mc
    pos = jnp.stack([mx, my, mc, me]).astype(jnp.int32)

    w_ab_t, w_oa, w_pw, *small = _all_gather(
        [ab_w_in[0].T.astype(WIRE), ab_w_out[0].astype(WIRE), ab_w_pw2[0].astype(WIRE),
         meta_tokens, ab_conv_w[0], sb_pre_norm, sb_post_norm], "gather_first")
    w_ab_t = w_ab_t.reshape(2816, D)
    w_oa = w_oa.reshape(D, D)
    w_pw = w_pw.reshape(512, 512)
    meta_full = jnp.moveaxis(small[0], 0, 1).reshape(N_META, D)
    conv_w = jnp.moveaxis(small[1], 0, 1).reshape(CONV_W, 512)
    sb_pre = jnp.moveaxis(small[2], 0, 1).reshape(1, D)
    sb_post = jnp.moveaxis(small[3], 0, 1).reshape(1, D)

    h0 = jnp.concatenate([jnp.zeros((PAD, D), F32), meta_full, x[0]], axis=0)
    tgt = jnp.concatenate([jnp.zeros((BLK, D), F32), loss_target[0]], axis=0)
    tables = _rope_tables(t)
    sinks = ab_sinks[0]

    p0, hn0 = _ab_in(h0, ab_pre_norm, w_ab_t, tables)
    att, (w_sb, w_os) = _swa_fwd(p0, sinks, [sb_w_in[0].astype(WIRE), sb_w_out[0].astype(WIRE)])
    w_os = w_os.reshape(D, D)
    c1 = _conv_fwd(p0, conv_w, ab_conv_b, ab_conv_ln_g, ab_conv_ln_b)
    h1, y0, mix = _ab_out(h0, p0, att, c1, w_pw, w_oa, ab_post_norm)
    p1, hn1 = _sb_in(h1, sb_pre, w_sb)
    o, ltot = _sb_fwd(p1)
    loss_part, dh2, dy1, m1, do, dgate, dg_sb_post = _sb_out(o, p1, w_os, h1, sb_post, tgt)

    dq1, dk1, dv1 = _sb_bwd(p1, ltot, do)
    dp1 = _Cols([(dq1, 0, 2), (dk1, 2, 2), (dv1, 4, 2), (dgate, 6, 2)], 512)

    def sibling_stage(parts, names, tag):
        got = _exchange_sibling(parts, "reduce_sibling_" + tag)
        return [_add_sibling(pos, p, r, "add_sibling_" + nm) for p, r, nm in zip(parts, got, names)]

    def finish(sums, got, names):
        return [_sum_chips(pos, s, r, "sum_chips_" + nm) for s, r, nm in zip(sums, got, names)]

    names1 = ["sb_in", "sb_out"]
    sums1 = sibling_stage([_dw_chunks(hn1, dp1, "dw_sb_in").reshape(4, 2, D, 512),
                           _dw_plain(m1, dy1, "dw_sb_out").reshape(4, 2, BLK, D)], names1, "sb")
    dh1, dy0, dga, dgb, datt, dc1, dc2, dg_sb_pre, dg_ab_post = _mid_bwd(
        dp1, w_sb, h1, sb_pre, dh2, y0, ab_post_norm, w_oa, p0, att, c1, w_pw)
    dglu, dconv_w, dconv_b, dln_g, dln_b = _conv_bwd(p0, dc1, conv_w, ab_conv_b, ab_conv_ln_g, ab_conv_ln_b)
    dq0, dkv0, dsinks, got1 = _swa_bwd(p0, datt, sinks, tables, sums1)
    g_sb_w_in, g_sb_w_out = finish(sums1, got1, names1)
    dp0 = _Cols([(dq0, 0, 2), (dkv0, 2, 1), (dga, 3, 2), (dglu, 5, 4), (dgb, 9, 2)], 256)
    dh0, dg_ab_pre = _ab_in_bwd(dp0, w_ab_t, h0, ab_pre_norm, dh1)

    names0 = ["ab_in", "ab_out", "pw2"]
    sums0 = sibling_stage([_dw_transposed(dp0, hn0, "dw_ab_in").reshape(4, 2, 352, D),
                           _dw_plain(mix, dy0, "dw_ab_out").reshape(4, 2, BLK, D),
                           _dw_plain(c1, dc2, "dw_pw2").reshape(4, 2, 64, 512)], names0, "ab")
    big0 = finish(sums0, _exchange_chips(sums0, "reduce_chips_ab"), names0)
    g_ab_w_in = big0[0].T
    g_ab_w_out, g_ab_w_pw2 = big0[1:]

    small_parts = [dh0[PAD:BLK], dg_ab_pre, dsinks, dconv_w, dconv_b, dln_g, dln_b,
                   dg_ab_post, dg_sb_pre, dg_sb_post]
    red = _reduce_small(_all_gather(small_parts, "gather_small_grads"), "reduce_small")
    col = lambda a, w: lax.dynamic_slice_in_dim(a, me * w, w, axis=1)
    g_meta = col(red[0], BLK)
    g_ab_pre = red[1]
    g_sinks = red[2][:, 0].reshape(1, 8)
    g_conv_w = col(red[3][:CONV_W], 64)
    g_conv_b, g_ln_g, g_ln_b, g_ab_post = red[4], red[5], red[6], red[7]
    g_sb_pre, g_sb_post = col(red[8], BLK), col(red[9], BLK)

    loss = lax.psum(loss_part[0, 0], ("x", "y", "c"))
    grad_x = dh0[BLK:][None]

    weights = [meta_tokens, ab_pre_norm, ab_w_in[0], ab_sinks, ab_conv_w[0], ab_conv_b, ab_conv_ln_g,
               ab_conv_ln_b, ab_w_pw2[0], ab_w_out[0], ab_post_norm, sb_pre_norm, sb_w_in[0],
               sb_w_out[0], sb_post_norm]
    grads = [g_meta, g_ab_pre, g_ab_w_in, g_sinks, g_conv_w, g_conv_b, g_ln_g, g_ln_b, g_ab_w_pw2,
             g_ab_w_out, g_ab_post, g_sb_pre, g_sb_w_in, g_sb_w_out, g_sb_post]
    ms = [m_meta_tokens, m_ab_pre_norm, m_ab_w_in[0], m_ab_sinks, m_ab_conv_w[0], m_ab_conv_b,
          m_ab_conv_ln_g, m_ab_conv_ln_b, m_ab_w_pw2[0], m_ab_w_out[0], m_ab_post_norm,
          m_sb_pre_norm, m_sb_w_in[0], m_sb_w_out[0], m_sb_post_norm]
    vs = [v_meta_tokens, v_ab_pre_norm, v_ab_w_in[0], v_ab_sinks, v_ab_conv_w[0], v_ab_conv_b,
          v_ab_conv_ln_g, v_ab_conv_ln_b, v_ab_w_pw2[0], v_ab_w_out[0], v_ab_post_norm,
          v_sb_pre_norm, v_sb_w_in[0], v_sb_w_out[0], v_sb_post_norm]
    lead = [w.ndim == 3 for w in (meta_tokens, ab_pre_norm, ab_w_in, ab_sinks, ab_conv_w, ab_conv_b,
                                   ab_conv_ln_g, ab_conv_ln_b, ab_w_pw2, ab_w_out, ab_post_norm,
                                   sb_pre_norm, sb_w_in, sb_w_out, sb_post_norm)]
    big_ids = [2, 8, 9, 12, 13]
    small_ids = [i for i in range(15) if i not in big_ids]
    deltas, new_m, new_v = [None] * 15, [None] * 15, [None] * 15
    for ids, nm in ((small_ids, "adamw_small"), (big_ids, "adamw_big")):
        d_, m_, v_ = _adamw([weights[i] for i in ids], [grads[i] for i in ids],
                            [ms[i] for i in ids], [vs[i] for i in ids], nm)
        for k, i in enumerate(ids):
            deltas[i], new_m[i], new_v[i] = d_[k], m_[k], v_[k]
    fix = lambda arrs: [a[None] if l else a for a, l in zip(arrs, lead)]
    return (loss, grad_x, *fix(grads), *fix(deltas), *fix(new_m), *fix(new_v))
```

```python
import functools

import numpy as np
import jax
import jax.numpy as jnp
from jax import lax
from jax.experimental import pallas as pl
from jax.experimental.pallas import tpu as pltpu

F32 = jnp.float32
MXU = jnp.bfloat16
ACT = jnp.bfloat16
WIRE = jnp.bfloat16

D = 1024
N_META = 16
BLK = 128
PAD = BLK - N_META
HEAD = 64
NEG = -1e30
EPS = 1e-6
LN_EPS = 1e-5
ROPE_THETA = 10000.0
SCALE = HEAD ** -0.5
CONV_W = 31
HALO = 32
LR, B1, B2, ADAM_EPS, WD, STEP = 0.001, 0.9, 0.999, 1e-08, 0.01, 10
VMEM_LIMIT = 56 * 1024 * 1024
MESH = pl.DeviceIdType.MESH

P0_SRC = (5, 6, 7, 8, 0, 1, 3, 4, 9, 10, 2)


def _cparams(sem=None):
    return pltpu.CompilerParams(dimension_semantics=sem, vmem_limit_bytes=VMEM_LIMIT)


def _tile(t, pref):
    for cand in (pref, 544, 272, 128):
        if cand <= pref and t % cand == 0:
            return cand
    raise ValueError(t)


def _sigmoid(x):
    return 1.0 / (1.0 + jnp.exp(-x))


def _silu_and_grad(x):
    s = _sigmoid(x)
    return x * s, s * (1.0 + x * (1.0 - s))


def _dot(a, b):
    return jnp.dot(a, b, preferred_element_type=F32)


def _dot_nt(a, b):
    return lax.dot_general(a, b, (((1,), (1,)), ((), ())), preferred_element_type=F32)


def _dot_tn(a, b):
    return lax.dot_general(a, b, (((0,), (0,)), ((), ())), preferred_element_type=F32)


def _rows(shape, base):
    return base + lax.broadcasted_iota(jnp.int32, shape, 0)


def _rope_tables(t):
    half = HEAD // 2
    inv = ROPE_THETA ** (-np.arange(half, dtype=np.float32) / half)
    pos = (np.arange(t) - PAD).astype(np.float32)
    ang = pos[:, None] * inv[None, :]
    lane = np.arange(BLK)
    cos = np.cos(ang)[:, lane % half].astype(np.float32)
    sin = np.sin(ang)[:, lane % half].astype(np.float32)
    first = (lane % HEAD) < half
    sin_a = np.where(first[None, :], -sin, 0.0).astype(np.float32)
    sin_b = np.where(first[None, :], 0.0, sin).astype(np.float32)
    return jnp.asarray(cos), jnp.asarray(sin_a), jnp.asarray(sin_b)


def _rope(v, cos, sin_a, sin_b):
    return v * cos + pltpu.roll(v, 96, 1) * sin_a + pltpu.roll(v, 32, 1) * sin_b


def _unrope(v, cos, sin_a, sin_b):
    return v * cos - pltpu.roll(v, 96, 1) * sin_a - pltpu.roll(v, 32, 1) * sin_b


def _coords():
    return lax.axis_index("x"), lax.axis_index("y"), lax.axis_index("c")


def _all_gather(arrs, name):
    plan = _GatherPlan(arrs)

    def body(*refs):
        plan.begin(refs)
        plan.end(refs)

    return pl.pallas_call(
        body, name=name, out_shape=plan.out_shape,
        in_specs=plan.specs, out_specs=plan.specs, scratch_shapes=plan.scratch,
    )(*arrs)


class _GatherPlan:
    def __init__(self, arrs):
        n = self.n = len(arrs)
        self.out_shape = [jax.ShapeDtypeStruct((8,) + a.shape, a.dtype) for a in arrs]
        self.specs = [pl.BlockSpec(memory_space=pl.ANY)] * n
        self.scratch = [pltpu.SemaphoreType.DMA((n, 7)), pltpu.SemaphoreType.DMA((n, 7)),
                        pltpu.SemaphoreType.DMA((n,))]

    def _copies(self, refs):
        n = self.n
        ins, outs = refs[:n], refs[n:2 * n]
        send_sems, recv_sems, local_sems = refs[2 * n:]
        x, y, c = _coords()
        me, sibling = (x, y, c), (x, y, 1 - c)
        chips = [(1 - x, y), (x, 1 - y), (1 - x, 1 - y)]

        def copy(a, k, block, to, src=None):
            dst = outs[a].at[4 * block[0] + 2 * block[1] + block[2]]
            return pltpu.make_async_remote_copy(
                src_ref=dst if src is None else src, dst_ref=dst,
                send_sem=send_sems.at[a, k], recv_sem=recv_sems.at[a, k],
                device_id=to, device_id_type=MESH)

        mine = [pltpu.make_async_copy(ins[a], outs[a].at[4 * x + 2 * y + c], local_sems.at[a])
                for a in range(n)]
        first = []
        for a in range(n):
            first.append(copy(a, 0, me, sibling, src=ins[a]))
            for j, chip in enumerate(chips):
                first.append(copy(a, 1 + j, me, (*chip, c), src=ins[a]))
        return copy, mine, first, (me, sibling, chips, c)

    def begin(self, refs):
        _, mine, first, _ = self._copies(refs)
        for cp in mine + first:
            cp.start()

    def end(self, refs):
        copy, mine, first, (me, sibling, chips, c) = self._copies(refs)
        passed = []
        for j, chip in enumerate(chips):
            for a in range(self.n):
                copy(a, 1 + j, (*chip, c), me).wait_recv()
                cp = copy(a, 4 + j, (*chip, c), sibling)
                cp.start()
                passed.append(cp)
        for a in range(self.n):
            copy(a, 0, sibling, me).wait_recv()
            for j, chip in enumerate(chips):
                copy(a, 4 + j, (*chip, 1 - c), me).wait_recv()
        for cp in first + passed:
            cp.wait_send()
        for cp in mine:
            cp.wait()


class _ChipsPlan:
    def __init__(self, sums):
        n = self.n = len(sums)
        self.out_shape = [jax.ShapeDtypeStruct((3,) + s.shape[1:], s.dtype) for s in sums]
        self.specs = [pl.BlockSpec(memory_space=pl.ANY)] * n
        self.scratch = [pltpu.SemaphoreType.DMA((n, 3)), pltpu.SemaphoreType.DMA((n, 3))]

    def _copies(self, refs):
        n = self.n
        ins, outs = refs[:n], refs[n:2 * n]
        send_sems, recv_sems = refs[2 * n:]
        x, y, c = _coords()
        chips = [(1 - x, y), (x, 1 - y), (1 - x, 1 - y)]
        return [pltpu.make_async_remote_copy(
            src_ref=ins[a].at[2 * chip[0] + chip[1]], dst_ref=outs[a].at[k],
            send_sem=send_sems.at[a, k], recv_sem=recv_sems.at[a, k],
            device_id=(*chip, c), device_id_type=MESH)
            for a in range(n) for k, chip in enumerate(chips)]

    def begin(self, refs):
        for cp in self._copies(refs):
            cp.start()

    def end(self, refs):
        for cp in self._copies(refs):
            cp.wait()


def _exchange_sibling(parts, name):
    n = len(parts)

    def body(*refs):
        ins, outs = refs[:n], refs[n:2 * n]
        send_sems, recv_sems = refs[2 * n:]
        x, y, c = _coords()
        copies = [pltpu.make_async_remote_copy(
            src_ref=ins[a].at[:, 1 - c], dst_ref=outs[a],
            send_sem=send_sems.at[a], recv_sem=recv_sems.at[a],
            device_id=(x, y, 1 - c), device_id_type=MESH) for a in range(n)]
        for cp in copies:
            cp.start()
        for cp in copies:
            cp.wait()

    any_spec = pl.BlockSpec(memory_space=pl.ANY)
    return pl.pallas_call(
        body, name=name,
        out_shape=[jax.ShapeDtypeStruct((4,) + p.shape[2:], p.dtype) for p in parts],
        in_specs=[any_spec] * n, out_specs=[any_spec] * n,
        scratch_shapes=[pltpu.SemaphoreType.DMA((n,)), pltpu.SemaphoreType.DMA((n,))],
    )(*parts)


def _exchange_chips(sums, name):
    plan = _ChipsPlan(sums)

    def body(*refs):
        plan.begin(refs)
        plan.end(refs)

    return pl.pallas_call(
        body, name=name, out_shape=plan.out_shape,
        in_specs=plan.specs, out_specs=plan.specs, scratch_shapes=plan.scratch,
    )(*sums)


def _add_sibling(pos, part, recv, name):
    _, _, r, c = part.shape

    def body(pos_ref, p_ref, r_ref, o_ref):
        o_ref[...] = (p_ref[...].astype(F32) + r_ref[...].astype(F32)).astype(o_ref.dtype)

    return pl.pallas_call(
        body, name=name,
        grid_spec=pltpu.PrefetchScalarGridSpec(
            num_scalar_prefetch=1, grid=(4,),
            in_specs=[pl.BlockSpec((None, None, r, c), lambda q, pos: (q, pos[2], 0, 0)),
                      pl.BlockSpec((None, r, c), lambda q, pos: (q, 0, 0))],
            out_specs=pl.BlockSpec((None, r, c), lambda q, pos: (q, 0, 0))),
        out_shape=jax.ShapeDtypeStruct((4, r, c), part.dtype),
        compiler_params=_cparams(("arbitrary",)),
    )(pos, part, recv)


def _sum_chips(pos, sums, recv, name):
    _, r, c = sums.shape

    def body(pos_ref, s_ref, r_ref, o_ref):
        g = s_ref[...].astype(F32)
        for k in range(3):
            g = g + r_ref[k].astype(F32)
        o_ref[...] = g

    return pl.pallas_call(
        body, name=name,
        grid_spec=pltpu.PrefetchScalarGridSpec(
            num_scalar_prefetch=1, grid=(1,),
            in_specs=[pl.BlockSpec((None, r, c), lambda i, pos: (2 * pos[0] + pos[1], 0, 0)),
                      pl.BlockSpec((3, r, c), lambda i, pos: (0, 0, 0))],
            out_specs=pl.BlockSpec((r, c), lambda i, pos: (0, 0))),
        out_shape=jax.ShapeDtypeStruct((r, c), F32),
        compiler_params=_cparams(("arbitrary",)),
    )(pos, sums, recv)


def _adamw(ws, gs, ms, vs, name):
    n = len(ws)
    c1 = 1.0 / (1.0 - B1 ** STEP)
    c2 = 1.0 / (1.0 - B2 ** STEP)

    def body(*refs):
        w_r, g_r, m_r, v_r = refs[:n], refs[n:2 * n], refs[2 * n:3 * n], refs[3 * n:4 * n]
        d_o, m_o, v_o = refs[4 * n:5 * n], refs[5 * n:6 * n], refs[6 * n:7 * n]
        for a in range(n):
            g = g_r[a][...]
            m = B1 * m_r[a][...] + (1.0 - B1) * g
            v = B2 * v_r[a][...] + (1.0 - B2) * (g * g)
            d_o[a][...] = -LR * ((m * c1) / (jnp.sqrt(v * c2) + ADAM_EPS) + WD * w_r[a][...])
            m_o[a][...] = m
            v_o[a][...] = v

    shapes = [jax.ShapeDtypeStruct(w.shape, F32) for w in ws]
    outs = pl.pallas_call(body, name=name, out_shape=shapes * 3,
                          compiler_params=_cparams())(*ws, *gs, *ms, *vs)
    return outs[:n], outs[n:2 * n], outs[2 * n:]


def _reduce_small(gathered, name):
    n = len(gathered)

    def body(*refs):
        for a in range(n):
            acc = refs[a][0]
            for k in range(1, 8):
                acc = acc + refs[a][k]
            refs[n + a][...] = acc

    return pl.pallas_call(
        body, name=name,
        out_shape=[jax.ShapeDtypeStruct(g.shape[1:], F32) for g in gathered],
        compiler_params=_cparams())(*gathered)


class _Cols:
    def __init__(self, pieces, tw):
        self.pieces, self.tw = pieces, tw
        self.arrays = [p[0] for p in pieces]
        self.n_tiles = sum(p[2] for p in pieces)

    def specs(self, tm, row_of, tile_of):
        out = []
        for _, first, cnt in self.pieces:
            def imap(*g, first=first, cnt=cnt):
                return (row_of(*g), jnp.clip(tile_of(*g) - first, 0, cnt - 1))
            out.append(pl.BlockSpec((tm, self.tw), imap))
        return out

    def apply(self, t, refs, fn):
        for ref, (_, first, cnt) in zip(refs, self.pieces):
            pl.when((t >= first) & (t < first + cnt))(functools.partial(fn, ref))


def _ab_in(h, g, w_t, tables, ride):
    t = h.shape[0]
    tm = _tile(t, 544)
    src = jnp.asarray(np.array(P0_SRC, np.int32))
    plan = _GatherPlan(ride)
    nr = plan.n

    def body(src_ref, h_ref, g_ref, w_ref, cos_ref, sa_ref, sb_ref, *rest):
        o_ref, hn_ref = rest[nr:nr + 2]
        hn_s = rest[2 * nr + 2]
        comm = (*rest[:nr], *rest[nr + 2:2 * nr + 2], *rest[2 * nr + 3:])
        i, j = pl.program_id(0), pl.program_id(1)
        pl.when((i == 0) & (j == 0))(lambda: plan.begin(comm))

        @pl.when(j == 0)
        def _():
            x = h_ref[...]
            hn = (x * lax.rsqrt(jnp.mean(x * x, -1, keepdims=True) + EPS) * g_ref[...]).astype(MXU)
            hn_s[...] = hn
            hn_ref[...] = hn.astype(ACT)

        acc = _dot_nt(hn_s[...], w_ref[...])
        rope = lambda v: _rope(v, cos_ref[...], sa_ref[...], sb_ref[...])

        @pl.when((j == 4) | (j == 5))
        def _():
            o_ref[:, :BLK] = rope(acc[:, :BLK])
            o_ref[:, BLK:] = rope(acc[:, BLK:])

        @pl.when(j == 10)
        def _():
            o_ref[:, :BLK] = rope(acc[:, :BLK])
            o_ref[:, BLK:] = acc[:, BLK:]

        @pl.when((j < 4) | ((j > 5) & (j < 10)))
        def _():
            o_ref[...] = acc

        pl.when((i == t // tm - 1) & (j == 10))(lambda: plan.end(comm))

    tab = pl.BlockSpec((tm, BLK), lambda i, j, s: (i, 0))
    outs = pl.pallas_call(
        body, name="ab_in",
        grid_spec=pltpu.PrefetchScalarGridSpec(
            num_scalar_prefetch=1, grid=(t // tm, 11),
            in_specs=[pl.BlockSpec((tm, D), lambda i, j, s: (i, 0)),
                      pl.BlockSpec((1, D), lambda i, j, s: (0, 0)),
                      pl.BlockSpec((256, D), lambda i, j, s: (s[j], 0)),
                      tab, tab, tab] + plan.specs,
            out_specs=[pl.BlockSpec((tm, 256), lambda i, j, s: (i, j)),
                       pl.BlockSpec((tm, D), lambda i, j, s: (i, 0))] + plan.specs,
            scratch_shapes=[pltpu.VMEM((tm, D), MXU)] + plan.scratch),
        out_shape=[jax.ShapeDtypeStruct((t, 2816), F32), jax.ShapeDtypeStruct((t, D), ACT)] + plan.out_shape,
        compiler_params=_cparams(("arbitrary", "arbitrary")),
    )(src, h, g, w_t, *tables, *ride)
    return outs[0], outs[1], outs[2:]


def _swa_mask(n):
    r = lax.broadcasted_iota(jnp.int32, (BLK, 3 * BLK), 0)
    c = lax.broadcasted_iota(jnp.int32, (BLK, 3 * BLK), 1)
    qpos = n * BLK + r
    bpos = (n - 2) * BLK + c
    meta_ok = (c >= PAD) & (c < BLK) & (qpos - c >= BLK)
    band_ok = (c >= BLK) & (bpos >= PAD) & (qpos >= bpos) & (qpos - bpos < BLK)
    return meta_ok | band_ok


def _swa_keys(kv_ref, n):
    def blk(b):
        return kv_ref[pl.ds(pl.multiple_of(b * BLK, BLK), BLK), :]
    kv = jnp.concatenate([kv_ref[0:BLK, :], blk(jnp.maximum(n - 1, 0)), blk(n)], axis=0)
    lo = lax.broadcasted_iota(jnp.int32, (1, BLK), 1) < HEAD
    out = []
    for part in (kv[:, :BLK], kv[:, BLK:]):
        rolled = pltpu.roll(part, HEAD, 1)
        out.append((jnp.where(lo, part, rolled).astype(MXU), jnp.where(lo, rolled, part).astype(MXU)))
    return out[0], out[1], lo


def _swa_stack(ref, g, lo):
    parts = []
    for p in (2 * g, 2 * g + 1):
        x = ref[:, p * BLK:(p + 1) * BLK]
        parts += [jnp.where(lo, x, 0.0), jnp.where(lo, 0.0, x)]
    return jnp.concatenate(parts, axis=0).astype(MXU)


def _swa_probs(qs, kd, mask4, sink_ref, g):
    sink = jnp.concatenate([jnp.full((BLK, 1), sink_ref[4 * g + h], F32) for h in range(4)], axis=0)
    s = jnp.where(mask4, _dot_nt(qs, kd) * SCALE, NEG)
    m = jnp.maximum(jnp.max(s, -1, keepdims=True), sink)
    e = jnp.exp(s - m)
    e_sink = jnp.exp(sink - m)
    inv = 1.0 / (jnp.sum(e, -1, keepdims=True) + e_sink)
    return e * inv, e_sink * inv


def _swa_unstack(x, lo):
    return [jnp.where(lo, x[0:BLK], x[BLK:2 * BLK]), jnp.where(lo, x[2 * BLK:3 * BLK], x[3 * BLK:])]


def _swa_fwd(p0, sinks, ride):
    t = p0.shape[0]
    plan = _GatherPlan(ride)
    nr = plan.n

    def body(sink_ref, q_ref, kv_ref, *rest):
        o_ref = rest[nr]
        comm = (*rest[:nr], *rest[nr + 1:])
        n = pl.program_id(0)
        pl.when(n == 0)(lambda: plan.begin(comm))
        kd, vd, lo = _swa_keys(kv_ref, n)
        mask4 = jnp.concatenate([_swa_mask(n)] * 4, axis=0)
        for g in range(2):
            pr, _ = _swa_probs(_swa_stack(q_ref, g, lo), kd[g], mask4, sink_ref, g)
            pairs = _swa_unstack(_dot(pr.astype(MXU), vd[g]), lo)
            for k in range(2):
                p = 2 * g + k
                o_ref[:, p * BLK:(p + 1) * BLK] = pairs[k]
        pl.when(n == t // BLK - 1)(lambda: plan.end(comm))

    outs = pl.pallas_call(
        body, name="swa_fwd", grid=(t // BLK,),
        in_specs=[pl.BlockSpec(memory_space=pltpu.SMEM),
                  pl.BlockSpec((BLK, 512), lambda n: (n, 2)),
                  pl.BlockSpec((t, 256), lambda n: (0, 10))] + plan.specs,
        out_specs=[pl.BlockSpec((BLK, 512), lambda n: (n, 0))] + plan.specs,
        out_shape=[jax.ShapeDtypeStruct((t, 512), F32)] + plan.out_shape,
        scratch_shapes=plan.scratch,
        compiler_params=_cparams(("arbitrary",)),
    )(sinks, p0, p0, *ride)
    return outs[0], outs[1:]


def _conv_window(u_w, w_ref, n_out, first):
    rows = u_w.shape[0]
    acc = None
    for j in range(CONV_W):
        shifted = pltpu.roll(u_w, (rows - (first + j)) % rows, 0)[:n_out]
        term = shifted * w_ref[j:j + 1, :]
        acc = term if acc is None else acc + term
    return acc


def _conv_fwd(p0, conv_w, conv_b, ln_g, ln_b):
    t = p0.shape[0]
    tm = _tile(t, 544)
    hb = tm // HALO

    def body(cur_ref, prev_ref, w_ref, b_ref, g_ref, bb_ref, o_ref):
        i = pl.program_id(0)
        glu = jnp.concatenate([prev_ref[...], cur_ref[...]], axis=0)
        rw = _rows((tm + HALO, 1), i * tm - HALO)
        u_w = jnp.where(rw >= PAD, glu[:, :512] * _sigmoid(glu[:, 512:]), 0.0)
        cv = _conv_window(u_w, w_ref, tm, HALO - (CONV_W - 1)) + b_ref[...]
        xc = cv - jnp.mean(cv, -1, keepdims=True)
        ln = xc * lax.rsqrt(jnp.mean(xc * xc, -1, keepdims=True) + LN_EPS) * g_ref[...] + bb_ref[...]
        o_ref[...] = (ln * _sigmoid(ln)).astype(ACT)

    vec = pl.BlockSpec((1, 512), lambda i: (0, 0))
    return pl.pallas_call(
        body, name="conv_fwd", grid=(t // tm,),
        in_specs=[pl.BlockSpec((tm, D), lambda i: (i, 0)),
                  pl.BlockSpec((HALO, D), lambda i: (jnp.maximum(i * hb - 1, 0), 0)),
                  pl.BlockSpec((CONV_W, 512), lambda i: (0, 0)), vec, vec, vec],
        out_specs=pl.BlockSpec((tm, 512), lambda i: (i, 0)),
        out_shape=jax.ShapeDtypeStruct((t, 512), ACT),
        compiler_params=_cparams(("arbitrary",)),
    )(p0, p0, conv_w, conv_b, ln_g, ln_b)


def _ab_out(h, p0, att, c1, w_pw2, w_out, g_post):
    t = h.shape[0]
    tm = _tile(t, 272)

    def body(h_ref, ga_ref, gb_ref, att_ref, c1_ref, pw_ref, wo_ref, g_ref, h1_ref, y_ref, mix_ref):
        i = pl.program_id(0)
        sga, _ = _silu_and_grad(ga_ref[...])
        sgb, _ = _silu_and_grad(gb_ref[...])
        a = att_ref[...] * sga
        c = _dot(c1_ref[...].astype(MXU), pw_ref[...]) * sgb
        mix = jnp.concatenate([a, c], axis=1).astype(MXU)
        y = _dot(mix, wo_ref[...])
        yn = y * lax.rsqrt(jnp.mean(y * y, -1, keepdims=True) + EPS) * g_ref[...]
        h1_ref[...] = jnp.where(_rows((tm, 1), i * tm) >= PAD, h_ref[...] + yn, 0.0)
        y_ref[...] = y
        mix_ref[...] = mix.astype(ACT)

    row = lambda w, idx: pl.BlockSpec((tm, w), lambda i: (i, idx))
    full = lambda a: pl.BlockSpec(a.shape, lambda i: (0, 0))
    return pl.pallas_call(
        body, name="ab_out", grid=(t // tm,),
        in_specs=[row(D, 0), row(512, 3), row(512, 4), row(512, 0), row(512, 0),
                  full(w_pw2), full(w_out), full(g_post)],
        out_specs=[row(D, 0), row(D, 0), row(D, 0)],
        out_shape=[jax.ShapeDtypeStruct((t, D), F32), jax.ShapeDtypeStruct((t, D), F32),
                   jax.ShapeDtypeStruct((t, D), ACT)],
        compiler_params=_cparams(("arbitrary",)),
    )(h, p0, p0, att, c1, w_pw2, w_out, g_post)


def _sb_in(h, g, w):
    t = h.shape[0]
    tm = _tile(t, 544)

    def body(h_ref, g_ref, w_ref, o_ref, hn_ref, hn_s):
        @pl.when(pl.program_id(1) == 0)
        def _():
            x = h_ref[...]
            hn = (x * lax.rsqrt(jnp.mean(x * x, -1, keepdims=True) + EPS) * g_ref[...]).astype(MXU)
            hn_s[...] = hn
            hn_ref[...] = hn.astype(ACT)

        o_ref[...] = _dot(hn_s[...], w_ref[...])

    return pl.pallas_call(
        body, name="sb_in", grid=(t // tm, 8),
        in_specs=[pl.BlockSpec((tm, D), lambda i, j: (i, 0)),
                  pl.BlockSpec((1, D), lambda i, j: (0, 0)),
                  pl.BlockSpec((None, D, 512), lambda i, j: (j, 0, 0))],
        out_specs=[pl.BlockSpec((tm, 512), lambda i, j: (i, j)),
                   pl.BlockSpec((tm, D), lambda i, j: (i, 0))],
        out_shape=[jax.ShapeDtypeStruct((t, 4096), F32), jax.ShapeDtypeStruct((t, D), ACT)],
        scratch_shapes=[pltpu.VMEM((tm, D), MXU)],
        compiler_params=_cparams(("arbitrary", "arbitrary")),
    )(h, g, w)


def _split_hi_lo(x):
    hi = x.astype(MXU)
    lo = (x - hi.astype(F32)).astype(MXU)
    return hi, lo


def _scan_matrix(suffix):
    j = lax.broadcasted_iota(jnp.int32, (2 * BLK, 2 * BLK), 0) % BLK
    s = lax.broadcasted_iota(jnp.int32, (2 * BLK, 2 * BLK), 1)
    keep = (s >= BLK) | ((j > s) if suffix else (j < s))
    return jnp.where(keep, 1.0, 0.0).astype(MXU)


def _scan_packed(hi_lo, b, mat):
    cols = slice(b * BLK, (b + 1) * BLK)
    both = _dot(jnp.concatenate([hi_lo[:BLK, cols], hi_lo[BLK:, cols]], axis=1), mat)
    return both[:, :BLK], both[:, BLK:]


KC = 4
CHUNK = KC * BLK
SLACK = CHUNK - BLK
GROUPS = 2


def _sb_logits(qm, kc, valid):
    z = _dot_nt(qm, kc)
    log_beta = jnp.minimum(z, 0.0) - jnp.log(1.0 + jnp.exp(-jnp.abs(z)))
    return log_beta, jnp.where(valid, log_beta - z, 0.0)


def _sb_valid(i, first_key):
    r = lax.broadcasted_iota(jnp.int32, (BLK, CHUNK), 0)
    c = lax.broadcasted_iota(jnp.int32, (BLK, CHUNK), 1)
    kpos = first_key + c
    return (kpos >= PAD) & (kpos < i * BLK + r)


def _sb_fwd(p1):
    t = p1.shape[0]

    w = GROUPS * BLK

    def body(q_ref, k_ref, v_ref, o_ref, lt_ref, k_s, v_s):
        i = pl.program_id(1)

        @pl.when(i == 0)
        def _():
            for src, dst in ((k_ref, k_s), (v_ref, v_s)):
                dst[0:SLACK, :] = jnp.zeros((SLACK, w), MXU)
                dst[SLACK:, :] = src[...].astype(MXU)

        lo = lax.broadcasted_iota(jnp.int32, (1, BLK), 1) < HEAD
        qm = []
        for g in range(GROUPS):
            q = q_ref[:, g * BLK:(g + 1) * BLK] * SCALE
            qm += [jnp.where(lo, q, 0.0).astype(MXU), jnp.where(lo, 0.0, q).astype(MXU)]
        mat = _scan_matrix(True)

        def step(s, carry):
            start = pl.multiple_of((i - KC * s) * BLK, BLK)
            valid = _sb_valid(i, start - SLACK)
            new, staged = [], []
            for h in range(2 * GROUPS):
                lanes = slice((h // 2) * BLK, (h // 2 + 1) * BLK)
                log_beta, log_1m = _sb_logits(qm[h], k_s[pl.ds(start, CHUNK), lanes], valid)
                staged.append((log_beta, jnp.concatenate(_split_hi_lo(log_1m), axis=0)))
            for h in range(2 * GROUPS):
                lanes = slice((h // 2) * BLK, (h // 2 + 1) * BLK)
                log_beta, hi_lo = staged[h]
                run, acc = carry[2 * h], carry[2 * h + 1]
                parts = [None] * KC
                for b in reversed(range(KC)):
                    after, total = _scan_packed(hi_lo, b, mat)
                    parts[b] = after + run
                    run = run + total
                a = jnp.where(valid, jnp.exp(log_beta + jnp.concatenate(parts, axis=1)), 0.0)
                new += [run, acc + _dot(a.astype(MXU), v_s[pl.ds(start, CHUNK), lanes])]
            return tuple(new)

        zero = jnp.zeros((BLK, BLK), F32)
        res = lax.fori_loop(0, (i + KC) // KC, step, (zero,) * (4 * GROUPS))
        for g in range(GROUPS):
            lanes = slice(g * BLK, (g + 1) * BLK)
            o_ref[:, lanes] = jnp.where(lo, res[4 * g + 1], res[4 * g + 3])
            lt_ref[:, lanes] = jnp.where(lo, res[4 * g], res[4 * g + 2])

    ng = D // w
    blk = pl.BlockSpec((BLK, w), lambda hp, i: (i, hp))
    return pl.pallas_call(
        body, name="sb_fwd", grid=(ng, t // BLK),
        in_specs=[blk,
                  pl.BlockSpec((t, w), lambda hp, i: (0, ng + hp)),
                  pl.BlockSpec((t, w), lambda hp, i: (0, 2 * ng + hp))],
        out_specs=[blk, blk],
        out_shape=[jax.ShapeDtypeStruct((t, D), F32)] * 2,
        scratch_shapes=[pltpu.VMEM((t + SLACK, w), MXU), pltpu.VMEM((t + SLACK, w), MXU)],
        compiler_params=_cparams(("arbitrary", "arbitrary")),
    )(p1, p1, p1)


def _sb_out(o, p1, w_out, h1, g_post, tgt):
    t = o.shape[0]
    tm = _tile(t, 272)

    def body(o_ref, g_ref, w_ref, h_ref, gp_ref, t_ref,
             loss_ref, dh_ref, dy_ref, m_ref, do_ref, dg_ref, dgp_ref):
        i = pl.program_id(0)

        @pl.when(i == 0)
        def _():
            loss_ref[...] = jnp.zeros_like(loss_ref)
            dgp_ref[...] = jnp.zeros_like(dgp_ref)

        gate = g_ref[...]
        sg, dsg = _silu_and_grad(gate)
        ov = o_ref[...]
        m = (ov * sg).astype(MXU)
        y = _dot(m, w_ref[...])
        r = lax.rsqrt(jnp.mean(y * y, -1, keepdims=True) + EPS)
        yhat = y * r
        h2 = h_ref[...] + yhat * gp_ref[...]
        diff = jnp.where(_rows((tm, 1), i * tm) >= BLK, h2 - t_ref[...], 0.0)
        loss_ref[...] += jnp.full(loss_ref.shape, 0.5 / D, F32) * jnp.sum(diff * diff)
        dh = diff * (1.0 / D)
        dgp_ref[...] += jnp.sum(dh * yhat, 0, keepdims=True)
        dyn = dh * gp_ref[...]
        dy = (r * (dyn - yhat * jnp.mean(dyn * yhat, -1, keepdims=True))).astype(MXU)
        dm = _dot_nt(dy, w_ref[...])
        dh_ref[...] = dh
        dy_ref[...] = dy.astype(ACT)
        m_ref[...] = m.astype(ACT)
        do_ref[...] = dm * sg
        dg_ref[...] = (dm * ov * dsg).astype(ACT)

    row = lambda idx: pl.BlockSpec((tm, D), lambda i: (i, idx))
    full = lambda a: pl.BlockSpec(a.shape, lambda i: (0, 0))
    acc = lambda s: pl.BlockSpec(s, lambda i: (0, 0))
    return pl.pallas_call(
        body, name="sb_out", grid=(t // tm,),
        in_specs=[row(0), row(3), full(w_out), row(0), full(g_post), row(0)],
        out_specs=[acc((8, BLK)), row(0), row(0), row(0), row(0), row(0), acc((1, D))],
        out_shape=[jax.ShapeDtypeStruct((8, BLK), F32), jax.ShapeDtypeStruct((t, D), F32),
                   jax.ShapeDtypeStruct((t, D), ACT), jax.ShapeDtypeStruct((t, D), ACT),
                   jax.ShapeDtypeStruct((t, D), F32), jax.ShapeDtypeStruct((t, D), ACT),
                   jax.ShapeDtypeStruct((1, D), F32)],
        compiler_params=_cparams(("arbitrary",)),
    )(o, p1, w_out, h1, g_post, tgt)


def _sb_bwd(p1, ltot, do):
    t = p1.shape[0]
    nb = t // BLK

    w = GROUPS * BLK

    def body(q_ref, k_ref, v_ref, lt_ref, do_ref, dq_ref, dk_ref, dv_ref, k_s, v_s, dk_s, dv_s):
        i = pl.program_id(1)
        lo = lax.broadcasted_iota(jnp.int32, (1, BLK), 1) < HEAD

        @pl.when(i == 0)
        def _():
            for src, dst in ((k_ref, k_s), (v_ref, v_s)):
                dst[0:t, :] = src[...].astype(MXU)
                dst[t:, :] = jnp.zeros((SLACK, w), MXU)
            dk_s[...] = jnp.zeros_like(dk_s)
            dv_s[...] = jnp.zeros_like(dv_s)

        qm, dom, row_total, q2, do2 = [], [], [], [], []
        for g in range(GROUPS):
            lanes = slice(g * BLK, (g + 1) * BLK)
            q, dout, lt = q_ref[:, lanes] * SCALE, do_ref[:, lanes], lt_ref[:, lanes]
            qm += [jnp.where(lo, q, 0.0).astype(MXU), jnp.where(lo, 0.0, q).astype(MXU)]
            dom += [jnp.where(lo, dout, 0.0).astype(MXU), jnp.where(lo, 0.0, dout).astype(MXU)]
            q2.append(jnp.concatenate(qm[-2:], axis=0))
            do2.append(jnp.concatenate(dom[-2:], axis=0))
            lt_r = pltpu.roll(lt, HEAD, 1)
            row_total += [jnp.where(lo, lt, lt_r), jnp.where(lo, lt_r, lt)]
        mat_l = _scan_matrix(True)
        mat_g = _scan_matrix(False)
        heads = range(2 * GROUPS)

        def step(s, carry):
            start = pl.multiple_of(s * CHUNK, BLK)
            keys = lambda ref, h: ref[pl.ds(start, CHUNK), (h // 2) * BLK:(h // 2 + 1) * BLK]
            valid = _sb_valid(i, start)
            new, dzs, probs, st1, st2 = [], [], [], [], []
            for h in heads:
                log_beta, log_1m = _sb_logits(qm[h], keys(k_s, h), valid)
                st1.append((log_beta, jnp.concatenate(_split_hi_lo(log_1m), axis=0)))
            for h in heads:
                log_beta, hi_lo = st1[h]
                run = carry[3 * h]
                parts = []
                for b in range(KC):
                    after, total = _scan_packed(hi_lo, b, mat_l)
                    run = run + total
                    parts.append(after + (row_total[h] - run))
                a = jnp.where(valid, jnp.exp(log_beta + jnp.concatenate(parts, axis=1)), 0.0)
                g = _dot_nt(dom[h], keys(v_s, h)) * a
                probs.append(a.astype(MXU))
                st2.append((run, g, jnp.concatenate(_split_hi_lo(g), axis=0)))
            for h in heads:
                run, g, hi_lo = st2[h]
                run_g, dq = carry[3 * h + 1], carry[3 * h + 2]
                parts = []
                for b in range(KC):
                    before, total_g = _scan_packed(hi_lo, b, mat_g)
                    parts.append(before + run_g)
                    run_g = run_g + total_g
                sig = jnp.exp(st1[h][0])
                dz = jnp.where(valid, g * (1.0 - sig) - sig * jnp.concatenate(parts, axis=1), 0.0)
                dzm = dz.astype(MXU)
                dzs.append(dzm)
                new += [run, run_g, dq + _dot(dzm, keys(k_s, h))]
            for g in range(GROUPS):
                lanes = slice(g * BLK, (g + 1) * BLK)
                dk_s[pl.ds(start, CHUNK), lanes] += _dot_tn(jnp.concatenate(dzs[2 * g:2 * g + 2], axis=0), q2[g])
                dv_s[pl.ds(start, CHUNK), lanes] += _dot_tn(jnp.concatenate(probs[2 * g:2 * g + 2], axis=0), do2[g])
            return tuple(new)

        zero = jnp.zeros((BLK, BLK), F32)
        res = lax.fori_loop(0, (i + KC) // KC, step, (zero,) * (6 * GROUPS))
        for g in range(GROUPS):
            dq = jnp.where(lo, res[6 * g + 2], res[6 * g + 5])
            dq_ref[:, g * BLK:(g + 1) * BLK] = (dq * SCALE).astype(ACT)

        @pl.when(i == nb - 1)
        def _():
            dk_ref[...] = dk_s[0:t, :].astype(ACT)
            dv_ref[...] = dv_s[0:t, :].astype(ACT)

    ng = D // w
    blk = pl.BlockSpec((BLK, w), lambda hp, i: (i, hp))
    col = lambda off: pl.BlockSpec((t, w), lambda hp, i: (0, off + hp))
    return pl.pallas_call(
        body, name="sb_bwd", grid=(ng, nb),
        in_specs=[blk, col(ng), col(2 * ng), blk, blk],
        out_specs=[blk, col(0), col(0)],
        out_shape=[jax.ShapeDtypeStruct((t, D), ACT)] * 3,
        scratch_shapes=[pltpu.VMEM((t + SLACK, w), MXU), pltpu.VMEM((t + SLACK, w), MXU),
                        pltpu.VMEM((t + SLACK, w), F32), pltpu.VMEM((t + SLACK, w), F32)],
        compiler_params=_cparams(("arbitrary", "arbitrary")),
    )(p1, p1, p1, ltot, do)


def _mid_bwd(dp1, w_sb, h1, g_pre1, dh2, y0, g_post0, w_out, p0, att, c1, w_pw2):
    t = h1.shape[0]
    tm = _tile(t, 272)

    def body(*refs):
        d_refs = refs[:4]
        (w_ref, h_ref, g1_ref, dh2_ref, y_ref, g0_ref, wo_ref, ga_ref, gb_ref, att_ref, c1_ref,
         pw_ref, dh1_ref, dy_ref, dga_ref, dgb_ref, datt_ref, dc1_ref, dc2_ref, dg1_ref, dg0_ref,
         acc) = refs[4:]
        i, j = pl.program_id(0), pl.program_id(1)

        @pl.when((i == 0) & (j == 0))
        def _():
            dg1_ref[...] = jnp.zeros_like(dg1_ref)
            dg0_ref[...] = jnp.zeros_like(dg0_ref)

        @pl.when(j == 0)
        def _():
            acc[...] = jnp.zeros_like(acc)

        def add(ref):
            acc[...] += _dot_nt(ref[...].astype(MXU), w_ref[...])
        dp1.apply(j, d_refs, add)

        @pl.when(j == 7)
        def _():
            dhn = acc[...]
            x = h_ref[...]
            r = lax.rsqrt(jnp.mean(x * x, -1, keepdims=True) + EPS)
            xhat = x * r
            dg1_ref[...] += jnp.sum(dhn * xhat, 0, keepdims=True)
            dxn = dhn * g1_ref[...]
            dh1 = dh2_ref[...] + r * (dxn - xhat * jnp.mean(dxn * xhat, -1, keepdims=True))
            dh1_ref[...] = dh1
            y = y_ref[...]
            ry = lax.rsqrt(jnp.mean(y * y, -1, keepdims=True) + EPS)
            yhat = y * ry
            dg0_ref[...] += jnp.sum(dh1 * yhat, 0, keepdims=True)
            dyn = dh1 * g0_ref[...]
            dy = (ry * (dyn - yhat * jnp.mean(dyn * yhat, -1, keepdims=True))).astype(MXU)
            dy_ref[...] = dy.astype(ACT)
            dmix = _dot_nt(dy, wo_ref[...])
            da, dc = dmix[:, :512], dmix[:, 512:]
            sga, dsga = _silu_and_grad(ga_ref[...])
            sgb, dsgb = _silu_and_grad(gb_ref[...])
            datt_ref[...] = da * sga
            dga_ref[...] = (da * att_ref[...] * dsga).astype(ACT)
            c2 = _dot(c1_ref[...].astype(MXU), pw_ref[...])
            dc2 = (dc * sgb).astype(MXU)
            dgb_ref[...] = (dc * c2 * dsgb).astype(ACT)
            dc2_ref[...] = dc2.astype(ACT)
            dc1_ref[...] = _dot_nt(dc2, pw_ref[...])

    row = lambda w, idx: pl.BlockSpec((tm, w), lambda i, j: (i, idx))
    full = lambda a: pl.BlockSpec(a.shape, lambda i, j: (0, 0))
    acc_spec = pl.BlockSpec((1, D), lambda i, j: (0, 0))
    sd = jax.ShapeDtypeStruct
    return pl.pallas_call(
        body, name="mid_bwd", grid=(t // tm, 8),
        in_specs=dp1.specs(tm, lambda i, j: i, lambda i, j: j) + [
            pl.BlockSpec((None, D, 512), lambda i, j: (j, 0, 0)),
            row(D, 0), full(g_pre1), row(D, 0), row(D, 0), full(g_post0), full(w_out),
            row(512, 3), row(512, 4), row(512, 0), row(512, 0), full(w_pw2)],
        out_specs=[row(D, 0), row(D, 0), row(512, 0), row(512, 0), row(512, 0), row(512, 0),
                   row(512, 0), acc_spec, acc_spec],
        out_shape=[sd((t, D), F32), sd((t, D), ACT), sd((t, 512), ACT), sd((t, 512), ACT),
                   sd((t, 512), F32), sd((t, 512), F32), sd((t, 512), ACT),
                   sd((1, D), F32), sd((1, D), F32)],
        scratch_shapes=[pltpu.VMEM((tm, D), F32)],
        compiler_params=_cparams(("arbitrary", "arbitrary")),
    )(*dp1.arrays, w_sb, h1, g_pre1, dh2, y0, g_post0, w_out, p0, p0, att, c1, w_pw2)


def _conv_bwd(p0, dc1, conv_w, conv_b, ln_g, ln_b):
    t = p0.shape[0]
    tm = _tile(t, 544)
    hb = tm // HALO
    last = t // HALO - 1

    def body(cur_ref, prev_ref, next_ref, d_ref, dn_ref, w_ref, b_ref, g_ref, bb_ref,
             dglu_ref, dw_ref, db_ref, dlg_ref, dlb_ref):
        i = pl.program_id(0)

        @pl.when(i == 0)
        def _():
            for ref in (dw_ref, db_ref, dlg_ref, dlb_ref):
                ref[...] = jnp.zeros_like(ref)

        glu = jnp.concatenate([prev_ref[...], cur_ref[...], next_ref[...]], axis=0)
        rw = _rows((tm + 2 * HALO, 1), i * tm - HALO)
        ga, sg = glu[:, :512], _sigmoid(glu[:, 512:])
        u_w = jnp.where((rw >= PAD) & (rw < t), ga * sg, 0.0)
        n_cv = tm + HALO
        cv = _conv_window(u_w, w_ref, n_cv, HALO - (CONV_W - 1)) + b_ref[...]
        xc = cv - jnp.mean(cv, -1, keepdims=True)
        rstd = lax.rsqrt(jnp.mean(xc * xc, -1, keepdims=True) + LN_EPS)
        cvhat = xc * rstd
        ln = cvhat * g_ref[...] + bb_ref[...]
        _, dsl = _silu_and_grad(ln)
        rc = _rows((n_cv, 1), i * tm)
        dc = jnp.concatenate([d_ref[...], dn_ref[...]], axis=0)
        dln = jnp.where(rc < t, dc * dsl, 0.0)
        dhat = dln * g_ref[...]
        dcv = rstd * (dhat - jnp.mean(dhat, -1, keepdims=True)
                      - cvhat * jnp.mean(dhat * cvhat, -1, keepdims=True))
        own = dcv[:tm]
        dlg_ref[...] += jnp.sum((dln * cvhat)[:tm], 0, keepdims=True)
        dlb_ref[...] += jnp.sum(dln[:tm], 0, keepdims=True)
        db_ref[...] += jnp.sum(own, 0, keepdims=True)
        rows = tm + 2 * HALO
        du = None
        for j in range(CONV_W):
            first = HALO - (CONV_W - 1) + j
            shifted = pltpu.roll(u_w, (rows - first) % rows, 0)[:tm]
            dw_ref[j:j + 1, :] += jnp.sum(own * shifted, 0, keepdims=True)
            back = pltpu.roll(dcv, (n_cv - (CONV_W - 1 - j)) % n_cv, 0)[:tm]
            term = back * w_ref[j:j + 1, :]
            du = term if du is None else du + term
        du = jnp.where(_rows((tm, 1), i * tm) >= PAD, du, 0.0)
        ga_c, sg_c = ga[HALO:HALO + tm], sg[HALO:HALO + tm]
        dglu_ref[:, :512] = (du * sg_c).astype(ACT)
        dglu_ref[:, 512:] = (du * ga_c * sg_c * (1.0 - sg_c)).astype(ACT)

    vec = pl.BlockSpec((1, 512), lambda i: (0, 0))
    nxt = lambda i: (jnp.minimum((i + 1) * hb, last), 0)
    return pl.pallas_call(
        body, name="conv_bwd", grid=(t // tm,),
        in_specs=[pl.BlockSpec((tm, D), lambda i: (i, 0)),
                  pl.BlockSpec((HALO, D), lambda i: (jnp.maximum(i * hb - 1, 0), 0)),
                  pl.BlockSpec((HALO, D), nxt),
                  pl.BlockSpec((tm, 512), lambda i: (i, 0)),
                  pl.BlockSpec((HALO, 512), nxt),
                  pl.BlockSpec((CONV_W, 512), lambda i: (0, 0)), vec, vec, vec],
        out_specs=[pl.BlockSpec((tm, D), lambda i: (i, 0)),
                   pl.BlockSpec((HALO, 512), lambda i: (0, 0)), vec, vec, vec],
        out_shape=[jax.ShapeDtypeStruct((t, D), ACT), jax.ShapeDtypeStruct((HALO, 512), F32)]
        + [jax.ShapeDtypeStruct((1, 512), F32)] * 3,
        compiler_params=_cparams(("arbitrary",)),
    )(p0, p0, p0, dc1, dc1, conv_w, conv_b, ln_g, ln_b)


def _swa_bwd(p0, datt, sinks, tables, ride):
    t = p0.shape[0]
    nb = t // BLK
    plan = _ChipsPlan(ride)
    nr = plan.n

    def body(sink_ref, q_ref, kv_ref, d_ref, cos_ref, sa_ref, sb_ref, *rest):
        dq_ref, dkv_ref, ds_ref = rest[nr:nr + 3]
        acc = rest[2 * nr + 3]
        comm = (*rest[:nr], *rest[nr + 3:2 * nr + 3], *rest[2 * nr + 4:])
        n = pl.program_id(0)
        pl.when(n == 0)(lambda: plan.begin(comm))

        @pl.when(n == 0)
        def _():
            acc[...] = jnp.zeros_like(acc)
            ds_ref[...] = jnp.zeros_like(ds_ref)

        kd, vd, lo = _swa_keys(kv_ref, n)
        mask4 = jnp.concatenate([_swa_mask(n)] * 4, axis=0)
        row0 = pl.multiple_of(n * BLK, BLK)
        tabs = [r[pl.ds(row0, BLK), :] for r in (cos_ref, sa_ref, sb_ref)]
        dk_g, dv_g = [], []
        for g in range(2):
            qs, dos = _swa_stack(q_ref, g, lo), _swa_stack(d_ref, g, lo)
            pr, p_sink = _swa_probs(qs, kd[g], mask4, sink_ref, g)
            dpr = _dot_nt(dos, vd[g])
            delta = jnp.sum(pr * dpr, -1, keepdims=True)
            dsc = (pr * (dpr - delta) * SCALE).astype(MXU)
            sunk = p_sink * delta
            for h in range(4):
                row = 4 * g + h
                ds_ref[row:row + 1, :] += jnp.full((1, BLK), -1.0, F32) * jnp.sum(sunk[h * BLK:(h + 1) * BLK])
            pairs = _swa_unstack(_dot(dsc, kd[g]), lo)
            for k in range(2):
                p = 2 * g + k
                dq_ref[:, p * BLK:(p + 1) * BLK] = _unrope(pairs[k], *tabs).astype(ACT)
            dk_g.append(_dot_tn(dsc, qs))
            dv_g.append(_dot_tn(pr.astype(MXU), dos))
        fold = lambda a: a + pltpu.roll(a, HEAD, 1)
        dk = jnp.where(lo, fold(dk_g[0]), fold(dk_g[1]))
        dv = jnp.where(lo, fold(dv_g[0]), fold(dv_g[1]))
        dkv = jnp.concatenate([dk, dv], axis=1)
        prev = pl.multiple_of(jnp.maximum(n - 1, 0) * BLK, BLK)
        acc[0:BLK, :] += dkv[0:BLK]
        acc[pl.ds(prev, BLK), :] += dkv[BLK:2 * BLK]
        acc[pl.ds(row0, BLK), :] += dkv[2 * BLK:]

        @pl.when(n == nb - 1)
        def _():
            dkv_ref[:, :BLK] = _unrope(acc[:, :BLK], cos_ref[...], sa_ref[...], sb_ref[...]).astype(ACT)
            dkv_ref[:, BLK:] = acc[:, BLK:].astype(ACT)

        pl.when(n == nb - 1)(lambda: plan.end(comm))

    tab = pl.BlockSpec((t, BLK), lambda n: (0, 0))
    outs = pl.pallas_call(
        body, name="swa_bwd", grid=(nb,),
        in_specs=[pl.BlockSpec(memory_space=pltpu.SMEM),
                  pl.BlockSpec((BLK, 512), lambda n: (n, 2)),
                  pl.BlockSpec((t, 256), lambda n: (0, 10)),
                  pl.BlockSpec((BLK, 512), lambda n: (n, 0)), tab, tab, tab] + plan.specs,
        out_specs=[pl.BlockSpec((BLK, 512), lambda n: (n, 0)),
                   pl.BlockSpec((t, 256), lambda n: (0, 0)),
                   pl.BlockSpec((8, BLK), lambda n: (0, 0))] + plan.specs,
        out_shape=[jax.ShapeDtypeStruct((t, 512), ACT), jax.ShapeDtypeStruct((t, 256), ACT),
                   jax.ShapeDtypeStruct((8, BLK), F32)] + plan.out_shape,
        scratch_shapes=[pltpu.VMEM((t, 256), F32)] + plan.scratch,
        compiler_params=_cparams(("arbitrary",)),
    )(sinks, p0, p0, datt, *tables, *ride)
    return outs[0], outs[1], outs[2], outs[3:]


def _ab_in_bwd(dp0, w_t, h0, g_pre, dh1, ride):
    t = h0.shape[0]
    tm = _tile(t, 544)
    plan = _ChipsPlan(ride)
    nr = plan.n

    def body(*refs):
        d_refs = refs[:5]
        w_ref, h_ref, g_ref, dh1_ref = refs[5:9]
        rest = refs[9:]
        dh0_ref, dg_ref = rest[nr:nr + 2]
        acc = rest[2 * nr + 2]
        comm = (*rest[:nr], *rest[nr + 2:2 * nr + 2], *rest[2 * nr + 3:])
        i, j = pl.program_id(0), pl.program_id(1)
        pl.when((i == 0) & (j == 0))(lambda: plan.begin(comm))

        @pl.when((i == 0) & (j == 0))
        def _():
            dg_ref[...] = jnp.zeros_like(dg_ref)

        @pl.when(j == 0)
        def _():
            acc[...] = jnp.zeros_like(acc)

        def add(ref):
            acc[...] += _dot(ref[...].astype(MXU), w_ref[...])
        dp0.apply(j, d_refs, add)

        @pl.when(j == 10)
        def _():
            dhn = acc[...]
            x = h_ref[...]
            r = lax.rsqrt(jnp.mean(x * x, -1, keepdims=True) + EPS)
            xhat = x * r
            dg_ref[...] += jnp.sum(dhn * xhat, 0, keepdims=True)
            dxn = dhn * g_ref[...]
            dh0_ref[...] = dh1_ref[...] + r * (dxn - xhat * jnp.mean(dxn * xhat, -1, keepdims=True))

        pl.when((i == t // tm - 1) & (j == 10))(lambda: plan.end(comm))

    row = pl.BlockSpec((tm, D), lambda i, j: (i, 0))
    vec = pl.BlockSpec((1, D), lambda i, j: (0, 0))
    outs = pl.pallas_call(
        body, name="ab_in_bwd", grid=(t // tm, 11),
        in_specs=dp0.specs(tm, lambda i, j: i, lambda i, j: j) + [
            pl.BlockSpec((256, D), lambda i, j: (j, 0)), row, vec, row] + plan.specs,
        out_specs=[row, vec] + plan.specs,
        out_shape=[jax.ShapeDtypeStruct((t, D), F32), jax.ShapeDtypeStruct((1, D), F32)] + plan.out_shape,
        scratch_shapes=[pltpu.VMEM((tm, D), F32)] + plan.scratch,
        compiler_params=_cparams(("arbitrary", "arbitrary")),
    )(*dp0.arrays, w_t, h0, g_pre, dh1, *ride)
    return outs[0], outs[1], outs[2:]


def _dw_plain(a, b, name):
    t, m = a.shape
    n = b.shape[1]
    tm = _tile(t, 544)
    tn = min(n, 512)
    nk = t // tm

    def body(a_ref, b_ref, o_ref, acc):
        k = pl.program_id(1)

        @pl.when(k == 0)
        def _():
            acc[...] = jnp.zeros_like(acc)

        acc[...] += _dot_tn(a_ref[...].astype(MXU), b_ref[...].astype(MXU))

        @pl.when(k == nk - 1)
        def _():
            o_ref[...] = acc[...].astype(WIRE)

    return pl.pallas_call(
        body, name=name, grid=(n // tn, nk),
        in_specs=[pl.BlockSpec((tm, m), lambda j, k: (k, 0)),
                  pl.BlockSpec((tm, tn), lambda j, k: (k, j))],
        out_specs=pl.BlockSpec((m, tn), lambda j, k: (0, j)),
        out_shape=jax.ShapeDtypeStruct((m, n), WIRE),
        scratch_shapes=[pltpu.VMEM((m, tn), F32)],
        compiler_params=_cparams(("arbitrary", "arbitrary")),
    )(a, b)


def _dw_chunks(hn, dp, name):
    t = hn.shape[0]
    tm = _tile(t, 544)
    nk = t // tm
    nt, tw = dp.n_tiles, dp.tw
    n_in = len(dp.arrays)

    def body(*refs):
        d_refs = refs[:n_in]
        h_ref, o_ref, acc = refs[n_in:]
        j, k = pl.program_id(0), pl.program_id(1)

        @pl.when(k == 0)
        def _():
            acc[...] = jnp.zeros_like(acc)

        def add(ref):
            acc[...] += _dot_tn(h_ref[...].astype(MXU), ref[...].astype(MXU))
        dp.apply(j, d_refs, add)

        @pl.when(k == nk - 1)
        def _():
            o_ref[...] = acc[...].astype(WIRE)

    return pl.pallas_call(
        body, name=name, grid=(nt, nk),
        in_specs=dp.specs(tm, lambda j, k: k, lambda j, k: j) + [
            pl.BlockSpec((tm, D), lambda j, k: (k, 0))],
        out_specs=pl.BlockSpec((None, D, tw), lambda j, k: (j, 0, 0)),
        out_shape=jax.ShapeDtypeStruct((nt, D, tw), WIRE),
        scratch_shapes=[pltpu.VMEM((D, tw), F32)],
        compiler_params=_cparams(("arbitrary", "arbitrary")),
    )(*dp.arrays, hn)


def _dw_transposed(dp, hn, name):
    t = hn.shape[0]
    tm = _tile(t, 544)
    nk = t // tm
    nt, tw = dp.n_tiles, dp.tw
    n_in = len(dp.arrays)

    def body(*refs):
        d_refs = refs[:n_in]
        h_ref, o_ref, acc = refs[n_in:]
        j, k = pl.program_id(0), pl.program_id(1)

        @pl.when(k == 0)
        def _():
            acc[...] = jnp.zeros_like(acc)

        def add(ref):
            acc[...] += _dot_tn(ref[...].astype(MXU), h_ref[...].astype(MXU))
        dp.apply(j, d_refs, add)

        @pl.when(k == nk - 1)
        def _():
            o_ref[...] = acc[...].astype(WIRE)

    return pl.pallas_call(
        body, name=name, grid=(nt, nk),
        in_specs=dp.specs(tm, lambda j, k: k, lambda j, k: j) + [
            pl.BlockSpec((tm, D), lambda j, k: (k, 0))],
        out_specs=pl.BlockSpec((tw, D), lambda j, k: (j, 0)),
        out_shape=jax.ShapeDtypeStruct((nt * tw, D), WIRE),
        scratch_shapes=[pltpu.VMEM((tw, D), F32)],
        compiler_params=_cparams(("arbitrary", "arbitrary")),
    )(*dp.arrays, hn)


def kernel(x, meta_tokens, ab_pre_norm, ab_w_in, ab_sinks, ab_conv_w, ab_conv_b, ab_conv_ln_g, ab_conv_ln_b, ab_w_pw2, ab_w_out, ab_post_norm, sb_pre_norm, sb_w_in, sb_w_out, sb_post_norm, loss_target, m_meta_tokens, m_ab_pre_norm, m_ab_w_in, m_ab_sinks, m_ab_conv_w, m_ab_conv_b, m_ab_conv_ln_g, m_ab_conv_ln_b, m_ab_w_pw2, m_ab_w_out, m_ab_post_norm, m_sb_pre_norm, m_sb_w_in, m_sb_w_out, m_sb_post_norm, v_meta_tokens, v_ab_pre_norm, v_ab_w_in, v_ab_sinks, v_ab_conv_w, v_ab_conv_b, v_ab_conv_ln_g, v_ab_conv_ln_b, v_ab_w_pw2, v_ab_w_out, v_ab_post_norm, v_sb_pre_norm, v_sb_w_in, v_sb_w_out, v_sb_post_norm):
    seq = x.shape[1]
    t = seq + BLK
    mx, my, mc = _coords()
    me = 4 * mx + 2 * my + mc
    pos = jnp.stack([mx, my, mc, me]).astype(jnp.int32)

    w_ab_t, *small = _all_gather(
        [ab_w_in[0].T.astype(WIRE), meta_tokens, ab_conv_w[0], sb_pre_norm, sb_post_norm], "gather_first")
    w_ab_t = w_ab_t.reshape(2816, D)
    meta_full = jnp.moveaxis(small[0], 0, 1).reshape(N_META, D)
    conv_w = jnp.moveaxis(small[1], 0, 1).reshape(CONV_W, 512)
    sb_pre = jnp.moveaxis(small[2], 0, 1).reshape(1, D)
    sb_post = jnp.moveaxis(small[3], 0, 1).reshape(1, D)

    h0 = jnp.concatenate([jnp.zeros((PAD, D), F32), meta_full, x[0]], axis=0)
    tgt = jnp.concatenate([jnp.zeros((BLK, D), F32), loss_target[0]], axis=0)
    tables = _rope_tables(t)
    sinks = ab_sinks[0]

    p0, hn0, (w_oa, w_pw, w_os) = _ab_in(
        h0, ab_pre_norm, w_ab_t, tables,
        [ab_w_out[0].astype(WIRE), ab_w_pw2[0].astype(WIRE), sb_w_out[0].astype(WIRE)])
    w_oa, w_os, w_pw = w_oa.reshape(D, D), w_os.reshape(D, D), w_pw.reshape(512, 512)
    att, (w_sb,) = _swa_fwd(p0, sinks, [sb_w_in[0].astype(WIRE)])
    c1 = _conv_fwd(p0, conv_w, ab_conv_b, ab_conv_ln_g, ab_conv_ln_b)
    h1, y0, mix = _ab_out(h0, p0, att, c1, w_pw, w_oa, ab_post_norm)
    p1, hn1 = _sb_in(h1, sb_pre, w_sb)
    o, ltot = _sb_fwd(p1)
    loss_part, dh2, dy1, m1, do, dgate, dg_sb_post = _sb_out(o, p1, w_os, h1, sb_post, tgt)

    dq1, dk1, dv1 = _sb_bwd(p1, ltot, do)
    dp1 = _Cols([(dq1, 0, 2), (dk1, 2, 2), (dv1, 4, 2), (dgate, 6, 2)], 512)

    def sibling_stage(parts, names, tag):
        got = _exchange_sibling(parts, "reduce_sibling_" + tag)
        return [_add_sibling(pos, p, r, "add_sibling_" + nm) for p, r, nm in zip(parts, got, names)]

    def finish(sums, got, names):
        return [_sum_chips(pos, s, r, "sum_chips_" + nm) for s, r, nm in zip(sums, got, names)]

    names1 = ["sb_in", "sb_out"]
    sums1 = sibling_stage([_dw_chunks(hn1, dp1, "dw_sb_in").reshape(4, 2, D, 512),
                           _dw_plain(m1, dy1, "dw_sb_out").reshape(4, 2, BLK, D)], names1, "sb")
    dh1, dy0, dga, dgb, datt, dc1, dc2, dg_sb_pre, dg_ab_post = _mid_bwd(
        dp1, w_sb, h1, sb_pre, dh2, y0, ab_post_norm, w_oa, p0, att, c1, w_pw)
    names2 = ["ab_out", "pw2"]
    sums2 = sibling_stage([_dw_plain(mix, dy0, "dw_ab_out").reshape(4, 2, BLK, D),
                           _dw_plain(c1, dc2, "dw_pw2").reshape(4, 2, 64, 512)], names2, "ab_out")
    dglu, dconv_w, dconv_b, dln_g, dln_b = _conv_bwd(p0, dc1, conv_w, ab_conv_b, ab_conv_ln_g, ab_conv_ln_b)
    dq0, dkv0, dsinks, got = _swa_bwd(p0, datt, sinks, tables, sums1 + sums2)
    g_sb_w_in, g_sb_w_out, g_ab_w_out, g_ab_w_pw2 = finish(sums1 + sums2, got, names1 + names2)
    dp0 = _Cols([(dq0, 0, 2), (dkv0, 2, 1), (dga, 3, 2), (dglu, 5, 4), (dgb, 9, 2)], 256)

    sums0 = sibling_stage([_dw_transposed(dp0, hn0, "dw_ab_in").reshape(4, 2, 352, D)], ["ab_in"], "ab_in")
    dh0, dg_ab_pre, got0 = _ab_in_bwd(dp0, w_ab_t, h0, ab_pre_norm, dh1, sums0)
    g_ab_w_in = finish(sums0, got0, ["ab_in"])[0].T

    small_parts = [dh0[PAD:BLK], dg_ab_pre, dsinks, dconv_w, dconv_b, dln_g, dln_b,
                   dg_ab_post, dg_sb_pre, dg_sb_post, loss_part]
    red = _reduce_small(_all_gather(small_parts, "gather_small_grads"), "reduce_small")
    col = lambda a, w: lax.dynamic_slice_in_dim(a, me * w, w, axis=1)
    g_meta = col(red[0], BLK)
    g_ab_pre = red[1]
    g_sinks = red[2][:, 0].reshape(1, 8)
    g_conv_w = col(red[3][:CONV_W], 64)
    g_conv_b, g_ln_g, g_ln_b, g_ab_post = red[4], red[5], red[6], red[7]
    g_sb_pre, g_sb_post = col(red[8], BLK), col(red[9], BLK)

    loss = red[10][0, 0]
    grad_x = dh0[BLK:][None]

    weights = [meta_tokens, ab_pre_norm, ab_w_in[0], ab_sinks, ab_conv_w[0], ab_conv_b, ab_conv_ln_g,
               ab_conv_ln_b, ab_w_pw2[0], ab_w_out[0], ab_post_norm, sb_pre_norm, sb_w_in[0],
               sb_w_out[0], sb_post_norm]
    grads = [g_meta, g_ab_pre, g_ab_w_in, g_sinks, g_conv_w, g_conv_b, g_ln_g, g_ln_b, g_ab_w_pw2,
             g_ab_w_out, g_ab_post, g_sb_pre, g_sb_w_in, g_sb_w_out, g_sb_post]
    ms = [m_meta_tokens, m_ab_pre_norm, m_ab_w_in[0], m_ab_sinks, m_ab_conv_w[0], m_ab_conv_b,
          m_ab_conv_ln_g, m_ab_conv_ln_b, m_ab_w_pw2[0], m_ab_w_out[0], m_ab_post_norm,
          m_sb_pre_norm, m_sb_w_in[0], m_sb_w_out[0], m_sb_post_norm]
    vs = [v_meta_tokens, v_ab_pre_norm, v_ab_w_in[0], v_ab_sinks, v_ab_conv_w[0], v_ab_conv_b,
          v_ab_conv_ln_g, v_ab_conv_ln_b, v_ab_w_pw2[0], v_ab_w_out[0], v_ab_post_norm,
          v_sb_pre_norm, v_sb_w_in[0], v_sb_w_out[0], v_sb_post_norm]
    lead = [w.ndim == 3 for w in (meta_tokens, ab_pre_norm, ab_w_in, ab_sinks, ab_conv_w, ab_conv_b,
                                   ab_conv_ln_g, ab_conv_ln_b, ab_w_pw2, ab_w_out, ab_post_norm,
                                   sb_pre_norm, sb_w_in, sb_w_out, sb_post_norm)]
    big_ids = [2, 8, 9, 12, 13]
    small_ids = [i for i in range(15) if i not in big_ids]
    deltas, new_m, new_v = [None] * 15, [None] * 15, [None] * 15
    for ids, nm in ((small_ids, "adamw_small"), (big_ids, "adamw_big")):
        d_, m_, v_ = _adamw([weights[i] for i in ids], [grads[i] for i in ids],
                            [ms[i] for i in ids], [vs[i] for i in ids], nm)
        for k, i in enumerate(ids):
            deltas[i], new_m[i], new_v[i] = d_[k], m_[k], v_[k]
    fix = lambda arrs: [a[None] if l else a for a, l in zip(arrs, lead)]
    return (loss, grad_x, *fix(grads), *fix(deltas), *fix(new_m), *fix(new_v))
```

```python
import functools

import numpy as np
import jax
import jax.numpy as jnp
from jax import lax
from jax.experimental import pallas as pl
from jax.experimental.pallas import tpu as pltpu

F32 = jnp.float32
MXU = jnp.bfloat16
ACT = jnp.bfloat16
WIRE = jnp.bfloat16

D = 1024
N_META = 16
BLK = 128
PAD = BLK - N_META
HEAD = 64
NEG = -1e30
EPS = 1e-6
LN_EPS = 1e-5
ROPE_THETA = 10000.0
SCALE = HEAD ** -0.5
CONV_W = 31
HALO = 32
LR, B1, B2, ADAM_EPS, WD, STEP = 0.001, 0.9, 0.999, 1e-08, 0.01, 10
VMEM_LIMIT = 56 * 1024 * 1024
MESH = pl.DeviceIdType.MESH

P0_SRC = (5, 6, 7, 8, 0, 1, 3, 4, 9, 10, 2)


def _cparams(sem=None):
    return pltpu.CompilerParams(dimension_semantics=sem, vmem_limit_bytes=VMEM_LIMIT)


def _tile(t, pref):
    for cand in (1088, 544, 272, 128):
        if cand <= pref and t % cand == 0:
            return cand
    raise ValueError(t)


def _sigmoid(x):
    return 1.0 / (1.0 + jnp.exp(-x))


def _silu_and_grad(x):
    s = _sigmoid(x)
    return x * s, s * (1.0 + x * (1.0 - s))


def _dot(a, b):
    return jnp.dot(a, b, preferred_element_type=F32)


def _dot_nt(a, b):
    return lax.dot_general(a, b, (((1,), (1,)), ((), ())), preferred_element_type=F32)


def _dot_tn(a, b):
    return lax.dot_general(a, b, (((0,), (0,)), ((), ())), preferred_element_type=F32)


def _rows(shape, base):
    return base + lax.broadcasted_iota(jnp.int32, shape, 0)


def _rope_tables(t):
    half = HEAD // 2
    inv = ROPE_THETA ** (-np.arange(half, dtype=np.float32) / half)
    pos = (np.arange(t) - PAD).astype(np.float32)
    ang = pos[:, None] * inv[None, :]
    lane = np.arange(BLK)
    cos = np.cos(ang)[:, lane % half].astype(np.float32)
    sin = np.sin(ang)[:, lane % half].astype(np.float32)
    first = (lane % HEAD) < half
    sin_a = np.where(first[None, :], -sin, 0.0).astype(np.float32)
    sin_b = np.where(first[None, :], 0.0, sin).astype(np.float32)
    return jnp.asarray(cos), jnp.asarray(sin_a), jnp.asarray(sin_b)


def _rope(v, cos, sin_a, sin_b):
    return v * cos + pltpu.roll(v, 96, 1) * sin_a + pltpu.roll(v, 32, 1) * sin_b


def _unrope(v, cos, sin_a, sin_b):
    return v * cos - pltpu.roll(v, 96, 1) * sin_a - pltpu.roll(v, 32, 1) * sin_b


def _coords():
    return lax.axis_index("x"), lax.axis_index("y"), lax.axis_index("c")


def _all_gather(arrs, name):
    plan = _GatherPlan(arrs)

    def body(*refs):
        plan.begin(refs)
        plan.end(refs)

    return pl.pallas_call(
        body, name=name, out_shape=plan.out_shape,
        in_specs=plan.specs, out_specs=plan.specs, scratch_shapes=plan.scratch,
    )(*arrs)


class _GatherPlan:
    def __init__(self, arrs):
        n = self.n = len(arrs)
        self.out_shape = [jax.ShapeDtypeStruct((8,) + a.shape, a.dtype) for a in arrs]
        self.specs = [pl.BlockSpec(memory_space=pl.ANY)] * n
        self.scratch = [pltpu.SemaphoreType.DMA((n, 7)), pltpu.SemaphoreType.DMA((n, 7)),
                        pltpu.SemaphoreType.DMA((n,))]

    def _copies(self, refs):
        n = self.n
        ins, outs = refs[:n], refs[n:2 * n]
        send_sems, recv_sems, local_sems = refs[2 * n:]
        x, y, c = _coords()
        me, sibling = (x, y, c), (x, y, 1 - c)
        chips = [(1 - x, y), (x, 1 - y), (1 - x, 1 - y)]

        def copy(a, k, block, to, src=None):
            dst = outs[a].at[4 * block[0] + 2 * block[1] + block[2]]
            return pltpu.make_async_remote_copy(
                src_ref=dst if src is None else src, dst_ref=dst,
                send_sem=send_sems.at[a, k], recv_sem=recv_sems.at[a, k],
                device_id=to, device_id_type=MESH)

        mine = [pltpu.make_async_copy(ins[a], outs[a].at[4 * x + 2 * y + c], local_sems.at[a])
                for a in range(n)]
        first = []
        for a in range(n):
            first.append(copy(a, 0, me, sibling, src=ins[a]))
            for j, chip in enumerate(chips):
                first.append(copy(a, 1 + j, me, (*chip, c), src=ins[a]))
        return copy, mine, first, (me, sibling, chips, c)

    def begin(self, refs):
        _, mine, first, _ = self._copies(refs)
        for cp in mine + first:
            cp.start()

    def end(self, refs):
        copy, mine, first, (me, sibling, chips, c) = self._copies(refs)
        passed = []
        for j, chip in enumerate(chips):
            for a in range(self.n):
                copy(a, 1 + j, (*chip, c), me).wait_recv()
                cp = copy(a, 4 + j, (*chip, c), sibling)
                cp.start()
                passed.append(cp)
        for a in range(self.n):
            copy(a, 0, sibling, me).wait_recv()
            for j, chip in enumerate(chips):
                copy(a, 4 + j, (*chip, 1 - c), me).wait_recv()
        for cp in first + passed:
            cp.wait_send()
        for cp in mine:
            cp.wait()


class _ChipsPlan:
    def __init__(self, sums):
        n = self.n = len(sums)
        self.out_shape = [jax.ShapeDtypeStruct((3,) + s.shape[1:], s.dtype) for s in sums]
        self.specs = [pl.BlockSpec(memory_space=pl.ANY)] * n
        self.scratch = [pltpu.SemaphoreType.DMA((n, 3)), pltpu.SemaphoreType.DMA((n, 3))]

    def _copies(self, refs):
        n = self.n
        ins, outs = refs[:n], refs[n:2 * n]
        send_sems, recv_sems = refs[2 * n:]
        x, y, c = _coords()
        chips = [(1 - x, y), (x, 1 - y), (1 - x, 1 - y)]
        return [pltpu.make_async_remote_copy(
            src_ref=ins[a].at[2 * chip[0] + chip[1]], dst_ref=outs[a].at[k],
            send_sem=send_sems.at[a, k], recv_sem=recv_sems.at[a, k],
            device_id=(*chip, c), device_id_type=MESH)
            for a in range(n) for k, chip in enumerate(chips)]

    def begin(self, refs):
        for cp in self._copies(refs):
            cp.start()

    def end(self, refs):
        for cp in self._copies(refs):
            cp.wait()


def _exchange_sibling(parts, name):
    n = len(parts)

    def body(*refs):
        ins, outs = refs[:n], refs[n:2 * n]
        send_sems, recv_sems = refs[2 * n:]
        x, y, c = _coords()
        copies = [pltpu.make_async_remote_copy(
            src_ref=ins[a].at[:, 1 - c], dst_ref=outs[a],
            send_sem=send_sems.at[a], recv_sem=recv_sems.at[a],
            device_id=(x, y, 1 - c), device_id_type=MESH) for a in range(n)]
        for cp in copies:
            cp.start()
        for cp in copies:
            cp.wait()

    any_spec = pl.BlockSpec(memory_space=pl.ANY)
    return pl.pallas_call(
        body, name=name,
        out_shape=[jax.ShapeDtypeStruct((4,) + p.shape[2:], p.dtype) for p in parts],
        in_specs=[any_spec] * n, out_specs=[any_spec] * n,
        scratch_shapes=[pltpu.SemaphoreType.DMA((n,)), pltpu.SemaphoreType.DMA((n,))],
    )(*parts)


def _exchange_chips(sums, name):
    plan = _ChipsPlan(sums)

    def body(*refs):
        plan.begin(refs)
        plan.end(refs)

    return pl.pallas_call(
        body, name=name, out_shape=plan.out_shape,
        in_specs=plan.specs, out_specs=plan.specs, scratch_shapes=plan.scratch,
    )(*sums)


def _add_sibling(pos, part, recv, name):
    _, _, r, c = part.shape

    def body(pos_ref, p_ref, r_ref, o_ref):
        o_ref[...] = (p_ref[...].astype(F32) + r_ref[...].astype(F32)).astype(o_ref.dtype)

    return pl.pallas_call(
        body, name=name,
        grid_spec=pltpu.PrefetchScalarGridSpec(
            num_scalar_prefetch=1, grid=(4,),
            in_specs=[pl.BlockSpec((None, None, r, c), lambda q, pos: (q, pos[2], 0, 0)),
                      pl.BlockSpec((None, r, c), lambda q, pos: (q, 0, 0))],
            out_specs=pl.BlockSpec((None, r, c), lambda q, pos: (q, 0, 0))),
        out_shape=jax.ShapeDtypeStruct((4, r, c), part.dtype),
        compiler_params=_cparams(("arbitrary",)),
    )(pos, part, recv)


def _sum_chips(pos, sums, recv, name):
    _, r, c = sums.shape

    def body(pos_ref, s_ref, r_ref, o_ref):
        g = s_ref[...].astype(F32)
        for k in range(3):
            g = g + r_ref[k].astype(F32)
        o_ref[...] = g

    return pl.pallas_call(
        body, name=name,
        grid_spec=pltpu.PrefetchScalarGridSpec(
            num_scalar_prefetch=1, grid=(1,),
            in_specs=[pl.BlockSpec((None, r, c), lambda i, pos: (2 * pos[0] + pos[1], 0, 0)),
                      pl.BlockSpec((3, r, c), lambda i, pos: (0, 0, 0))],
            out_specs=pl.BlockSpec((r, c), lambda i, pos: (0, 0))),
        out_shape=jax.ShapeDtypeStruct((r, c), F32),
        compiler_params=_cparams(("arbitrary",)),
    )(pos, sums, recv)


def _adamw(ws, gs, ms, vs, name):
    n = len(ws)
    c1 = 1.0 / (1.0 - B1 ** STEP)
    c2 = 1.0 / (1.0 - B2 ** STEP)

    def body(*refs):
        w_r, g_r, m_r, v_r = refs[:n], refs[n:2 * n], refs[2 * n:3 * n], refs[3 * n:4 * n]
        d_o, m_o, v_o = refs[4 * n:5 * n], refs[5 * n:6 * n], refs[6 * n:7 * n]
        for a in range(n):
            g = g_r[a][...]
            m = B1 * m_r[a][...] + (1.0 - B1) * g
            v = B2 * v_r[a][...] + (1.0 - B2) * (g * g)
            d_o[a][...] = -LR * ((m * c1) / (jnp.sqrt(v * c2) + ADAM_EPS) + WD * w_r[a][...])
            m_o[a][...] = m
            v_o[a][...] = v

    shapes = [jax.ShapeDtypeStruct(w.shape, F32) for w in ws]
    outs = pl.pallas_call(body, name=name, out_shape=shapes * 3,
                          compiler_params=_cparams())(*ws, *gs, *ms, *vs)
    return outs[:n], outs[n:2 * n], outs[2 * n:]


def _reduce_small(gathered, name):
    n = len(gathered)

    def body(*refs):
        for a in range(n):
            acc = refs[a][0]
            for k in range(1, 8):
                acc = acc + refs[a][k]
            refs[n + a][...] = acc

    return pl.pallas_call(
        body, name=name,
        out_shape=[jax.ShapeDtypeStruct(g.shape[1:], F32) for g in gathered],
        compiler_params=_cparams())(*gathered)


class _Cols:
    def __init__(self, pieces, tw):
        self.pieces, self.tw = pieces, tw
        self.arrays = [p[0] for p in pieces]
        self.n_tiles = sum(p[2] for p in pieces)

    def specs(self, tm, row_of, tile_of):
        out = []
        for _, first, cnt in self.pieces:
            def imap(*g, first=first, cnt=cnt):
                return (row_of(*g), jnp.clip(tile_of(*g) - first, 0, cnt - 1))
            out.append(pl.BlockSpec((tm, self.tw), imap))
        return out

    def apply(self, t, refs, fn):
        for ref, (_, first, cnt) in zip(refs, self.pieces):
            pl.when((t >= first) & (t < first + cnt))(functools.partial(fn, ref))


def _ab_in(h, g, w_t, tables, ride):
    t = h.shape[0]
    tm = _tile(t, 1088)
    src = jnp.asarray(np.array(P0_SRC, np.int32))
    plan = _GatherPlan(ride)
    nr = plan.n

    def body(src_ref, h_ref, g_ref, w_ref, cos_ref, sa_ref, sb_ref, *rest):
        o_ref, hn_ref = rest[nr:nr + 2]
        hn_s = rest[2 * nr + 2]
        comm = (*rest[:nr], *rest[nr + 2:2 * nr + 2], *rest[2 * nr + 3:])
        i, j = pl.program_id(0), pl.program_id(1)
        pl.when((i == 0) & (j == 0))(lambda: plan.begin(comm))

        @pl.when(j == 0)
        def _():
            x = h_ref[...]
            hn = (x * lax.rsqrt(jnp.mean(x * x, -1, keepdims=True) + EPS) * g_ref[...]).astype(MXU)
            hn_s[...] = hn
            hn_ref[...] = hn.astype(ACT)

        acc = _dot_nt(hn_s[...], w_ref[...])
        rope = lambda v: _rope(v, cos_ref[...], sa_ref[...], sb_ref[...])

        @pl.when((j == 4) | (j == 5))
        def _():
            o_ref[:, :BLK] = rope(acc[:, :BLK])
            o_ref[:, BLK:] = rope(acc[:, BLK:])

        @pl.when(j == 10)
        def _():
            o_ref[:, :BLK] = rope(acc[:, :BLK])
            o_ref[:, BLK:] = acc[:, BLK:]

        @pl.when((j < 4) | ((j > 5) & (j < 10)))
        def _():
            o_ref[...] = acc

        pl.when((i == t // tm - 1) & (j == 10))(lambda: plan.end(comm))

    tab = pl.BlockSpec((tm, BLK), lambda i, j, s: (i, 0))
    outs = pl.pallas_call(
        body, name="ab_in",
        grid_spec=pltpu.PrefetchScalarGridSpec(
            num_scalar_prefetch=1, grid=(t // tm, 11),
            in_specs=[pl.BlockSpec((tm, D), lambda i, j, s: (i, 0)),
                      pl.BlockSpec((1, D), lambda i, j, s: (0, 0)),
                      pl.BlockSpec((256, D), lambda i, j, s: (s[j], 0)),
                      tab, tab, tab] + plan.specs,
            out_specs=[pl.BlockSpec((tm, 256), lambda i, j, s: (i, j)),
                       pl.BlockSpec((tm, D), lambda i, j, s: (i, 0))] + plan.specs,
            scratch_shapes=[pltpu.VMEM((tm, D), MXU)] + plan.scratch),
        out_shape=[jax.ShapeDtypeStruct((t, 2816), F32), jax.ShapeDtypeStruct((t, D), ACT)] + plan.out_shape,
        compiler_params=_cparams(("arbitrary", "arbitrary")),
    )(src, h, g, w_t, *tables, *ride)
    return outs[0], outs[1], outs[2:]


def _swa_mask(n):
    r = lax.broadcasted_iota(jnp.int32, (BLK, 3 * BLK), 0)
    c = lax.broadcasted_iota(jnp.int32, (BLK, 3 * BLK), 1)
    qpos = n * BLK + r
    bpos = (n - 2) * BLK + c
    meta_ok = (c >= PAD) & (c < BLK) & (qpos - c >= BLK)
    band_ok = (c >= BLK) & (bpos >= PAD) & (qpos >= bpos) & (qpos - bpos < BLK)
    return meta_ok | band_ok


def _swa_keys(kv_ref, n):
    def blk(b):
        return kv_ref[pl.ds(pl.multiple_of(b * BLK, BLK), BLK), :]
    kv = jnp.concatenate([kv_ref[0:BLK, :], blk(jnp.maximum(n - 1, 0)), blk(n)], axis=0)
    lo = lax.broadcasted_iota(jnp.int32, (1, BLK), 1) < HEAD
    out = []
    for part in (kv[:, :BLK], kv[:, BLK:]):
        rolled = pltpu.roll(part, HEAD, 1)
        out.append((jnp.where(lo, part, rolled).astype(MXU), jnp.where(lo, rolled, part).astype(MXU)))
    return out[0], out[1], lo


def _swa_stack(ref, g, lo):
    parts = []
    for p in (2 * g, 2 * g + 1):
        x = ref[:, p * BLK:(p + 1) * BLK]
        parts += [jnp.where(lo, x, 0.0), jnp.where(lo, 0.0, x)]
    return jnp.concatenate(parts, axis=0).astype(MXU)


def _swa_probs(qs, kd, mask4, sink_ref, g):
    sink = jnp.concatenate([jnp.full((BLK, 1), sink_ref[4 * g + h], F32) for h in range(4)], axis=0)
    s = jnp.where(mask4, _dot_nt(qs, kd) * SCALE, NEG)
    m = jnp.maximum(jnp.max(s, -1, keepdims=True), sink)
    e = jnp.exp(s - m)
    e_sink = jnp.exp(sink - m)
    inv = 1.0 / (jnp.sum(e, -1, keepdims=True) + e_sink)
    return e * inv, e_sink * inv


def _swa_unstack(x, lo):
    return [jnp.where(lo, x[0:BLK], x[BLK:2 * BLK]), jnp.where(lo, x[2 * BLK:3 * BLK], x[3 * BLK:])]


def _swa_fwd(p0, sinks, ride):
    t = p0.shape[0]
    plan = _GatherPlan(ride)
    nr = plan.n

    def body(sink_ref, q_ref, kv_ref, *rest):
        o_ref = rest[nr]
        comm = (*rest[:nr], *rest[nr + 1:])
        n = pl.program_id(0)
        pl.when(n == 0)(lambda: plan.begin(comm))
        kd, vd, lo = _swa_keys(kv_ref, n)
        mask4 = jnp.concatenate([_swa_mask(n)] * 4, axis=0)
        for g in range(2):
            pr, _ = _swa_probs(_swa_stack(q_ref, g, lo), kd[g], mask4, sink_ref, g)
            pairs = _swa_unstack(_dot(pr.astype(MXU), vd[g]), lo)
            for k in range(2):
                p = 2 * g + k
                o_ref[:, p * BLK:(p + 1) * BLK] = pairs[k]
        pl.when(n == t // BLK - 1)(lambda: plan.end(comm))

    outs = pl.pallas_call(
        body, name="swa_fwd", grid=(t // BLK,),
        in_specs=[pl.BlockSpec(memory_space=pltpu.SMEM),
                  pl.BlockSpec((BLK, 512), lambda n: (n, 2)),
                  pl.BlockSpec((t, 256), lambda n: (0, 10))] + plan.specs,
        out_specs=[pl.BlockSpec((BLK, 512), lambda n: (n, 0))] + plan.specs,
        out_shape=[jax.ShapeDtypeStruct((t, 512), F32)] + plan.out_shape,
        scratch_shapes=plan.scratch,
        compiler_params=_cparams(("arbitrary",)),
    )(sinks, p0, p0, *ride)
    return outs[0], outs[1:]


def _conv_window(u_w, w_ref, n_out, first):
    rows = u_w.shape[0]
    acc = None
    for j in range(CONV_W):
        shifted = pltpu.roll(u_w, (rows - (first + j)) % rows, 0)[:n_out]
        term = shifted * w_ref[j:j + 1, :]
        acc = term if acc is None else acc + term
    return acc


def _conv_fwd(p0, conv_w, conv_b, ln_g, ln_b):
    t = p0.shape[0]
    tm = _tile(t, 544)
    hb = tm // HALO

    def body(cur_ref, prev_ref, w_ref, b_ref, g_ref, bb_ref, o_ref):
        i = pl.program_id(0)
        glu = jnp.concatenate([prev_ref[...], cur_ref[...]], axis=0)
        rw = _rows((tm + HALO, 1), i * tm - HALO)
        u_w = jnp.where(rw >= PAD, glu[:, :512] * _sigmoid(glu[:, 512:]), 0.0)
        cv = _conv_window(u_w, w_ref, tm, HALO - (CONV_W - 1)) + b_ref[...]
        xc = cv - jnp.mean(cv, -1, keepdims=True)
        ln = xc * lax.rsqrt(jnp.mean(xc * xc, -1, keepdims=True) + LN_EPS) * g_ref[...] + bb_ref[...]
        o_ref[...] = (ln * _sigmoid(ln)).astype(ACT)

    vec = pl.BlockSpec((1, 512), lambda i: (0, 0))
    return pl.pallas_call(
        body, name="conv_fwd", grid=(t // tm,),
        in_specs=[pl.BlockSpec((tm, D), lambda i: (i, 0)),
                  pl.BlockSpec((HALO, D), lambda i: (jnp.maximum(i * hb - 1, 0), 0)),
                  pl.BlockSpec((CONV_W, 512), lambda i: (0, 0)), vec, vec, vec],
        out_specs=pl.BlockSpec((tm, 512), lambda i: (i, 0)),
        out_shape=jax.ShapeDtypeStruct((t, 512), ACT),
        compiler_params=_cparams(("arbitrary",)),
    )(p0, p0, conv_w, conv_b, ln_g, ln_b)


def _ab_out(h, p0, att, c1, w_pw2, w_out, g_post):
    t = h.shape[0]
    tm = _tile(t, 272)

    def body(h_ref, ga_ref, gb_ref, att_ref, c1_ref, pw_ref, wo_ref, g_ref, h1_ref, y_ref, mix_ref):
        i = pl.program_id(0)
        sga, _ = _silu_and_grad(ga_ref[...])
        sgb, _ = _silu_and_grad(gb_ref[...])
        a = att_ref[...] * sga
        c = _dot(c1_ref[...].astype(MXU), pw_ref[...]) * sgb
        mix = jnp.concatenate([a, c], axis=1).astype(MXU)
        y = _dot(mix, wo_ref[...])
        yn = y * lax.rsqrt(jnp.mean(y * y, -1, keepdims=True) + EPS) * g_ref[...]
        h1_ref[...] = jnp.where(_rows((tm, 1), i * tm) >= PAD, h_ref[...] + yn, 0.0)
        y_ref[...] = y
        mix_ref[...] = mix.astype(ACT)

    row = lambda w, idx: pl.BlockSpec((tm, w), lambda i: (i, idx))
    full = lambda a: pl.BlockSpec(a.shape, lambda i: (0, 0))
    return pl.pallas_call(
        body, name="ab_out", grid=(t // tm,),
        in_specs=[row(D, 0), row(512, 3), row(512, 4), row(512, 0), row(512, 0),
                  full(w_pw2), full(w_out), full(g_post)],
        out_specs=[row(D, 0), row(D, 0), row(D, 0)],
        out_shape=[jax.ShapeDtypeStruct((t, D), F32), jax.ShapeDtypeStruct((t, D), F32),
                   jax.ShapeDtypeStruct((t, D), ACT)],
        compiler_params=_cparams(("arbitrary",)),
    )(h, p0, p0, att, c1, w_pw2, w_out, g_post)


def _sb_in(h, g, w):
    t = h.shape[0]
    tm = _tile(t, 1088)

    def body(h_ref, g_ref, w_ref, o_ref, hn_ref, hn_s):
        @pl.when(pl.program_id(1) == 0)
        def _():
            x = h_ref[...]
            hn = (x * lax.rsqrt(jnp.mean(x * x, -1, keepdims=True) + EPS) * g_ref[...]).astype(MXU)
            hn_s[...] = hn
            hn_ref[...] = hn.astype(ACT)

        o_ref[...] = _dot(hn_s[...], w_ref[...])

    return pl.pallas_call(
        body, name="sb_in", grid=(t // tm, 8),
        in_specs=[pl.BlockSpec((tm, D), lambda i, j: (i, 0)),
                  pl.BlockSpec((1, D), lambda i, j: (0, 0)),
                  pl.BlockSpec((None, D, 512), lambda i, j: (j, 0, 0))],
        out_specs=[pl.BlockSpec((tm, 512), lambda i, j: (i, j)),
                   pl.BlockSpec((tm, D), lambda i, j: (i, 0))],
        out_shape=[jax.ShapeDtypeStruct((t, 4096), F32), jax.ShapeDtypeStruct((t, D), ACT)],
        scratch_shapes=[pltpu.VMEM((tm, D), MXU)],
        compiler_params=_cparams(("arbitrary", "arbitrary")),
    )(h, g, w)


def _split_hi_lo(x):
    hi = x.astype(MXU)
    lo = (x - hi.astype(F32)).astype(MXU)
    return hi, lo


def _scan_matrix(suffix):
    j = lax.broadcasted_iota(jnp.int32, (2 * BLK, 2 * BLK), 0) % BLK
    s = lax.broadcasted_iota(jnp.int32, (2 * BLK, 2 * BLK), 1)
    keep = (s >= BLK) | ((j > s) if suffix else (j < s))
    return jnp.where(keep, 1.0, 0.0).astype(MXU)


def _scan_packed(hi_lo, b, mat):
    cols = slice(b * BLK, (b + 1) * BLK)
    both = _dot(jnp.concatenate([hi_lo[:BLK, cols], hi_lo[BLK:, cols]], axis=1), mat)
    return both[:, :BLK], both[:, BLK:]


KC = 4
CHUNK = KC * BLK
SLACK = CHUNK - BLK
GROUPS = 2
GROUPS_FWD = 4


def _sb_logits(qm, kc, valid):
    z = _dot_nt(qm, kc)
    log_beta = jnp.minimum(z, 0.0) - jnp.log(1.0 + jnp.exp(-jnp.abs(z)))
    return log_beta, jnp.where(valid, log_beta - z, 0.0)


def _sb_valid(i, first_key):
    r = lax.broadcasted_iota(jnp.int32, (BLK, CHUNK), 0)
    c = lax.broadcasted_iota(jnp.int32, (BLK, CHUNK), 1)
    kpos = first_key + c
    return (kpos >= PAD) & (kpos < i * BLK + r)


def _sb_fwd(p1):
    t = p1.shape[0]
    groups = GROUPS_FWD
    w = groups * BLK

    def body(q_ref, k_ref, v_ref, o_ref, lt_ref, k_s, v_s):
        i = pl.program_id(1)

        @pl.when(i == 0)
        def _():
            for src, dst in ((k_ref, k_s), (v_ref, v_s)):
                dst[0:SLACK, :] = jnp.zeros((SLACK, w), MXU)
                dst[SLACK:, :] = src[...].astype(MXU)

        lo = lax.broadcasted_iota(jnp.int32, (1, BLK), 1) < HEAD
        qm = []
        for g in range(groups):
            q = q_ref[:, g * BLK:(g + 1) * BLK] * SCALE
            qm += [jnp.where(lo, q, 0.0).astype(MXU), jnp.where(lo, 0.0, q).astype(MXU)]
        mat = _scan_matrix(True)
        heads = range(2 * groups)

        def step(s, carry):
            start = pl.multiple_of((i - KC * s) * BLK, BLK)
            valid = _sb_valid(i, start - SLACK)
            new, staged = [], []
            for h in heads:
                lanes = slice((h // 2) * BLK, (h // 2 + 1) * BLK)
                log_beta, log_1m = _sb_logits(qm[h], k_s[pl.ds(start, CHUNK), lanes], valid)
                staged.append((log_beta, jnp.concatenate(_split_hi_lo(log_1m), axis=0)))
            probs = []
            for h in heads:
                log_beta, hi_lo = staged[h]
                run = carry[2 * h]
                parts = [None] * KC
                for b in reversed(range(KC)):
                    after, total = _scan_packed(hi_lo, b, mat)
                    parts[b] = after + run
                    run = run + total
                a = jnp.where(valid, jnp.exp(log_beta + jnp.concatenate(parts, axis=1)), 0.0)
                probs.append((run, a.astype(MXU)))
            for h in heads:
                lanes = slice((h // 2) * BLK, (h // 2 + 1) * BLK)
                run, a = probs[h]
                new += [run, carry[2 * h + 1] + _dot(a, v_s[pl.ds(start, CHUNK), lanes])]
            return tuple(new)

        zero = jnp.zeros((BLK, BLK), F32)
        res = lax.fori_loop(0, (i + KC) // KC, step, (zero,) * (4 * groups))
        for g in range(groups):
            lanes = slice(g * BLK, (g + 1) * BLK)
            o_ref[:, lanes] = jnp.where(lo, res[4 * g + 1], res[4 * g + 3])
            lt_ref[:, lanes] = jnp.where(lo, res[4 * g], res[4 * g + 2])

    ng = D // w
    blk = pl.BlockSpec((BLK, w), lambda hp, i: (i, hp))
    return pl.pallas_call(
        body, name="sb_fwd", grid=(ng, t // BLK),
        in_specs=[blk,
                  pl.BlockSpec((t, w), lambda hp, i: (0, ng + hp)),
                  pl.BlockSpec((t, w), lambda hp, i: (0, 2 * ng + hp))],
        out_specs=[blk, blk],
        out_shape=[jax.ShapeDtypeStruct((t, D), F32)] * 2,
        scratch_shapes=[pltpu.VMEM((t + SLACK, w), MXU), pltpu.VMEM((t + SLACK, w), MXU)],
        compiler_params=_cparams(("arbitrary", "arbitrary")),
    )(p1, p1, p1)


def _sb_out(o, p1, w_out, h1, g_post, tgt):
    t = o.shape[0]
    tm = _tile(t, 272)

    def body(o_ref, g_ref, w_ref, h_ref, gp_ref, t_ref,
             loss_ref, dh_ref, dy_ref, m_ref, do_ref, dg_ref, dgp_ref):
        i = pl.program_id(0)

        @pl.when(i == 0)
        def _():
            loss_ref[...] = jnp.zeros_like(loss_ref)
            dgp_ref[...] = jnp.zeros_like(dgp_ref)

        gate = g_ref[...]
        sg, dsg = _silu_and_grad(gate)
        ov = o_ref[...]
        m = (ov * sg).astype(MXU)
        y = _dot(m, w_ref[...])
        r = lax.rsqrt(jnp.mean(y * y, -1, keepdims=True) + EPS)
        yhat = y * r
        h2 = h_ref[...] + yhat * gp_ref[...]
        diff = jnp.where(_rows((tm, 1), i * tm) >= BLK, h2 - t_ref[...], 0.0)
        loss_ref[...] += jnp.full(loss_ref.shape, 0.5 / D, F32) * jnp.sum(diff * diff)
        dh = diff * (1.0 / D)
        dgp_ref[...] += jnp.sum(dh * yhat, 0, keepdims=True)
        dyn = dh * gp_ref[...]
        dy = (r * (dyn - yhat * jnp.mean(dyn * yhat, -1, keepdims=True))).astype(MXU)
        dm = _dot_nt(dy, w_ref[...])
        dh_ref[...] = dh
        dy_ref[...] = dy.astype(ACT)
        m_ref[...] = m.astype(ACT)
        do_ref[...] = dm * sg
        dg_ref[...] = (dm * ov * dsg).astype(ACT)

    row = lambda idx: pl.BlockSpec((tm, D), lambda i: (i, idx))
    full = lambda a: pl.BlockSpec(a.shape, lambda i: (0, 0))
    acc = lambda s: pl.BlockSpec(s, lambda i: (0, 0))
    return pl.pallas_call(
        body, name="sb_out", grid=(t // tm,),
        in_specs=[row(0), row(3), full(w_out), row(0), full(g_post), row(0)],
        out_specs=[acc((8, BLK)), row(0), row(0), row(0), row(0), row(0), acc((1, D))],
        out_shape=[jax.ShapeDtypeStruct((8, BLK), F32), jax.ShapeDtypeStruct((t, D), F32),
                   jax.ShapeDtypeStruct((t, D), ACT), jax.ShapeDtypeStruct((t, D), ACT),
                   jax.ShapeDtypeStruct((t, D), F32), jax.ShapeDtypeStruct((t, D), ACT),
                   jax.ShapeDtypeStruct((1, D), F32)],
        compiler_params=_cparams(("arbitrary",)),
    )(o, p1, w_out, h1, g_post, tgt)


def _sb_bwd(p1, ltot, do):
    t = p1.shape[0]
    nb = t // BLK

    w = GROUPS * BLK

    def body(q_ref, k_ref, v_ref, lt_ref, do_ref, dq_ref, dk_ref, dv_ref, k_s, v_s, dk_s, dv_s):
        i = pl.program_id(1)
        lo = lax.broadcasted_iota(jnp.int32, (1, BLK), 1) < HEAD

        @pl.when(i == 0)
        def _():
            for src, dst in ((k_ref, k_s), (v_ref, v_s)):
                dst[0:t, :] = src[...].astype(MXU)
                dst[t:, :] = jnp.zeros((SLACK, w), MXU)
            dk_s[...] = jnp.zeros_like(dk_s)
            dv_s[...] = jnp.zeros_like(dv_s)

        qm, dom, row_total, q2, do2 = [], [], [], [], []
        for g in range(GROUPS):
            lanes = slice(g * BLK, (g + 1) * BLK)
            q, dout, lt = q_ref[:, lanes] * SCALE, do_ref[:, lanes], lt_ref[:, lanes]
            qm += [jnp.where(lo, q, 0.0).astype(MXU), jnp.where(lo, 0.0, q).astype(MXU)]
            dom += [jnp.where(lo, dout, 0.0).astype(MXU), jnp.where(lo, 0.0, dout).astype(MXU)]
            q2.append(jnp.concatenate(qm[-2:], axis=0))
            do2.append(jnp.concatenate(dom[-2:], axis=0))
            lt_r = pltpu.roll(lt, HEAD, 1)
            row_total += [jnp.where(lo, lt, lt_r), jnp.where(lo, lt_r, lt)]
        mat_l = _scan_matrix(True)
        mat_g = _scan_matrix(False)
        heads = range(2 * GROUPS)

        def step(s, carry):
            start = pl.multiple_of(s * CHUNK, BLK)
            keys = lambda ref, h: ref[pl.ds(start, CHUNK), (h // 2) * BLK:(h // 2 + 1) * BLK]
            valid = _sb_valid(i, start)
            new, dzs, probs, st1, st2, st3 = [], [], [], [], [], []
            for h in heads:
                log_beta, log_1m = _sb_logits(qm[h], keys(k_s, h), valid)
                st1.append((log_beta, jnp.concatenate(_split_hi_lo(log_1m), axis=0),
                            _dot_nt(dom[h], keys(v_s, h))))
            for h in heads:
                log_beta, hi_lo, da = st1[h]
                run = carry[3 * h]
                parts = []
                for b in range(KC):
                    after, total = _scan_packed(hi_lo, b, mat_l)
                    run = run + total
                    parts.append(after + (row_total[h] - run))
                a = jnp.where(valid, jnp.exp(log_beta + jnp.concatenate(parts, axis=1)), 0.0)
                g = da * a
                probs.append(a.astype(MXU))
                st2.append((run, g, jnp.concatenate(_split_hi_lo(g), axis=0)))
            for h in heads:
                run, g, hi_lo = st2[h]
                run_g = carry[3 * h + 1]
                parts = []
                for b in range(KC):
                    before, total_g = _scan_packed(hi_lo, b, mat_g)
                    parts.append(before + run_g)
                    run_g = run_g + total_g
                sig = jnp.exp(st1[h][0])
                dz = jnp.where(valid, g * (1.0 - sig) - sig * jnp.concatenate(parts, axis=1), 0.0)
                dzs.append(dz.astype(MXU))
                st3.append((run, run_g))
            for h in heads:
                new += [*st3[h], carry[3 * h + 2] + _dot(dzs[h], keys(k_s, h))]
            for g in range(GROUPS):
                lanes = slice(g * BLK, (g + 1) * BLK)
                dk_s[pl.ds(start, CHUNK), lanes] += _dot_tn(jnp.concatenate(dzs[2 * g:2 * g + 2], axis=0), q2[g])
                dv_s[pl.ds(start, CHUNK), lanes] += _dot_tn(jnp.concatenate(probs[2 * g:2 * g + 2], axis=0), do2[g])
            return tuple(new)

        zero = jnp.zeros((BLK, BLK), F32)
        res = lax.fori_loop(0, (i + KC) // KC, step, (zero,) * (6 * GROUPS))
        for g in range(GROUPS):
            dq = jnp.where(lo, res[6 * g + 2], res[6 * g + 5])
            dq_ref[:, g * BLK:(g + 1) * BLK] = (dq * SCALE).astype(ACT)

        @pl.when(i == nb - 1)
        def _():
            dk_ref[...] = dk_s[0:t, :].astype(ACT)
            dv_ref[...] = dv_s[0:t, :].astype(ACT)

    ng = D // w
    blk = pl.BlockSpec((BLK, w), lambda hp, i: (i, hp))
    col = lambda off: pl.BlockSpec((t, w), lambda hp, i: (0, off + hp))
    return pl.pallas_call(
        body, name="sb_bwd", grid=(ng, nb),
        in_specs=[blk, col(ng), col(2 * ng), blk, blk],
        out_specs=[blk, col(0), col(0)],
        out_shape=[jax.ShapeDtypeStruct((t, D), ACT)] * 3,
        scratch_shapes=[pltpu.VMEM((t + SLACK, w), MXU), pltpu.VMEM((t + SLACK, w), MXU),
                        pltpu.VMEM((t + SLACK, w), F32), pltpu.VMEM((t + SLACK, w), F32)],
        compiler_params=_cparams(("arbitrary", "arbitrary")),
    )(p1, p1, p1, ltot, do)


def _mid_bwd(dp1, w_sb, h1, g_pre1, dh2, y0, g_post0, w_out, p0, att, c1, w_pw2):
    t = h1.shape[0]
    tm = _tile(t, 272)

    def body(*refs):
        d_refs = refs[:4]
        (w_ref, h_ref, g1_ref, dh2_ref, y_ref, g0_ref, wo_ref, ga_ref, gb_ref, att_ref, c1_ref,
         pw_ref, dh1_ref, dy_ref, dga_ref, dgb_ref, datt_ref, dc1_ref, dc2_ref, dg1_ref, dg0_ref,
         acc) = refs[4:]
        i, j = pl.program_id(0), pl.program_id(1)

        @pl.when((i == 0) & (j == 0))
        def _():
            dg1_ref[...] = jnp.zeros_like(dg1_ref)
            dg0_ref[...] = jnp.zeros_like(dg0_ref)

        @pl.when(j == 0)
        def _():
            acc[...] = jnp.zeros_like(acc)

        def add(ref):
            acc[...] += _dot_nt(ref[...].astype(MXU), w_ref[...])
        dp1.apply(j, d_refs, add)

        @pl.when(j == 7)
        def _():
            dhn = acc[...]
            x = h_ref[...]
            r = lax.rsqrt(jnp.mean(x * x, -1, keepdims=True) + EPS)
            xhat = x * r
            dg1_ref[...] += jnp.sum(dhn * xhat, 0, keepdims=True)
            dxn = dhn * g1_ref[...]
            dh1 = dh2_ref[...] + r * (dxn - xhat * jnp.mean(dxn * xhat, -1, keepdims=True))
            dh1_ref[...] = dh1
            y = y_ref[...]
            ry = lax.rsqrt(jnp.mean(y * y, -1, keepdims=True) + EPS)
            yhat = y * ry
            dg0_ref[...] += jnp.sum(dh1 * yhat, 0, keepdims=True)
            dyn = dh1 * g0_ref[...]
            dy = (ry * (dyn - yhat * jnp.mean(dyn * yhat, -1, keepdims=True))).astype(MXU)
            dy_ref[...] = dy.astype(ACT)
            dmix = _dot_nt(dy, wo_ref[...])
            da, dc = dmix[:, :512], dmix[:, 512:]
            sga, dsga = _silu_and_grad(ga_ref[...])
            sgb, dsgb = _silu_and_grad(gb_ref[...])
            datt_ref[...] = da * sga
            dga_ref[...] = (da * att_ref[...] * dsga).astype(ACT)
            c2 = _dot(c1_ref[...].astype(MXU), pw_ref[...])
            dc2 = (dc * sgb).astype(MXU)
            dgb_ref[...] = (dc * c2 * dsgb).astype(ACT)
            dc2_ref[...] = dc2.astype(ACT)
            dc1_ref[...] = _dot_nt(dc2, pw_ref[...])

    row = lambda w, idx: pl.BlockSpec((tm, w), lambda i, j: (i, idx))
    full = lambda a: pl.BlockSpec(a.shape, lambda i, j: (0, 0))
    acc_spec = pl.BlockSpec((1, D), lambda i, j: (0, 0))
    sd = jax.ShapeDtypeStruct
    return pl.pallas_call(
        body, name="mid_bwd", grid=(t // tm, 8),
        in_specs=dp1.specs(tm, lambda i, j: i, lambda i, j: j) + [
            pl.BlockSpec((None, D, 512), lambda i, j: (j, 0, 0)),
            row(D, 0), full(g_pre1), row(D, 0), row(D, 0), full(g_post0), full(w_out),
            row(512, 3), row(512, 4), row(512, 0), row(512, 0), full(w_pw2)],
        out_specs=[row(D, 0), row(D, 0), row(512, 0), row(512, 0), row(512, 0), row(512, 0),
                   row(512, 0), acc_spec, acc_spec],
        out_shape=[sd((t, D), F32), sd((t, D), ACT), sd((t, 512), ACT), sd((t, 512), ACT),
                   sd((t, 512), F32), sd((t, 512), F32), sd((t, 512), ACT),
                   sd((1, D), F32), sd((1, D), F32)],
        scratch_shapes=[pltpu.VMEM((tm, D), F32)],
        compiler_params=_cparams(("arbitrary", "arbitrary")),
    )(*dp1.arrays, w_sb, h1, g_pre1, dh2, y0, g_post0, w_out, p0, p0, att, c1, w_pw2)


def _conv_bwd(p0, dc1, conv_w, conv_b, ln_g, ln_b):
    t = p0.shape[0]
    tm = _tile(t, 544)
    hb = tm // HALO
    last = t // HALO - 1

    def body(cur_ref, prev_ref, next_ref, d_ref, dn_ref, w_ref, b_ref, g_ref, bb_ref,
             dglu_ref, dw_ref, db_ref, dlg_ref, dlb_ref):
        i = pl.program_id(0)

        @pl.when(i == 0)
        def _():
            for ref in (dw_ref, db_ref, dlg_ref, dlb_ref):
                ref[...] = jnp.zeros_like(ref)

        glu = jnp.concatenate([prev_ref[...], cur_ref[...], next_ref[...]], axis=0)
        rw = _rows((tm + 2 * HALO, 1), i * tm - HALO)
        ga, sg = glu[:, :512], _sigmoid(glu[:, 512:])
        u_w = jnp.where((rw >= PAD) & (rw < t), ga * sg, 0.0)
        n_cv = tm + HALO
        cv = _conv_window(u_w, w_ref, n_cv, HALO - (CONV_W - 1)) + b_ref[...]
        xc = cv - jnp.mean(cv, -1, keepdims=True)
        rstd = lax.rsqrt(jnp.mean(xc * xc, -1, keepdims=True) + LN_EPS)
        cvhat = xc * rstd
        ln = cvhat * g_ref[...] + bb_ref[...]
        _, dsl = _silu_and_grad(ln)
        rc = _rows((n_cv, 1), i * tm)
        dc = jnp.concatenate([d_ref[...], dn_ref[...]], axis=0)
        dln = jnp.where(rc < t, dc * dsl, 0.0)
        dhat = dln * g_ref[...]
        dcv = rstd * (dhat - jnp.mean(dhat, -1, keepdims=True)
                      - cvhat * jnp.mean(dhat * cvhat, -1, keepdims=True))
        own = dcv[:tm]
        dlg_ref[...] += jnp.sum((dln * cvhat)[:tm], 0, keepdims=True)
        dlb_ref[...] += jnp.sum(dln[:tm], 0, keepdims=True)
        db_ref[...] += jnp.sum(own, 0, keepdims=True)
        rows = tm + 2 * HALO
        du = None
        for j in range(CONV_W):
            first = HALO - (CONV_W - 1) + j
            shifted = pltpu.roll(u_w, (rows - first) % rows, 0)[:tm]
            dw_ref[j:j + 1, :] += jnp.sum(own * shifted, 0, keepdims=True)
            back = pltpu.roll(dcv, (n_cv - (CONV_W - 1 - j)) % n_cv, 0)[:tm]
            term = back * w_ref[j:j + 1, :]
            du = term if du is None else du + term
        du = jnp.where(_rows((tm, 1), i * tm) >= PAD, du, 0.0)
        ga_c, sg_c = ga[HALO:HALO + tm], sg[HALO:HALO + tm]
        dglu_ref[:, :512] = (du * sg_c).astype(ACT)
        dglu_ref[:, 512:] = (du * ga_c * sg_c * (1.0 - sg_c)).astype(ACT)

    vec = pl.BlockSpec((1, 512), lambda i: (0, 0))
    nxt = lambda i: (jnp.minimum((i + 1) * hb, last), 0)
    return pl.pallas_call(
        body, name="conv_bwd", grid=(t // tm,),
        in_specs=[pl.BlockSpec((tm, D), lambda i: (i, 0)),
                  pl.BlockSpec((HALO, D), lambda i: (jnp.maximum(i * hb - 1, 0), 0)),
                  pl.BlockSpec((HALO, D), nxt),
                  pl.BlockSpec((tm, 512), lambda i: (i, 0)),
                  pl.BlockSpec((HALO, 512), nxt),
                  pl.BlockSpec((CONV_W, 512), lambda i: (0, 0)), vec, vec, vec],
        out_specs=[pl.BlockSpec((tm, D), lambda i: (i, 0)),
                   pl.BlockSpec((HALO, 512), lambda i: (0, 0)), vec, vec, vec],
        out_shape=[jax.ShapeDtypeStruct((t, D), ACT), jax.ShapeDtypeStruct((HALO, 512), F32)]
        + [jax.ShapeDtypeStruct((1, 512), F32)] * 3,
        compiler_params=_cparams(("arbitrary",)),
    )(p0, p0, p0, dc1, dc1, conv_w, conv_b, ln_g, ln_b)


def _swa_bwd(p0, datt, sinks, tables, ride):
    t = p0.shape[0]
    nb = t // BLK
    plan = _ChipsPlan(ride)
    nr = plan.n

    def body(sink_ref, q_ref, kv_ref, d_ref, cos_ref, sa_ref, sb_ref, *rest):
        dq_ref, dkv_ref, ds_ref = rest[nr:nr + 3]
        acc = rest[2 * nr + 3]
        comm = (*rest[:nr], *rest[nr + 3:2 * nr + 3], *rest[2 * nr + 4:])
        n = pl.program_id(0)
        pl.when(n == 0)(lambda: plan.begin(comm))

        @pl.when(n == 0)
        def _():
            acc[...] = jnp.zeros_like(acc)
            ds_ref[...] = jnp.zeros_like(ds_ref)

        kd, vd, lo = _swa_keys(kv_ref, n)
        mask4 = jnp.concatenate([_swa_mask(n)] * 4, axis=0)
        row0 = pl.multiple_of(n * BLK, BLK)
        tabs = [r[pl.ds(row0, BLK), :] for r in (cos_ref, sa_ref, sb_ref)]
        dk_g, dv_g = [], []
        for g in range(2):
            qs, dos = _swa_stack(q_ref, g, lo), _swa_stack(d_ref, g, lo)
            pr, p_sink = _swa_probs(qs, kd[g], mask4, sink_ref, g)
            dpr = _dot_nt(dos, vd[g])
            delta = jnp.sum(pr * dpr, -1, keepdims=True)
            dsc = (pr * (dpr - delta) * SCALE).astype(MXU)
            sunk = p_sink * delta
            for h in range(4):
                row = 4 * g + h
                ds_ref[row:row + 1, :] += jnp.full((1, BLK), -1.0, F32) * jnp.sum(sunk[h * BLK:(h + 1) * BLK])
            pairs = _swa_unstack(_dot(dsc, kd[g]), lo)
            for k in range(2):
                p = 2 * g + k
                dq_ref[:, p * BLK:(p + 1) * BLK] = _unrope(pairs[k], *tabs).astype(ACT)
            dk_g.append(_dot_tn(dsc, qs))
            dv_g.append(_dot_tn(pr.astype(MXU), dos))
        fold = lambda a: a + pltpu.roll(a, HEAD, 1)
        dk = jnp.where(lo, fold(dk_g[0]), fold(dk_g[1]))
        dv = jnp.where(lo, fold(dv_g[0]), fold(dv_g[1]))
        dkv = jnp.concatenate([dk, dv], axis=1)
        prev = pl.multiple_of(jnp.maximum(n - 1, 0) * BLK, BLK)
        acc[0:BLK, :] += dkv[0:BLK]
        acc[pl.ds(prev, BLK), :] += dkv[BLK:2 * BLK]
        acc[pl.ds(row0, BLK), :] += dkv[2 * BLK:]

        @pl.when(n == nb - 1)
        def _():
            dkv_ref[:, :BLK] = _unrope(acc[:, :BLK], cos_ref[...], sa_ref[...], sb_ref[...]).astype(ACT)
            dkv_ref[:, BLK:] = acc[:, BLK:].astype(ACT)

        pl.when(n == nb - 1)(lambda: plan.end(comm))

    tab = pl.BlockSpec((t, BLK), lambda n: (0, 0))
    outs = pl.pallas_call(
        body, name="swa_bwd", grid=(nb,),
        in_specs=[pl.BlockSpec(memory_space=pltpu.SMEM),
                  pl.BlockSpec((BLK, 512), lambda n: (n, 2)),
                  pl.BlockSpec((t, 256), lambda n: (0, 10)),
                  pl.BlockSpec((BLK, 512), lambda n: (n, 0)), tab, tab, tab] + plan.specs,
        out_specs=[pl.BlockSpec((BLK, 512), lambda n: (n, 0)),
                   pl.BlockSpec((t, 256), lambda n: (0, 0)),
                   pl.BlockSpec((8, BLK), lambda n: (0, 0))] + plan.specs,
        out_shape=[jax.ShapeDtypeStruct((t, 512), ACT), jax.ShapeDtypeStruct((t, 256), ACT),
                   jax.ShapeDtypeStruct((8, BLK), F32)] + plan.out_shape,
        scratch_shapes=[pltpu.VMEM((t, 256), F32)] + plan.scratch,
        compiler_params=_cparams(("arbitrary",)),
    )(sinks, p0, p0, datt, *tables, *ride)
    return outs[0], outs[1], outs[2], outs[3:]


def _ab_in_bwd(dp0, w_t, h0, g_pre, dh1, ride):
    t = h0.shape[0]
    tm = _tile(t, 544)
    plan = _ChipsPlan(ride)
    nr = plan.n

    def body(*refs):
        d_refs = refs[:5]
        w_ref, h_ref, g_ref, dh1_ref = refs[5:9]
        rest = refs[9:]
        dh0_ref, dg_ref = rest[nr:nr + 2]
        acc = rest[2 * nr + 2]
        comm = (*rest[:nr], *rest[nr + 2:2 * nr + 2], *rest[2 * nr + 3:])
        i, j = pl.program_id(0), pl.program_id(1)
        pl.when((i == 0) & (j == 0))(lambda: plan.begin(comm))

        @pl.when((i == 0) & (j == 0))
        def _():
            dg_ref[...] = jnp.zeros_like(dg_ref)

        @pl.when(j == 0)
        def _():
            acc[...] = jnp.zeros_like(acc)

        def add(ref):
            acc[...] += _dot(ref[...].astype(MXU), w_ref[...])
        dp0.apply(j, d_refs, add)

        @pl.when(j == 10)
        def _():
            dhn = acc[...]
            x = h_ref[...]
            r = lax.rsqrt(jnp.mean(x * x, -1, keepdims=True) + EPS)
            xhat = x * r
            dg_ref[...] += jnp.sum(dhn * xhat, 0, keepdims=True)
            dxn = dhn * g_ref[...]
            dh0_ref[...] = dh1_ref[...] + r * (dxn - xhat * jnp.mean(dxn * xhat, -1, keepdims=True))

        pl.when((i == t // tm - 1) & (j == 10))(lambda: plan.end(comm))

    row = pl.BlockSpec((tm, D), lambda i, j: (i, 0))
    vec = pl.BlockSpec((1, D), lambda i, j: (0, 0))
    outs = pl.pallas_call(
        body, name="ab_in_bwd", grid=(t // tm, 11),
        in_specs=dp0.specs(tm, lambda i, j: i, lambda i, j: j) + [
            pl.BlockSpec((256, D), lambda i, j: (j, 0)), row, vec, row] + plan.specs,
        out_specs=[row, vec] + plan.specs,
        out_shape=[jax.ShapeDtypeStruct((t, D), F32), jax.ShapeDtypeStruct((1, D), F32)] + plan.out_shape,
        scratch_shapes=[pltpu.VMEM((tm, D), F32)] + plan.scratch,
        compiler_params=_cparams(("arbitrary", "arbitrary")),
    )(*dp0.arrays, w_t, h0, g_pre, dh1, *ride)
    return outs[0], outs[1], outs[2:]


def _dw_plain(a, b, name):
    t, m = a.shape
    n = b.shape[1]
    tm = _tile(t, 1088)
    tn = min(n, 512)
    nk = t // tm

    def body(a_ref, b_ref, o_ref, acc):
        k = pl.program_id(1)

        @pl.when(k == 0)
        def _():
            acc[...] = jnp.zeros_like(acc)

        acc[...] += _dot_tn(a_ref[...].astype(MXU), b_ref[...].astype(MXU))

        @pl.when(k == nk - 1)
        def _():
            o_ref[...] = acc[...].astype(WIRE)

    return pl.pallas_call(
        body, name=name, grid=(n // tn, nk),
        in_specs=[pl.BlockSpec((tm, m), lambda j, k: (k, 0)),
                  pl.BlockSpec((tm, tn), lambda j, k: (k, j))],
        out_specs=pl.BlockSpec((m, tn), lambda j, k: (0, j)),
        out_shape=jax.ShapeDtypeStruct((m, n), WIRE),
        scratch_shapes=[pltpu.VMEM((m, tn), F32)],
        compiler_params=_cparams(("arbitrary", "arbitrary")),
    )(a, b)


def _dw_chunks(hn, dp, name):
    t = hn.shape[0]
    tm = _tile(t, 1088)
    nk = t // tm
    nt, tw = dp.n_tiles, dp.tw
    n_in = len(dp.arrays)

    def body(*refs):
        d_refs = refs[:n_in]
        h_ref, o_ref, acc = refs[n_in:]
        j, k = pl.program_id(0), pl.program_id(1)

        @pl.when(k == 0)
        def _():
            acc[...] = jnp.zeros_like(acc)

        def add(ref):
            acc[...] += _dot_tn(h_ref[...].astype(MXU), ref[...].astype(MXU))
        dp.apply(j, d_refs, add)

        @pl.when(k == nk - 1)
        def _():
            o_ref[...] = acc[...].astype(WIRE)

    return pl.pallas_call(
        body, name=name, grid=(nt, nk),
        in_specs=dp.specs(tm, lambda j, k: k, lambda j, k: j) + [
            pl.BlockSpec((tm, D), lambda j, k: (k, 0))],
        out_specs=pl.BlockSpec((None, D, tw), lambda j, k: (j, 0, 0)),
        out_shape=jax.ShapeDtypeStruct((nt, D, tw), WIRE),
        scratch_shapes=[pltpu.VMEM((D, tw), F32)],
        compiler_params=_cparams(("arbitrary", "arbitrary")),
    )(*dp.arrays, hn)


def _dw_transposed(dp, hn, name):
    t = hn.shape[0]
    tm = _tile(t, 1088)
    nk = t // tm
    nt, tw = dp.n_tiles, dp.tw
    n_in = len(dp.arrays)

    def body(*refs):
        d_refs = refs[:n_in]
        h_ref, o_ref, acc = refs[n_in:]
        j, k = pl.program_id(0), pl.program_id(1)

        @pl.when(k == 0)
        def _():
            acc[...] = jnp.zeros_like(acc)

        def add(ref):
            acc[...] += _dot_tn(ref[...].astype(MXU), h_ref[...].astype(MXU))
        dp.apply(j, d_refs, add)

        @pl.when(k == nk - 1)
        def _():
            o_ref[...] = acc[...].astype(WIRE)

    return pl.pallas_call(
        body, name=name, grid=(nt, nk),
        in_specs=dp.specs(tm, lambda j, k: k, lambda j, k: j) + [
            pl.BlockSpec((tm, D), lambda j, k: (k, 0))],
        out_specs=pl.BlockSpec((tw, D), lambda j, k: (j, 0)),
        out_shape=jax.ShapeDtypeStruct((nt * tw, D), WIRE),
        scratch_shapes=[pltpu.VMEM((tw, D), F32)],
        compiler_params=_cparams(("arbitrary", "arbitrary")),
    )(*dp.arrays, hn)


def kernel(x, meta_tokens, ab_pre_norm, ab_w_in, ab_sinks, ab_conv_w, ab_conv_b, ab_conv_ln_g, ab_conv_ln_b, ab_w_pw2, ab_w_out, ab_post_norm, sb_pre_norm, sb_w_in, sb_w_out, sb_post_norm, loss_target, m_meta_tokens, m_ab_pre_norm, m_ab_w_in, m_ab_sinks, m_ab_conv_w, m_ab_conv_b, m_ab_conv_ln_g, m_ab_conv_ln_b, m_ab_w_pw2, m_ab_w_out, m_ab_post_norm, m_sb_pre_norm, m_sb_w_in, m_sb_w_out, m_sb_post_norm, v_meta_tokens, v_ab_pre_norm, v_ab_w_in, v_ab_sinks, v_ab_conv_w, v_ab_conv_b, v_ab_conv_ln_g, v_ab_conv_ln_b, v_ab_w_pw2, v_ab_w_out, v_ab_post_norm, v_sb_pre_norm, v_sb_w_in, v_sb_w_out, v_sb_post_norm):
    seq = x.shape[1]
    t = seq + BLK
    mx, my, mc = _coords()
    me = 4 * mx + 2 * my + mc
    pos = jnp.stack([mx, my, mc, me]).astype(jnp.int32)

    w_ab_t, *small = _all_gather(
        [ab_w_in[0].T.astype(WIRE), meta_tokens, ab_conv_w[0], sb_pre_norm, sb_post_norm], "gather_first")
    w_ab_t = w_ab_t.reshape(2816, D)
    meta_full = jnp.moveaxis(small[0], 0, 1).reshape(N_META, D)
    conv_w = jnp.moveaxis(small[1], 0, 1).reshape(CONV_W, 512)
    sb_pre = jnp.moveaxis(small[2], 0, 1).reshape(1, D)
    sb_post = jnp.moveaxis(small[3], 0, 1).reshape(1, D)

    h0 = jnp.concatenate([jnp.zeros((PAD, D), F32), meta_full, x[0]], axis=0)
    tgt = jnp.concatenate([jnp.zeros((BLK, D), F32), loss_target[0]], axis=0)
    tables = _rope_tables(t)
    sinks = ab_sinks[0]

    p0, hn0, (w_oa, w_pw, w_os) = _ab_in(
        h0, ab_pre_norm, w_ab_t, tables,
        [ab_w_out[0].astype(WIRE), ab_w_pw2[0].astype(WIRE), sb_w_out[0].astype(WIRE)])
    w_oa, w_os, w_pw = w_oa.reshape(D, D), w_os.reshape(D, D), w_pw.reshape(512, 512)
    att, (w_sb,) = _swa_fwd(p0, sinks, [sb_w_in[0].astype(WIRE)])
    c1 = _conv_fwd(p0, conv_w, ab_conv_b, ab_conv_ln_g, ab_conv_ln_b)
    h1, y0, mix = _ab_out(h0, p0, att, c1, w_pw, w_oa, ab_post_norm)
    p1, hn1 = _sb_in(h1, sb_pre, w_sb)
    o, ltot = _sb_fwd(p1)
    loss_part, dh2, dy1, m1, do, dgate, dg_sb_post = _sb_out(o, p1, w_os, h1, sb_post, tgt)

    dq1, dk1, dv1 = _sb_bwd(p1, ltot, do)
    dp1 = _Cols([(dq1, 0, 2), (dk1, 2, 2), (dv1, 4, 2), (dgate, 6, 2)], 512)

    def sibling_stage(parts, names, tag):
        got = _exchange_sibling(parts, "reduce_sibling_" + tag)
        return [_add_sibling(pos, p, r, "add_sibling_" + nm) for p, r, nm in zip(parts, got, names)]

    def finish(sums, got, names):
        return [_sum_chips(pos, s, r, "sum_chips_" + nm) for s, r, nm in zip(sums, got, names)]

    names1 = ["sb_in", "sb_out"]
    sums1 = sibling_stage([_dw_chunks(hn1, dp1, "dw_sb_in").reshape(4, 2, D, 512),
                           _dw_plain(m1, dy1, "dw_sb_out").reshape(4, 2, BLK, D)], names1, "sb")
    dh1, dy0, dga, dgb, datt, dc1, dc2, dg_sb_pre, dg_ab_post = _mid_bwd(
        dp1, w_sb, h1, sb_pre, dh2, y0, ab_post_norm, w_oa, p0, att, c1, w_pw)
    names2 = ["ab_out", "pw2"]
    sums2 = sibling_stage([_dw_plain(mix, dy0, "dw_ab_out").reshape(4, 2, BLK, D),
                           _dw_plain(c1, dc2, "dw_pw2").reshape(4, 2, 64, 512)], names2, "ab_out")
    dglu, dconv_w, dconv_b, dln_g, dln_b = _conv_bwd(p0, dc1, conv_w, ab_conv_b, ab_conv_ln_g, ab_conv_ln_b)
    dq0, dkv0, dsinks, got = _swa_bwd(p0, datt, sinks, tables, sums1 + sums2)
    g_sb_w_in, g_sb_w_out, g_ab_w_out, g_ab_w_pw2 = finish(sums1 + sums2, got, names1 + names2)
    dp0 = _Cols([(dq0, 0, 2), (dkv0, 2, 1), (dga, 3, 2), (dglu, 5, 4), (dgb, 9, 2)], 256)

    sums0 = sibling_stage([_dw_transposed(dp0, hn0, "dw_ab_in").reshape(4, 2, 352, D)], ["ab_in"], "ab_in")
    dh0, dg_ab_pre, got0 = _ab_in_bwd(dp0, w_ab_t, h0, ab_pre_norm, dh1, sums0)
    g_ab_w_in = finish(sums0, got0, ["ab_in"])[0].T

    small_parts = [dh0[PAD:BLK], dg_ab_pre, dsinks, dconv_w, dconv_b, dln_g, dln_b,
                   dg_ab_post, dg_sb_pre, dg_sb_post, loss_part]
    red = _reduce_small(_all_gather(small_parts, "gather_small_grads"), "reduce_small")
    col = lambda a, w: lax.dynamic_slice_in_dim(a, me * w, w, axis=1)
    g_meta = col(red[0], BLK)
    g_ab_pre = red[1]
    g_sinks = red[2][:, 0].reshape(1, 8)
    g_conv_w = col(red[3][:CONV_W], 64)
    g_conv_b, g_ln_g, g_ln_b, g_ab_post = red[4], red[5], red[6], red[7]
    g_sb_pre, g_sb_post = col(red[8], BLK), col(red[9], BLK)

    loss = red[10][0, 0]
    grad_x = dh0[BLK:][None]

    weights = [meta_tokens, ab_pre_norm, ab_w_in[0], ab_sinks, ab_conv_w[0], ab_conv_b, ab_conv_ln_g,
               ab_conv_ln_b, ab_w_pw2[0], ab_w_out[0], ab_post_norm, sb_pre_norm, sb_w_in[0],
               sb_w_out[0], sb_post_norm]
    grads = [g_meta, g_ab_pre, g_ab_w_in, g_sinks, g_conv_w, g_conv_b, g_ln_g, g_ln_b, g_ab_w_pw2,
             g_ab_w_out, g_ab_post, g_sb_pre, g_sb_w_in, g_sb_w_out, g_sb_post]
    ms = [m_meta_tokens, m_ab_pre_norm, m_ab_w_in[0], m_ab_sinks, m_ab_conv_w[0], m_ab_conv_b,
          m_ab_conv_ln_g, m_ab_conv_ln_b, m_ab_w_pw2[0], m_ab_w_out[0], m_ab_post_norm,
          m_sb_pre_norm, m_sb_w_in[0], m_sb_w_out[0], m_sb_post_norm]
    vs = [v_meta_tokens, v_ab_pre_norm, v_ab_w_in[0], v_ab_sinks, v_ab_conv_w[0], v_ab_conv_b,
          v_ab_conv_ln_g, v_ab_conv_ln_b, v_ab_w_pw2[0], v_ab_w_out[0], v_ab_post_norm,
          v_sb_pre_norm, v_sb_w_in[0], v_sb_w_out[0], v_sb_post_norm]
    lead = [w.ndim == 3 for w in (meta_tokens, ab_pre_norm, ab_w_in, ab_sinks, ab_conv_w, ab_conv_b,
                                   ab_conv_ln_g, ab_conv_ln_b, ab_w_pw2, ab_w_out, ab_post_norm,
                                   sb_pre_norm, sb_w_in, sb_w_out, sb_post_norm)]
    big_ids = [2, 8, 9, 12, 13]
    small_ids = [i for i in range(15) if i not in big_ids]
    deltas, new_m, new_v = [None] * 15, [None] * 15, [None] * 15
    for ids, nm in ((small_ids, "adamw_small"), (big_ids, "adamw_big")):
        d_, m_, v_ = _adamw([weights[i] for i in ids], [grads[i] for i in ids],
                            [ms[i] for i in ids], [vs[i] for i in ids], nm)
        for k, i in enumerate(ids):
            deltas[i], new_m[i], new_v[i] = d_[k], m_[k], v_[k]
    fix = lambda arrs: [a[None] if l else a for a, l in zip(arrs, lead)]
    return (loss, grad_x, *fix(grads), *fix(deltas), *fix(new_m), *fix(new_v))
```

```python
import functools

import numpy as np
import jax
import jax.numpy as jnp
from jax import lax
from jax.experimental import pallas as pl
from jax.experimental.pallas import tpu as pltpu

F32 = jnp.float32
MXU = jnp.bfloat16
ACT = jnp.bfloat16
WIRE = jnp.bfloat16

D = 1024
N_META = 16
BLK = 128
PAD = BLK - N_META
HEAD = 64
NEG = -1e30
EPS = 1e-6
LN_EPS = 1e-5
ROPE_THETA = 10000.0
SCALE = HEAD ** -0.5
CONV_W = 31
HALO = 32
LR, B1, B2, ADAM_EPS, WD, STEP = 0.001, 0.9, 0.999, 1e-08, 0.01, 10
VMEM_LIMIT = 56 * 1024 * 1024
MESH = pl.DeviceIdType.MESH

P0_SRC = (5, 6, 7, 8, 0, 1, 3, 4, 9, 10, 2)


def _cparams(sem=None):
    return pltpu.CompilerParams(dimension_semantics=sem, vmem_limit_bytes=VMEM_LIMIT)


def _tile(t, pref):
    for cand in (1088, 544, 272, 128):
        if cand <= pref and t % cand == 0:
            return cand
    raise ValueError(t)


def _sigmoid(x):
    return 1.0 / (1.0 + jnp.exp(-x))


def _silu_and_grad(x):
    s = _sigmoid(x)
    return x * s, s * (1.0 + x * (1.0 - s))


def _dot(a, b):
    return jnp.dot(a, b, preferred_element_type=F32)


def _dot_nt(a, b):
    return lax.dot_general(a, b, (((1,), (1,)), ((), ())), preferred_element_type=F32)


def _dot_tn(a, b):
    return lax.dot_general(a, b, (((0,), (0,)), ((), ())), preferred_element_type=F32)


def _rows(shape, base):
    return base + lax.broadcasted_iota(jnp.int32, shape, 0)


def _rope_tables(t):
    half = HEAD // 2
    inv = ROPE_THETA ** (-np.arange(half, dtype=np.float32) / half)
    pos = (np.arange(t) - PAD).astype(np.float32)
    ang = pos[:, None] * inv[None, :]
    lane = np.arange(BLK)
    cos = np.cos(ang)[:, lane % half].astype(np.float32)
    sin = np.sin(ang)[:, lane % half].astype(np.float32)
    first = (lane % HEAD) < half
    sin_a = np.where(first[None, :], -sin, 0.0).astype(np.float32)
    sin_b = np.where(first[None, :], 0.0, sin).astype(np.float32)
    return jnp.asarray(cos), jnp.asarray(sin_a), jnp.asarray(sin_b)


def _rope(v, cos, sin_a, sin_b):
    return v * cos + pltpu.roll(v, 96, 1) * sin_a + pltpu.roll(v, 32, 1) * sin_b


def _unrope(v, cos, sin_a, sin_b):
    return v * cos - pltpu.roll(v, 96, 1) * sin_a - pltpu.roll(v, 32, 1) * sin_b


def _coords():
    return lax.axis_index("x"), lax.axis_index("y"), lax.axis_index("c")


def _all_gather(arrs, name):
    plan = _GatherPlan(arrs)

    def body(*refs):
        plan.begin(refs)
        plan.end(refs)

    return pl.pallas_call(
        body, name=name, out_shape=plan.out_shape,
        in_specs=plan.specs, out_specs=plan.specs, scratch_shapes=plan.scratch,
    )(*arrs)


class _GatherPlan:
    def __init__(self, arrs):
        n = self.n = len(arrs)
        self.out_shape = [jax.ShapeDtypeStruct((8,) + a.shape, a.dtype) for a in arrs]
        self.specs = [pl.BlockSpec(memory_space=pl.ANY)] * n
        self.scratch = [pltpu.SemaphoreType.DMA((n, 7)), pltpu.SemaphoreType.DMA((n, 7)),
                        pltpu.SemaphoreType.DMA((n,))]

    def _copies(self, refs):
        n = self.n
        ins, outs = refs[:n], refs[n:2 * n]
        send_sems, recv_sems, local_sems = refs[2 * n:]
        x, y, c = _coords()
        me, sibling = (x, y, c), (x, y, 1 - c)
        chips = [(1 - x, y), (x, 1 - y), (1 - x, 1 - y)]

        def copy(a, k, block, to, src=None):
            dst = outs[a].at[4 * block[0] + 2 * block[1] + block[2]]
            return pltpu.make_async_remote_copy(
                src_ref=dst if src is None else src, dst_ref=dst,
                send_sem=send_sems.at[a, k], recv_sem=recv_sems.at[a, k],
                device_id=to, device_id_type=MESH)

        mine = [pltpu.make_async_copy(ins[a], outs[a].at[4 * x + 2 * y + c], local_sems.at[a])
                for a in range(n)]
        first = []
        for a in range(n):
            first.append(copy(a, 0, me, sibling, src=ins[a]))
            for j, chip in enumerate(chips):
                first.append(copy(a, 1 + j, me, (*chip, c), src=ins[a]))
        return copy, mine, first, (me, sibling, chips, c)

    def begin(self, refs):
        _, mine, first, _ = self._copies(refs)
        for cp in mine + first:
            cp.start()

    def end(self, refs):
        copy, mine, first, (me, sibling, chips, c) = self._copies(refs)
        passed = []
        for j, chip in enumerate(chips):
            for a in range(self.n):
                copy(a, 1 + j, (*chip, c), me).wait_recv()
                cp = copy(a, 4 + j, (*chip, c), sibling)
                cp.start()
                passed.append(cp)
        for a in range(self.n):
            copy(a, 0, sibling, me).wait_recv()
            for j, chip in enumerate(chips):
                copy(a, 4 + j, (*chip, 1 - c), me).wait_recv()
        for cp in first + passed:
            cp.wait_send()
        for cp in mine:
            cp.wait()


class _ChipsPlan:
    def __init__(self, sums):
        n = self.n = len(sums)
        self.out_shape = [jax.ShapeDtypeStruct((3,) + s.shape[1:], s.dtype) for s in sums]
        self.specs = [pl.BlockSpec(memory_space=pl.ANY)] * n
        self.scratch = [pltpu.SemaphoreType.DMA((n, 3)), pltpu.SemaphoreType.DMA((n, 3))]

    def _copies(self, refs):
        n = self.n
        ins, outs = refs[:n], refs[n:2 * n]
        send_sems, recv_sems = refs[2 * n:]
        x, y, c = _coords()
        chips = [(1 - x, y), (x, 1 - y), (1 - x, 1 - y)]
        return [pltpu.make_async_remote_copy(
            src_ref=ins[a].at[2 * chip[0] + chip[1]], dst_ref=outs[a].at[k],
            send_sem=send_sems.at[a, k], recv_sem=recv_sems.at[a, k],
            device_id=(*chip, c), device_id_type=MESH)
            for a in range(n) for k, chip in enumerate(chips)]

    def begin(self, refs):
        for cp in self._copies(refs):
            cp.start()

    def end(self, refs):
        for cp in self._copies(refs):
            cp.wait()


def _exchange_sibling(parts, name):
    n = len(parts)

    def body(*refs):
        ins, outs = refs[:n], refs[n:2 * n]
        send_sems, recv_sems = refs[2 * n:]
        x, y, c = _coords()
        copies = [pltpu.make_async_remote_copy(
            src_ref=ins[a].at[:, 1 - c], dst_ref=outs[a],
            send_sem=send_sems.at[a], recv_sem=recv_sems.at[a],
            device_id=(x, y, 1 - c), device_id_type=MESH) for a in range(n)]
        for cp in copies:
            cp.start()
        for cp in copies:
            cp.wait()

    any_spec = pl.BlockSpec(memory_space=pl.ANY)
    return pl.pallas_call(
        body, name=name,
        out_shape=[jax.ShapeDtypeStruct((4,) + p.shape[2:], p.dtype) for p in parts],
        in_specs=[any_spec] * n, out_specs=[any_spec] * n,
        scratch_shapes=[pltpu.SemaphoreType.DMA((n,)), pltpu.SemaphoreType.DMA((n,))],
    )(*parts)


def _exchange_chips(sums, name):
    plan = _ChipsPlan(sums)

    def body(*refs):
        plan.begin(refs)
        plan.end(refs)

    return pl.pallas_call(
        body, name=name, out_shape=plan.out_shape,
        in_specs=plan.specs, out_specs=plan.specs, scratch_shapes=plan.scratch,
    )(*sums)


def _add_sibling(pos, part, recv, name):
    _, _, r, c = part.shape

    def body(pos_ref, p_ref, r_ref, o_ref):
        o_ref[...] = (p_ref[...].astype(F32) + r_ref[...].astype(F32)).astype(o_ref.dtype)

    return pl.pallas_call(
        body, name=name,
        grid_spec=pltpu.PrefetchScalarGridSpec(
            num_scalar_prefetch=1, grid=(4,),
            in_specs=[pl.BlockSpec((None, None, r, c), lambda q, pos: (q, pos[2], 0, 0)),
                      pl.BlockSpec((None, r, c), lambda q, pos: (q, 0, 0))],
            out_specs=pl.BlockSpec((None, r, c), lambda q, pos: (q, 0, 0))),
        out_shape=jax.ShapeDtypeStruct((4, r, c), part.dtype),
        compiler_params=_cparams(("arbitrary",)),
    )(pos, part, recv)


def _sum_chips(pos, sums, recv, name):
    _, r, c = sums.shape

    def body(pos_ref, s_ref, r_ref, o_ref):
        g = s_ref[...].astype(F32)
        for k in range(3):
            g = g + r_ref[k].astype(F32)
        o_ref[...] = g

    return pl.pallas_call(
        body, name=name,
        grid_spec=pltpu.PrefetchScalarGridSpec(
            num_scalar_prefetch=1, grid=(1,),
            in_specs=[pl.BlockSpec((None, r, c), lambda i, pos: (2 * pos[0] + pos[1], 0, 0)),
                      pl.BlockSpec((3, r, c), lambda i, pos: (0, 0, 0))],
            out_specs=pl.BlockSpec((r, c), lambda i, pos: (0, 0))),
        out_shape=jax.ShapeDtypeStruct((r, c), F32),
        compiler_params=_cparams(("arbitrary",)),
    )(pos, sums, recv)


def _adamw(ws, gs, ms, vs, name):
    n = len(ws)
    c1 = 1.0 / (1.0 - B1 ** STEP)
    c2 = 1.0 / (1.0 - B2 ** STEP)

    def body(*refs):
        w_r, g_r, m_r, v_r = refs[:n], refs[n:2 * n], refs[2 * n:3 * n], refs[3 * n:4 * n]
        d_o, m_o, v_o = refs[4 * n:5 * n], refs[5 * n:6 * n], refs[6 * n:7 * n]
        for a in range(n):
            g = g_r[a][...]
            m = B1 * m_r[a][...] + (1.0 - B1) * g
            v = B2 * v_r[a][...] + (1.0 - B2) * (g * g)
            d_o[a][...] = -LR * ((m * c1) / (jnp.sqrt(v * c2) + ADAM_EPS) + WD * w_r[a][...])
            m_o[a][...] = m
            v_o[a][...] = v

    shapes = [jax.ShapeDtypeStruct(w.shape, F32) for w in ws]
    outs = pl.pallas_call(body, name=name, out_shape=shapes * 3,
                          compiler_params=_cparams())(*ws, *gs, *ms, *vs)
    return outs[:n], outs[n:2 * n], outs[2 * n:]


def _reduce_small(gathered, name):
    n = len(gathered)

    def body(*refs):
        for a in range(n):
            acc = refs[a][0]
            for k in range(1, 8):
                acc = acc + refs[a][k]
            refs[n + a][...] = acc

    return pl.pallas_call(
        body, name=name,
        out_shape=[jax.ShapeDtypeStruct(g.shape[1:], F32) for g in gathered],
        compiler_params=_cparams())(*gathered)


class _Cols:
    def __init__(self, pieces, tw):
        self.pieces, self.tw = pieces, tw
        self.arrays = [p[0] for p in pieces]
        self.n_tiles = sum(p[2] for p in pieces)

    def specs(self, tm, row_of, tile_of):
        out = []
        for _, first, cnt in self.pieces:
            def imap(*g, first=first, cnt=cnt):
                return (row_of(*g), jnp.clip(tile_of(*g) - first, 0, cnt - 1))
            out.append(pl.BlockSpec((tm, self.tw), imap))
        return out

    def apply(self, t, refs, fn):
        for ref, (_, first, cnt) in zip(refs, self.pieces):
            pl.when((t >= first) & (t < first + cnt))(functools.partial(fn, ref))


def _ab_in(h, g, w_t, tables, ride):
    t = h.shape[0]
    tm = _tile(t, 1088)
    src = jnp.asarray(np.array(P0_SRC, np.int32))
    plan = _GatherPlan(ride)
    nr = plan.n

    def body(src_ref, h_ref, g_ref, w_ref, cos_ref, sa_ref, sb_ref, *rest):
        o_ref, hn_ref = rest[nr:nr + 2]
        hn_s = rest[2 * nr + 2]
        comm = (*rest[:nr], *rest[nr + 2:2 * nr + 2], *rest[2 * nr + 3:])
        i, j = pl.program_id(0), pl.program_id(1)
        pl.when((i == 0) & (j == 0))(lambda: plan.begin(comm))

        @pl.when(j == 0)
        def _():
            x = h_ref[...]
            hn = (x * lax.rsqrt(jnp.mean(x * x, -1, keepdims=True) + EPS) * g_ref[...]).astype(MXU)
            hn_s[...] = hn
            hn_ref[...] = hn.astype(ACT)

        acc = _dot_nt(hn_s[...], w_ref[...])
        rope = lambda v: _rope(v, cos_ref[...], sa_ref[...], sb_ref[...])

        @pl.when((j == 4) | (j == 5))
        def _():
            o_ref[:, :BLK] = rope(acc[:, :BLK])
            o_ref[:, BLK:] = rope(acc[:, BLK:])

        @pl.when(j == 10)
        def _():
            o_ref[:, :BLK] = rope(acc[:, :BLK])
            o_ref[:, BLK:] = acc[:, BLK:]

        @pl.when((j < 4) | ((j > 5) & (j < 10)))
        def _():
            o_ref[...] = acc

        pl.when((i == t // tm - 1) & (j == 10))(lambda: plan.end(comm))

    tab = pl.BlockSpec((tm, BLK), lambda i, j, s: (i, 0))
    outs = pl.pallas_call(
        body, name="ab_in",
        grid_spec=pltpu.PrefetchScalarGridSpec(
            num_scalar_prefetch=1, grid=(t // tm, 11),
            in_specs=[pl.BlockSpec((tm, D), lambda i, j, s: (i, 0)),
                      pl.BlockSpec((1, D), lambda i, j, s: (0, 0)),
                      pl.BlockSpec((256, D), lambda i, j, s: (s[j], 0)),
                      tab, tab, tab] + plan.specs,
            out_specs=[pl.BlockSpec((tm, 256), lambda i, j, s: (i, j)),
                       pl.BlockSpec((tm, D), lambda i, j, s: (i, 0))] + plan.specs,
            scratch_shapes=[pltpu.VMEM((tm, D), MXU)] + plan.scratch),
        out_shape=[jax.ShapeDtypeStruct((t, 2816), F32), jax.ShapeDtypeStruct((t, D), ACT)] + plan.out_shape,
        compiler_params=_cparams(("arbitrary", "arbitrary")),
    )(src, h, g, w_t, *tables, *ride)
    return outs[0], outs[1], outs[2:]


def _swa_mask(n):
    r = lax.broadcasted_iota(jnp.int32, (BLK, 3 * BLK), 0)
    c = lax.broadcasted_iota(jnp.int32, (BLK, 3 * BLK), 1)
    qpos = n * BLK + r
    bpos = (n - 2) * BLK + c
    meta_ok = (c >= PAD) & (c < BLK) & (qpos - c >= BLK)
    band_ok = (c >= BLK) & (bpos >= PAD) & (qpos >= bpos) & (qpos - bpos < BLK)
    return meta_ok | band_ok


def _swa_keys(kv_ref, n):
    def blk(b):
        return kv_ref[pl.ds(pl.multiple_of(b * BLK, BLK), BLK), :]
    kv = jnp.concatenate([kv_ref[0:BLK, :], blk(jnp.maximum(n - 1, 0)), blk(n)], axis=0)
    lo = lax.broadcasted_iota(jnp.int32, (1, BLK), 1) < HEAD
    out = []
    for part in (kv[:, :BLK], kv[:, BLK:]):
        rolled = pltpu.roll(part, HEAD, 1)
        out.append((jnp.where(lo, part, rolled).astype(MXU), jnp.where(lo, rolled, part).astype(MXU)))
    return out[0], out[1], lo


def _swa_stack(ref, g, lo):
    parts = []
    for p in (2 * g, 2 * g + 1):
        x = ref[:, p * BLK:(p + 1) * BLK]
        parts += [jnp.where(lo, x, 0.0), jnp.where(lo, 0.0, x)]
    return jnp.concatenate(parts, axis=0).astype(MXU)


def _swa_probs(qs, kd, mask4, sink_ref, g):
    sink = jnp.concatenate([jnp.full((BLK, 1), sink_ref[4 * g + h], F32) for h in range(4)], axis=0)
    s = jnp.where(mask4, _dot_nt(qs, kd) * SCALE, NEG)
    m = jnp.maximum(jnp.max(s, -1, keepdims=True), sink)
    e = jnp.exp(s - m)
    e_sink = jnp.exp(sink - m)
    inv = 1.0 / (jnp.sum(e, -1, keepdims=True) + e_sink)
    return e * inv, e_sink * inv


def _swa_unstack(x, lo):
    return [jnp.where(lo, x[0:BLK], x[BLK:2 * BLK]), jnp.where(lo, x[2 * BLK:3 * BLK], x[3 * BLK:])]


def _swa_fwd(p0, sinks, ride):
    t = p0.shape[0]
    plan = _GatherPlan(ride)
    nr = plan.n

    def body(sink_ref, q_ref, kv_ref, *rest):
        o_ref = rest[nr]
        comm = (*rest[:nr], *rest[nr + 1:])
        n = pl.program_id(0)
        pl.when(n == 0)(lambda: plan.begin(comm))
        kd, vd, lo = _swa_keys(kv_ref, n)
        mask4 = jnp.concatenate([_swa_mask(n)] * 4, axis=0)
        for g in range(2):
            pr, _ = _swa_probs(_swa_stack(q_ref, g, lo), kd[g], mask4, sink_ref, g)
            pairs = _swa_unstack(_dot(pr.astype(MXU), vd[g]), lo)
            for k in range(2):
                p = 2 * g + k
                o_ref[:, p * BLK:(p + 1) * BLK] = pairs[k]
        pl.when(n == t // BLK - 1)(lambda: plan.end(comm))

    outs = pl.pallas_call(
        body, name="swa_fwd", grid=(t // BLK,),
        in_specs=[pl.BlockSpec(memory_space=pltpu.SMEM),
                  pl.BlockSpec((BLK, 512), lambda n: (n, 2)),
                  pl.BlockSpec((t, 256), lambda n: (0, 10))] + plan.specs,
        out_specs=[pl.BlockSpec((BLK, 512), lambda n: (n, 0))] + plan.specs,
        out_shape=[jax.ShapeDtypeStruct((t, 512), F32)] + plan.out_shape,
        scratch_shapes=plan.scratch,
        compiler_params=_cparams(("arbitrary",)),
    )(sinks, p0, p0, *ride)
    return outs[0], outs[1:]


def _conv_window(u_w, w_ref, n_out, first):
    rows = u_w.shape[0]
    acc = None
    for j in range(CONV_W):
        shifted = pltpu.roll(u_w, (rows - (first + j)) % rows, 0)[:n_out]
        term = shifted * w_ref[j:j + 1, :]
        acc = term if acc is None else acc + term
    return acc


def _conv_fwd(p0, conv_w, conv_b, ln_g, ln_b):
    t = p0.shape[0]
    tm = _tile(t, 544)
    hb = tm // HALO

    def body(cur_ref, prev_ref, w_ref, b_ref, g_ref, bb_ref, o_ref):
        i = pl.program_id(0)
        glu = jnp.concatenate([prev_ref[...], cur_ref[...]], axis=0)
        rw = _rows((tm + HALO, 1), i * tm - HALO)
        u_w = jnp.where(rw >= PAD, glu[:, :512] * _sigmoid(glu[:, 512:]), 0.0)
        cv = _conv_window(u_w, w_ref, tm, HALO - (CONV_W - 1)) + b_ref[...]
        xc = cv - jnp.mean(cv, -1, keepdims=True)
        ln = xc * lax.rsqrt(jnp.mean(xc * xc, -1, keepdims=True) + LN_EPS) * g_ref[...] + bb_ref[...]
        o_ref[...] = (ln * _sigmoid(ln)).astype(ACT)

    vec = pl.BlockSpec((1, 512), lambda i: (0, 0))
    return pl.pallas_call(
        body, name="conv_fwd", grid=(t // tm,),
        in_specs=[pl.BlockSpec((tm, D), lambda i: (i, 0)),
                  pl.BlockSpec((HALO, D), lambda i: (jnp.maximum(i * hb - 1, 0), 0)),
                  pl.BlockSpec((CONV_W, 512), lambda i: (0, 0)), vec, vec, vec],
        out_specs=pl.BlockSpec((tm, 512), lambda i: (i, 0)),
        out_shape=jax.ShapeDtypeStruct((t, 512), ACT),
        compiler_params=_cparams(("arbitrary",)),
    )(p0, p0, conv_w, conv_b, ln_g, ln_b)


def _ab_out(h, p0, att, c1, w_pw2, w_out, g_post):
    t = h.shape[0]
    tm = _tile(t, 544)

    def body(h_ref, ga_ref, gb_ref, att_ref, c1_ref, pw_ref, wo_ref, g_ref, h1_ref, y_ref, mix_ref):
        i = pl.program_id(0)
        sga, _ = _silu_and_grad(ga_ref[...])
        sgb, _ = _silu_and_grad(gb_ref[...])
        a = att_ref[...] * sga
        c = _dot(c1_ref[...].astype(MXU), pw_ref[...]) * sgb
        mix = jnp.concatenate([a, c], axis=1).astype(MXU)
        y = _dot(mix, wo_ref[...])
        yn = y * lax.rsqrt(jnp.mean(y * y, -1, keepdims=True) + EPS) * g_ref[...]
        h1_ref[...] = jnp.where(_rows((tm, 1), i * tm) >= PAD, h_ref[...] + yn, 0.0)
        y_ref[...] = y
        mix_ref[...] = mix.astype(ACT)

    row = lambda w, idx: pl.BlockSpec((tm, w), lambda i: (i, idx))
    full = lambda a: pl.BlockSpec(a.shape, lambda i: (0, 0))
    return pl.pallas_call(
        body, name="ab_out", grid=(t // tm,),
        in_specs=[row(D, 0), row(512, 3), row(512, 4), row(512, 0), row(512, 0),
                  full(w_pw2), full(w_out), full(g_post)],
        out_specs=[row(D, 0), row(D, 0), row(D, 0)],
        out_shape=[jax.ShapeDtypeStruct((t, D), F32), jax.ShapeDtypeStruct((t, D), F32),
                   jax.ShapeDtypeStruct((t, D), ACT)],
        compiler_params=_cparams(("arbitrary",)),
    )(h, p0, p0, att, c1, w_pw2, w_out, g_post)


def _sb_in(h, g, w):
    t = h.shape[0]
    tm = _tile(t, 1088)

    def body(h_ref, g_ref, w_ref, o_ref, hn_ref, hn_s):
        @pl.when(pl.program_id(1) == 0)
        def _():
            x = h_ref[...]
            hn = (x * lax.rsqrt(jnp.mean(x * x, -1, keepdims=True) + EPS) * g_ref[...]).astype(MXU)
            hn_s[...] = hn
            hn_ref[...] = hn.astype(ACT)

        o_ref[...] = _dot(hn_s[...], w_ref[...])

    return pl.pallas_call(
        body, name="sb_in", grid=(t // tm, 8),
        in_specs=[pl.BlockSpec((tm, D), lambda i, j: (i, 0)),
                  pl.BlockSpec((1, D), lambda i, j: (0, 0)),
                  pl.BlockSpec((None, D, 512), lambda i, j: (j, 0, 0))],
        out_specs=[pl.BlockSpec((tm, 512), lambda i, j: (i, j)),
                   pl.BlockSpec((tm, D), lambda i, j: (i, 0))],
        out_shape=[jax.ShapeDtypeStruct((t, 4096), F32), jax.ShapeDtypeStruct((t, D), ACT)],
        scratch_shapes=[pltpu.VMEM((tm, D), MXU)],
        compiler_params=_cparams(("arbitrary", "arbitrary")),
    )(h, g, w)


def _split_hi_lo(x):
    hi = x.astype(MXU)
    lo = (x - hi.astype(F32)).astype(MXU)
    return hi, lo


def _scan_matrix(suffix):
    j = lax.broadcasted_iota(jnp.int32, (2 * BLK, 2 * BLK), 0) % BLK
    s = lax.broadcasted_iota(jnp.int32, (2 * BLK, 2 * BLK), 1)
    keep = (s >= BLK) | ((j > s) if suffix else (j < s))
    return jnp.where(keep, 1.0, 0.0).astype(MXU)


def _scan_packed(hi_lo, b, mat):
    cols = slice(b * BLK, (b + 1) * BLK)
    both = _dot(jnp.concatenate([hi_lo[:BLK, cols], hi_lo[BLK:, cols]], axis=1), mat)
    return both[:, :BLK], both[:, BLK:]


KC = 4
CHUNK = KC * BLK
SLACK = CHUNK - BLK
GROUPS = 2
GROUPS_FWD = 4


def _sb_logits(qm, kc, valid):
    z = _dot_nt(qm, kc)
    log_beta = jnp.minimum(z, 0.0) - jnp.log(1.0 + jnp.exp(-jnp.abs(z)))
    return log_beta, jnp.where(valid, log_beta - z, 0.0)


def _sb_valid(i, first_key):
    r = lax.broadcasted_iota(jnp.int32, (BLK, CHUNK), 0)
    c = lax.broadcasted_iota(jnp.int32, (BLK, CHUNK), 1)
    kpos = first_key + c
    return (kpos >= PAD) & (kpos < i * BLK + r)


def _sb_fwd(p1):
    t = p1.shape[0]
    groups = GROUPS_FWD
    w = groups * BLK

    def body(q_ref, k_ref, v_ref, o_ref, lt_ref, k_s, v_s):
        i = pl.program_id(1)

        @pl.when(i == 0)
        def _():
            for src, dst in ((k_ref, k_s), (v_ref, v_s)):
                dst[0:SLACK, :] = jnp.zeros((SLACK, w), MXU)
                dst[SLACK:, :] = src[...].astype(MXU)

        lo = lax.broadcasted_iota(jnp.int32, (1, BLK), 1) < HEAD
        qm = []
        for g in range(groups):
            q = q_ref[:, g * BLK:(g + 1) * BLK] * SCALE
            qm += [jnp.where(lo, q, 0.0).astype(MXU), jnp.where(lo, 0.0, q).astype(MXU)]
        mat = _scan_matrix(True)
        heads = range(2 * groups)

        def step(s, carry):
            start = pl.multiple_of((i - KC * s) * BLK, BLK)
            valid = _sb_valid(i, start - SLACK)
            new, staged = [], []
            for h in heads:
                lanes = slice((h // 2) * BLK, (h // 2 + 1) * BLK)
                log_beta, log_1m = _sb_logits(qm[h], k_s[pl.ds(start, CHUNK), lanes], valid)
                staged.append((log_beta, jnp.concatenate(_split_hi_lo(log_1m), axis=0)))
            probs = []
            for h in heads:
                log_beta, hi_lo = staged[h]
                run = carry[2 * h]
                parts = [None] * KC
                for b in reversed(range(KC)):
                    after, total = _scan_packed(hi_lo, b, mat)
                    parts[b] = after + run
                    run = run + total
                a = jnp.where(valid, jnp.exp(log_beta + jnp.concatenate(parts, axis=1)), 0.0)
                probs.append((run, a.astype(MXU)))
            for h in heads:
                lanes = slice((h // 2) * BLK, (h // 2 + 1) * BLK)
                run, a = probs[h]
                new += [run, carry[2 * h + 1] + _dot(a, v_s[pl.ds(start, CHUNK), lanes])]
            return tuple(new)

        zero = jnp.zeros((BLK, BLK), F32)
        res = lax.fori_loop(0, (i + KC) // KC, step, (zero,) * (4 * groups))
        for g in range(groups):
            lanes = slice(g * BLK, (g + 1) * BLK)
            o_ref[:, lanes] = jnp.where(lo, res[4 * g + 1], res[4 * g + 3])
            lt_ref[:, lanes] = jnp.where(lo, res[4 * g], res[4 * g + 2])

    ng = D // w
    blk = pl.BlockSpec((BLK, w), lambda hp, i: (i, hp))
    return pl.pallas_call(
        body, name="sb_fwd", grid=(ng, t // BLK),
        in_specs=[blk,
                  pl.BlockSpec((t, w), lambda hp, i: (0, ng + hp)),
                  pl.BlockSpec((t, w), lambda hp, i: (0, 2 * ng + hp))],
        out_specs=[blk, blk],
        out_shape=[jax.ShapeDtypeStruct((t, D), F32)] * 2,
        scratch_shapes=[pltpu.VMEM((t + SLACK, w), MXU), pltpu.VMEM((t + SLACK, w), MXU)],
        compiler_params=_cparams(("arbitrary", "arbitrary")),
    )(p1, p1, p1)


def _sb_out(o, p1, w_out, h1, g_post, tgt):
    t = o.shape[0]
    tm = _tile(t, 544)

    def body(o_ref, g_ref, w_ref, h_ref, gp_ref, t_ref,
             loss_ref, dh_ref, dy_ref, m_ref, do_ref, dg_ref, dgp_ref):
        i = pl.program_id(0)

        @pl.when(i == 0)
        def _():
            loss_ref[...] = jnp.zeros_like(loss_ref)
            dgp_ref[...] = jnp.zeros_like(dgp_ref)

        gate = g_ref[...]
        sg, dsg = _silu_and_grad(gate)
        ov = o_ref[...]
        m = (ov * sg).astype(MXU)
        y = _dot(m, w_ref[...])
        r = lax.rsqrt(jnp.mean(y * y, -1, keepdims=True) + EPS)
        yhat = y * r
        h2 = h_ref[...] + yhat * gp_ref[...]
        diff = jnp.where(_rows((tm, 1), i * tm) >= BLK, h2 - t_ref[...], 0.0)
        loss_ref[...] += jnp.full(loss_ref.shape, 0.5 / D, F32) * jnp.sum(diff * diff)
        dh = diff * (1.0 / D)
        dgp_ref[...] += jnp.sum(dh * yhat, 0, keepdims=True)
        dyn = dh * gp_ref[...]
        dy = (r * (dyn - yhat * jnp.mean(dyn * yhat, -1, keepdims=True))).astype(MXU)
        dm = _dot_nt(dy, w_ref[...])
        dh_ref[...] = dh
        dy_ref[...] = dy.astype(ACT)
        m_ref[...] = m.astype(ACT)
        do_ref[...] = dm * sg
        dg_ref[...] = (dm * ov * dsg).astype(ACT)

    row = lambda idx: pl.BlockSpec((tm, D), lambda i: (i, idx))
    full = lambda a: pl.BlockSpec(a.shape, lambda i: (0, 0))
    acc = lambda s: pl.BlockSpec(s, lambda i: (0, 0))
    return pl.pallas_call(
        body, name="sb_out", grid=(t // tm,),
        in_specs=[row(0), row(3), full(w_out), row(0), full(g_post), row(0)],
        out_specs=[acc((8, BLK)), row(0), row(0), row(0), row(0), row(0), acc((1, D))],
        out_shape=[jax.ShapeDtypeStruct((8, BLK), F32), jax.ShapeDtypeStruct((t, D), F32),
                   jax.ShapeDtypeStruct((t, D), ACT), jax.ShapeDtypeStruct((t, D), ACT),
                   jax.ShapeDtypeStruct((t, D), F32), jax.ShapeDtypeStruct((t, D), ACT),
                   jax.ShapeDtypeStruct((1, D), F32)],
        compiler_params=_cparams(("arbitrary",)),
    )(o, p1, w_out, h1, g_post, tgt)


def _sb_bwd(p1, ltot, do):
    t = p1.shape[0]
    nb = t // BLK

    w = GROUPS * BLK

    def body(q_ref, k_ref, v_ref, lt_ref, do_ref, dq_ref, dk_ref, dv_ref, k_s, v_s, dk_s, dv_s):
        i = pl.program_id(1)
        lo = lax.broadcasted_iota(jnp.int32, (1, BLK), 1) < HEAD

        @pl.when(i == 0)
        def _():
            for src, dst in ((k_ref, k_s), (v_ref, v_s)):
                dst[0:t, :] = src[...].astype(MXU)
                dst[t:, :] = jnp.zeros((SLACK, w), MXU)
            dk_s[...] = jnp.zeros_like(dk_s)
            dv_s[...] = jnp.zeros_like(dv_s)

        qm, dom, row_total, q2, do2 = [], [], [], [], []
        for g in range(GROUPS):
            lanes = slice(g * BLK, (g + 1) * BLK)
            q, dout, lt = q_ref[:, lanes] * SCALE, do_ref[:, lanes], lt_ref[:, lanes]
            qm += [jnp.where(lo, q, 0.0).astype(MXU), jnp.where(lo, 0.0, q).astype(MXU)]
            dom += [jnp.where(lo, dout, 0.0).astype(MXU), jnp.where(lo, 0.0, dout).astype(MXU)]
            q2.append(jnp.concatenate(qm[-2:], axis=0))
            do2.append(jnp.concatenate(dom[-2:], axis=0))
            lt_r = pltpu.roll(lt, HEAD, 1)
            row_total += [jnp.where(lo, lt, lt_r), jnp.where(lo, lt_r, lt)]
        mat_l = _scan_matrix(True)
        mat_g = _scan_matrix(False)
        heads = range(2 * GROUPS)

        def step(s, carry):
            start = pl.multiple_of(s * CHUNK, BLK)
            keys = lambda ref, h: ref[pl.ds(start, CHUNK), (h // 2) * BLK:(h // 2 + 1) * BLK]
            valid = _sb_valid(i, start)
            new, dzs, probs, st1, st2, st3 = [], [], [], [], [], []
            for h in heads:
                log_beta, log_1m = _sb_logits(qm[h], keys(k_s, h), valid)
                st1.append((log_beta, jnp.concatenate(_split_hi_lo(log_1m), axis=0),
                            _dot_nt(dom[h], keys(v_s, h))))
            for h in heads:
                log_beta, hi_lo, da = st1[h]
                run = carry[3 * h]
                parts = []
                for b in range(KC):
                    after, total = _scan_packed(hi_lo, b, mat_l)
                    run = run + total
                    parts.append(after + (row_total[h] - run))
                a = jnp.where(valid, jnp.exp(log_beta + jnp.concatenate(parts, axis=1)), 0.0)
                g = da * a
                probs.append(a.astype(MXU))
                st2.append((run, g, jnp.concatenate(_split_hi_lo(g), axis=0)))
            for h in heads:
                run, g, hi_lo = st2[h]
                run_g = carry[3 * h + 1]
                parts = []
                for b in range(KC):
                    before, total_g = _scan_packed(hi_lo, b, mat_g)
                    parts.append(before + run_g)
                    run_g = run_g + total_g
                sig = jnp.exp(st1[h][0])
                dz = jnp.where(valid, g - sig * (g + jnp.concatenate(parts, axis=1)), 0.0)
                dzs.append(dz.astype(MXU))
                st3.append((run, run_g))
            for h in heads:
                new += [*st3[h], carry[3 * h + 2] + _dot(dzs[h], keys(k_s, h))]
            for g in range(GROUPS):
                lanes = slice(g * BLK, (g + 1) * BLK)
                dk_s[pl.ds(start, CHUNK), lanes] += _dot_tn(jnp.concatenate(dzs[2 * g:2 * g + 2], axis=0), q2[g])
                dv_s[pl.ds(start, CHUNK), lanes] += _dot_tn(jnp.concatenate(probs[2 * g:2 * g + 2], axis=0), do2[g])
            return tuple(new)

        zero = jnp.zeros((BLK, BLK), F32)
        res = lax.fori_loop(0, (i + KC) // KC, step, (zero,) * (6 * GROUPS))
        for g in range(GROUPS):
            dq = jnp.where(lo, res[6 * g + 2], res[6 * g + 5])
            dq_ref[:, g * BLK:(g + 1) * BLK] = (dq * SCALE).astype(ACT)

        @pl.when(i == nb - 1)
        def _():
            dk_ref[...] = dk_s[0:t, :].astype(ACT)
            dv_ref[...] = dv_s[0:t, :].astype(ACT)

    ng = D // w
    blk = pl.BlockSpec((BLK, w), lambda hp, i: (i, hp))
    col = lambda off: pl.BlockSpec((t, w), lambda hp, i: (0, off + hp))
    return pl.pallas_call(
        body, name="sb_bwd", grid=(ng, nb),
        in_specs=[blk, col(ng), col(2 * ng), blk, blk],
        out_specs=[blk, col(0), col(0)],
        out_shape=[jax.ShapeDtypeStruct((t, D), ACT)] * 3,
        scratch_shapes=[pltpu.VMEM((t + SLACK, w), MXU), pltpu.VMEM((t + SLACK, w), MXU),
                        pltpu.VMEM((t + SLACK, w), F32), pltpu.VMEM((t + SLACK, w), F32)],
        compiler_params=_cparams(("arbitrary", "arbitrary")),
    )(p1, p1, p1, ltot, do)


def _norm_bwd(x, g, dy, eps):
    r = lax.rsqrt(jnp.mean(x * x, -1, keepdims=True) + eps)
    xhat = x * r
    dxn = dy * g
    return r * (dxn - xhat * jnp.mean(dxn * xhat, -1, keepdims=True)), jnp.sum(dy * xhat, 0, keepdims=True)


def _sb_in_bwd(dparts, w_sb, h1, g_pre1, dh2):
    t = h1.shape[0]
    tm = _tile(t, 544)

    def body(dq_ref, dk_ref, dv_ref, dg_ref, w_ref, h_ref, g_ref, dh2_ref, dh1_ref, dgn_ref):
        @pl.when(pl.program_id(0) == 0)
        def _():
            dgn_ref[...] = jnp.zeros_like(dgn_ref)

        dhn = None
        for a, ref in enumerate((dq_ref, dk_ref, dv_ref, dg_ref)):
            for b in range(2):
                term = _dot_nt(ref[:, b * 512:(b + 1) * 512].astype(MXU), w_ref[2 * a + b])
                dhn = term if dhn is None else dhn + term
        dx, dg = _norm_bwd(h_ref[...], g_ref[...], dhn, EPS)
        dgn_ref[...] += dg
        dh1_ref[...] = dh2_ref[...] + dx

    row = pl.BlockSpec((tm, D), lambda i: (i, 0))
    vec = pl.BlockSpec((1, D), lambda i: (0, 0))
    return pl.pallas_call(
        body, name="sb_in_bwd", grid=(t // tm,),
        in_specs=[row, row, row, row, pl.BlockSpec(w_sb.shape, lambda i: (0, 0, 0)), row, vec, row],
        out_specs=[row, vec],
        out_shape=[jax.ShapeDtypeStruct((t, D), F32), jax.ShapeDtypeStruct((1, D), F32)],
        compiler_params=_cparams(("arbitrary",)),
    )(*dparts, w_sb, h1, g_pre1, dh2)


def _ab_out_bwd(dh1, y0, g_post0, w_out, p0, att, c1, w_pw2):
    t = dh1.shape[0]
    tm = _tile(t, 544)

    def body(dh1_ref, y_ref, g0_ref, wo_ref, ga_ref, gb_ref, att_ref, c1_ref, pw_ref,
             dy_ref, dga_ref, dgb_ref, datt_ref, dc1_ref, dc2_ref, dg0_ref):
        @pl.when(pl.program_id(0) == 0)
        def _():
            dg0_ref[...] = jnp.zeros_like(dg0_ref)

        dy, dg = _norm_bwd(y_ref[...], g0_ref[...], dh1_ref[...], EPS)
        dg0_ref[...] += dg
        dy = dy.astype(MXU)
        dy_ref[...] = dy.astype(ACT)
        dmix = _dot_nt(dy, wo_ref[...])
        da, dc = dmix[:, :512], dmix[:, 512:]
        sga, dsga = _silu_and_grad(ga_ref[...])
        sgb, dsgb = _silu_and_grad(gb_ref[...])
        datt_ref[...] = da * sga
        dga_ref[...] = (da * att_ref[...] * dsga).astype(ACT)
        c2 = _dot(c1_ref[...].astype(MXU), pw_ref[...])
        dc2 = (dc * sgb).astype(MXU)
        dgb_ref[...] = (dc * c2 * dsgb).astype(ACT)
        dc2_ref[...] = dc2.astype(ACT)
        dc1_ref[...] = _dot_nt(dc2, pw_ref[...])

    row = lambda w, idx: pl.BlockSpec((tm, w), lambda i: (i, idx))
    full = lambda a: pl.BlockSpec(a.shape, lambda i: (0, 0))
    sd = jax.ShapeDtypeStruct
    return pl.pallas_call(
        body, name="ab_out_bwd", grid=(t // tm,),
        in_specs=[row(D, 0), row(D, 0), full(g_post0), full(w_out), row(512, 3), row(512, 4),
                  row(512, 0), row(512, 0), full(w_pw2)],
        out_specs=[row(D, 0), row(512, 0), row(512, 0), row(512, 0), row(512, 0), row(512, 0),
                   pl.BlockSpec((1, D), lambda i: (0, 0))],
        out_shape=[sd((t, D), ACT), sd((t, 512), ACT), sd((t, 512), ACT), sd((t, 512), F32),
                   sd((t, 512), F32), sd((t, 512), ACT), sd((1, D), F32)],
        compiler_params=_cparams(("arbitrary",)),
    )(dh1, y0, g_post0, w_out, p0, p0, att, c1, w_pw2)


def _conv_bwd(p0, dc1, conv_w, conv_b, ln_g, ln_b):
    t = p0.shape[0]
    tm = _tile(t, 544)
    hb = tm // HALO
    last = t // HALO - 1

    def body(cur_ref, prev_ref, next_ref, d_ref, dn_ref, w_ref, b_ref, g_ref, bb_ref,
             dglu_ref, dw_ref, db_ref, dlg_ref, dlb_ref):
        i = pl.program_id(0)

        @pl.when(i == 0)
        def _():
            for ref in (dw_ref, db_ref, dlg_ref, dlb_ref):
                ref[...] = jnp.zeros_like(ref)

        glu = jnp.concatenate([prev_ref[...], cur_ref[...], next_ref[...]], axis=0)
        rw = _rows((tm + 2 * HALO, 1), i * tm - HALO)
        ga, sg = glu[:, :512], _sigmoid(glu[:, 512:])
        u_w = jnp.where((rw >= PAD) & (rw < t), ga * sg, 0.0)
        n_cv = tm + HALO
        cv = _conv_window(u_w, w_ref, n_cv, HALO - (CONV_W - 1)) + b_ref[...]
        xc = cv - jnp.mean(cv, -1, keepdims=True)
        rstd = lax.rsqrt(jnp.mean(xc * xc, -1, keepdims=True) + LN_EPS)
        cvhat = xc * rstd
        ln = cvhat * g_ref[...] + bb_ref[...]
        _, dsl = _silu_and_grad(ln)
        rc = _rows((n_cv, 1), i * tm)
        dc = jnp.concatenate([d_ref[...], dn_ref[...]], axis=0)
        dln = jnp.where(rc < t, dc * dsl, 0.0)
        dhat = dln * g_ref[...]
        dcv = rstd * (dhat - jnp.mean(dhat, -1, keepdims=True)
                      - cvhat * jnp.mean(dhat * cvhat, -1, keepdims=True))
        own = dcv[:tm]
        dlg_ref[...] += jnp.sum((dln * cvhat)[:tm], 0, keepdims=True)
        dlb_ref[...] += jnp.sum(dln[:tm], 0, keepdims=True)
        db_ref[...] += jnp.sum(own, 0, keepdims=True)
        rows = tm + 2 * HALO
        du = None
        for j in range(CONV_W):
            first = HALO - (CONV_W - 1) + j
            shifted = pltpu.roll(u_w, (rows - first) % rows, 0)[:tm]
            dw_ref[j:j + 1, :] += jnp.sum(own * shifted, 0, keepdims=True)
            back = pltpu.roll(dcv, (n_cv - (CONV_W - 1 - j)) % n_cv, 0)[:tm]
            term = back * w_ref[j:j + 1, :]
            du = term if du is None else du + term
        du = jnp.where(_rows((tm, 1), i * tm) >= PAD, du, 0.0)
        ga_c, sg_c = ga[HALO:HALO + tm], sg[HALO:HALO + tm]
        dglu_ref[:, :512] = (du * sg_c).astype(ACT)
        dglu_ref[:, 512:] = (du * ga_c * sg_c * (1.0 - sg_c)).astype(ACT)

    vec = pl.BlockSpec((1, 512), lambda i: (0, 0))
    nxt = lambda i: (jnp.minimum((i + 1) * hb, last), 0)
    return pl.pallas_call(
        body, name="conv_bwd", grid=(t // tm,),
        in_specs=[pl.BlockSpec((tm, D), lambda i: (i, 0)),
                  pl.BlockSpec((HALO, D), lambda i: (jnp.maximum(i * hb - 1, 0), 0)),
                  pl.BlockSpec((HALO, D), nxt),
                  pl.BlockSpec((tm, 512), lambda i: (i, 0)),
                  pl.BlockSpec((HALO, 512), nxt),
                  pl.BlockSpec((CONV_W, 512), lambda i: (0, 0)), vec, vec, vec],
        out_specs=[pl.BlockSpec((tm, D), lambda i: (i, 0)),
                   pl.BlockSpec((HALO, 512), lambda i: (0, 0)), vec, vec, vec],
        out_shape=[jax.ShapeDtypeStruct((t, D), ACT), jax.ShapeDtypeStruct((HALO, 512), F32)]
        + [jax.ShapeDtypeStruct((1, 512), F32)] * 3,
        compiler_params=_cparams(("arbitrary",)),
    )(p0, p0, p0, dc1, dc1, conv_w, conv_b, ln_g, ln_b)


def _swa_bwd(p0, datt, sinks, tables, ride):
    t = p0.shape[0]
    nb = t // BLK
    plan = _ChipsPlan(ride)
    nr = plan.n

    def body(sink_ref, q_ref, kv_ref, d_ref, cos_ref, sa_ref, sb_ref, *rest):
        dq_ref, dkv_ref, ds_ref = rest[nr:nr + 3]
        acc = rest[2 * nr + 3]
        comm = (*rest[:nr], *rest[nr + 3:2 * nr + 3], *rest[2 * nr + 4:])
        n = pl.program_id(0)
        pl.when(n == 0)(lambda: plan.begin(comm))

        @pl.when(n == 0)
        def _():
            acc[...] = jnp.zeros_like(acc)
            ds_ref[...] = jnp.zeros_like(ds_ref)

        kd, vd, lo = _swa_keys(kv_ref, n)
        mask4 = jnp.concatenate([_swa_mask(n)] * 4, axis=0)
        row0 = pl.multiple_of(n * BLK, BLK)
        tabs = [r[pl.ds(row0, BLK), :] for r in (cos_ref, sa_ref, sb_ref)]
        dk_g, dv_g = [], []
        for g in range(2):
            qs, dos = _swa_stack(q_ref, g, lo), _swa_stack(d_ref, g, lo)
            pr, p_sink = _swa_probs(qs, kd[g], mask4, sink_ref, g)
            dpr = _dot_nt(dos, vd[g])
            delta = jnp.sum(pr * dpr, -1, keepdims=True)
            dsc = (pr * (dpr - delta) * SCALE).astype(MXU)
            sunk = p_sink * delta
            for h in range(4):
                row = 4 * g + h
                ds_ref[row:row + 1, :] += jnp.full((1, BLK), -1.0, F32) * jnp.sum(sunk[h * BLK:(h + 1) * BLK])
            pairs = _swa_unstack(_dot(dsc, kd[g]), lo)
            for k in range(2):
                p = 2 * g + k
                dq_ref[:, p * BLK:(p + 1) * BLK] = _unrope(pairs[k], *tabs).astype(ACT)
            dk_g.append(_dot_tn(dsc, qs))
            dv_g.append(_dot_tn(pr.astype(MXU), dos))
        fold = lambda a: a + pltpu.roll(a, HEAD, 1)
        dk = jnp.where(lo, fold(dk_g[0]), fold(dk_g[1]))
        dv = jnp.where(lo, fold(dv_g[0]), fold(dv_g[1]))
        dkv = jnp.concatenate([dk, dv], axis=1)
        prev = pl.multiple_of(jnp.maximum(n - 1, 0) * BLK, BLK)
        acc[0:BLK, :] += dkv[0:BLK]
        acc[pl.ds(prev, BLK), :] += dkv[BLK:2 * BLK]
        acc[pl.ds(row0, BLK), :] += dkv[2 * BLK:]

        @pl.when(n == nb - 1)
        def _():
            dkv_ref[:, :BLK] = _unrope(acc[:, :BLK], cos_ref[...], sa_ref[...], sb_ref[...]).astype(ACT)
            dkv_ref[:, BLK:] = acc[:, BLK:].astype(ACT)

        pl.when(n == nb - 1)(lambda: plan.end(comm))

    tab = pl.BlockSpec((t, BLK), lambda n: (0, 0))
    outs = pl.pallas_call(
        body, name="swa_bwd", grid=(nb,),
        in_specs=[pl.BlockSpec(memory_space=pltpu.SMEM),
                  pl.BlockSpec((BLK, 512), lambda n: (n, 2)),
                  pl.BlockSpec((t, 256), lambda n: (0, 10)),
                  pl.BlockSpec((BLK, 512), lambda n: (n, 0)), tab, tab, tab] + plan.specs,
        out_specs=[pl.BlockSpec((BLK, 512), lambda n: (n, 0)),
                   pl.BlockSpec((t, 256), lambda n: (0, 0)),
                   pl.BlockSpec((8, BLK), lambda n: (0, 0))] + plan.specs,
        out_shape=[jax.ShapeDtypeStruct((t, 512), ACT), jax.ShapeDtypeStruct((t, 256), ACT),
                   jax.ShapeDtypeStruct((8, BLK), F32)] + plan.out_shape,
        scratch_shapes=[pltpu.VMEM((t, 256), F32)] + plan.scratch,
        compiler_params=_cparams(("arbitrary",)),
    )(sinks, p0, p0, datt, *tables, *ride)
    return outs[0], outs[1], outs[2], outs[3:]


def _ab_in_bwd(dp0, w_t, h0, g_pre, dh1, ride):
    t = h0.shape[0]
    tm = _tile(t, 544)
    plan = _ChipsPlan(ride)
    nr = plan.n

    pieces = [(p[0], p[1] * dp0.tw, p[2] * dp0.tw) for p in dp0.pieces]

    def body(*refs):
        d_refs = refs[:5]
        w_ref, h_ref, g_ref, dh1_ref = refs[5:9]
        rest = refs[9:]
        dh0_ref, dg_ref = rest[nr:nr + 2]
        comm = (*rest[:nr], *rest[nr + 2:])
        i = pl.program_id(0)
        pl.when(i == 0)(lambda: plan.begin(comm))

        @pl.when(i == 0)
        def _():
            dg_ref[...] = jnp.zeros_like(dg_ref)

        dhn = None
        for ref, (_, first, rows) in zip(d_refs, pieces):
            term = _dot(ref[...].astype(MXU), w_ref[first:first + rows, :])
            dhn = term if dhn is None else dhn + term
        dx, dg = _norm_bwd(h_ref[...], g_ref[...], dhn, EPS)
        dg_ref[...] += dg
        dh0_ref[...] = dh1_ref[...] + dx
        pl.when(i == t // tm - 1)(lambda: plan.end(comm))

    row = pl.BlockSpec((tm, D), lambda i: (i, 0))
    vec = pl.BlockSpec((1, D), lambda i: (0, 0))
    outs = pl.pallas_call(
        body, name="ab_in_bwd", grid=(t // tm,),
        in_specs=[pl.BlockSpec((tm, rows), lambda i: (i, 0)) for _, _, rows in pieces] + [
            pl.BlockSpec(w_t.shape, lambda i: (0, 0)), row, vec, row] + plan.specs,
        out_specs=[row, vec] + plan.specs,
        out_shape=[jax.ShapeDtypeStruct((t, D), F32), jax.ShapeDtypeStruct((1, D), F32)] + plan.out_shape,
        scratch_shapes=plan.scratch,
        compiler_params=_cparams(("arbitrary",)),
    )(*dp0.arrays, w_t, h0, g_pre, dh1, *ride)
    return outs[0], outs[1], outs[2:]


def _dw_plain(a, b, name):
    t, m = a.shape
    n = b.shape[1]
    tm = _tile(t, 1088)
    tn = min(n, 512)
    nk = t // tm

    def body(a_ref, b_ref, o_ref, acc):
        k = pl.program_id(1)

        @pl.when(k == 0)
        def _():
            acc[...] = jnp.zeros_like(acc)

        acc[...] += _dot_tn(a_ref[...].astype(MXU), b_ref[...].astype(MXU))

        @pl.when(k == nk - 1)
        def _():
            o_ref[...] = acc[...].astype(WIRE)

    return pl.pallas_call(
        body, name=name, grid=(n // tn, nk),
        in_specs=[pl.BlockSpec((tm, m), lambda j, k: (k, 0)),
                  pl.BlockSpec((tm, tn), lambda j, k: (k, j))],
        out_specs=pl.BlockSpec((m, tn), lambda j, k: (0, j)),
        out_shape=jax.ShapeDtypeStruct((m, n), WIRE),
        scratch_shapes=[pltpu.VMEM((m, tn), F32)],
        compiler_params=_cparams(("arbitrary", "arbitrary")),
    )(a, b)


def _dw_chunks(hn, dp, name):
    t = hn.shape[0]
    tm = _tile(t, 1088)
    nk = t // tm
    nt, tw = dp.n_tiles, dp.tw
    n_in = len(dp.arrays)

    def body(*refs):
        d_refs = refs[:n_in]
        h_ref, o_ref, acc = refs[n_in:]
        j, k = pl.program_id(0), pl.program_id(1)

        @pl.when(k == 0)
        def _():
            acc[...] = jnp.zeros_like(acc)

        def add(ref):
            acc[...] += _dot_tn(h_ref[...].astype(MXU), ref[...].astype(MXU))
        dp.apply(j, d_refs, add)

        @pl.when(k == nk - 1)
        def _():
            o_ref[...] = acc[...].astype(WIRE)

    return pl.pallas_call(
        body, name=name, grid=(nt, nk),
        in_specs=dp.specs(tm, lambda j, k: k, lambda j, k: j) + [
            pl.BlockSpec((tm, D), lambda j, k: (k, 0))],
        out_specs=pl.BlockSpec((None, D, tw), lambda j, k: (j, 0, 0)),
        out_shape=jax.ShapeDtypeStruct((nt, D, tw), WIRE),
        scratch_shapes=[pltpu.VMEM((D, tw), F32)],
        compiler_params=_cparams(("arbitrary", "arbitrary")),
    )(*dp.arrays, hn)


def _dw_transposed(dp, hn, name):
    t = hn.shape[0]
    tm = _tile(t, 1088)
    nk = t // tm
    nt, tw = dp.n_tiles, dp.tw
    n_in = len(dp.arrays)

    def body(*refs):
        d_refs = refs[:n_in]
        h_ref, o_ref, acc = refs[n_in:]
        j, k = pl.program_id(0), pl.program_id(1)

        @pl.when(k == 0)
        def _():
            acc[...] = jnp.zeros_like(acc)

        def add(ref):
            acc[...] += _dot_tn(ref[...].astype(MXU), h_ref[...].astype(MXU))
        dp.apply(j, d_refs, add)

        @pl.when(k == nk - 1)
        def _():
            o_ref[...] = acc[...].astype(WIRE)

    return pl.pallas_call(
        body, name=name, grid=(nt, nk),
        in_specs=dp.specs(tm, lambda j, k: k, lambda j, k: j) + [
            pl.BlockSpec((tm, D), lambda j, k: (k, 0))],
        out_specs=pl.BlockSpec((tw, D), lambda j, k: (j, 0)),
        out_shape=jax.ShapeDtypeStruct((nt * tw, D), WIRE),
        scratch_shapes=[pltpu.VMEM((tw, D), F32)],
        compiler_params=_cparams(("arbitrary", "arbitrary")),
    )(*dp.arrays, hn)


def kernel(x, meta_tokens, ab_pre_norm, ab_w_in, ab_sinks, ab_conv_w, ab_conv_b, ab_conv_ln_g, ab_conv_ln_b, ab_w_pw2, ab_w_out, ab_post_norm, sb_pre_norm, sb_w_in, sb_w_out, sb_post_norm, loss_target, m_meta_tokens, m_ab_pre_norm, m_ab_w_in, m_ab_sinks, m_ab_conv_w, m_ab_conv_b, m_ab_conv_ln_g, m_ab_conv_ln_b, m_ab_w_pw2, m_ab_w_out, m_ab_post_norm, m_sb_pre_norm, m_sb_w_in, m_sb_w_out, m_sb_post_norm, v_meta_tokens, v_ab_pre_norm, v_ab_w_in, v_ab_sinks, v_ab_conv_w, v_ab_conv_b, v_ab_conv_ln_g, v_ab_conv_ln_b, v_ab_w_pw2, v_ab_w_out, v_ab_post_norm, v_sb_pre_norm, v_sb_w_in, v_sb_w_out, v_sb_post_norm):
    seq = x.shape[1]
    t = seq + BLK
    mx, my, mc = _coords()
    me = 4 * mx + 2 * my + mc
    pos = jnp.stack([mx, my, mc, me]).astype(jnp.int32)

    w_ab_t, *small = _all_gather(
        [ab_w_in[0].T.astype(WIRE), meta_tokens, ab_conv_w[0], sb_pre_norm, sb_post_norm], "gather_first")
    w_ab_t = w_ab_t.reshape(2816, D)
    meta_full = jnp.moveaxis(small[0], 0, 1).reshape(N_META, D)
    conv_w = jnp.moveaxis(small[1], 0, 1).reshape(CONV_W, 512)
    sb_pre = jnp.moveaxis(small[2], 0, 1).reshape(1, D)
    sb_post = jnp.moveaxis(small[3], 0, 1).reshape(1, D)

    h0 = jnp.concatenate([jnp.zeros((PAD, D), F32), meta_full, x[0]], axis=0)
    tgt = jnp.concatenate([jnp.zeros((BLK, D), F32), loss_target[0]], axis=0)
    tables = _rope_tables(t)
    sinks = ab_sinks[0]

    p0, hn0, (w_oa, w_pw, w_os) = _ab_in(
        h0, ab_pre_norm, w_ab_t, tables,
        [ab_w_out[0].astype(WIRE), ab_w_pw2[0].astype(WIRE), sb_w_out[0].astype(WIRE)])
    w_oa, w_os, w_pw = w_oa.reshape(D, D), w_os.reshape(D, D), w_pw.reshape(512, 512)
    att, (w_sb,) = _swa_fwd(p0, sinks, [sb_w_in[0].astype(WIRE)])
    c1 = _conv_fwd(p0, conv_w, ab_conv_b, ab_conv_ln_g, ab_conv_ln_b)
    h1, y0, mix = _ab_out(h0, p0, att, c1, w_pw, w_oa, ab_post_norm)
    p1, hn1 = _sb_in(h1, sb_pre, w_sb)
    o, ltot = _sb_fwd(p1)
    loss_part, dh2, dy1, m1, do, dgate, dg_sb_post = _sb_out(o, p1, w_os, h1, sb_post, tgt)

    dq1, dk1, dv1 = _sb_bwd(p1, ltot, do)
    dp1 = _Cols([(dq1, 0, 2), (dk1, 2, 2), (dv1, 4, 2), (dgate, 6, 2)], 512)

    def sibling_stage(parts, names, tag):
        got = _exchange_sibling(parts, "reduce_sibling_" + tag)
        return [_add_sibling(pos, p, r, "add_sibling_" + nm) for p, r, nm in zip(parts, got, names)]

    def finish(sums, got, names):
        return [_sum_chips(pos, s, r, "sum_chips_" + nm) for s, r, nm in zip(sums, got, names)]

    names1 = ["sb_in", "sb_out"]
    sums1 = sibling_stage([_dw_chunks(hn1, dp1, "dw_sb_in").reshape(4, 2, D, 512),
                           _dw_plain(m1, dy1, "dw_sb_out").reshape(4, 2, BLK, D)], names1, "sb")
    dh1, dg_sb_pre = _sb_in_bwd([dq1, dk1, dv1, dgate], w_sb, h1, sb_pre, dh2)
    dy0, dga, dgb, datt, dc1, dc2, dg_ab_post = _ab_out_bwd(dh1, y0, ab_post_norm, w_oa, p0, att, c1, w_pw)
    names2 = ["ab_out", "pw2"]
    sums2 = sibling_stage([_dw_plain(mix, dy0, "dw_ab_out").reshape(4, 2, BLK, D),
                           _dw_plain(c1, dc2, "dw_pw2").reshape(4, 2, 64, 512)], names2, "ab_out")
    dglu, dconv_w, dconv_b, dln_g, dln_b = _conv_bwd(p0, dc1, conv_w, ab_conv_b, ab_conv_ln_g, ab_conv_ln_b)
    dq0, dkv0, dsinks, got = _swa_bwd(p0, datt, sinks, tables, sums1 + sums2)
    g_sb_w_in, g_sb_w_out, g_ab_w_out, g_ab_w_pw2 = finish(sums1 + sums2, got, names1 + names2)
    dp0 = _Cols([(dq0, 0, 2), (dkv0, 2, 1), (dga, 3, 2), (dglu, 5, 4), (dgb, 9, 2)], 256)

    sums0 = sibling_stage([_dw_transposed(dp0, hn0, "dw_ab_in").reshape(4, 2, 352, D)], ["ab_in"], "ab_in")
    dh0, dg_ab_pre, got0 = _ab_in_bwd(dp0, w_ab_t, h0, ab_pre_norm, dh1, sums0)
    g_ab_w_in = finish(sums0, got0, ["ab_in"])[0].T

    small_parts = [dh0[PAD:BLK], dg_ab_pre, dsinks, dconv_w, dconv_b, dln_g, dln_b,
                   dg_ab_post, dg_sb_pre, dg_sb_post, loss_part]
    red = _reduce_small(_all_gather(small_parts, "gather_small_grads"), "reduce_small")
    col = lambda a, w: lax.dynamic_slice_in_dim(a, me * w, w, axis=1)
    g_meta = col(red[0], BLK)
    g_ab_pre = red[1]
    g_sinks = red[2][:, 0].reshape(1, 8)
    g_conv_w = col(red[3][:CONV_W], 64)
    g_conv_b, g_ln_g, g_ln_b, g_ab_post = red[4], red[5], red[6], red[7]
    g_sb_pre, g_sb_post = col(red[8], BLK), col(red[9], BLK)

    loss = red[10][0, 0]
    grad_x = dh0[BLK:][None]

    weights = [meta_tokens, ab_pre_norm, ab_w_in[0], ab_sinks, ab_conv_w[0], ab_conv_b, ab_conv_ln_g,
               ab_conv_ln_b, ab_w_pw2[0], ab_w_out[0], ab_post_norm, sb_pre_norm, sb_w_in[0],
               sb_w_out[0], sb_post_norm]
    grads = [g_meta, g_ab_pre, g_ab_w_in, g_sinks, g_conv_w, g_conv_b, g_ln_g, g_ln_b, g_ab_w_pw2,
             g_ab_w_out, g_ab_post, g_sb_pre, g_sb_w_in, g_sb_w_out, g_sb_post]
    ms = [m_meta_tokens, m_ab_pre_norm, m_ab_w_in[0], m_ab_sinks, m_ab_conv_w[0], m_ab_conv_b,
          m_ab_conv_ln_g, m_ab_conv_ln_b, m_ab_w_pw2[0], m_ab_w_out[0], m_ab_post_norm,
          m_sb_pre_norm, m_sb_w_in[0], m_sb_w_out[0], m_sb_post_norm]
    vs = [v_meta_tokens, v_ab_pre_norm, v_ab_w_in[0], v_ab_sinks, v_ab_conv_w[0], v_ab_conv_b,
          v_ab_conv_ln_g, v_ab_conv_ln_b, v_ab_w_pw2[0], v_ab_w_out[0], v_ab_post_norm,
          v_sb_pre_norm, v_sb_w_in[0], v_sb_w_out[0], v_sb_post_norm]
    lead = [w.ndim == 3 for w in (meta_tokens, ab_pre_norm, ab_w_in, ab_sinks, ab_conv_w, ab_conv_b,
                                   ab_conv_ln_g, ab_conv_ln_b, ab_w_pw2, ab_w_out, ab_post_norm,
                                   sb_pre_norm, sb_w_in, sb_w_out, sb_post_norm)]
    big_ids = [2, 8, 9, 12, 13]
    small_ids = [i for i in range(15) if i not in big_ids]
    deltas, new_m, new_v = [None] * 15, [None] * 15, [None] * 15
    for ids, nm in ((small_ids, "adamw_small"), (big_ids, "adamw_big")):
        d_, m_, v_ = _adamw([weights[i] for i in ids], [grads[i] for i in ids],
                            [ms[i] for i in ids], [vs[i] for i in ids], nm)
        for k, i in enumerate(ids):
            deltas[i], new_m[i], new_v[i] = d_[k], m_[k], v_[k]
    fix = lambda arrs: [a[None] if l else a for a, l in zip(arrs, lead)]
    return (loss, grad_x, *fix(grads), *fix(deltas), *fix(new_m), *fix(new_v))
```

```python
import functools

import numpy as np
import jax
import jax.numpy as jnp
from jax import lax
from jax.experimental import pallas as pl
from jax.experimental.pallas import tpu as pltpu

F32 = jnp.float32
MXU = jnp.bfloat16
ACT = jnp.bfloat16
WIRE = jnp.bfloat16

D = 1024
N_META = 16
BLK = 128
PAD = BLK - N_META
HEAD = 64
NEG = -1e30
EPS = 1e-6
LN_EPS = 1e-5
ROPE_THETA = 10000.0
SCALE = HEAD ** -0.5
CONV_W = 31
HALO = 32
LR, B1, B2, ADAM_EPS, WD, STEP = 0.001, 0.9, 0.999, 1e-08, 0.01, 10
VMEM_LIMIT = 56 * 1024 * 1024
MESH = pl.DeviceIdType.MESH

P0_SRC = (5, 6, 7, 8, 0, 1, 3, 4, 9, 10, 2)


def _cparams(sem=None):
    return pltpu.CompilerParams(dimension_semantics=sem, vmem_limit_bytes=VMEM_LIMIT)


def _tile(t, pref):
    for cand in (1088, 544, 272, 128):
        if cand <= pref and t % cand == 0:
            return cand
    raise ValueError(t)


def _sigmoid(x):
    return 1.0 / (1.0 + jnp.exp(-x))


def _silu_and_grad(x):
    s = _sigmoid(x)
    return x * s, s * (1.0 + x * (1.0 - s))


def _dot(a, b):
    return jnp.dot(a, b, preferred_element_type=F32)


def _dot_nt(a, b):
    return lax.dot_general(a, b, (((1,), (1,)), ((), ())), preferred_element_type=F32)


def _dot_tn(a, b):
    return lax.dot_general(a, b, (((0,), (0,)), ((), ())), preferred_element_type=F32)


def _rows(shape, base):
    return base + lax.broadcasted_iota(jnp.int32, shape, 0)


def _rope_tables(t):
    half = HEAD // 2
    inv = ROPE_THETA ** (-np.arange(half, dtype=np.float32) / half)
    pos = (np.arange(t) - PAD).astype(np.float32)
    ang = pos[:, None] * inv[None, :]
    lane = np.arange(BLK)
    cos = np.cos(ang)[:, lane % half].astype(np.float32)
    sin = np.sin(ang)[:, lane % half].astype(np.float32)
    first = (lane % HEAD) < half
    sin_a = np.where(first[None, :], -sin, 0.0).astype(np.float32)
    sin_b = np.where(first[None, :], 0.0, sin).astype(np.float32)
    return jnp.asarray(cos), jnp.asarray(sin_a), jnp.asarray(sin_b)


def _rope(v, cos, sin_a, sin_b):
    return v * cos + pltpu.roll(v, 96, 1) * sin_a + pltpu.roll(v, 32, 1) * sin_b


def _unrope(v, cos, sin_a, sin_b):
    return v * cos - pltpu.roll(v, 96, 1) * sin_a - pltpu.roll(v, 32, 1) * sin_b


def _coords():
    return lax.axis_index("x"), lax.axis_index("y"), lax.axis_index("c")


def _all_gather(arrs, name):
    plan = _GatherPlan(arrs)

    def body(*refs):
        plan.begin(refs)
        plan.end(refs)

    return pl.pallas_call(
        body, name=name, out_shape=plan.out_shape,
        in_specs=plan.specs, out_specs=plan.specs, scratch_shapes=plan.scratch,
    )(*arrs)


class _GatherPlan:
    def __init__(self, arrs):
        n = self.n = len(arrs)
        self.out_shape = [jax.ShapeDtypeStruct((8,) + a.shape, a.dtype) for a in arrs]
        self.specs = [pl.BlockSpec(memory_space=pl.ANY)] * n
        self.scratch = [pltpu.SemaphoreType.DMA((n, 7)), pltpu.SemaphoreType.DMA((n, 7)),
                        pltpu.SemaphoreType.DMA((n,))]

    def _copies(self, refs):
        n = self.n
        ins, outs = refs[:n], refs[n:2 * n]
        send_sems, recv_sems, local_sems = refs[2 * n:]
        x, y, c = _coords()
        me, sibling = (x, y, c), (x, y, 1 - c)
        chips = [(1 - x, y), (x, 1 - y), (1 - x, 1 - y)]

        def copy(a, k, block, to, src=None):
            dst = outs[a].at[4 * block[0] + 2 * block[1] + block[2]]
            return pltpu.make_async_remote_copy(
                src_ref=dst if src is None else src, dst_ref=dst,
                send_sem=send_sems.at[a, k], recv_sem=recv_sems.at[a, k],
                device_id=to, device_id_type=MESH)

        mine = [pltpu.make_async_copy(ins[a], outs[a].at[4 * x + 2 * y + c], local_sems.at[a])
                for a in range(n)]
        first = []
        for a in range(n):
            first.append(copy(a, 0, me, sibling, src=ins[a]))
            for j, chip in enumerate(chips):
                first.append(copy(a, 1 + j, me, (*chip, c), src=ins[a]))
        return copy, mine, first, (me, sibling, chips, c)

    def begin(self, refs):
        _, mine, first, _ = self._copies(refs)
        for cp in mine + first:
            cp.start()

    def end(self, refs):
        copy, mine, first, (me, sibling, chips, c) = self._copies(refs)
        passed = []
        for j, chip in enumerate(chips):
            for a in range(self.n):
                copy(a, 1 + j, (*chip, c), me).wait_recv()
                cp = copy(a, 4 + j, (*chip, c), sibling)
                cp.start()
                passed.append(cp)
        for a in range(self.n):
            copy(a, 0, sibling, me).wait_recv()
            for j, chip in enumerate(chips):
                copy(a, 4 + j, (*chip, 1 - c), me).wait_recv()
        for cp in first + passed:
            cp.wait_send()
        for cp in mine:
            cp.wait()


class _ChipsPlan:
    def __init__(self, sums):
        n = self.n = len(sums)
        self.out_shape = [jax.ShapeDtypeStruct((3,) + s.shape[1:], s.dtype) for s in sums]
        self.specs = [pl.BlockSpec(memory_space=pl.ANY)] * n
        self.scratch = [pltpu.SemaphoreType.DMA((n, 3)), pltpu.SemaphoreType.DMA((n, 3))]

    def _copies(self, refs):
        n = self.n
        ins, outs = refs[:n], refs[n:2 * n]
        send_sems, recv_sems = refs[2 * n:]
        x, y, c = _coords()
        chips = [(1 - x, y), (x, 1 - y), (1 - x, 1 - y)]
        return [pltpu.make_async_remote_copy(
            src_ref=ins[a].at[2 * chip[0] + chip[1]], dst_ref=outs[a].at[k],
            send_sem=send_sems.at[a, k], recv_sem=recv_sems.at[a, k],
            device_id=(*chip, c), device_id_type=MESH)
            for a in range(n) for k, chip in enumerate(chips)]

    def begin(self, refs):
        for cp in self._copies(refs):
            cp.start()

    def end(self, refs):
        for cp in self._copies(refs):
            cp.wait()


def _exchange_sibling(parts, name):
    n = len(parts)

    def body(*refs):
        ins, outs = refs[:n], refs[n:2 * n]
        send_sems, recv_sems = refs[2 * n:]
        x, y, c = _coords()
        copies = [pltpu.make_async_remote_copy(
            src_ref=ins[a].at[:, 1 - c], dst_ref=outs[a],
            send_sem=send_sems.at[a], recv_sem=recv_sems.at[a],
            device_id=(x, y, 1 - c), device_id_type=MESH) for a in range(n)]
        for cp in copies:
            cp.start()
        for cp in copies:
            cp.wait()

    any_spec = pl.BlockSpec(memory_space=pl.ANY)
    return pl.pallas_call(
        body, name=name,
        out_shape=[jax.ShapeDtypeStruct((4,) + p.shape[2:], p.dtype) for p in parts],
        in_specs=[any_spec] * n, out_specs=[any_spec] * n,
        scratch_shapes=[pltpu.SemaphoreType.DMA((n,)), pltpu.SemaphoreType.DMA((n,))],
    )(*parts)


def _exchange_chips(sums, name):
    plan = _ChipsPlan(sums)

    def body(*refs):
        plan.begin(refs)
        plan.end(refs)

    return pl.pallas_call(
        body, name=name, out_shape=plan.out_shape,
        in_specs=plan.specs, out_specs=plan.specs, scratch_shapes=plan.scratch,
    )(*sums)


def _add_sibling(pos, part, recv, name):
    _, _, r, c = part.shape

    def body(pos_ref, p_ref, r_ref, o_ref):
        o_ref[...] = (p_ref[...].astype(F32) + r_ref[...].astype(F32)).astype(o_ref.dtype)

    return pl.pallas_call(
        body, name=name,
        grid_spec=pltpu.PrefetchScalarGridSpec(
            num_scalar_prefetch=1, grid=(4,),
            in_specs=[pl.BlockSpec((None, None, r, c), lambda q, pos: (q, pos[2], 0, 0)),
                      pl.BlockSpec((None, r, c), lambda q, pos: (q, 0, 0))],
            out_specs=pl.BlockSpec((None, r, c), lambda q, pos: (q, 0, 0))),
        out_shape=jax.ShapeDtypeStruct((4, r, c), part.dtype),
        compiler_params=_cparams(("arbitrary",)),
    )(pos, part, recv)


def _sum_chips(pos, sums, recv, name):
    _, r, c = sums.shape

    def body(pos_ref, s_ref, r_ref, o_ref):
        g = s_ref[...].astype(F32)
        for k in range(3):
            g = g + r_ref[k].astype(F32)
        o_ref[...] = g

    return pl.pallas_call(
        body, name=name,
        grid_spec=pltpu.PrefetchScalarGridSpec(
            num_scalar_prefetch=1, grid=(1,),
            in_specs=[pl.BlockSpec((None, r, c), lambda i, pos: (2 * pos[0] + pos[1], 0, 0)),
                      pl.BlockSpec((3, r, c), lambda i, pos: (0, 0, 0))],
            out_specs=pl.BlockSpec((r, c), lambda i, pos: (0, 0))),
        out_shape=jax.ShapeDtypeStruct((r, c), F32),
        compiler_params=_cparams(("arbitrary",)),
    )(pos, sums, recv)


def _adamw(ws, gs, ms, vs, name):
    n = len(ws)
    c1 = 1.0 / (1.0 - B1 ** STEP)
    c2 = 1.0 / (1.0 - B2 ** STEP)

    def body(*refs):
        w_r, g_r, m_r, v_r = refs[:n], refs[n:2 * n], refs[2 * n:3 * n], refs[3 * n:4 * n]
        d_o, m_o, v_o = refs[4 * n:5 * n], refs[5 * n:6 * n], refs[6 * n:7 * n]
        for a in range(n):
            g = g_r[a][...]
            m = B1 * m_r[a][...] + (1.0 - B1) * g
            v = B2 * v_r[a][...] + (1.0 - B2) * (g * g)
            d_o[a][...] = -LR * ((m * c1) / (jnp.sqrt(v * c2) + ADAM_EPS) + WD * w_r[a][...])
            m_o[a][...] = m
            v_o[a][...] = v

    shapes = [jax.ShapeDtypeStruct(w.shape, F32) for w in ws]
    outs = pl.pallas_call(body, name=name, out_shape=shapes * 3,
                          compiler_params=_cparams())(*ws, *gs, *ms, *vs)
    return outs[:n], outs[n:2 * n], outs[2 * n:]


def _reduce_small(gathered, name):
    n = len(gathered)

    def body(*refs):
        for a in range(n):
            acc = refs[a][0]
            for k in range(1, 8):
                acc = acc + refs[a][k]
            refs[n + a][...] = acc

    return pl.pallas_call(
        body, name=name,
        out_shape=[jax.ShapeDtypeStruct(g.shape[1:], F32) for g in gathered],
        compiler_params=_cparams())(*gathered)


class _Cols:
    def __init__(self, pieces, tw):
        self.pieces, self.tw = pieces, tw
        self.arrays = [p[0] for p in pieces]
        self.n_tiles = sum(p[2] for p in pieces)

    def specs(self, tm, row_of, tile_of):
        out = []
        for _, first, cnt in self.pieces:
            def imap(*g, first=first, cnt=cnt):
                return (row_of(*g), jnp.clip(tile_of(*g) - first, 0, cnt - 1))
            out.append(pl.BlockSpec((tm, self.tw), imap))
        return out

    def apply(self, t, refs, fn):
        for ref, (_, first, cnt) in zip(refs, self.pieces):
            pl.when((t >= first) & (t < first + cnt))(functools.partial(fn, ref))


def _ab_in(h, g, w_t, tables, ride):
    t = h.shape[0]
    tm = _tile(t, 1088)
    src = jnp.asarray(np.array(P0_SRC, np.int32))
    plan = _GatherPlan(ride)
    nr = plan.n

    def body(src_ref, h_ref, g_ref, w_ref, cos_ref, sa_ref, sb_ref, *rest):
        o_ref, hn_ref = rest[nr:nr + 2]
        hn_s = rest[2 * nr + 2]
        comm = (*rest[:nr], *rest[nr + 2:2 * nr + 2], *rest[2 * nr + 3:])
        i, j = pl.program_id(0), pl.program_id(1)
        pl.when((i == 0) & (j == 0))(lambda: plan.begin(comm))

        @pl.when(j == 0)
        def _():
            x = h_ref[...]
            hn = (x * lax.rsqrt(jnp.mean(x * x, -1, keepdims=True) + EPS) * g_ref[...]).astype(MXU)
            hn_s[...] = hn
            hn_ref[...] = hn.astype(ACT)

        acc = _dot_nt(hn_s[...], w_ref[...])
        rope = lambda v: _rope(v, cos_ref[...], sa_ref[...], sb_ref[...])

        @pl.when((j == 4) | (j == 5))
        def _():
            o_ref[:, :BLK] = rope(acc[:, :BLK])
            o_ref[:, BLK:] = rope(acc[:, BLK:])

        @pl.when(j == 10)
        def _():
            o_ref[:, :BLK] = rope(acc[:, :BLK])
            o_ref[:, BLK:] = acc[:, BLK:]

        @pl.when((j < 4) | ((j > 5) & (j < 10)))
        def _():
            o_ref[...] = acc

        pl.when((i == t // tm - 1) & (j == 10))(lambda: plan.end(comm))

    tab = pl.BlockSpec((tm, BLK), lambda i, j, s: (i, 0))
    outs = pl.pallas_call(
        body, name="ab_in",
        grid_spec=pltpu.PrefetchScalarGridSpec(
            num_scalar_prefetch=1, grid=(t // tm, 11),
            in_specs=[pl.BlockSpec((tm, D), lambda i, j, s: (i, 0)),
                      pl.BlockSpec((1, D), lambda i, j, s: (0, 0)),
                      pl.BlockSpec((256, D), lambda i, j, s: (s[j], 0)),
                      tab, tab, tab] + plan.specs,
            out_specs=[pl.BlockSpec((tm, 256), lambda i, j, s: (i, j)),
                       pl.BlockSpec((tm, D), lambda i, j, s: (i, 0))] + plan.specs,
            scratch_shapes=[pltpu.VMEM((tm, D), MXU)] + plan.scratch),
        out_shape=[jax.ShapeDtypeStruct((t, 2816), F32), jax.ShapeDtypeStruct((t, D), ACT)] + plan.out_shape,
        compiler_params=_cparams(("arbitrary", "arbitrary")),
    )(src, h, g, w_t, *tables, *ride)
    return outs[0], outs[1], outs[2:]


def _swa_mask(n):
    r = lax.broadcasted_iota(jnp.int32, (BLK, 3 * BLK), 0)
    c = lax.broadcasted_iota(jnp.int32, (BLK, 3 * BLK), 1)
    qpos = n * BLK + r
    bpos = (n - 2) * BLK + c
    meta_ok = (c >= PAD) & (c < BLK) & (qpos - c >= BLK)
    band_ok = (c >= BLK) & (bpos >= PAD) & (qpos >= bpos) & (qpos - bpos < BLK)
    return meta_ok | band_ok


def _swa_keys(kv_ref, n):
    def blk(b):
        return kv_ref[pl.ds(pl.multiple_of(b * BLK, BLK), BLK), :]
    kv = jnp.concatenate([kv_ref[0:BLK, :], blk(jnp.maximum(n - 1, 0)), blk(n)], axis=0)
    lo = lax.broadcasted_iota(jnp.int32, (1, BLK), 1) < HEAD
    out = []
    for part in (kv[:, :BLK], kv[:, BLK:]):
        rolled = pltpu.roll(part, HEAD, 1)
        out.append((jnp.where(lo, part, rolled).astype(MXU), jnp.where(lo, rolled, part).astype(MXU)))
    return out[0], out[1], lo


def _swa_stack(ref, g, lo):
    parts = []
    for p in (2 * g, 2 * g + 1):
        x = ref[:, p * BLK:(p + 1) * BLK]
        parts += [jnp.where(lo, x, 0.0), jnp.where(lo, 0.0, x)]
    return jnp.concatenate(parts, axis=0).astype(MXU)


def _swa_probs(qs, kd, mask4, sink_ref, g):
    sink = jnp.concatenate([jnp.full((BLK, 1), sink_ref[4 * g + h], F32) for h in range(4)], axis=0)
    s = jnp.where(mask4, _dot_nt(qs, kd) * SCALE, NEG)
    m = jnp.maximum(jnp.max(s, -1, keepdims=True), sink)
    e = jnp.exp(s - m)
    e_sink = jnp.exp(sink - m)
    inv = 1.0 / (jnp.sum(e, -1, keepdims=True) + e_sink)
    return e * inv, e_sink * inv


def _swa_unstack(x, lo):
    return [jnp.where(lo, x[0:BLK], x[BLK:2 * BLK]), jnp.where(lo, x[2 * BLK:3 * BLK], x[3 * BLK:])]


def _swa_fwd(p0, sinks, ride):
    t = p0.shape[0]
    plan = _GatherPlan(ride)
    nr = plan.n

    def body(sink_ref, q_ref, kv_ref, *rest):
        o_ref = rest[nr]
        comm = (*rest[:nr], *rest[nr + 1:])
        n = pl.program_id(0)
        pl.when(n == 0)(lambda: plan.begin(comm))
        kd, vd, lo = _swa_keys(kv_ref, n)
        mask4 = jnp.concatenate([_swa_mask(n)] * 4, axis=0)
        for g in range(2):
            pr, _ = _swa_probs(_swa_stack(q_ref, g, lo), kd[g], mask4, sink_ref, g)
            pairs = _swa_unstack(_dot(pr.astype(MXU), vd[g]), lo)
            for k in range(2):
                p = 2 * g + k
                o_ref[:, p * BLK:(p + 1) * BLK] = pairs[k]
        pl.when(n == t // BLK - 1)(lambda: plan.end(comm))

    outs = pl.pallas_call(
        body, name="swa_fwd", grid=(t // BLK,),
        in_specs=[pl.BlockSpec(memory_space=pltpu.SMEM),
                  pl.BlockSpec((BLK, 512), lambda n: (n, 2)),
                  pl.BlockSpec((t, 256), lambda n: (0, 10))] + plan.specs,
        out_specs=[pl.BlockSpec((BLK, 512), lambda n: (n, 0))] + plan.specs,
        out_shape=[jax.ShapeDtypeStruct((t, 512), F32)] + plan.out_shape,
        scratch_shapes=plan.scratch,
        compiler_params=_cparams(("arbitrary",)),
    )(sinks, p0, p0, *ride)
    return outs[0], outs[1:]


def _conv_window(u_w, w_ref, n_out, first):
    rows = u_w.shape[0]
    acc = None
    for j in range(CONV_W):
        shifted = pltpu.roll(u_w, (rows - (first + j)) % rows, 0)[:n_out]
        term = shifted * w_ref[j:j + 1, :]
        acc = term if acc is None else acc + term
    return acc


def _conv_fwd(p0, conv_w, conv_b, ln_g, ln_b):
    t = p0.shape[0]
    tm = _tile(t, 544)
    hb = tm // HALO

    def body(cur_ref, prev_ref, w_ref, b_ref, g_ref, bb_ref, o_ref):
        i = pl.program_id(0)
        glu = jnp.concatenate([prev_ref[...], cur_ref[...]], axis=0)
        rw = _rows((tm + HALO, 1), i * tm - HALO)
        u_w = jnp.where(rw >= PAD, glu[:, :512] * _sigmoid(glu[:, 512:]), 0.0)
        cv = _conv_window(u_w, w_ref, tm, HALO - (CONV_W - 1)) + b_ref[...]
        xc = cv - jnp.mean(cv, -1, keepdims=True)
        ln = xc * lax.rsqrt(jnp.mean(xc * xc, -1, keepdims=True) + LN_EPS) * g_ref[...] + bb_ref[...]
        o_ref[...] = (ln * _sigmoid(ln)).astype(ACT)

    vec = pl.BlockSpec((1, 512), lambda i: (0, 0))
    return pl.pallas_call(
        body, name="conv_fwd", grid=(t // tm,),
        in_specs=[pl.BlockSpec((tm, D), lambda i: (i, 0)),
                  pl.BlockSpec((HALO, D), lambda i: (jnp.maximum(i * hb - 1, 0), 0)),
                  pl.BlockSpec((CONV_W, 512), lambda i: (0, 0)), vec, vec, vec],
        out_specs=pl.BlockSpec((tm, 512), lambda i: (i, 0)),
        out_shape=jax.ShapeDtypeStruct((t, 512), ACT),
        compiler_params=_cparams(("arbitrary",)),
    )(p0, p0, conv_w, conv_b, ln_g, ln_b)


def _ab_out(h, p0, att, c1, w_pw2, w_out, g_post):
    t = h.shape[0]
    tm = _tile(t, 544)

    def body(h_ref, ga_ref, gb_ref, att_ref, c1_ref, pw_ref, wo_ref, g_ref, h1_ref, y_ref, mix_ref):
        i = pl.program_id(0)
        sga, _ = _silu_and_grad(ga_ref[...])
        sgb, _ = _silu_and_grad(gb_ref[...])
        a = att_ref[...] * sga
        c = _dot(c1_ref[...].astype(MXU), pw_ref[...]) * sgb
        mix = jnp.concatenate([a, c], axis=1).astype(MXU)
        y = _dot(mix, wo_ref[...])
        yn = y * lax.rsqrt(jnp.mean(y * y, -1, keepdims=True) + EPS) * g_ref[...]
        h1_ref[...] = jnp.where(_rows((tm, 1), i * tm) >= PAD, h_ref[...] + yn, 0.0)
        y_ref[...] = y
        mix_ref[...] = mix.astype(ACT)

    row = lambda w, idx: pl.BlockSpec((tm, w), lambda i: (i, idx))
    full = lambda a: pl.BlockSpec(a.shape, lambda i: (0, 0))
    return pl.pallas_call(
        body, name="ab_out", grid=(t // tm,),
        in_specs=[row(D, 0), row(512, 3), row(512, 4), row(512, 0), row(512, 0),
                  full(w_pw2), full(w_out), full(g_post)],
        out_specs=[row(D, 0), row(D, 0), row(D, 0)],
        out_shape=[jax.ShapeDtypeStruct((t, D), F32), jax.ShapeDtypeStruct((t, D), F32),
                   jax.ShapeDtypeStruct((t, D), ACT)],
        compiler_params=_cparams(("arbitrary",)),
    )(h, p0, p0, att, c1, w_pw2, w_out, g_post)


def _sb_in(h, g, w):
    t = h.shape[0]
    tm = _tile(t, 1088)

    def body(h_ref, g_ref, w_ref, o_ref, hn_ref, hn_s):
        @pl.when(pl.program_id(1) == 0)
        def _():
            x = h_ref[...]
            hn = (x * lax.rsqrt(jnp.mean(x * x, -1, keepdims=True) + EPS) * g_ref[...]).astype(MXU)
            hn_s[...] = hn
            hn_ref[...] = hn.astype(ACT)

        o_ref[...] = _dot(hn_s[...], w_ref[...])

    return pl.pallas_call(
        body, name="sb_in", grid=(t // tm, 8),
        in_specs=[pl.BlockSpec((tm, D), lambda i, j: (i, 0)),
                  pl.BlockSpec((1, D), lambda i, j: (0, 0)),
                  pl.BlockSpec((None, D, 512), lambda i, j: (j, 0, 0))],
        out_specs=[pl.BlockSpec((tm, 512), lambda i, j: (i, j)),
                   pl.BlockSpec((tm, D), lambda i, j: (i, 0))],
        out_shape=[jax.ShapeDtypeStruct((t, 4096), F32), jax.ShapeDtypeStruct((t, D), ACT)],
        scratch_shapes=[pltpu.VMEM((tm, D), MXU)],
        compiler_params=_cparams(("arbitrary", "arbitrary")),
    )(h, g, w)


def _split_hi_lo(x):
    hi = x.astype(MXU)
    lo = (x - hi.astype(F32)).astype(MXU)
    return hi, lo


def _scan_matrix(suffix):
    j = lax.broadcasted_iota(jnp.int32, (2 * BLK, 2 * BLK), 0) % BLK
    s = lax.broadcasted_iota(jnp.int32, (2 * BLK, 2 * BLK), 1)
    keep = (s >= BLK) | ((j > s) if suffix else (j < s))
    return jnp.where(keep, 1.0, 0.0).astype(MXU)


def _scan_packed(hi_lo, b, mat):
    cols = slice(b * BLK, (b + 1) * BLK)
    both = _dot(jnp.concatenate([hi_lo[:BLK, cols], hi_lo[BLK:, cols]], axis=1), mat)
    return both[:, :BLK], both[:, BLK:]


KC = 4
GROUPS = 2
GROUPS_FWD = 4


def _sb_logits(qm, kc, valid):
    z = _dot_nt(qm, kc)
    log_beta = jnp.minimum(z, 0.0) - jnp.log(1.0 + jnp.exp(-jnp.abs(z)))
    return log_beta, jnp.where(valid, log_beta - z, 0.0)


def _sb_valid(i, first_key, chunk):
    r = lax.broadcasted_iota(jnp.int32, (BLK, chunk), 0)
    c = lax.broadcasted_iota(jnp.int32, (BLK, chunk), 1)
    kpos = first_key + c
    return (kpos >= PAD) & (kpos < i * BLK + r)


def _sb_walk(i, step_of, init):
    n_full = (i + 1) // KC
    full = step_of(KC)
    carry = lax.fori_loop(0, n_full, lambda s, c: full(s * KC, c), init)
    rest = [lambda c: c] + [functools.partial(step_of(k), n_full * KC) for k in range(1, KC)]
    return lax.switch(i + 1 - n_full * KC, rest, carry)


def _sb_fwd(p1):
    t = p1.shape[0]
    groups = GROUPS_FWD
    w = groups * BLK

    def body(q_ref, k_ref, v_ref, o_ref, lt_ref, k_s, v_s):
        i = pl.program_id(1)

        @pl.when(i == 0)
        def _():
            k_s[...] = k_ref[...].astype(MXU)
            v_s[...] = v_ref[...].astype(MXU)

        lo = lax.broadcasted_iota(jnp.int32, (1, BLK), 1) < HEAD
        qm = []
        for g in range(groups):
            q = q_ref[:, g * BLK:(g + 1) * BLK] * SCALE
            qm += [jnp.where(lo, q, 0.0).astype(MXU), jnp.where(lo, 0.0, q).astype(MXU)]
        mat = _scan_matrix(True)
        heads = range(2 * groups)

        def step_of(kc):
            chunk = kc * BLK

            def step(done, carry):
                start = pl.multiple_of((i + 1 - done - kc) * BLK, BLK)
                valid = _sb_valid(i, start, chunk)
                new, staged = [], []
                for h in heads:
                    lanes = slice((h // 2) * BLK, (h // 2 + 1) * BLK)
                    log_beta, log_1m = _sb_logits(qm[h], k_s[pl.ds(start, chunk), lanes], valid)
                    staged.append((log_beta, jnp.concatenate(_split_hi_lo(log_1m), axis=0)))
                probs = []
                for h in heads:
                    log_beta, hi_lo = staged[h]
                    run = carry[2 * h]
                    parts = [None] * kc
                    for b in reversed(range(kc)):
                        after, total = _scan_packed(hi_lo, b, mat)
                        parts[b] = after + run
                        run = run + total
                    a = jnp.where(valid, jnp.exp(log_beta + jnp.concatenate(parts, axis=1)), 0.0)
                    probs.append((run, a.astype(MXU)))
                for h in heads:
                    lanes = slice((h // 2) * BLK, (h // 2 + 1) * BLK)
                    run, a = probs[h]
                    new += [run, carry[2 * h + 1] + _dot(a, v_s[pl.ds(start, chunk), lanes])]
                return tuple(new)
            return step

        zero = jnp.zeros((BLK, BLK), F32)
        res = _sb_walk(i, step_of, (zero,) * (4 * groups))
        for g in range(groups):
            lanes = slice(g * BLK, (g + 1) * BLK)
            o_ref[:, lanes] = jnp.where(lo, res[4 * g + 1], res[4 * g + 3])
            lt_ref[:, lanes] = jnp.where(lo, res[4 * g], res[4 * g + 2])

    ng = D // w
    blk = pl.BlockSpec((BLK, w), lambda hp, i: (i, hp))
    return pl.pallas_call(
        body, name="sb_fwd", grid=(ng, t // BLK),
        in_specs=[blk,
                  pl.BlockSpec((t, w), lambda hp, i: (0, ng + hp)),
                  pl.BlockSpec((t, w), lambda hp, i: (0, 2 * ng + hp))],
        out_specs=[blk, blk],
        out_shape=[jax.ShapeDtypeStruct((t, D), F32)] * 2,
        scratch_shapes=[pltpu.VMEM((t, w), MXU), pltpu.VMEM((t, w), MXU)],
        compiler_params=_cparams(("arbitrary", "arbitrary")),
    )(p1, p1, p1)


def _sb_out(o, p1, w_out, h1, g_post, tgt):
    t = o.shape[0]
    tm = _tile(t, 544)

    def body(o_ref, g_ref, w_ref, h_ref, gp_ref, t_ref,
             loss_ref, dh_ref, dy_ref, m_ref, do_ref, dg_ref, dgp_ref):
        i = pl.program_id(0)

        @pl.when(i == 0)
        def _():
            loss_ref[...] = jnp.zeros_like(loss_ref)
            dgp_ref[...] = jnp.zeros_like(dgp_ref)

        gate = g_ref[...]
        sg, dsg = _silu_and_grad(gate)
        ov = o_ref[...]
        m = (ov * sg).astype(MXU)
        y = _dot(m, w_ref[...])
        r = lax.rsqrt(jnp.mean(y * y, -1, keepdims=True) + EPS)
        yhat = y * r
        h2 = h_ref[...] + yhat * gp_ref[...]
        diff = jnp.where(_rows((tm, 1), i * tm) >= BLK, h2 - t_ref[...], 0.0)
        loss_ref[...] += jnp.full(loss_ref.shape, 0.5 / D, F32) * jnp.sum(diff * diff)
        dh = diff * (1.0 / D)
        dgp_ref[...] += jnp.sum(dh * yhat, 0, keepdims=True)
        dyn = dh * gp_ref[...]
        dy = (r * (dyn - yhat * jnp.mean(dyn * yhat, -1, keepdims=True))).astype(MXU)
        dm = _dot_nt(dy, w_ref[...])
        dh_ref[...] = dh
        dy_ref[...] = dy.astype(ACT)
        m_ref[...] = m.astype(ACT)
        do_ref[...] = dm * sg
        dg_ref[...] = (dm * ov * dsg).astype(ACT)

    row = lambda idx: pl.BlockSpec((tm, D), lambda i: (i, idx))
    full = lambda a: pl.BlockSpec(a.shape, lambda i: (0, 0))
    acc = lambda s: pl.BlockSpec(s, lambda i: (0, 0))
    return pl.pallas_call(
        body, name="sb_out", grid=(t // tm,),
        in_specs=[row(0), row(3), full(w_out), row(0), full(g_post), row(0)],
        out_specs=[acc((8, BLK)), row(0), row(0), row(0), row(0), row(0), acc((1, D))],
        out_shape=[jax.ShapeDtypeStruct((8, BLK), F32), jax.ShapeDtypeStruct((t, D), F32),
                   jax.ShapeDtypeStruct((t, D), ACT), jax.ShapeDtypeStruct((t, D), ACT),
                   jax.ShapeDtypeStruct((t, D), F32), jax.ShapeDtypeStruct((t, D), ACT),
                   jax.ShapeDtypeStruct((1, D), F32)],
        compiler_params=_cparams(("arbitrary",)),
    )(o, p1, w_out, h1, g_post, tgt)


def _sb_bwd(p1, ltot, do):
    t = p1.shape[0]
    nb = t // BLK

    w = GROUPS * BLK

    def body(q_ref, k_ref, v_ref, lt_ref, do_ref, dq_ref, dk_ref, dv_ref, k_s, v_s, dk_s, dv_s):
        i = pl.program_id(1)
        lo = lax.broadcasted_iota(jnp.int32, (1, BLK), 1) < HEAD

        @pl.when(i == 0)
        def _():
            k_s[...] = k_ref[...].astype(MXU)
            v_s[...] = v_ref[...].astype(MXU)
            dk_s[...] = jnp.zeros_like(dk_s)
            dv_s[...] = jnp.zeros_like(dv_s)

        qm, dom, row_total, q2, do2 = [], [], [], [], []
        for g in range(GROUPS):
            lanes = slice(g * BLK, (g + 1) * BLK)
            q, dout, lt = q_ref[:, lanes] * SCALE, do_ref[:, lanes], lt_ref[:, lanes]
            qm += [jnp.where(lo, q, 0.0).astype(MXU), jnp.where(lo, 0.0, q).astype(MXU)]
            dom += [jnp.where(lo, dout, 0.0).astype(MXU), jnp.where(lo, 0.0, dout).astype(MXU)]
            q2.append(jnp.concatenate(qm[-2:], axis=0))
            do2.append(jnp.concatenate(dom[-2:], axis=0))
            lt_r = pltpu.roll(lt, HEAD, 1)
            row_total += [jnp.where(lo, lt, lt_r), jnp.where(lo, lt_r, lt)]
        mat_l = _scan_matrix(True)
        mat_g = _scan_matrix(False)
        heads = range(2 * GROUPS)

        def step_of(kc):
            chunk = kc * BLK

            def step(done, carry):
                start = pl.multiple_of(done * BLK, BLK)
                keys = lambda ref, h: ref[pl.ds(start, chunk), (h // 2) * BLK:(h // 2 + 1) * BLK]
                valid = _sb_valid(i, start, chunk)
                new, dzs, probs, st1, st2, st3 = [], [], [], [], [], []
                for h in heads:
                    log_beta, log_1m = _sb_logits(qm[h], keys(k_s, h), valid)
                    st1.append((log_beta, jnp.concatenate(_split_hi_lo(log_1m), axis=0),
                                _dot_nt(dom[h], keys(v_s, h))))
                for h in heads:
                    log_beta, hi_lo, da = st1[h]
                    run = carry[3 * h]
                    parts = []
                    for b in range(kc):
                        after, total = _scan_packed(hi_lo, b, mat_l)
                        run = run + total
                        parts.append(after + (row_total[h] - run))
                    a = jnp.where(valid, jnp.exp(log_beta + jnp.concatenate(parts, axis=1)), 0.0)
                    g = da * a
                    probs.append(a.astype(MXU))
                    st2.append((run, g, jnp.concatenate(_split_hi_lo(g), axis=0)))
                for h in heads:
                    run, g, hi_lo = st2[h]
                    run_g = carry[3 * h + 1]
                    parts = []
                    for b in range(kc):
                        before, total_g = _scan_packed(hi_lo, b, mat_g)
                        parts.append(before + run_g)
                        run_g = run_g + total_g
                    sig = jnp.exp(st1[h][0])
                    dz = jnp.where(valid, g - sig * (g + jnp.concatenate(parts, axis=1)), 0.0)
                    dzs.append(dz.astype(MXU))
                    st3.append((run, run_g))
                for h in heads:
                    new += [*st3[h], carry[3 * h + 2] + _dot(dzs[h], keys(k_s, h))]
                for g in range(GROUPS):
                    lanes = slice(g * BLK, (g + 1) * BLK)
                    dk_s[pl.ds(start, chunk), lanes] += _dot_tn(jnp.concatenate(dzs[2 * g:2 * g + 2], axis=0), q2[g])
                    dv_s[pl.ds(start, chunk), lanes] += _dot_tn(jnp.concatenate(probs[2 * g:2 * g + 2], axis=0), do2[g])
                return tuple(new)
            return step

        zero = jnp.zeros((BLK, BLK), F32)
        res = _sb_walk(i, step_of, (zero,) * (6 * GROUPS))
        for g in range(GROUPS):
            dq = jnp.where(lo, res[6 * g + 2], res[6 * g + 5])
            dq_ref[:, g * BLK:(g + 1) * BLK] = (dq * SCALE).astype(ACT)

        @pl.when(i == nb - 1)
        def _():
            dk_ref[...] = dk_s[...].astype(ACT)
            dv_ref[...] = dv_s[...].astype(ACT)

    ng = D // w
    blk = pl.BlockSpec((BLK, w), lambda hp, i: (i, hp))
    col = lambda off: pl.BlockSpec((t, w), lambda hp, i: (0, off + hp))
    return pl.pallas_call(
        body, name="sb_bwd", grid=(ng, nb),
        in_specs=[blk, col(ng), col(2 * ng), blk, blk],
        out_specs=[blk, col(0), col(0)],
        out_shape=[jax.ShapeDtypeStruct((t, D), ACT)] * 3,
        scratch_shapes=[pltpu.VMEM((t, w), MXU), pltpu.VMEM((t, w), MXU),
                        pltpu.VMEM((t, w), F32), pltpu.VMEM((t, w), F32)],
        compiler_params=_cparams(("arbitrary", "arbitrary")),
    )(p1, p1, p1, ltot, do)


def _norm_bwd(x, g, dy, eps):
    r = lax.rsqrt(jnp.mean(x * x, -1, keepdims=True) + eps)
    xhat = x * r
    dxn = dy * g
    return r * (dxn - xhat * jnp.mean(dxn * xhat, -1, keepdims=True)), jnp.sum(dy * xhat, 0, keepdims=True)


def _sb_in_bwd(dparts, w_sb, h1, g_pre1, dh2):
    t = h1.shape[0]
    tm = _tile(t, 544)

    def body(dq_ref, dk_ref, dv_ref, dg_ref, w_ref, h_ref, g_ref, dh2_ref, dh1_ref, dgn_ref):
        @pl.when(pl.program_id(0) == 0)
        def _():
            dgn_ref[...] = jnp.zeros_like(dgn_ref)

        dhn = None
        for a, ref in enumerate((dq_ref, dk_ref, dv_ref, dg_ref)):
            for b in range(2):
                term = _dot_nt(ref[:, b * 512:(b + 1) * 512].astype(MXU), w_ref[2 * a + b])
                dhn = term if dhn is None else dhn + term
        dx, dg = _norm_bwd(h_ref[...], g_ref[...], dhn, EPS)
        dgn_ref[...] += dg
        dh1_ref[...] = dh2_ref[...] + dx

    row = pl.BlockSpec((tm, D), lambda i: (i, 0))
    vec = pl.BlockSpec((1, D), lambda i: (0, 0))
    return pl.pallas_call(
        body, name="sb_in_bwd", grid=(t // tm,),
        in_specs=[row, row, row, row, pl.BlockSpec(w_sb.shape, lambda i: (0, 0, 0)), row, vec, row],
        out_specs=[row, vec],
        out_shape=[jax.ShapeDtypeStruct((t, D), F32), jax.ShapeDtypeStruct((1, D), F32)],
        compiler_params=_cparams(("arbitrary",)),
    )(*dparts, w_sb, h1, g_pre1, dh2)


def _ab_out_bwd(dh1, y0, g_post0, w_out, p0, att, c1, w_pw2):
    t = dh1.shape[0]
    tm = _tile(t, 544)

    def body(dh1_ref, y_ref, g0_ref, wo_ref, ga_ref, gb_ref, att_ref, c1_ref, pw_ref,
             dy_ref, dga_ref, dgb_ref, datt_ref, dc1_ref, dc2_ref, dg0_ref):
        @pl.when(pl.program_id(0) == 0)
        def _():
            dg0_ref[...] = jnp.zeros_like(dg0_ref)

        dy, dg = _norm_bwd(y_ref[...], g0_ref[...], dh1_ref[...], EPS)
        dg0_ref[...] += dg
        dy = dy.astype(MXU)
        dy_ref[...] = dy.astype(ACT)
        dmix = _dot_nt(dy, wo_ref[...])
        da, dc = dmix[:, :512], dmix[:, 512:]
        sga, dsga = _silu_and_grad(ga_ref[...])
        sgb, dsgb = _silu_and_grad(gb_ref[...])
        datt_ref[...] = da * sga
        dga_ref[...] = (da * att_ref[...] * dsga).astype(ACT)
        c2 = _dot(c1_ref[...].astype(MXU), pw_ref[...])
        dc2 = (dc * sgb).astype(MXU)
        dgb_ref[...] = (dc * c2 * dsgb).astype(ACT)
        dc2_ref[...] = dc2.astype(ACT)
        dc1_ref[...] = _dot_nt(dc2, pw_ref[...])

    row = lambda w, idx: pl.BlockSpec((tm, w), lambda i: (i, idx))
    full = lambda a: pl.BlockSpec(a.shape, lambda i: (0, 0))
    sd = jax.ShapeDtypeStruct
    return pl.pallas_call(
        body, name="ab_out_bwd", grid=(t // tm,),
        in_specs=[row(D, 0), row(D, 0), full(g_post0), full(w_out), row(512, 3), row(512, 4),
                  row(512, 0), row(512, 0), full(w_pw2)],
        out_specs=[row(D, 0), row(512, 0), row(512, 0), row(512, 0), row(512, 0), row(512, 0),
                   pl.BlockSpec((1, D), lambda i: (0, 0))],
        out_shape=[sd((t, D), ACT), sd((t, 512), ACT), sd((t, 512), ACT), sd((t, 512), F32),
                   sd((t, 512), F32), sd((t, 512), ACT), sd((1, D), F32)],
        compiler_params=_cparams(("arbitrary",)),
    )(dh1, y0, g_post0, w_out, p0, p0, att, c1, w_pw2)


def _conv_bwd(p0, dc1, conv_w, conv_b, ln_g, ln_b):
    t = p0.shape[0]
    tm = _tile(t, 544)
    hb = tm // HALO
    last = t // HALO - 1

    def body(cur_ref, prev_ref, next_ref, d_ref, dn_ref, w_ref, b_ref, g_ref, bb_ref,
             dglu_ref, dw_ref, db_ref, dlg_ref, dlb_ref):
        i = pl.program_id(0)

        @pl.when(i == 0)
        def _():
            for ref in (dw_ref, db_ref, dlg_ref, dlb_ref):
                ref[...] = jnp.zeros_like(ref)

        glu = jnp.concatenate([prev_ref[...], cur_ref[...], next_ref[...]], axis=0)
        rw = _rows((tm + 2 * HALO, 1), i * tm - HALO)
        ga, sg = glu[:, :512], _sigmoid(glu[:, 512:])
        u_w = jnp.where((rw >= PAD) & (rw < t), ga * sg, 0.0)
        n_cv = tm + HALO
        cv = _conv_window(u_w, w_ref, n_cv, HALO - (CONV_W - 1)) + b_ref[...]
        xc = cv - jnp.mean(cv, -1, keepdims=True)
        rstd = lax.rsqrt(jnp.mean(xc * xc, -1, keepdims=True) + LN_EPS)
        cvhat = xc * rstd
        ln = cvhat * g_ref[...] + bb_ref[...]
        _, dsl = _silu_and_grad(ln)
        rc = _rows((n_cv, 1), i * tm)
        dc = jnp.concatenate([d_ref[...], dn_ref[...]], axis=0)
        dln = jnp.where(rc < t, dc * dsl, 0.0)
        dhat = dln * g_ref[...]
        dcv = rstd * (dhat - jnp.mean(dhat, -1, keepdims=True)
                      - cvhat * jnp.mean(dhat * cvhat, -1, keepdims=True))
        own = dcv[:tm]
        dlg_ref[...] += jnp.sum((dln * cvhat)[:tm], 0, keepdims=True)
        dlb_ref[...] += jnp.sum(dln[:tm], 0, keepdims=True)
        db_ref[...] += jnp.sum(own, 0, keepdims=True)
        rows = tm + 2 * HALO
        du = None
        for j in range(CONV_W):
            first = HALO - (CONV_W - 1) + j
            shifted = pltpu.roll(u_w, (rows - first) % rows, 0)[:tm]
            dw_ref[j:j + 1, :] += jnp.sum(own * shifted, 0, keepdims=True)
            back = pltpu.roll(dcv, (n_cv - (CONV_W - 1 - j)) % n_cv, 0)[:tm]
            term = back * w_ref[j:j + 1, :]
            du = term if du is None else du + term
        du = jnp.where(_rows((tm, 1), i * tm) >= PAD, du, 0.0)
        ga_c, sg_c = ga[HALO:HALO + tm], sg[HALO:HALO + tm]
        dglu_ref[:, :512] = (du * sg_c).astype(ACT)
        dglu_ref[:, 512:] = (du * ga_c * sg_c * (1.0 - sg_c)).astype(ACT)

    vec = pl.BlockSpec((1, 512), lambda i: (0, 0))
    nxt = lambda i: (jnp.minimum((i + 1) * hb, last), 0)
    return pl.pallas_call(
        body, name="conv_bwd", grid=(t // tm,),
        in_specs=[pl.BlockSpec((tm, D), lambda i: (i, 0)),
                  pl.BlockSpec((HALO, D), lambda i: (jnp.maximum(i * hb - 1, 0), 0)),
                  pl.BlockSpec((HALO, D), nxt),
                  pl.BlockSpec((tm, 512), lambda i: (i, 0)),
                  pl.BlockSpec((HALO, 512), nxt),
                  pl.BlockSpec((CONV_W, 512), lambda i: (0, 0)), vec, vec, vec],
        out_specs=[pl.BlockSpec((tm, D), lambda i: (i, 0)),
                   pl.BlockSpec((HALO, 512), lambda i: (0, 0)), vec, vec, vec],
        out_shape=[jax.ShapeDtypeStruct((t, D), ACT), jax.ShapeDtypeStruct((HALO, 512), F32)]
        + [jax.ShapeDtypeStruct((1, 512), F32)] * 3,
        compiler_params=_cparams(("arbitrary",)),
    )(p0, p0, p0, dc1, dc1, conv_w, conv_b, ln_g, ln_b)


def _swa_bwd(p0, datt, sinks, tables, ride):
    t = p0.shape[0]
    nb = t // BLK
    plan = _ChipsPlan(ride)
    nr = plan.n

    def body(sink_ref, q_ref, kv_ref, d_ref, cos_ref, sa_ref, sb_ref, *rest):
        dq_ref, dkv_ref, ds_ref = rest[nr:nr + 3]
        acc = rest[2 * nr + 3]
        comm = (*rest[:nr], *rest[nr + 3:2 * nr + 3], *rest[2 * nr + 4:])
        n = pl.program_id(0)
        pl.when(n == 0)(lambda: plan.begin(comm))

        @pl.when(n == 0)
        def _():
            acc[...] = jnp.zeros_like(acc)
            ds_ref[...] = jnp.zeros_like(ds_ref)

        kd, vd, lo = _swa_keys(kv_ref, n)
        mask4 = jnp.concatenate([_swa_mask(n)] * 4, axis=0)
        row0 = pl.multiple_of(n * BLK, BLK)
        tabs = [r[pl.ds(row0, BLK), :] for r in (cos_ref, sa_ref, sb_ref)]
        dk_g, dv_g = [], []
        for g in range(2):
            qs, dos = _swa_stack(q_ref, g, lo), _swa_stack(d_ref, g, lo)
            pr, p_sink = _swa_probs(qs, kd[g], mask4, sink_ref, g)
            dpr = _dot_nt(dos, vd[g])
            delta = jnp.sum(pr * dpr, -1, keepdims=True)
            dsc = (pr * (dpr - delta) * SCALE).astype(MXU)
            sunk = p_sink * delta
            for h in range(4):
                row = 4 * g + h
                ds_ref[row:row + 1, :] += jnp.full((1, BLK), -1.0, F32) * jnp.sum(sunk[h * BLK:(h + 1) * BLK])
            pairs = _swa_unstack(_dot(dsc, kd[g]), lo)
            for k in range(2):
                p = 2 * g + k
                dq_ref[:, p * BLK:(p + 1) * BLK] = _unrope(pairs[k], *tabs).astype(ACT)
            dk_g.append(_dot_tn(dsc, qs))
            dv_g.append(_dot_tn(pr.astype(MXU), dos))
        fold = lambda a: a + pltpu.roll(a, HEAD, 1)
        dk = jnp.where(lo, fold(dk_g[0]), fold(dk_g[1]))
        dv = jnp.where(lo, fold(dv_g[0]), fold(dv_g[1]))
        dkv = jnp.concatenate([dk, dv], axis=1)
        prev = pl.multiple_of(jnp.maximum(n - 1, 0) * BLK, BLK)
        acc[0:BLK, :] += dkv[0:BLK]
        acc[pl.ds(prev, BLK), :] += dkv[BLK:2 * BLK]
        acc[pl.ds(row0, BLK), :] += dkv[2 * BLK:]

        @pl.when(n == nb - 1)
        def _():
            dkv_ref[:, :BLK] = _unrope(acc[:, :BLK], cos_ref[...], sa_ref[...], sb_ref[...]).astype(ACT)
            dkv_ref[:, BLK:] = acc[:, BLK:].astype(ACT)

        pl.when(n == nb - 1)(lambda: plan.end(comm))

    tab = pl.BlockSpec((t, BLK), lambda n: (0, 0))
    outs = pl.pallas_call(
        body, name="swa_bwd", grid=(nb,),
        in_specs=[pl.BlockSpec(memory_space=pltpu.SMEM),
                  pl.BlockSpec((BLK, 512), lambda n: (n, 2)),
                  pl.BlockSpec((t, 256), lambda n: (0, 10)),
                  pl.BlockSpec((BLK, 512), lambda n: (n, 0)), tab, tab, tab] + plan.specs,
        out_specs=[pl.BlockSpec((BLK, 512), lambda n: (n, 0)),
                   pl.BlockSpec((t, 256), lambda n: (0, 0)),
                   pl.BlockSpec((8, BLK), lambda n: (0, 0))] + plan.specs,
        out_shape=[jax.ShapeDtypeStruct((t, 512), ACT), jax.ShapeDtypeStruct((t, 256), ACT),
                   jax.ShapeDtypeStruct((8, BLK), F32)] + plan.out_shape,
        scratch_shapes=[pltpu.VMEM((t, 256), F32)] + plan.scratch,
        compiler_params=_cparams(("arbitrary",)),
    )(sinks, p0, p0, datt, *tables, *ride)
    return outs[0], outs[1], outs[2], outs[3:]


def _ab_in_bwd(dp0, w_t, h0, g_pre, dh1, ride):
    t = h0.shape[0]
    tm = _tile(t, 544)
    plan = _ChipsPlan(ride)
    nr = plan.n

    pieces = [(p[0], p[1] * dp0.tw, p[2] * dp0.tw) for p in dp0.pieces]

    def body(*refs):
        d_refs = refs[:5]
        w_ref, h_ref, g_ref, dh1_ref = refs[5:9]
        rest = refs[9:]
        dh0_ref, dg_ref = rest[nr:nr + 2]
        comm = (*rest[:nr], *rest[nr + 2:])
        i = pl.program_id(0)
        pl.when(i == 0)(lambda: plan.begin(comm))

        @pl.when(i == 0)
        def _():
            dg_ref[...] = jnp.zeros_like(dg_ref)

        dhn = None
        for ref, (_, first, rows) in zip(d_refs, pieces):
            term = _dot(ref[...].astype(MXU), w_ref[first:first + rows, :])
            dhn = term if dhn is None else dhn + term
        dx, dg = _norm_bwd(h_ref[...], g_ref[...], dhn, EPS)
        dg_ref[...] += dg
        dh0_ref[...] = dh1_ref[...] + dx
        pl.when(i == t // tm - 1)(lambda: plan.end(comm))

    row = pl.BlockSpec((tm, D), lambda i: (i, 0))
    vec = pl.BlockSpec((1, D), lambda i: (0, 0))
    outs = pl.pallas_call(
        body, name="ab_in_bwd", grid=(t // tm,),
        in_specs=[pl.BlockSpec((tm, rows), lambda i: (i, 0)) for _, _, rows in pieces] + [
            pl.BlockSpec(w_t.shape, lambda i: (0, 0)), row, vec, row] + plan.specs,
        out_specs=[row, vec] + plan.specs,
        out_shape=[jax.ShapeDtypeStruct((t, D), F32), jax.ShapeDtypeStruct((1, D), F32)] + plan.out_shape,
        scratch_shapes=plan.scratch,
        compiler_params=_cparams(("arbitrary",)),
    )(*dp0.arrays, w_t, h0, g_pre, dh1, *ride)
    return outs[0], outs[1], outs[2:]


def _dw_plain(a, b, name):
    t, m = a.shape
    n = b.shape[1]
    tm = _tile(t, 1088)
    tn = min(n, 512)
    nk = t // tm

    def body(a_ref, b_ref, o_ref, acc):
        k = pl.program_id(1)

        @pl.when(k == 0)
        def _():
            acc[...] = jnp.zeros_like(acc)

        acc[...] += _dot_tn(a_ref[...].astype(MXU), b_ref[...].astype(MXU))

        @pl.when(k == nk - 1)
        def _():
            o_ref[...] = acc[...].astype(WIRE)

    return pl.pallas_call(
        body, name=name, grid=(n // tn, nk),
        in_specs=[pl.BlockSpec((tm, m), lambda j, k: (k, 0)),
                  pl.BlockSpec((tm, tn), lambda j, k: (k, j))],
        out_specs=pl.BlockSpec((m, tn), lambda j, k: (0, j)),
        out_shape=jax.ShapeDtypeStruct((m, n), WIRE),
        scratch_shapes=[pltpu.VMEM((m, tn), F32)],
        compiler_params=_cparams(("arbitrary", "arbitrary")),
    )(a, b)


def _dw_chunks(hn, dp, name):
    t = hn.shape[0]
    tm = _tile(t, 1088)
    nk = t // tm
    nt, tw = dp.n_tiles, dp.tw
    n_in = len(dp.arrays)

    def body(*refs):
        d_refs = refs[:n_in]
        h_ref, o_ref, acc = refs[n_in:]
        j, k = pl.program_id(0), pl.program_id(1)

        @pl.when(k == 0)
        def _():
            acc[...] = jnp.zeros_like(acc)

        def add(ref):
            acc[...] += _dot_tn(h_ref[...].astype(MXU), ref[...].astype(MXU))
        dp.apply(j, d_refs, add)

        @pl.when(k == nk - 1)
        def _():
            o_ref[...] = acc[...].astype(WIRE)

    return pl.pallas_call(
        body, name=name, grid=(nt, nk),
        in_specs=dp.specs(tm, lambda j, k: k, lambda j, k: j) + [
            pl.BlockSpec((tm, D), lambda j, k: (k, 0))],
        out_specs=pl.BlockSpec((None, D, tw), lambda j, k: (j, 0, 0)),
        out_shape=jax.ShapeDtypeStruct((nt, D, tw), WIRE),
        scratch_shapes=[pltpu.VMEM((D, tw), F32)],
        compiler_params=_cparams(("arbitrary", "arbitrary")),
    )(*dp.arrays, hn)


def _dw_transposed(dp, hn, name):
    t = hn.shape[0]
    tm = _tile(t, 1088)
    nk = t // tm
    nt, tw = dp.n_tiles, dp.tw
    n_in = len(dp.arrays)

    def body(*refs):
        d_refs = refs[:n_in]
        h_ref, o_ref, acc = refs[n_in:]
        j, k = pl.program_id(0), pl.program_id(1)

        @pl.when(k == 0)
        def _():
            acc[...] = jnp.zeros_like(acc)

        def add(ref):
            acc[...] += _dot_tn(ref[...].astype(MXU), h_ref[...].astype(MXU))
        dp.apply(j, d_refs, add)

        @pl.when(k == nk - 1)
        def _():
            o_ref[...] = acc[...].astype(WIRE)

    return pl.pallas_call(
        body, name=name, grid=(nt, nk),
        in_specs=dp.specs(tm, lambda j, k: k, lambda j, k: j) + [
            pl.BlockSpec((tm, D), lambda j, k: (k, 0))],
        out_specs=pl.BlockSpec((tw, D), lambda j, k: (j, 0)),
        out_shape=jax.ShapeDtypeStruct((nt * tw, D), WIRE),
        scratch_shapes=[pltpu.VMEM((tw, D), F32)],
        compiler_params=_cparams(("arbitrary", "arbitrary")),
    )(*dp.arrays, hn)


def kernel(x, meta_tokens, ab_pre_norm, ab_w_in, ab_sinks, ab_conv_w, ab_conv_b, ab_conv_ln_g, ab_conv_ln_b, ab_w_pw2, ab_w_out, ab_post_norm, sb_pre_norm, sb_w_in, sb_w_out, sb_post_norm, loss_target, m_meta_tokens, m_ab_pre_norm, m_ab_w_in, m_ab_sinks, m_ab_conv_w, m_ab_conv_b, m_ab_conv_ln_g, m_ab_conv_ln_b, m_ab_w_pw2, m_ab_w_out, m_ab_post_norm, m_sb_pre_norm, m_sb_w_in, m_sb_w_out, m_sb_post_norm, v_meta_tokens, v_ab_pre_norm, v_ab_w_in, v_ab_sinks, v_ab_conv_w, v_ab_conv_b, v_ab_conv_ln_g, v_ab_conv_ln_b, v_ab_w_pw2, v_ab_w_out, v_ab_post_norm, v_sb_pre_norm, v_sb_w_in, v_sb_w_out, v_sb_post_norm):
    seq = x.shape[1]
    t = seq + BLK
    mx, my, mc = _coords()
    me = 4 * mx + 2 * my + mc
    pos = jnp.stack([mx, my, mc, me]).astype(jnp.int32)

    w_ab_t, *small = _all_gather(
        [ab_w_in[0].T.astype(WIRE), meta_tokens, ab_conv_w[0], sb_pre_norm, sb_post_norm], "gather_first")
    w_ab_t = w_ab_t.reshape(2816, D)
    meta_full = jnp.moveaxis(small[0], 0, 1).reshape(N_META, D)
    conv_w = jnp.moveaxis(small[1], 0, 1).reshape(CONV_W, 512)
    sb_pre = jnp.moveaxis(small[2], 0, 1).reshape(1, D)
    sb_post = jnp.moveaxis(small[3], 0, 1).reshape(1, D)

    h0 = jnp.concatenate([jnp.zeros((PAD, D), F32), meta_full, x[0]], axis=0)
    tgt = jnp.concatenate([jnp.zeros((BLK, D), F32), loss_target[0]], axis=0)
    tables = _rope_tables(t)
    sinks = ab_sinks[0]

    p0, hn0, (w_oa, w_pw, w_os) = _ab_in(
        h0, ab_pre_norm, w_ab_t, tables,
        [ab_w_out[0].astype(WIRE), ab_w_pw2[0].astype(WIRE), sb_w_out[0].astype(WIRE)])
    w_oa, w_os, w_pw = w_oa.reshape(D, D), w_os.reshape(D, D), w_pw.reshape(512, 512)
    att, (w_sb,) = _swa_fwd(p0, sinks, [sb_w_in[0].astype(WIRE)])
    c1 = _conv_fwd(p0, conv_w, ab_conv_b, ab_conv_ln_g, ab_conv_ln_b)
    h1, y0, mix = _ab_out(h0, p0, att, c1, w_pw, w_oa, ab_post_norm)
    p1, hn1 = _sb_in(h1, sb_pre, w_sb)
    o, ltot = _sb_fwd(p1)
    loss_part, dh2, dy1, m1, do, dgate, dg_sb_post = _sb_out(o, p1, w_os, h1, sb_post, tgt)

    dq1, dk1, dv1 = _sb_bwd(p1, ltot, do)
    dp1 = _Cols([(dq1, 0, 2), (dk1, 2, 2), (dv1, 4, 2), (dgate, 6, 2)], 512)

    def sibling_stage(parts, names, tag):
        got = _exchange_sibling(parts, "reduce_sibling_" + tag)
        return [_add_sibling(pos, p, r, "add_sibling_" + nm) for p, r, nm in zip(parts, got, names)]

    def finish(sums, got, names):
        return [_sum_chips(pos, s, r, "sum_chips_" + nm) for s, r, nm in zip(sums, got, names)]

    names1 = ["sb_in", "sb_out"]
    sums1 = sibling_stage([_dw_chunks(hn1, dp1, "dw_sb_in").reshape(4, 2, D, 512),
                           _dw_plain(m1, dy1, "dw_sb_out").reshape(4, 2, BLK, D)], names1, "sb")
    dh1, dg_sb_pre = _sb_in_bwd([dq1, dk1, dv1, dgate], w_sb, h1, sb_pre, dh2)
    dy0, dga, dgb, datt, dc1, dc2, dg_ab_post = _ab_out_bwd(dh1, y0, ab_post_norm, w_oa, p0, att, c1, w_pw)
    names2 = ["ab_out", "pw2"]
    sums2 = sibling_stage([_dw_plain(mix, dy0, "dw_ab_out").reshape(4, 2, BLK, D),
                           _dw_plain(c1, dc2, "dw_pw2").reshape(4, 2, 64, 512)], names2, "ab_out")
    dglu, dconv_w, dconv_b, dln_g, dln_b = _conv_bwd(p0, dc1, conv_w, ab_conv_b, ab_conv_ln_g, ab_conv_ln_b)
    dq0, dkv0, dsinks, got = _swa_bwd(p0, datt, sinks, tables, sums1 + sums2)
    g_sb_w_in, g_sb_w_out, g_ab_w_out, g_ab_w_pw2 = finish(sums1 + sums2, got, names1 + names2)
    dp0 = _Cols([(dq0, 0, 2), (dkv0, 2, 1), (dga, 3, 2), (dglu, 5, 4), (dgb, 9, 2)], 256)

    sums0 = sibling_stage([_dw_transposed(dp0, hn0, "dw_ab_in").reshape(4, 2, 352, D)], ["ab_in"], "ab_in")
    dh0, dg_ab_pre, got0 = _ab_in_bwd(dp0, w_ab_t, h0, ab_pre_norm, dh1, sums0)
    g_ab_w_in = finish(sums0, got0, ["ab_in"])[0].T

    small_parts = [dh0[PAD:BLK], dg_ab_pre, dsinks, dconv_w, dconv_b, dln_g, dln_b,
                   dg_ab_post, dg_sb_pre, dg_sb_post, loss_part]
    red = _reduce_small(_all_gather(small_parts, "gather_small_grads"), "reduce_small")
    col = lambda a, w: lax.dynamic_slice_in_dim(a, me * w, w, axis=1)
    g_meta = col(red[0], BLK)
    g_ab_pre = red[1]
    g_sinks = red[2][:, 0].reshape(1, 8)
    g_conv_w = col(red[3][:CONV_W], 64)
    g_conv_b, g_ln_g, g_ln_b, g_ab_post = red[4], red[5], red[6], red[7]
    g_sb_pre, g_sb_post = col(red[8], BLK), col(red[9], BLK)

    loss = red[10][0, 0]
    grad_x = dh0[BLK:][None]

    weights = [meta_tokens, ab_pre_norm, ab_w_in[0], ab_sinks, ab_conv_w[0], ab_conv_b, ab_conv_ln_g,
               ab_conv_ln_b, ab_w_pw2[0], ab_w_out[0], ab_post_norm, sb_pre_norm, sb_w_in[0],
               sb_w_out[0], sb_post_norm]
    grads = [g_meta, g_ab_pre, g_ab_w_in, g_sinks, g_conv_w, g_conv_b, g_ln_g, g_ln_b, g_ab_w_pw2,
             g_ab_w_out, g_ab_post, g_sb_pre, g_sb_w_in, g_sb_w_out, g_sb_post]
    ms = [m_meta_tokens, m_ab_pre_norm, m_ab_w_in[0], m_ab_sinks, m_ab_conv_w[0], m_ab_conv_b,
          m_ab_conv_ln_g, m_ab_conv_ln_b, m_ab_w_pw2[0], m_ab_w_out[0], m_ab_post_norm,
          m_sb_pre_norm, m_sb_w_in[0], m_sb_w_out[0], m_sb_post_norm]
    vs = [v_meta_tokens, v_ab_pre_norm, v_ab_w_in[0], v_ab_sinks, v_ab_conv_w[0], v_ab_conv_b,
          v_ab_conv_ln_g, v_ab_conv_ln_b, v_ab_w_pw2[0], v_ab_w_out[0], v_ab_post_norm,
          v_sb_pre_norm, v_sb_w_in[0], v_sb_w_out[0], v_sb_post_norm]
    lead = [w.ndim == 3 for w in (meta_tokens, ab_pre_norm, ab_w_in, ab_sinks, ab_conv_w, ab_conv_b,
                                   ab_conv_ln_g, ab_conv_ln_b, ab_w_pw2, ab_w_out, ab_post_norm,
                                   sb_pre_norm, sb_w_in, sb_w_out, sb_post_norm)]
    big_ids = [2, 8, 9, 12, 13]
    small_ids = [i for i in range(15) if i not in big_ids]
    deltas, new_m, new_v = [None] * 15, [None] * 15, [None] * 15
    for ids, nm in ((small_ids, "adamw_small"), (big_ids, "adamw_big")):
        d_, m_, v_ = _adamw([weights[i] for i in ids], [grads[i] for i in ids],
                            [ms[i] for i in ids], [vs[i] for i in ids], nm)
        for k, i in enumerate(ids):
            deltas[i], new_m[i], new_v[i] = d_[k], m_[k], v_[k]
    fix = lambda arrs: [a[None] if l else a for a, l in zip(arrs, lead)]
    return (loss, grad_x, *fix(grads), *fix(deltas), *fix(new_m), *fix(new_v))
```

```python
import functools

import numpy as np
import jax
import jax.numpy as jnp
from jax import lax
from jax.experimental import pallas as pl
from jax.experimental.pallas import tpu as pltpu

F32 = jnp.float32
MXU = jnp.bfloat16
ACT = jnp.bfloat16
WIRE = jnp.bfloat16

D = 1024
N_META = 16
BLK = 128
PAD = BLK - N_META
HEAD = 64
NEG = -1e30
EPS = 1e-6
LN_EPS = 1e-5
ROPE_THETA = 10000.0
SCALE = HEAD ** -0.5
CONV_W = 31
HALO = 32
LR, B1, B2, ADAM_EPS, WD, STEP = 0.001, 0.9, 0.999, 1e-08, 0.01, 10
VMEM_LIMIT = 56 * 1024 * 1024
MESH = pl.DeviceIdType.MESH

P0_SRC = (5, 6, 7, 8, 0, 1, 3, 4, 9, 10, 2)


def _cparams(sem=None):
    return pltpu.CompilerParams(dimension_semantics=sem, vmem_limit_bytes=VMEM_LIMIT)


def _tile(t, pref):
    for cand in (1088, 544, 272, 128):
        if cand <= pref and t % cand == 0:
            return cand
    raise ValueError(t)


def _sigmoid(x):
    return 1.0 / (1.0 + jnp.exp(-x))


def _silu_and_grad(x):
    s = _sigmoid(x)
    return x * s, s * (1.0 + x * (1.0 - s))


def _dot(a, b):
    return jnp.dot(a, b, preferred_element_type=F32)


def _dot_nt(a, b):
    return lax.dot_general(a, b, (((1,), (1,)), ((), ())), preferred_element_type=F32)


def _dot_tn(a, b):
    return lax.dot_general(a, b, (((0,), (0,)), ((), ())), preferred_element_type=F32)


def _rows(shape, base):
    return base + lax.broadcasted_iota(jnp.int32, shape, 0)


def _rope_tables(t):
    half = HEAD // 2
    inv = ROPE_THETA ** (-np.arange(half, dtype=np.float32) / half)
    pos = (np.arange(t) - PAD).astype(np.float32)
    ang = pos[:, None] * inv[None, :]
    lane = np.arange(BLK)
    cos = np.cos(ang)[:, lane % half].astype(np.float32)
    sin = np.sin(ang)[:, lane % half].astype(np.float32)
    first = (lane % HEAD) < half
    sin_a = np.where(first[None, :], -sin, 0.0).astype(np.float32)
    sin_b = np.where(first[None, :], 0.0, sin).astype(np.float32)
    return jnp.asarray(cos), jnp.asarray(sin_a), jnp.asarray(sin_b)


def _rope(v, cos, sin_a, sin_b):
    return v * cos + pltpu.roll(v, 96, 1) * sin_a + pltpu.roll(v, 32, 1) * sin_b


def _unrope(v, cos, sin_a, sin_b):
    return v * cos - pltpu.roll(v, 96, 1) * sin_a - pltpu.roll(v, 32, 1) * sin_b


def _coords():
    return lax.axis_index("x"), lax.axis_index("y"), lax.axis_index("c")


def _all_gather(arrs, name):
    plan = _GatherPlan(arrs)

    def body(*refs):
        plan.begin(refs)
        plan.end(refs)

    return pl.pallas_call(
        body, name=name, out_shape=plan.out_shape,
        in_specs=plan.specs, out_specs=plan.specs, scratch_shapes=plan.scratch,
    )(*arrs)


class _GatherPlan:
    def __init__(self, arrs):
        n = self.n = len(arrs)
        self.out_shape = [jax.ShapeDtypeStruct((8,) + a.shape, a.dtype) for a in arrs]
        self.specs = [pl.BlockSpec(memory_space=pl.ANY)] * n
        self.scratch = [pltpu.SemaphoreType.DMA((n, 7)), pltpu.SemaphoreType.DMA((n, 7)),
                        pltpu.SemaphoreType.DMA((n,))]

    def _copies(self, refs):
        n = self.n
        ins, outs = refs[:n], refs[n:2 * n]
        send_sems, recv_sems, local_sems = refs[2 * n:]
        x, y, c = _coords()
        me, sibling = (x, y, c), (x, y, 1 - c)
        chips = [(1 - x, y), (x, 1 - y), (1 - x, 1 - y)]

        def copy(a, k, block, to, src=None):
            dst = outs[a].at[4 * block[0] + 2 * block[1] + block[2]]
            return pltpu.make_async_remote_copy(
                src_ref=dst if src is None else src, dst_ref=dst,
                send_sem=send_sems.at[a, k], recv_sem=recv_sems.at[a, k],
                device_id=to, device_id_type=MESH)

        mine = [pltpu.make_async_copy(ins[a], outs[a].at[4 * x + 2 * y + c], local_sems.at[a])
                for a in range(n)]
        first = []
        for a in range(n):
            first.append(copy(a, 0, me, sibling, src=ins[a]))
            for j, chip in enumerate(chips):
                first.append(copy(a, 1 + j, me, (*chip, c), src=ins[a]))
        return copy, mine, first, (me, sibling, chips, c)

    def begin(self, refs):
        _, mine, first, _ = self._copies(refs)
        for cp in mine + first:
            cp.start()

    def end(self, refs):
        copy, mine, first, (me, sibling, chips, c) = self._copies(refs)
        passed = []
        for j, chip in enumerate(chips):
            for a in range(self.n):
                copy(a, 1 + j, (*chip, c), me).wait_recv()
                cp = copy(a, 4 + j, (*chip, c), sibling)
                cp.start()
                passed.append(cp)
        for a in range(self.n):
            copy(a, 0, sibling, me).wait_recv()
            for j, chip in enumerate(chips):
                copy(a, 4 + j, (*chip, 1 - c), me).wait_recv()
        for cp in first + passed:
            cp.wait_send()
        for cp in mine:
            cp.wait()


class _ChipsPlan:
    def __init__(self, sums):
        n = self.n = len(sums)
        self.out_shape = [jax.ShapeDtypeStruct((3,) + s.shape[1:], s.dtype) for s in sums]
        self.specs = [pl.BlockSpec(memory_space=pl.ANY)] * n
        self.scratch = [pltpu.SemaphoreType.DMA((n, 3)), pltpu.SemaphoreType.DMA((n, 3))]

    def _copies(self, refs):
        n = self.n
        ins, outs = refs[:n], refs[n:2 * n]
        send_sems, recv_sems = refs[2 * n:]
        x, y, c = _coords()
        chips = [(1 - x, y), (x, 1 - y), (1 - x, 1 - y)]
        return [pltpu.make_async_remote_copy(
            src_ref=ins[a].at[2 * chip[0] + chip[1]], dst_ref=outs[a].at[k],
            send_sem=send_sems.at[a, k], recv_sem=recv_sems.at[a, k],
            device_id=(*chip, c), device_id_type=MESH)
            for a in range(n) for k, chip in enumerate(chips)]

    def begin(self, refs):
        for cp in self._copies(refs):
            cp.start()

    def end(self, refs):
        for cp in self._copies(refs):
            cp.wait()


def _exchange_sibling(parts, name):
    n = len(parts)

    def body(*refs):
        ins, outs = refs[:n], refs[n:2 * n]
        send_sems, recv_sems = refs[2 * n:]
        x, y, c = _coords()
        copies = [pltpu.make_async_remote_copy(
            src_ref=ins[a].at[:, 1 - c], dst_ref=outs[a],
            send_sem=send_sems.at[a], recv_sem=recv_sems.at[a],
            device_id=(x, y, 1 - c), device_id_type=MESH) for a in range(n)]
        for cp in copies:
            cp.start()
        for cp in copies:
            cp.wait()

    any_spec = pl.BlockSpec(memory_space=pl.ANY)
    return pl.pallas_call(
        body, name=name,
        out_shape=[jax.ShapeDtypeStruct((4,) + p.shape[2:], p.dtype) for p in parts],
        in_specs=[any_spec] * n, out_specs=[any_spec] * n,
        scratch_shapes=[pltpu.SemaphoreType.DMA((n,)), pltpu.SemaphoreType.DMA((n,))],
    )(*parts)


def _exchange_chips(sums, name):
    plan = _ChipsPlan(sums)

    def body(*refs):
        plan.begin(refs)
        plan.end(refs)

    return pl.pallas_call(
        body, name=name, out_shape=plan.out_shape,
        in_specs=plan.specs, out_specs=plan.specs, scratch_shapes=plan.scratch,
    )(*sums)


def _add_sibling(pos, part, recv, name):
    _, _, r, c = part.shape

    def body(pos_ref, p_ref, r_ref, o_ref):
        o_ref[...] = (p_ref[...].astype(F32) + r_ref[...].astype(F32)).astype(o_ref.dtype)

    return pl.pallas_call(
        body, name=name,
        grid_spec=pltpu.PrefetchScalarGridSpec(
            num_scalar_prefetch=1, grid=(4,),
            in_specs=[pl.BlockSpec((None, None, r, c), lambda q, pos: (q, pos[2], 0, 0)),
                      pl.BlockSpec((None, r, c), lambda q, pos: (q, 0, 0))],
            out_specs=pl.BlockSpec((None, r, c), lambda q, pos: (q, 0, 0))),
        out_shape=jax.ShapeDtypeStruct((4, r, c), part.dtype),
        compiler_params=_cparams(("arbitrary",)),
    )(pos, part, recv)


def _sum_chips(pos, sums, recv, name):
    _, r, c = sums.shape

    def body(pos_ref, s_ref, r_ref, o_ref):
        g = s_ref[...].astype(F32)
        for k in range(3):
            g = g + r_ref[k].astype(F32)
        o_ref[...] = g

    return pl.pallas_call(
        body, name=name,
        grid_spec=pltpu.PrefetchScalarGridSpec(
            num_scalar_prefetch=1, grid=(1,),
            in_specs=[pl.BlockSpec((None, r, c), lambda i, pos: (2 * pos[0] + pos[1], 0, 0)),
                      pl.BlockSpec((3, r, c), lambda i, pos: (0, 0, 0))],
            out_specs=pl.BlockSpec((r, c), lambda i, pos: (0, 0))),
        out_shape=jax.ShapeDtypeStruct((r, c), F32),
        compiler_params=_cparams(("arbitrary",)),
    )(pos, sums, recv)


def _adamw(ws, gs, ms, vs, name):
    n = len(ws)
    c1 = 1.0 / (1.0 - B1 ** STEP)
    c2 = 1.0 / (1.0 - B2 ** STEP)

    def body(*refs):
        w_r, g_r, m_r, v_r = refs[:n], refs[n:2 * n], refs[2 * n:3 * n], refs[3 * n:4 * n]
        d_o, m_o, v_o = refs[4 * n:5 * n], refs[5 * n:6 * n], refs[6 * n:7 * n]
        for a in range(n):
            g = g_r[a][...]
            m = B1 * m_r[a][...] + (1.0 - B1) * g
            v = B2 * v_r[a][...] + (1.0 - B2) * (g * g)
            d_o[a][...] = -LR * ((m * c1) / (jnp.sqrt(v * c2) + ADAM_EPS) + WD * w_r[a][...])
            m_o[a][...] = m
            v_o[a][...] = v

    shapes = [jax.ShapeDtypeStruct(w.shape, F32) for w in ws]
    outs = pl.pallas_call(body, name=name, out_shape=shapes * 3,
                          compiler_params=_cparams())(*ws, *gs, *ms, *vs)
    return outs[:n], outs[n:2 * n], outs[2 * n:]


def _reduce_small(gathered, name):
    n = len(gathered)

    def body(*refs):
        for a in range(n):
            acc = refs[a][0]
            for k in range(1, 8):
                acc = acc + refs[a][k]
            refs[n + a][...] = acc

    return pl.pallas_call(
        body, name=name,
        out_shape=[jax.ShapeDtypeStruct(g.shape[1:], F32) for g in gathered],
        compiler_params=_cparams())(*gathered)


class _Cols:
    def __init__(self, pieces, tw):
        self.pieces, self.tw = pieces, tw
        self.arrays = [p[0] for p in pieces]
        self.n_tiles = sum(p[2] for p in pieces)

    def specs(self, tm, row_of, tile_of):
        out = []
        for _, first, cnt in self.pieces:
            def imap(*g, first=first, cnt=cnt):
                return (row_of(*g), jnp.clip(tile_of(*g) - first, 0, cnt - 1))
            out.append(pl.BlockSpec((tm, self.tw), imap))
        return out

    def apply(self, t, refs, fn):
        for ref, (_, first, cnt) in zip(refs, self.pieces):
            pl.when((t >= first) & (t < first + cnt))(functools.partial(fn, ref))


def _ab_in(h, g, w_t, tables, ride):
    t = h.shape[0]
    tm = _tile(t, 1088)
    src = jnp.asarray(np.array(P0_SRC, np.int32))
    plan = _GatherPlan(ride)
    nr = plan.n

    def body(src_ref, h_ref, g_ref, w_ref, cos_ref, sa_ref, sb_ref, *rest):
        o_ref, hn_ref = rest[nr:nr + 2]
        hn_s = rest[2 * nr + 2]
        comm = (*rest[:nr], *rest[nr + 2:2 * nr + 2], *rest[2 * nr + 3:])
        i, j = pl.program_id(0), pl.program_id(1)
        pl.when((i == 0) & (j == 0))(lambda: plan.begin(comm))

        @pl.when(j == 0)
        def _():
            x = h_ref[...]
            hn = (x * lax.rsqrt(jnp.mean(x * x, -1, keepdims=True) + EPS) * g_ref[...]).astype(MXU)
            hn_s[...] = hn
            hn_ref[...] = hn.astype(ACT)

        acc = _dot_nt(hn_s[...], w_ref[...])
        rope = lambda v: _rope(v, cos_ref[...], sa_ref[...], sb_ref[...])

        @pl.when((j == 4) | (j == 5))
        def _():
            o_ref[:, :BLK] = rope(acc[:, :BLK])
            o_ref[:, BLK:] = rope(acc[:, BLK:])

        @pl.when(j == 10)
        def _():
            o_ref[:, :BLK] = rope(acc[:, :BLK])
            o_ref[:, BLK:] = acc[:, BLK:]

        @pl.when((j < 4) | ((j > 5) & (j < 10)))
        def _():
            o_ref[...] = acc

        pl.when((i == t // tm - 1) & (j == 10))(lambda: plan.end(comm))

    tab = pl.BlockSpec((tm, BLK), lambda i, j, s: (i, 0))
    outs = pl.pallas_call(
        body, name="ab_in",
        grid_spec=pltpu.PrefetchScalarGridSpec(
            num_scalar_prefetch=1, grid=(t // tm, 11),
            in_specs=[pl.BlockSpec((tm, D), lambda i, j, s: (i, 0)),
                      pl.BlockSpec((1, D), lambda i, j, s: (0, 0)),
                      pl.BlockSpec((256, D), lambda i, j, s: (s[j], 0)),
                      tab, tab, tab] + plan.specs,
            out_specs=[pl.BlockSpec((tm, 256), lambda i, j, s: (i, j)),
                       pl.BlockSpec((tm, D), lambda i, j, s: (i, 0))] + plan.specs,
            scratch_shapes=[pltpu.VMEM((tm, D), MXU)] + plan.scratch),
        out_shape=[jax.ShapeDtypeStruct((t, 2816), F32), jax.ShapeDtypeStruct((t, D), ACT)] + plan.out_shape,
        compiler_params=_cparams(("arbitrary", "arbitrary")),
    )(src, h, g, w_t, *tables, *ride)
    return outs[0], outs[1], outs[2:]


def _swa_mask(n):
    c = lax.broadcasted_iota(jnp.int32, (3 * BLK, 4 * BLK), 0)
    r = lax.broadcasted_iota(jnp.int32, (3 * BLK, 4 * BLK), 1) & (BLK - 1)
    qpos = n * BLK + r
    bpos = (n - 2) * BLK + c
    meta_ok = (c >= PAD) & (c < BLK) & (qpos - c >= BLK)
    band_ok = (c >= BLK) & (bpos >= PAD) & (qpos >= bpos) & (qpos - bpos < BLK)
    return meta_ok | band_ok


def _swa_keys(kv_ref, n):
    def blk(b):
        return kv_ref[pl.ds(pl.multiple_of(b * BLK, BLK), BLK), :]
    kv = jnp.concatenate([kv_ref[0:BLK, :], blk(jnp.maximum(n - 1, 0)), blk(n)], axis=0)
    lo = lax.broadcasted_iota(jnp.int32, (1, BLK), 1) < HEAD
    out = []
    for part in (kv[:, :BLK], kv[:, BLK:]):
        rolled = pltpu.roll(part, HEAD, 1)
        out.append((jnp.where(lo, part, rolled).astype(MXU), jnp.where(lo, rolled, part).astype(MXU)))
    return out[0], out[1], lo


def _swa_stack(ref, g, lo):
    parts = []
    for p in (2 * g, 2 * g + 1):
        x = ref[:, p * BLK:(p + 1) * BLK]
        parts += [jnp.where(lo, x, 0.0), jnp.where(lo, 0.0, x)]
    return jnp.concatenate(parts, axis=0).astype(MXU)


def _swa_probs(qs, kd, mask, sink_ref, g):
    sink = jnp.concatenate([jnp.full((1, BLK), sink_ref[4 * g + h], F32) for h in range(4)], axis=1)
    s = jnp.where(mask, _dot_nt(kd, qs) * SCALE, NEG)
    m = jnp.maximum(jnp.max(s, 0, keepdims=True), sink)
    e = jnp.exp(s - m)
    e_sink = jnp.exp(sink - m)
    inv = 1.0 / (jnp.sum(e, 0, keepdims=True) + e_sink)
    return e * inv, e_sink * inv


def _swa_unstack(x_t, lo):
    x = x_t.T
    return [jnp.where(lo, x[0:BLK], x[BLK:2 * BLK]), jnp.where(lo, x[2 * BLK:3 * BLK], x[3 * BLK:])]


def _swa_fwd(p0, sinks, ride):
    t = p0.shape[0]
    plan = _GatherPlan(ride)
    nr = plan.n

    def body(sink_ref, q_ref, kv_ref, *rest):
        o_ref = rest[nr]
        comm = (*rest[:nr], *rest[nr + 1:])
        n = pl.program_id(0)
        pl.when(n == 0)(lambda: plan.begin(comm))
        kd, vd, lo = _swa_keys(kv_ref, n)
        mask = _swa_mask(n)
        for g in range(2):
            pr, _ = _swa_probs(_swa_stack(q_ref, g, lo), kd[g], mask, sink_ref, g)
            pairs = _swa_unstack(_dot_tn(vd[g], pr.astype(MXU)), lo)
            for k in range(2):
                p = 2 * g + k
                o_ref[:, p * BLK:(p + 1) * BLK] = pairs[k]
        pl.when(n == t // BLK - 1)(lambda: plan.end(comm))

    outs = pl.pallas_call(
        body, name="swa_fwd", grid=(t // BLK,),
        in_specs=[pl.BlockSpec(memory_space=pltpu.SMEM),
                  pl.BlockSpec((BLK, 512), lambda n: (n, 2)),
                  pl.BlockSpec((t, 256), lambda n: (0, 10))] + plan.specs,
        out_specs=[pl.BlockSpec((BLK, 512), lambda n: (n, 0))] + plan.specs,
        out_shape=[jax.ShapeDtypeStruct((t, 512), F32)] + plan.out_shape,
        scratch_shapes=plan.scratch,
        compiler_params=_cparams(("arbitrary",)),
    )(sinks, p0, p0, *ride)
    return outs[0], outs[1:]


def _conv_window(u_w, w_ref, n_out, first):
    rows = u_w.shape[0]
    acc = None
    for j in range(CONV_W):
        shifted = pltpu.roll(u_w, (rows - (first + j)) % rows, 0)[:n_out]
        term = shifted * w_ref[j:j + 1, :]
        acc = term if acc is None else acc + term
    return acc


def _conv_fwd(p0, conv_w, conv_b, ln_g, ln_b):
    t = p0.shape[0]
    tm = _tile(t, 544)
    hb = tm // HALO

    def body(cur_ref, prev_ref, w_ref, b_ref, g_ref, bb_ref, o_ref):
        i = pl.program_id(0)
        glu = jnp.concatenate([prev_ref[...], cur_ref[...]], axis=0)
        rw = _rows((tm + HALO, 1), i * tm - HALO)
        u_w = jnp.where(rw >= PAD, glu[:, :512] * _sigmoid(glu[:, 512:]), 0.0)
        cv = _conv_window(u_w, w_ref, tm, HALO - (CONV_W - 1)) + b_ref[...]
        xc = cv - jnp.mean(cv, -1, keepdims=True)
        ln = xc * lax.rsqrt(jnp.mean(xc * xc, -1, keepdims=True) + LN_EPS) * g_ref[...] + bb_ref[...]
        o_ref[...] = (ln * _sigmoid(ln)).astype(ACT)

    vec = pl.BlockSpec((1, 512), lambda i: (0, 0))
    return pl.pallas_call(
        body, name="conv_fwd", grid=(t // tm,),
        in_specs=[pl.BlockSpec((tm, D), lambda i: (i, 0)),
                  pl.BlockSpec((HALO, D), lambda i: (jnp.maximum(i * hb - 1, 0), 0)),
                  pl.BlockSpec((CONV_W, 512), lambda i: (0, 0)), vec, vec, vec],
        out_specs=pl.BlockSpec((tm, 512), lambda i: (i, 0)),
        out_shape=jax.ShapeDtypeStruct((t, 512), ACT),
        compiler_params=_cparams(("arbitrary",)),
    )(p0, p0, conv_w, conv_b, ln_g, ln_b)


def _ab_out(h, p0, att, c1, w_pw2, w_out, g_post):
    t = h.shape[0]
    tm = _tile(t, 544)

    def body(h_ref, ga_ref, gb_ref, att_ref, c1_ref, pw_ref, wo_ref, g_ref, h1_ref, y_ref, mix_ref):
        i = pl.program_id(0)
        sga, _ = _silu_and_grad(ga_ref[...])
        sgb, _ = _silu_and_grad(gb_ref[...])
        a = att_ref[...] * sga
        c = _dot(c1_ref[...].astype(MXU), pw_ref[...]) * sgb
        mix = jnp.concatenate([a, c], axis=1).astype(MXU)
        y = _dot(mix, wo_ref[...])
        yn = y * lax.rsqrt(jnp.mean(y * y, -1, keepdims=True) + EPS) * g_ref[...]
        h1_ref[...] = jnp.where(_rows((tm, 1), i * tm) >= PAD, h_ref[...] + yn, 0.0)
        y_ref[...] = y
        mix_ref[...] = mix.astype(ACT)

    row = lambda w, idx: pl.BlockSpec((tm, w), lambda i: (i, idx))
    full = lambda a: pl.BlockSpec(a.shape, lambda i: (0, 0))
    return pl.pallas_call(
        body, name="ab_out", grid=(t // tm,),
        in_specs=[row(D, 0), row(512, 3), row(512, 4), row(512, 0), row(512, 0),
                  full(w_pw2), full(w_out), full(g_post)],
        out_specs=[row(D, 0), row(D, 0), row(D, 0)],
        out_shape=[jax.ShapeDtypeStruct((t, D), F32), jax.ShapeDtypeStruct((t, D), F32),
                   jax.ShapeDtypeStruct((t, D), ACT)],
        compiler_params=_cparams(("arbitrary",)),
    )(h, p0, p0, att, c1, w_pw2, w_out, g_post)


def _sb_in(h, g, w):
    t = h.shape[0]
    tm = _tile(t, 1088)

    def body(h_ref, g_ref, w_ref, o_ref, hn_ref, hn_s):
        @pl.when(pl.program_id(1) == 0)
        def _():
            x = h_ref[...]
            hn = (x * lax.rsqrt(jnp.mean(x * x, -1, keepdims=True) + EPS) * g_ref[...]).astype(MXU)
            hn_s[...] = hn
            hn_ref[...] = hn.astype(ACT)

        o_ref[...] = _dot(hn_s[...], w_ref[...])

    return pl.pallas_call(
        body, name="sb_in", grid=(t // tm, 8),
        in_specs=[pl.BlockSpec((tm, D), lambda i, j: (i, 0)),
                  pl.BlockSpec((1, D), lambda i, j: (0, 0)),
                  pl.BlockSpec((None, D, 512), lambda i, j: (j, 0, 0))],
        out_specs=[pl.BlockSpec((tm, 512), lambda i, j: (i, j)),
                   pl.BlockSpec((tm, D), lambda i, j: (i, 0))],
        out_shape=[jax.ShapeDtypeStruct((t, 4096), F32), jax.ShapeDtypeStruct((t, D), ACT)],
        scratch_shapes=[pltpu.VMEM((tm, D), MXU)],
        compiler_params=_cparams(("arbitrary", "arbitrary")),
    )(h, g, w)


def _split_hi_lo(x):
    hi = x.astype(MXU)
    lo = (x - hi.astype(F32)).astype(MXU)
    return hi, lo


def _scan_matrix(suffix):
    j = lax.broadcasted_iota(jnp.int32, (2 * BLK, 2 * BLK), 0) % BLK
    s = lax.broadcasted_iota(jnp.int32, (2 * BLK, 2 * BLK), 1)
    keep = (s >= BLK) | ((j > s) if suffix else (j < s))
    return jnp.where(keep, 1.0, 0.0).astype(MXU)


def _scan_packed(hi_lo, b, mat):
    cols = slice(b * BLK, (b + 1) * BLK)
    both = _dot(jnp.concatenate([hi_lo[:BLK, cols], hi_lo[BLK:, cols]], axis=1), mat)
    return both[:, :BLK], both[:, BLK:]


KC = 4
GROUPS = 2
GROUPS_FWD = 4


def _sb_logits(qm, kc, valid):
    z = _dot_nt(qm, kc)
    log_beta = jnp.minimum(z, 0.0) - jnp.log(1.0 + jnp.exp(-jnp.abs(z)))
    return log_beta, jnp.where(valid, log_beta - z, 0.0)


def _sb_valid(i, first_key, chunk):
    r = lax.broadcasted_iota(jnp.int32, (BLK, chunk), 0)
    c = lax.broadcasted_iota(jnp.int32, (BLK, chunk), 1)
    kpos = first_key + c
    return (kpos >= PAD) & (kpos < i * BLK + r)


def _sb_walk(i, step_of, init):
    n_full = (i + 1) // KC
    full = step_of(KC)
    carry = lax.fori_loop(0, n_full, lambda s, c: full(s * KC, c), init)
    rest = [lambda c: c] + [functools.partial(step_of(k), n_full * KC) for k in range(1, KC)]
    return lax.switch(i + 1 - n_full * KC, rest, carry)


def _sb_fwd(p1):
    t = p1.shape[0]
    groups = GROUPS_FWD
    w = groups * BLK

    def body(q_ref, k_ref, v_ref, o_ref, lt_ref, k_s, v_s):
        i = pl.program_id(1)

        @pl.when(i == 0)
        def _():
            k_s[...] = k_ref[...].astype(MXU)
            v_s[...] = v_ref[...].astype(MXU)

        lo = lax.broadcasted_iota(jnp.int32, (1, BLK), 1) < HEAD
        qm = []
        for g in range(groups):
            q = q_ref[:, g * BLK:(g + 1) * BLK] * SCALE
            qm += [jnp.where(lo, q, 0.0).astype(MXU), jnp.where(lo, 0.0, q).astype(MXU)]
        mat = _scan_matrix(True)
        heads = range(2 * groups)

        def step_of(kc):
            chunk = kc * BLK

            def step(done, carry):
                start = pl.multiple_of((i + 1 - done - kc) * BLK, BLK)
                valid = _sb_valid(i, start, chunk)
                new, staged = [], []
                for h in heads:
                    lanes = slice((h // 2) * BLK, (h // 2 + 1) * BLK)
                    log_beta, log_1m = _sb_logits(qm[h], k_s[pl.ds(start, chunk), lanes], valid)
                    staged.append((log_beta, jnp.concatenate(_split_hi_lo(log_1m), axis=0)))
                probs = []
                for h in heads:
                    log_beta, hi_lo = staged[h]
                    run = carry[2 * h]
                    parts = [None] * kc
                    for b in reversed(range(kc)):
                        after, total = _scan_packed(hi_lo, b, mat)
                        parts[b] = after + run
                        run = run + total
                    a = jnp.where(valid, jnp.exp(log_beta + jnp.concatenate(parts, axis=1)), 0.0)
                    probs.append((run, a.astype(MXU)))
                for h in heads:
                    lanes = slice((h // 2) * BLK, (h // 2 + 1) * BLK)
                    run, a = probs[h]
                    new += [run, carry[2 * h + 1] + _dot(a, v_s[pl.ds(start, chunk), lanes])]
                return tuple(new)
            return step

        zero = jnp.zeros((BLK, BLK), F32)
        res = _sb_walk(i, step_of, (zero,) * (4 * groups))
        for g in range(groups):
            lanes = slice(g * BLK, (g + 1) * BLK)
            o_ref[:, lanes] = jnp.where(lo, res[4 * g + 1], res[4 * g + 3])
            lt_ref[:, lanes] = jnp.where(lo, res[4 * g], res[4 * g + 2])

    ng = D // w
    blk = pl.BlockSpec((BLK, w), lambda hp, i: (i, hp))
    return pl.pallas_call(
        body, name="sb_fwd", grid=(ng, t // BLK),
        in_specs=[blk,
                  pl.BlockSpec((t, w), lambda hp, i: (0, ng + hp)),
                  pl.BlockSpec((t, w), lambda hp, i: (0, 2 * ng + hp))],
        out_specs=[blk, blk],
        out_shape=[jax.ShapeDtypeStruct((t, D), F32)] * 2,
        scratch_shapes=[pltpu.VMEM((t, w), MXU), pltpu.VMEM((t, w), MXU)],
        compiler_params=_cparams(("arbitrary", "arbitrary")),
    )(p1, p1, p1)


def _sb_out(o, p1, w_out, h1, g_post, tgt):
    t = o.shape[0]
    tm = _tile(t, 544)

    def body(o_ref, g_ref, w_ref, h_ref, gp_ref, t_ref,
             loss_ref, dh_ref, dy_ref, m_ref, do_ref, dg_ref, dgp_ref):
        i = pl.program_id(0)

        @pl.when(i == 0)
        def _():
            loss_ref[...] = jnp.zeros_like(loss_ref)
            dgp_ref[...] = jnp.zeros_like(dgp_ref)

        gate = g_ref[...]
        sg, dsg = _silu_and_grad(gate)
        ov = o_ref[...]
        m = (ov * sg).astype(MXU)
        y = _dot(m, w_ref[...])
        r = lax.rsqrt(jnp.mean(y * y, -1, keepdims=True) + EPS)
        yhat = y * r
        h2 = h_ref[...] + yhat * gp_ref[...]
        diff = jnp.where(_rows((tm, 1), i * tm) >= BLK, h2 - t_ref[...], 0.0)
        loss_ref[...] += jnp.full(loss_ref.shape, 0.5 / D, F32) * jnp.sum(diff * diff)
        dh = diff * (1.0 / D)
        dgp_ref[...] += jnp.sum(dh * yhat, 0, keepdims=True)
        dyn = dh * gp_ref[...]
        dy = (r * (dyn - yhat * jnp.mean(dyn * yhat, -1, keepdims=True))).astype(MXU)
        dm = _dot_nt(dy, w_ref[...])
        dh_ref[...] = dh
        dy_ref[...] = dy.astype(ACT)
        m_ref[...] = m.astype(ACT)
        do_ref[...] = dm * sg
        dg_ref[...] = (dm * ov * dsg).astype(ACT)

    row = lambda idx: pl.BlockSpec((tm, D), lambda i: (i, idx))
    full = lambda a: pl.BlockSpec(a.shape, lambda i: (0, 0))
    acc = lambda s: pl.BlockSpec(s, lambda i: (0, 0))
    return pl.pallas_call(
        body, name="sb_out", grid=(t // tm,),
        in_specs=[row(0), row(3), full(w_out), row(0), full(g_post), row(0)],
        out_specs=[acc((8, BLK)), row(0), row(0), row(0), row(0), row(0), acc((1, D))],
        out_shape=[jax.ShapeDtypeStruct((8, BLK), F32), jax.ShapeDtypeStruct((t, D), F32),
                   jax.ShapeDtypeStruct((t, D), ACT), jax.ShapeDtypeStruct((t, D), ACT),
                   jax.ShapeDtypeStruct((t, D), F32), jax.ShapeDtypeStruct((t, D), ACT),
                   jax.ShapeDtypeStruct((1, D), F32)],
        compiler_params=_cparams(("arbitrary",)),
    )(o, p1, w_out, h1, g_post, tgt)


def _sb_bwd(p1, ltot, do):
    t = p1.shape[0]
    nb = t // BLK

    w = GROUPS * BLK

    def body(q_ref, k_ref, v_ref, lt_ref, do_ref, dq_ref, dk_ref, dv_ref, k_s, v_s, dk_s, dv_s):
        i = pl.program_id(1)
        lo = lax.broadcasted_iota(jnp.int32, (1, BLK), 1) < HEAD

        @pl.when(i == 0)
        def _():
            k_s[...] = k_ref[...].astype(MXU)
            v_s[...] = v_ref[...].astype(MXU)
            dk_s[...] = jnp.zeros_like(dk_s)
            dv_s[...] = jnp.zeros_like(dv_s)

        qm, dom, row_total, q2, do2 = [], [], [], [], []
        for g in range(GROUPS):
            lanes = slice(g * BLK, (g + 1) * BLK)
            q, dout, lt = q_ref[:, lanes] * SCALE, do_ref[:, lanes], lt_ref[:, lanes]
            qm += [jnp.where(lo, q, 0.0).astype(MXU), jnp.where(lo, 0.0, q).astype(MXU)]
            dom += [jnp.where(lo, dout, 0.0).astype(MXU), jnp.where(lo, 0.0, dout).astype(MXU)]
            q2.append(jnp.concatenate(qm[-2:], axis=0))
            do2.append(jnp.concatenate(dom[-2:], axis=0))
            lt_r = pltpu.roll(lt, HEAD, 1)
            row_total += [jnp.where(lo, lt, lt_r), jnp.where(lo, lt_r, lt)]
        mat_l = _scan_matrix(True)
        mat_g = _scan_matrix(False)
        heads = range(2 * GROUPS)

        def step_of(kc):
            chunk = kc * BLK

            def step(done, carry):
                start = pl.multiple_of(done * BLK, BLK)
                keys = lambda ref, h: ref[pl.ds(start, chunk), (h // 2) * BLK:(h // 2 + 1) * BLK]
                valid = _sb_valid(i, start, chunk)
                new, dzs, probs, st1, st2, st3 = [], [], [], [], [], []
                for h in heads:
                    log_beta, log_1m = _sb_logits(qm[h], keys(k_s, h), valid)
                    st1.append((log_beta, jnp.concatenate(_split_hi_lo(log_1m), axis=0),
                                _dot_nt(dom[h], keys(v_s, h))))
                for h in heads:
                    log_beta, hi_lo, da = st1[h]
                    run = carry[3 * h]
                    parts = []
                    for b in range(kc):
                        after, total = _scan_packed(hi_lo, b, mat_l)
                        run = run + total
                        parts.append(after + (row_total[h] - run))
                    a = jnp.where(valid, jnp.exp(log_beta + jnp.concatenate(parts, axis=1)), 0.0)
                    g = da * a
                    probs.append(a.astype(MXU))
                    st2.append((run, g, jnp.concatenate(_split_hi_lo(g), axis=0)))
                for h in heads:
                    run, g, hi_lo = st2[h]
                    run_g = carry[3 * h + 1]
                    parts = []
                    for b in range(kc):
                        before, total_g = _scan_packed(hi_lo, b, mat_g)
                        parts.append(before + run_g)
                        run_g = run_g + total_g
                    sig = jnp.exp(st1[h][0])
                    dz = jnp.where(valid, g - sig * (g + jnp.concatenate(parts, axis=1)), 0.0)
                    dzs.append(dz.astype(MXU))
                    st3.append((run, run_g))
                for h in heads:
                    new += [*st3[h], carry[3 * h + 2] + _dot(dzs[h], keys(k_s, h))]
                for g in range(GROUPS):
                    lanes = slice(g * BLK, (g + 1) * BLK)
                    dk_s[pl.ds(start, chunk), lanes] += _dot_tn(jnp.concatenate(dzs[2 * g:2 * g + 2], axis=0), q2[g])
                    dv_s[pl.ds(start, chunk), lanes] += _dot_tn(jnp.concatenate(probs[2 * g:2 * g + 2], axis=0), do2[g])
                return tuple(new)
            return step

        zero = jnp.zeros((BLK, BLK), F32)
        res = _sb_walk(i, step_of, (zero,) * (6 * GROUPS))
        for g in range(GROUPS):
            dq = jnp.where(lo, res[6 * g + 2], res[6 * g + 5])
            dq_ref[:, g * BLK:(g + 1) * BLK] = (dq * SCALE).astype(ACT)

        @pl.when(i == nb - 1)
        def _():
            dk_ref[...] = dk_s[...].astype(ACT)
            dv_ref[...] = dv_s[...].astype(ACT)

    ng = D // w
    blk = pl.BlockSpec((BLK, w), lambda hp, i: (i, hp))
    col = lambda off: pl.BlockSpec((t, w), lambda hp, i: (0, off + hp))
    return pl.pallas_call(
        body, name="sb_bwd", grid=(ng, nb),
        in_specs=[blk, col(ng), col(2 * ng), blk, blk],
        out_specs=[blk, col(0), col(0)],
        out_shape=[jax.ShapeDtypeStruct((t, D), ACT)] * 3,
        scratch_shapes=[pltpu.VMEM((t, w), MXU), pltpu.VMEM((t, w), MXU),
                        pltpu.VMEM((t, w), F32), pltpu.VMEM((t, w), F32)],
        compiler_params=_cparams(("arbitrary", "arbitrary")),
    )(p1, p1, p1, ltot, do)


def _norm_bwd(x, g, dy, eps):
    r = lax.rsqrt(jnp.mean(x * x, -1, keepdims=True) + eps)
    xhat = x * r
    dxn = dy * g
    return r * (dxn - xhat * jnp.mean(dxn * xhat, -1, keepdims=True)), jnp.sum(dy * xhat, 0, keepdims=True)


def _sb_in_bwd(dparts, w_sb, h1, g_pre1, dh2):
    t = h1.shape[0]
    tm = _tile(t, 544)

    def body(dq_ref, dk_ref, dv_ref, dg_ref, w_ref, h_ref, g_ref, dh2_ref, dh1_ref, dgn_ref):
        @pl.when(pl.program_id(0) == 0)
        def _():
            dgn_ref[...] = jnp.zeros_like(dgn_ref)

        dhn = None
        for a, ref in enumerate((dq_ref, dk_ref, dv_ref, dg_ref)):
            for b in range(2):
                term = _dot_nt(ref[:, b * 512:(b + 1) * 512].astype(MXU), w_ref[2 * a + b])
                dhn = term if dhn is None else dhn + term
        dx, dg = _norm_bwd(h_ref[...], g_ref[...], dhn, EPS)
        dgn_ref[...] += dg
        dh1_ref[...] = dh2_ref[...] + dx

    row = pl.BlockSpec((tm, D), lambda i: (i, 0))
    vec = pl.BlockSpec((1, D), lambda i: (0, 0))
    return pl.pallas_call(
        body, name="sb_in_bwd", grid=(t // tm,),
        in_specs=[row, row, row, row, pl.BlockSpec(w_sb.shape, lambda i: (0, 0, 0)), row, vec, row],
        out_specs=[row, vec],
        out_shape=[jax.ShapeDtypeStruct((t, D), F32), jax.ShapeDtypeStruct((1, D), F32)],
        compiler_params=_cparams(("arbitrary",)),
    )(*dparts, w_sb, h1, g_pre1, dh2)


def _ab_out_bwd(dh1, y0, g_post0, w_out, p0, att, c1, w_pw2):
    t = dh1.shape[0]
    tm = _tile(t, 544)

    def body(dh1_ref, y_ref, g0_ref, wo_ref, ga_ref, gb_ref, att_ref, c1_ref, pw_ref,
             dy_ref, dga_ref, dgb_ref, datt_ref, dc1_ref, dc2_ref, dg0_ref):
        @pl.when(pl.program_id(0) == 0)
        def _():
            dg0_ref[...] = jnp.zeros_like(dg0_ref)

        dy, dg = _norm_bwd(y_ref[...], g0_ref[...], dh1_ref[...], EPS)
        dg0_ref[...] += dg
        dy = dy.astype(MXU)
        dy_ref[...] = dy.astype(ACT)
        dmix = _dot_nt(dy, wo_ref[...])
        da, dc = dmix[:, :512], dmix[:, 512:]
        sga, dsga = _silu_and_grad(ga_ref[...])
        sgb, dsgb = _silu_and_grad(gb_ref[...])
        datt_ref[...] = da * sga
        dga_ref[...] = (da * att_ref[...] * dsga).astype(ACT)
        c2 = _dot(c1_ref[...].astype(MXU), pw_ref[...])
        dc2 = (dc * sgb).astype(MXU)
        dgb_ref[...] = (dc * c2 * dsgb).astype(ACT)
        dc2_ref[...] = dc2.astype(ACT)
        dc1_ref[...] = _dot_nt(dc2, pw_ref[...])

    row = lambda w, idx: pl.BlockSpec((tm, w), lambda i: (i, idx))
    full = lambda a: pl.BlockSpec(a.shape, lambda i: (0, 0))
    sd = jax.ShapeDtypeStruct
    return pl.pallas_call(
        body, name="ab_out_bwd", grid=(t // tm,),
        in_specs=[row(D, 0), row(D, 0), full(g_post0), full(w_out), row(512, 3), row(512, 4),
                  row(512, 0), row(512, 0), full(w_pw2)],
        out_specs=[row(D, 0), row(512, 0), row(512, 0), row(512, 0), row(512, 0), row(512, 0),
                   pl.BlockSpec((1, D), lambda i: (0, 0))],
        out_shape=[sd((t, D), ACT), sd((t, 512), ACT), sd((t, 512), ACT), sd((t, 512), F32),
                   sd((t, 512), F32), sd((t, 512), ACT), sd((1, D), F32)],
        compiler_params=_cparams(("arbitrary",)),
    )(dh1, y0, g_post0, w_out, p0, p0, att, c1, w_pw2)


def _conv_bwd(p0, dc1, conv_w, conv_b, ln_g, ln_b, ride):
    t = p0.shape[0]
    tm = _tile(t, 544)
    hb = tm // HALO
    last = t // HALO - 1
    plan = _ChipsPlan(ride)
    nr = plan.n

    def body(cur_ref, prev_ref, next_ref, d_ref, dn_ref, w_ref, b_ref, g_ref, bb_ref, *rest):
        dglu_ref, dw_ref, db_ref, dlg_ref, dlb_ref = rest[nr:nr + 5]
        comm = (*rest[:nr], *rest[nr + 5:])
        i = pl.program_id(0)
        pl.when(i == 0)(lambda: plan.begin(comm))

        @pl.when(i == 0)
        def _():
            for ref in (dw_ref, db_ref, dlg_ref, dlb_ref):
                ref[...] = jnp.zeros_like(ref)

        glu = jnp.concatenate([prev_ref[...], cur_ref[...], next_ref[...]], axis=0)
        rw = _rows((tm + 2 * HALO, 1), i * tm - HALO)
        ga, sg = glu[:, :512], _sigmoid(glu[:, 512:])
        u_w = jnp.where((rw >= PAD) & (rw < t), ga * sg, 0.0)
        n_cv = tm + HALO
        cv = _conv_window(u_w, w_ref, n_cv, HALO - (CONV_W - 1)) + b_ref[...]
        xc = cv - jnp.mean(cv, -1, keepdims=True)
        rstd = lax.rsqrt(jnp.mean(xc * xc, -1, keepdims=True) + LN_EPS)
        cvhat = xc * rstd
        ln = cvhat * g_ref[...] + bb_ref[...]
        _, dsl = _silu_and_grad(ln)
        rc = _rows((n_cv, 1), i * tm)
        dc = jnp.concatenate([d_ref[...], dn_ref[...]], axis=0)
        dln = jnp.where(rc < t, dc * dsl, 0.0)
        dhat = dln * g_ref[...]
        dcv = rstd * (dhat - jnp.mean(dhat, -1, keepdims=True)
                      - cvhat * jnp.mean(dhat * cvhat, -1, keepdims=True))
        own = dcv[:tm]
        dlg_ref[...] += jnp.sum((dln * cvhat)[:tm], 0, keepdims=True)
        dlb_ref[...] += jnp.sum(dln[:tm], 0, keepdims=True)
        db_ref[...] += jnp.sum(own, 0, keepdims=True)
        rows = tm + 2 * HALO
        du = None
        for j in range(CONV_W):
            first = HALO - (CONV_W - 1) + j
            shifted = pltpu.roll(u_w, (rows - first) % rows, 0)[:tm]
            dw_ref[j:j + 1, :] += jnp.sum(own * shifted, 0, keepdims=True)
            back = pltpu.roll(dcv, (n_cv - (CONV_W - 1 - j)) % n_cv, 0)[:tm]
            term = back * w_ref[j:j + 1, :]
            du = term if du is None else du + term
        du = jnp.where(_rows((tm, 1), i * tm) >= PAD, du, 0.0)
        ga_c, sg_c = ga[HALO:HALO + tm], sg[HALO:HALO + tm]
        dglu_ref[:, :512] = (du * sg_c).astype(ACT)
        dglu_ref[:, 512:] = (du * ga_c * sg_c * (1.0 - sg_c)).astype(ACT)
        pl.when(i == t // tm - 1)(lambda: plan.end(comm))

    vec = pl.BlockSpec((1, 512), lambda i: (0, 0))
    nxt = lambda i: (jnp.minimum((i + 1) * hb, last), 0)
    outs = pl.pallas_call(
        body, name="conv_bwd", grid=(t // tm,),
        in_specs=[pl.BlockSpec((tm, D), lambda i: (i, 0)),
                  pl.BlockSpec((HALO, D), lambda i: (jnp.maximum(i * hb - 1, 0), 0)),
                  pl.BlockSpec((HALO, D), nxt),
                  pl.BlockSpec((tm, 512), lambda i: (i, 0)),
                  pl.BlockSpec((HALO, 512), nxt),
                  pl.BlockSpec((CONV_W, 512), lambda i: (0, 0)), vec, vec, vec] + plan.specs,
        out_specs=[pl.BlockSpec((tm, D), lambda i: (i, 0)),
                   pl.BlockSpec((HALO, 512), lambda i: (0, 0)), vec, vec, vec] + plan.specs,
        out_shape=[jax.ShapeDtypeStruct((t, D), ACT), jax.ShapeDtypeStruct((HALO, 512), F32)]
        + [jax.ShapeDtypeStruct((1, 512), F32)] * 3 + plan.out_shape,
        scratch_shapes=plan.scratch,
        compiler_params=_cparams(("arbitrary",)),
    )(p0, p0, p0, dc1, dc1, conv_w, conv_b, ln_g, ln_b, *ride)
    return outs[:5], outs[5:]


def _swa_bwd(p0, datt, sinks, tables, ride):
    t = p0.shape[0]
    nb = t // BLK
    plan = _ChipsPlan(ride)
    nr = plan.n

    def body(sink_ref, q_ref, kv_ref, d_ref, cos_ref, sa_ref, sb_ref, *rest):
        dq_ref, dkv_ref, ds_ref = rest[nr:nr + 3]
        acc = rest[2 * nr + 3]
        comm = (*rest[:nr], *rest[nr + 3:2 * nr + 3], *rest[2 * nr + 4:])
        n = pl.program_id(0)
        pl.when(n == 0)(lambda: plan.begin(comm))

        @pl.when(n == 0)
        def _():
            acc[...] = jnp.zeros_like(acc)
            ds_ref[...] = jnp.zeros_like(ds_ref)

        kd, vd, lo = _swa_keys(kv_ref, n)
        mask = _swa_mask(n)
        row0 = pl.multiple_of(n * BLK, BLK)
        tabs = [r[pl.ds(row0, BLK), :] for r in (cos_ref, sa_ref, sb_ref)]
        dk_g, dv_g, st1, st2 = [], [], [], []
        for g in range(2):
            qs, dos = _swa_stack(q_ref, g, lo), _swa_stack(d_ref, g, lo)
            pr, p_sink = _swa_probs(qs, kd[g], mask, sink_ref, g)
            st1.append((qs, dos, pr, p_sink, _dot_nt(vd[g], dos)))
        for g in range(2):
            qs, dos, pr, p_sink, dpr = st1[g]
            delta = jnp.sum(pr * dpr, 0, keepdims=True)
            st2.append((pr * (dpr - delta) * SCALE).astype(MXU))
            sunk = p_sink * delta
            for h in range(4):
                row = 4 * g + h
                ds_ref[row:row + 1, :] += jnp.full((1, BLK), -1.0, F32) * jnp.sum(sunk[:, h * BLK:(h + 1) * BLK])
        for g in range(2):
            qs, dos, pr, _, _ = st1[g]
            dsc = st2[g]
            pairs = _swa_unstack(_dot_tn(kd[g], dsc), lo)
            for k in range(2):
                p = 2 * g + k
                dq_ref[:, p * BLK:(p + 1) * BLK] = _unrope(pairs[k], *tabs).astype(ACT)
            dk_g.append(_dot(dsc, qs))
            dv_g.append(_dot(pr.astype(MXU), dos))
        fold = lambda a: a + pltpu.roll(a, HEAD, 1)
        dk = jnp.where(lo, fold(dk_g[0]), fold(dk_g[1]))
        dv = jnp.where(lo, fold(dv_g[0]), fold(dv_g[1]))
        dkv = jnp.concatenate([dk, dv], axis=1)
        prev = pl.multiple_of(jnp.maximum(n - 1, 0) * BLK, BLK)
        acc[0:BLK, :] += dkv[0:BLK]
        acc[pl.ds(prev, BLK), :] += dkv[BLK:2 * BLK]
        acc[pl.ds(row0, BLK), :] += dkv[2 * BLK:]

        @pl.when(n == nb - 1)
        def _():
            dkv_ref[:, :BLK] = _unrope(acc[:, :BLK], cos_ref[...], sa_ref[...], sb_ref[...]).astype(ACT)
            dkv_ref[:, BLK:] = acc[:, BLK:].astype(ACT)

        pl.when(n == nb - 1)(lambda: plan.end(comm))

    tab = pl.BlockSpec((t, BLK), lambda n: (0, 0))
    outs = pl.pallas_call(
        body, name="swa_bwd", grid=(nb,),
        in_specs=[pl.BlockSpec(memory_space=pltpu.SMEM),
                  pl.BlockSpec((BLK, 512), lambda n: (n, 2)),
                  pl.BlockSpec((t, 256), lambda n: (0, 10)),
                  pl.BlockSpec((BLK, 512), lambda n: (n, 0)), tab, tab, tab] + plan.specs,
        out_specs=[pl.BlockSpec((BLK, 512), lambda n: (n, 0)),
                   pl.BlockSpec((t, 256), lambda n: (0, 0)),
                   pl.BlockSpec((8, BLK), lambda n: (0, 0))] + plan.specs,
        out_shape=[jax.ShapeDtypeStruct((t, 512), ACT), jax.ShapeDtypeStruct((t, 256), ACT),
                   jax.ShapeDtypeStruct((8, BLK), F32)] + plan.out_shape,
        scratch_shapes=[pltpu.VMEM((t, 256), F32)] + plan.scratch,
        compiler_params=_cparams(("arbitrary",)),
    )(sinks, p0, p0, datt, *tables, *ride)
    return outs[0], outs[1], outs[2], outs[3:]


def _ab_in_bwd(dp0, w_t, h0, g_pre, dh1, ride):
    t = h0.shape[0]
    tm = _tile(t, 544)
    plan = _ChipsPlan(ride)
    nr = plan.n

    pieces = [(p[0], p[1] * dp0.tw, p[2] * dp0.tw) for p in dp0.pieces]

    def body(*refs):
        d_refs = refs[:5]
        w_ref, h_ref, g_ref, dh1_ref = refs[5:9]
        rest = refs[9:]
        dh0_ref, dg_ref = rest[nr:nr + 2]
        comm = (*rest[:nr], *rest[nr + 2:])
        i = pl.program_id(0)
        pl.when(i == 0)(lambda: plan.begin(comm))

        @pl.when(i == 0)
        def _():
            dg_ref[...] = jnp.zeros_like(dg_ref)

        dhn = None
        for ref, (_, first, rows) in zip(d_refs, pieces):
            term = _dot(ref[...].astype(MXU), w_ref[first:first + rows, :])
            dhn = term if dhn is None else dhn + term
        dx, dg = _norm_bwd(h_ref[...], g_ref[...], dhn, EPS)
        dg_ref[...] += dg
        dh0_ref[...] = dh1_ref[...] + dx
        pl.when(i == t // tm - 1)(lambda: plan.end(comm))

    row = pl.BlockSpec((tm, D), lambda i: (i, 0))
    vec = pl.BlockSpec((1, D), lambda i: (0, 0))
    outs = pl.pallas_call(
        body, name="ab_in_bwd", grid=(t // tm,),
        in_specs=[pl.BlockSpec((tm, rows), lambda i: (i, 0)) for _, _, rows in pieces] + [
            pl.BlockSpec(w_t.shape, lambda i: (0, 0)), row, vec, row] + plan.specs,
        out_specs=[row, vec] + plan.specs,
        out_shape=[jax.ShapeDtypeStruct((t, D), F32), jax.ShapeDtypeStruct((1, D), F32)] + plan.out_shape,
        scratch_shapes=plan.scratch,
        compiler_params=_cparams(("arbitrary",)),
    )(*dp0.arrays, w_t, h0, g_pre, dh1, *ride)
    return outs[0], outs[1], outs[2:]


def _dw_plain(a, b, name):
    t, m = a.shape
    n = b.shape[1]
    tm = _tile(t, 1088)
    tn = min(n, 512)
    nk = t // tm

    def body(a_ref, b_ref, o_ref, acc):
        k = pl.program_id(1)

        @pl.when(k == 0)
        def _():
            acc[...] = jnp.zeros_like(acc)

        acc[...] += _dot_tn(a_ref[...].astype(MXU), b_ref[...].astype(MXU))

        @pl.when(k == nk - 1)
        def _():
            o_ref[...] = acc[...].astype(WIRE)

    return pl.pallas_call(
        body, name=name, grid=(n // tn, nk),
        in_specs=[pl.BlockSpec((tm, m), lambda j, k: (k, 0)),
                  pl.BlockSpec((tm, tn), lambda j, k: (k, j))],
        out_specs=pl.BlockSpec((m, tn), lambda j, k: (0, j)),
        out_shape=jax.ShapeDtypeStruct((m, n), WIRE),
        scratch_shapes=[pltpu.VMEM((m, tn), F32)],
        compiler_params=_cparams(("arbitrary", "arbitrary")),
    )(a, b)


def _dw_chunks(hn, dp, name):
    t = hn.shape[0]
    tm = _tile(t, 1088)
    nk = t // tm
    nt, tw = dp.n_tiles, dp.tw
    n_in = len(dp.arrays)

    def body(*refs):
        d_refs = refs[:n_in]
        h_ref, o_ref, acc = refs[n_in:]
        j, k = pl.program_id(0), pl.program_id(1)

        @pl.when(k == 0)
        def _():
            acc[...] = jnp.zeros_like(acc)

        def add(ref):
            acc[...] += _dot_tn(h_ref[...].astype(MXU), ref[...].astype(MXU))
        dp.apply(j, d_refs, add)

        @pl.when(k == nk - 1)
        def _():
            o_ref[...] = acc[...].astype(WIRE)

    return pl.pallas_call(
        body, name=name, grid=(nt, nk),
        in_specs=dp.specs(tm, lambda j, k: k, lambda j, k: j) + [
            pl.BlockSpec((tm, D), lambda j, k: (k, 0))],
        out_specs=pl.BlockSpec((None, D, tw), lambda j, k: (j, 0, 0)),
        out_shape=jax.ShapeDtypeStruct((nt, D, tw), WIRE),
        scratch_shapes=[pltpu.VMEM((D, tw), F32)],
        compiler_params=_cparams(("arbitrary", "arbitrary")),
    )(*dp.arrays, hn)


def _dw_transposed(dp, hn, name):
    t = hn.shape[0]
    tm = _tile(t, 1088)
    nk = t // tm
    nt, tw = dp.n_tiles, dp.tw
    n_in = len(dp.arrays)

    def body(*refs):
        d_refs = refs[:n_in]
        h_ref, o_ref, acc = refs[n_in:]
        j, k = pl.program_id(0), pl.program_id(1)

        @pl.when(k == 0)
        def _():
            acc[...] = jnp.zeros_like(acc)

        def add(ref):
            acc[...] += _dot_tn(ref[...].astype(MXU), h_ref[...].astype(MXU))
        dp.apply(j, d_refs, add)

        @pl.when(k == nk - 1)
        def _():
            o_ref[...] = acc[...].astype(WIRE)

    return pl.pallas_call(
        body, name=name, grid=(nt, nk),
        in_specs=dp.specs(tm, lambda j, k: k, lambda j, k: j) + [
            pl.BlockSpec((tm, D), lambda j, k: (k, 0))],
        out_specs=pl.BlockSpec((tw, D), lambda j, k: (j, 0)),
        out_shape=jax.ShapeDtypeStruct((nt * tw, D), WIRE),
        scratch_shapes=[pltpu.VMEM((tw, D), F32)],
        compiler_params=_cparams(("arbitrary", "arbitrary")),
    )(*dp.arrays, hn)


def kernel(x, meta_tokens, ab_pre_norm, ab_w_in, ab_sinks, ab_conv_w, ab_conv_b, ab_conv_ln_g, ab_conv_ln_b, ab_w_pw2, ab_w_out, ab_post_norm, sb_pre_norm, sb_w_in, sb_w_out, sb_post_norm, loss_target, m_meta_tokens, m_ab_pre_norm, m_ab_w_in, m_ab_sinks, m_ab_conv_w, m_ab_conv_b, m_ab_conv_ln_g, m_ab_conv_ln_b, m_ab_w_pw2, m_ab_w_out, m_ab_post_norm, m_sb_pre_norm, m_sb_w_in, m_sb_w_out, m_sb_post_norm, v_meta_tokens, v_ab_pre_norm, v_ab_w_in, v_ab_sinks, v_ab_conv_w, v_ab_conv_b, v_ab_conv_ln_g, v_ab_conv_ln_b, v_ab_w_pw2, v_ab_w_out, v_ab_post_norm, v_sb_pre_norm, v_sb_w_in, v_sb_w_out, v_sb_post_norm):
    seq = x.shape[1]
    t = seq + BLK
    mx, my, mc = _coords()
    me = 4 * mx + 2 * my + mc
    pos = jnp.stack([mx, my, mc, me]).astype(jnp.int32)

    w_ab_t, *small = _all_gather(
        [ab_w_in[0].T.astype(WIRE), meta_tokens, ab_conv_w[0], sb_pre_norm, sb_post_norm], "gather_first")
    w_ab_t = w_ab_t.reshape(2816, D)
    meta_full = jnp.moveaxis(small[0], 0, 1).reshape(N_META, D)
    conv_w = jnp.moveaxis(small[1], 0, 1).reshape(CONV_W, 512)
    sb_pre = jnp.moveaxis(small[2], 0, 1).reshape(1, D)
    sb_post = jnp.moveaxis(small[3], 0, 1).reshape(1, D)

    h0 = jnp.concatenate([jnp.zeros((PAD, D), F32), meta_full, x[0]], axis=0)
    tgt = jnp.concatenate([jnp.zeros((BLK, D), F32), loss_target[0]], axis=0)
    tables = _rope_tables(t)
    sinks = ab_sinks[0]

    p0, hn0, (w_oa, w_pw, w_os) = _ab_in(
        h0, ab_pre_norm, w_ab_t, tables,
        [ab_w_out[0].astype(WIRE), ab_w_pw2[0].astype(WIRE), sb_w_out[0].astype(WIRE)])
    w_oa, w_os, w_pw = w_oa.reshape(D, D), w_os.reshape(D, D), w_pw.reshape(512, 512)
    att, (w_sb,) = _swa_fwd(p0, sinks, [sb_w_in[0].astype(WIRE)])
    c1 = _conv_fwd(p0, conv_w, ab_conv_b, ab_conv_ln_g, ab_conv_ln_b)
    h1, y0, mix = _ab_out(h0, p0, att, c1, w_pw, w_oa, ab_post_norm)
    p1, hn1 = _sb_in(h1, sb_pre, w_sb)
    o, ltot = _sb_fwd(p1)
    loss_part, dh2, dy1, m1, do, dgate, dg_sb_post = _sb_out(o, p1, w_os, h1, sb_post, tgt)

    dq1, dk1, dv1 = _sb_bwd(p1, ltot, do)
    dp1 = _Cols([(dq1, 0, 2), (dk1, 2, 2), (dv1, 4, 2), (dgate, 6, 2)], 512)

    def sibling_stage(parts, names, tag):
        got = _exchange_sibling(parts, "reduce_sibling_" + tag)
        return [_add_sibling(pos, p, r, "add_sibling_" + nm) for p, r, nm in zip(parts, got, names)]

    def finish(sums, got, names):
        return [_sum_chips(pos, s, r, "sum_chips_" + nm) for s, r, nm in zip(sums, got, names)]

    names1 = ["sb_in", "sb_out"]
    sums1 = sibling_stage([_dw_chunks(hn1, dp1, "dw_sb_in").reshape(4, 2, D, 512),
                           _dw_plain(m1, dy1, "dw_sb_out").reshape(4, 2, BLK, D)], names1, "sb")
    dh1, dg_sb_pre = _sb_in_bwd([dq1, dk1, dv1, dgate], w_sb, h1, sb_pre, dh2)
    dy0, dga, dgb, datt, dc1, dc2, dg_ab_post = _ab_out_bwd(dh1, y0, ab_post_norm, w_oa, p0, att, c1, w_pw)
    names2 = ["ab_out", "pw2"]
    sums2 = sibling_stage([_dw_plain(mix, dy0, "dw_ab_out").reshape(4, 2, BLK, D),
                           _dw_plain(c1, dc2, "dw_pw2").reshape(4, 2, 64, 512)], names2, "ab_out")
    (dglu, dconv_w, dconv_b, dln_g, dln_b), got1 = _conv_bwd(
        p0, dc1, conv_w, ab_conv_b, ab_conv_ln_g, ab_conv_ln_b, sums1)
    dq0, dkv0, dsinks, got2 = _swa_bwd(p0, datt, sinks, tables, sums2)
    g_sb_w_in, g_sb_w_out = finish(sums1, got1, names1)
    g_ab_w_out, g_ab_w_pw2 = finish(sums2, got2, names2)
    dp0 = _Cols([(dq0, 0, 2), (dkv0, 2, 1), (dga, 3, 2), (dglu, 5, 4), (dgb, 9, 2)], 256)

    sums0 = sibling_stage([_dw_transposed(dp0, hn0, "dw_ab_in").reshape(4, 2, 352, D)], ["ab_in"], "ab_in")
    dh0, dg_ab_pre, got0 = _ab_in_bwd(dp0, w_ab_t, h0, ab_pre_norm, dh1, sums0)
    g_ab_w_in = finish(sums0, got0, ["ab_in"])[0].T

    small_parts = [dh0[PAD:BLK], dg_ab_pre, dsinks, dconv_w, dconv_b, dln_g, dln_b,
                   dg_ab_post, dg_sb_pre, dg_sb_post, loss_part]
    red = _reduce_small(_all_gather(small_parts, "gather_small_grads"), "reduce_small")
    col = lambda a, w: lax.dynamic_slice_in_dim(a, me * w, w, axis=1)
    g_meta = col(red[0], BLK)
    g_ab_pre = red[1]
    g_sinks = red[2][:, 0].reshape(1, 8)
    g_conv_w = col(red[3][:CONV_W], 64)
    g_conv_b, g_ln_g, g_ln_b, g_ab_post = red[4], red[5], red[6], red[7]
    g_sb_pre, g_sb_post = col(red[8], BLK), col(red[9], BLK)

    loss = red[10][0, 0]
    grad_x = dh0[BLK:][None]

    weights = [meta_tokens, ab_pre_norm, ab_w_in[0], ab_sinks, ab_conv_w[0], ab_conv_b, ab_conv_ln_g,
               ab_conv_ln_b, ab_w_pw2[0], ab_w_out[0], ab_post_norm, sb_pre_norm, sb_w_in[0],
               sb_w_out[0], sb_post_norm]
    grads = [g_meta, g_ab_pre, g_ab_w_in, g_sinks, g_conv_w, g_conv_b, g_ln_g, g_ln_b, g_ab_w_pw2,
             g_ab_w_out, g_ab_post, g_sb_pre, g_sb_w_in, g_sb_w_out, g_sb_post]
    ms = [m_meta_tokens, m_ab_pre_norm, m_ab_w_in[0], m_ab_sinks, m_ab_conv_w[0], m_ab_conv_b,
          m_ab_conv_ln_g, m_ab_conv_ln_b, m_ab_w_pw2[0], m_ab_w_out[0], m_ab_post_norm,
          m_sb_pre_norm, m_sb_w_in[0], m_sb_w_out[0], m_sb_post_norm]
    vs = [v_meta_tokens, v_ab_pre_norm, v_ab_w_in[0], v_ab_sinks, v_ab_conv_w[0], v_ab_conv_b,
          v_ab_conv_ln_g, v_ab_conv_ln_b, v_ab_w_pw2[0], v_ab_w_out[0], v_ab_post_norm,
          v_sb_pre_norm, v_sb_w_in[0], v_sb_w_out[0], v_sb_post_norm]
    lead = [w.ndim == 3 for w in (meta_tokens, ab_pre_norm, ab_w_in, ab_sinks, ab_conv_w, ab_conv_b,
                                   ab_conv_ln_g, ab_conv_ln_b, ab_w_pw2, ab_w_out, ab_post_norm,
                                   sb_pre_norm, sb_w_in, sb_w_out, sb_post_norm)]
    big_ids = [2, 8, 9, 12, 13]
    small_ids = [i for i in range(15) if i not in big_ids]
    deltas, new_m, new_v = [None] * 15, [None] * 15, [None] * 15
    for ids, nm in ((small_ids, "adamw_small"), (big_ids, "adamw_big")):
        d_, m_, v_ = _adamw([weights[i] for i in ids], [grads[i] for i in ids],
                            [ms[i] for i in ids], [vs[i] for i in ids], nm)
        for k, i in enumerate(ids):
            deltas[i], new_m[i], new_v[i] = d_[k], m_[k], v_[k]
    fix = lambda arrs: [a[None] if l else a for a, l in zip(arrs, lead)]
    return (loss, grad_x, *fix(grads), *fix(deltas), *fix(new_m), *fix(new_v))
```

```python
import functools

import numpy as np
import jax
import jax.numpy as jnp
from jax import lax
from jax.experimental import pallas as pl
from jax.experimental.pallas import tpu as pltpu

F32 = jnp.float32
MXU = jnp.bfloat16
ACT = jnp.bfloat16
WIRE = jnp.bfloat16

D = 1024
N_META = 16
BLK = 128
PAD = BLK - N_META
HEAD = 64
NEG = -1e30
EPS = 1e-6
LN_EPS = 1e-5
ROPE_THETA = 10000.0
SCALE = HEAD ** -0.5
CONV_W = 31
HALO = 32
LR, B1, B2, ADAM_EPS, WD, STEP = 0.001, 0.9, 0.999, 1e-08, 0.01, 10
VMEM_LIMIT = 56 * 1024 * 1024
MESH = pl.DeviceIdType.MESH

P0_SRC = (5, 6, 7, 8, 0, 1, 3, 4, 9, 10, 2)


def _cparams(sem=None):
    return pltpu.CompilerParams(dimension_semantics=sem, vmem_limit_bytes=VMEM_LIMIT)


def _tile(t, pref):
    for cand in (1088, 544, 272, 128):
        if cand <= pref and t % cand == 0:
            return cand
    raise ValueError(t)


def _sigmoid(x):
    return 1.0 / (1.0 + jnp.exp(-x))


def _silu_and_grad(x):
    s = _sigmoid(x)
    return x * s, s * (1.0 + x * (1.0 - s))


def _dot(a, b):
    return jnp.dot(a, b, preferred_element_type=F32)


def _dot_nt(a, b):
    return lax.dot_general(a, b, (((1,), (1,)), ((), ())), preferred_element_type=F32)


def _dot_tn(a, b):
    return lax.dot_general(a, b, (((0,), (0,)), ((), ())), preferred_element_type=F32)


def _rows(shape, base):
    return base + lax.broadcasted_iota(jnp.int32, shape, 0)


def _rope_tables(t):
    half = HEAD // 2
    inv = ROPE_THETA ** (-np.arange(half, dtype=np.float32) / half)
    pos = (np.arange(t) - PAD).astype(np.float32)
    ang = pos[:, None] * inv[None, :]
    lane = np.arange(BLK)
    cos = np.cos(ang)[:, lane % half].astype(np.float32)
    sin = np.sin(ang)[:, lane % half].astype(np.float32)
    first = (lane % HEAD) < half
    sin_a = np.where(first[None, :], -sin, 0.0).astype(np.float32)
    sin_b = np.where(first[None, :], 0.0, sin).astype(np.float32)
    return jnp.asarray(cos), jnp.asarray(sin_a), jnp.asarray(sin_b)


def _rope(v, cos, sin_a, sin_b):
    return v * cos + pltpu.roll(v, 96, 1) * sin_a + pltpu.roll(v, 32, 1) * sin_b


def _unrope(v, cos, sin_a, sin_b):
    return v * cos - pltpu.roll(v, 96, 1) * sin_a - pltpu.roll(v, 32, 1) * sin_b


def _coords():
    return lax.axis_index("x"), lax.axis_index("y"), lax.axis_index("c")


def _all_gather(arrs, name):
    plan = _GatherPlan(arrs)

    def body(*refs):
        plan.begin(refs)
        plan.end(refs)

    return pl.pallas_call(
        body, name=name, out_shape=plan.out_shape,
        in_specs=plan.specs, out_specs=plan.specs, scratch_shapes=plan.scratch,
    )(*arrs)


class _GatherPlan:
    def __init__(self, arrs):
        n = self.n = len(arrs)
        self.out_shape = [jax.ShapeDtypeStruct((8,) + a.shape, a.dtype) for a in arrs]
        self.specs = [pl.BlockSpec(memory_space=pl.ANY)] * n
        self.scratch = [pltpu.SemaphoreType.DMA((n, 7)), pltpu.SemaphoreType.DMA((n, 7)),
                        pltpu.SemaphoreType.DMA((n,))]

    def _copies(self, refs):
        n = self.n
        ins, outs = refs[:n], refs[n:2 * n]
        send_sems, recv_sems, local_sems = refs[2 * n:]
        x, y, c = _coords()
        me, sibling = (x, y, c), (x, y, 1 - c)
        chips = [(1 - x, y), (x, 1 - y), (1 - x, 1 - y)]

        def copy(a, k, block, to, src=None):
            dst = outs[a].at[4 * block[0] + 2 * block[1] + block[2]]
            return pltpu.make_async_remote_copy(
                src_ref=dst if src is None else src, dst_ref=dst,
                send_sem=send_sems.at[a, k], recv_sem=recv_sems.at[a, k],
                device_id=to, device_id_type=MESH)

        mine = [pltpu.make_async_copy(ins[a], outs[a].at[4 * x + 2 * y + c], local_sems.at[a])
                for a in range(n)]
        first = []
        for a in range(n):
            first.append(copy(a, 0, me, sibling, src=ins[a]))
            for j, chip in enumerate(chips):
                first.append(copy(a, 1 + j, me, (*chip, c), src=ins[a]))
        return copy, mine, first, (me, sibling, chips, c)

    def begin(self, refs):
        _, mine, first, _ = self._copies(refs)
        for cp in mine + first:
            cp.start()

    def end(self, refs):
        copy, mine, first, (me, sibling, chips, c) = self._copies(refs)
        passed = []
        for j, chip in enumerate(chips):
            for a in range(self.n):
                copy(a, 1 + j, (*chip, c), me).wait_recv()
                cp = copy(a, 4 + j, (*chip, c), sibling)
                cp.start()
                passed.append(cp)
        for a in range(self.n):
            copy(a, 0, sibling, me).wait_recv()
            for j, chip in enumerate(chips):
                copy(a, 4 + j, (*chip, 1 - c), me).wait_recv()
        for cp in first + passed:
            cp.wait_send()
        for cp in mine:
            cp.wait()


class _ChipsPlan:
    def __init__(self, sums):
        n = self.n = len(sums)
        self.out_shape = [jax.ShapeDtypeStruct((3,) + s.shape[1:], s.dtype) for s in sums]
        self.specs = [pl.BlockSpec(memory_space=pl.ANY)] * n
        self.scratch = [pltpu.SemaphoreType.DMA((n, 3)), pltpu.SemaphoreType.DMA((n, 3))]

    def _copies(self, refs):
        n = self.n
        ins, outs = refs[:n], refs[n:2 * n]
        send_sems, recv_sems = refs[2 * n:]
        x, y, c = _coords()
        chips = [(1 - x, y), (x, 1 - y), (1 - x, 1 - y)]
        return [pltpu.make_async_remote_copy(
            src_ref=ins[a].at[2 * chip[0] + chip[1]], dst_ref=outs[a].at[k],
            send_sem=send_sems.at[a, k], recv_sem=recv_sems.at[a, k],
            device_id=(*chip, c), device_id_type=MESH)
            for a in range(n) for k, chip in enumerate(chips)]

    def begin(self, refs):
        for cp in self._copies(refs):
            cp.start()

    def end(self, refs):
        for cp in self._copies(refs):
            cp.wait()


def _exchange_sibling(parts, name):
    plan = _SiblingPlan(parts)

    def body(*refs):
        plan.begin(refs)
        plan.end(refs)

    return pl.pallas_call(
        body, name=name, out_shape=plan.out_shape,
        in_specs=plan.specs, out_specs=plan.specs, scratch_shapes=plan.scratch,
    )(*parts)


class _SiblingPlan:
    def __init__(self, parts):
        n = self.n = len(parts)
        self.out_shape = [jax.ShapeDtypeStruct((4,) + p.shape[2:], p.dtype) for p in parts]
        self.specs = [pl.BlockSpec(memory_space=pl.ANY)] * n
        self.scratch = [pltpu.SemaphoreType.DMA((n,)), pltpu.SemaphoreType.DMA((n,))]

    def _copies(self, refs):
        n = self.n
        ins, outs = refs[:n], refs[n:2 * n]
        send_sems, recv_sems = refs[2 * n:]
        x, y, c = _coords()
        return [pltpu.make_async_remote_copy(
            src_ref=ins[a].at[:, 1 - c], dst_ref=outs[a],
            send_sem=send_sems.at[a], recv_sem=recv_sems.at[a],
            device_id=(x, y, 1 - c), device_id_type=MESH) for a in range(n)]

    def begin(self, refs):
        for cp in self._copies(refs):
            cp.start()

    def end(self, refs):
        for cp in self._copies(refs):
            cp.wait()


def _exchange_chips(sums, name):
    plan = _ChipsPlan(sums)

    def body(*refs):
        plan.begin(refs)
        plan.end(refs)

    return pl.pallas_call(
        body, name=name, out_shape=plan.out_shape,
        in_specs=plan.specs, out_specs=plan.specs, scratch_shapes=plan.scratch,
    )(*sums)


def _add_sibling(pos, part, recv, name):
    _, _, r, c = part.shape

    def body(pos_ref, p_ref, r_ref, o_ref):
        o_ref[...] = (p_ref[...].astype(F32) + r_ref[...].astype(F32)).astype(o_ref.dtype)

    return pl.pallas_call(
        body, name=name,
        grid_spec=pltpu.PrefetchScalarGridSpec(
            num_scalar_prefetch=1, grid=(4,),
            in_specs=[pl.BlockSpec((None, None, r, c), lambda q, pos: (q, pos[2], 0, 0)),
                      pl.BlockSpec((None, r, c), lambda q, pos: (q, 0, 0))],
            out_specs=pl.BlockSpec((None, r, c), lambda q, pos: (q, 0, 0))),
        out_shape=jax.ShapeDtypeStruct((4, r, c), part.dtype),
        compiler_params=_cparams(("arbitrary",)),
    )(pos, part, recv)


def _sum_chips(pos, sums, recv, name):
    _, r, c = sums.shape

    def body(pos_ref, s_ref, r_ref, o_ref):
        g = s_ref[...].astype(F32)
        for k in range(3):
            g = g + r_ref[k].astype(F32)
        o_ref[...] = g

    return pl.pallas_call(
        body, name=name,
        grid_spec=pltpu.PrefetchScalarGridSpec(
            num_scalar_prefetch=1, grid=(1,),
            in_specs=[pl.BlockSpec((None, r, c), lambda i, pos: (2 * pos[0] + pos[1], 0, 0)),
                      pl.BlockSpec((3, r, c), lambda i, pos: (0, 0, 0))],
            out_specs=pl.BlockSpec((r, c), lambda i, pos: (0, 0))),
        out_shape=jax.ShapeDtypeStruct((r, c), F32),
        compiler_params=_cparams(("arbitrary",)),
    )(pos, sums, recv)


def _adamw(ws, gs, ms, vs, name):
    n = len(ws)
    c1 = 1.0 / (1.0 - B1 ** STEP)
    c2 = 1.0 / (1.0 - B2 ** STEP)

    def body(*refs):
        w_r, g_r, m_r, v_r = refs[:n], refs[n:2 * n], refs[2 * n:3 * n], refs[3 * n:4 * n]
        d_o, m_o, v_o = refs[4 * n:5 * n], refs[5 * n:6 * n], refs[6 * n:7 * n]
        for a in range(n):
            g = g_r[a][...]
            m = B1 * m_r[a][...] + (1.0 - B1) * g
            v = B2 * v_r[a][...] + (1.0 - B2) * (g * g)
            d_o[a][...] = -LR * ((m * c1) / (jnp.sqrt(v * c2) + ADAM_EPS) + WD * w_r[a][...])
            m_o[a][...] = m
            v_o[a][...] = v

    shapes = [jax.ShapeDtypeStruct(w.shape, F32) for w in ws]
    outs = pl.pallas_call(body, name=name, out_shape=shapes * 3,
                          compiler_params=_cparams())(*ws, *gs, *ms, *vs)
    return outs[:n], outs[n:2 * n], outs[2 * n:]


def _reduce_small(gathered, name):
    n = len(gathered)

    def body(*refs):
        for a in range(n):
            acc = refs[a][0]
            for k in range(1, 8):
                acc = acc + refs[a][k]
            refs[n + a][...] = acc

    return pl.pallas_call(
        body, name=name,
        out_shape=[jax.ShapeDtypeStruct(g.shape[1:], F32) for g in gathered],
        compiler_params=_cparams())(*gathered)


class _Cols:
    def __init__(self, pieces, tw):
        self.pieces, self.tw = pieces, tw
        self.arrays = [p[0] for p in pieces]
        self.n_tiles = sum(p[2] for p in pieces)

    def specs(self, tm, row_of, tile_of):
        out = []
        for _, first, cnt in self.pieces:
            def imap(*g, first=first, cnt=cnt):
                return (row_of(*g), jnp.clip(tile_of(*g) - first, 0, cnt - 1))
            out.append(pl.BlockSpec((tm, self.tw), imap))
        return out

    def apply(self, t, refs, fn):
        for ref, (_, first, cnt) in zip(refs, self.pieces):
            pl.when((t >= first) & (t < first + cnt))(functools.partial(fn, ref))


def _ab_in(h, g, w_t, tables, ride):
    t = h.shape[0]
    tm = _tile(t, 1088)
    src = jnp.asarray(np.array(P0_SRC, np.int32))
    plan = _GatherPlan(ride)
    nr = plan.n

    def body(src_ref, h_ref, g_ref, w_ref, cos_ref, sa_ref, sb_ref, *rest):
        o_ref, hn_ref = rest[nr:nr + 2]
        hn_s = rest[2 * nr + 2]
        comm = (*rest[:nr], *rest[nr + 2:2 * nr + 2], *rest[2 * nr + 3:])
        i, j = pl.program_id(0), pl.program_id(1)
        pl.when((i == 0) & (j == 0))(lambda: plan.begin(comm))

        @pl.when(j == 0)
        def _():
            x = h_ref[...]
            hn = (x * lax.rsqrt(jnp.mean(x * x, -1, keepdims=True) + EPS) * g_ref[...]).astype(MXU)
            hn_s[...] = hn
            hn_ref[...] = hn.astype(ACT)

        acc = _dot_nt(hn_s[...], w_ref[...])
        rope = lambda v: _rope(v, cos_ref[...], sa_ref[...], sb_ref[...])

        @pl.when((j == 4) | (j == 5))
        def _():
            o_ref[:, :BLK] = rope(acc[:, :BLK])
            o_ref[:, BLK:] = rope(acc[:, BLK:])

        @pl.when(j == 10)
        def _():
            o_ref[:, :BLK] = rope(acc[:, :BLK])
            o_ref[:, BLK:] = acc[:, BLK:]

        @pl.when((j < 4) | ((j > 5) & (j < 10)))
        def _():
            o_ref[...] = acc

        pl.when((i == t // tm - 1) & (j == 10))(lambda: plan.end(comm))

    tab = pl.BlockSpec((tm, BLK), lambda i, j, s: (i, 0))
    outs = pl.pallas_call(
        body, name="ab_in",
        grid_spec=pltpu.PrefetchScalarGridSpec(
            num_scalar_prefetch=1, grid=(t // tm, 11),
            in_specs=[pl.BlockSpec((tm, D), lambda i, j, s: (i, 0)),
                      pl.BlockSpec((1, D), lambda i, j, s: (0, 0)),
                      pl.BlockSpec((256, D), lambda i, j, s: (s[j], 0)),
                      tab, tab, tab] + plan.specs,
            out_specs=[pl.BlockSpec((tm, 256), lambda i, j, s: (i, j)),
                       pl.BlockSpec((tm, D), lambda i, j, s: (i, 0))] + plan.specs,
            scratch_shapes=[pltpu.VMEM((tm, D), MXU)] + plan.scratch),
        out_shape=[jax.ShapeDtypeStruct((t, 2816), F32), jax.ShapeDtypeStruct((t, D), ACT)] + plan.out_shape,
        compiler_params=_cparams(("arbitrary", "arbitrary")),
    )(src, h, g, w_t, *tables, *ride)
    return outs[0], outs[1], outs[2:]


def _swa_mask(n):
    c = lax.broadcasted_iota(jnp.int32, (3 * BLK, 4 * BLK), 0)
    r = lax.broadcasted_iota(jnp.int32, (3 * BLK, 4 * BLK), 1) & (BLK - 1)
    qpos = n * BLK + r
    bpos = (n - 2) * BLK + c
    meta_ok = (c >= PAD) & (c < BLK) & (qpos - c >= BLK)
    band_ok = (c >= BLK) & (bpos >= PAD) & (qpos >= bpos) & (qpos - bpos < BLK)
    return meta_ok | band_ok


def _swa_keys(kv_ref, n):
    def blk(b):
        return kv_ref[pl.ds(pl.multiple_of(b * BLK, BLK), BLK), :]
    kv = jnp.concatenate([kv_ref[0:BLK, :], blk(jnp.maximum(n - 1, 0)), blk(n)], axis=0)
    lo = lax.broadcasted_iota(jnp.int32, (1, BLK), 1) < HEAD
    out = []
    for part in (kv[:, :BLK], kv[:, BLK:]):
        rolled = pltpu.roll(part, HEAD, 1)
        out.append((jnp.where(lo, part, rolled).astype(MXU), jnp.where(lo, rolled, part).astype(MXU)))
    return out[0], out[1], lo


def _swa_stack(ref, g, lo):
    parts = []
    for p in (2 * g, 2 * g + 1):
        x = ref[:, p * BLK:(p + 1) * BLK]
        parts += [jnp.where(lo, x, 0.0), jnp.where(lo, 0.0, x)]
    return jnp.concatenate(parts, axis=0).astype(MXU)


def _swa_probs(qs, kd, mask, sink_ref, g):
    sink = jnp.concatenate([jnp.full((1, BLK), sink_ref[4 * g + h], F32) for h in range(4)], axis=1)
    s = jnp.where(mask, _dot_nt(kd, qs) * SCALE, NEG)
    m = jnp.maximum(jnp.max(s, 0, keepdims=True), sink)
    e = jnp.exp(s - m)
    e_sink = jnp.exp(sink - m)
    inv = 1.0 / (jnp.sum(e, 0, keepdims=True) + e_sink)
    return e * inv, e_sink * inv


def _swa_unstack(x_t, lo):
    x = x_t.T
    return [jnp.where(lo, x[0:BLK], x[BLK:2 * BLK]), jnp.where(lo, x[2 * BLK:3 * BLK], x[3 * BLK:])]


def _swa_fwd(p0, sinks, ride):
    t = p0.shape[0]
    plan = _GatherPlan(ride)
    nr = plan.n

    def body(sink_ref, q_ref, kv_ref, *rest):
        o_ref = rest[nr]
        comm = (*rest[:nr], *rest[nr + 1:])
        n = pl.program_id(0)
        pl.when(n == 0)(lambda: plan.begin(comm))
        kd, vd, lo = _swa_keys(kv_ref, n)
        mask = _swa_mask(n)
        for g in range(2):
            pr, _ = _swa_probs(_swa_stack(q_ref, g, lo), kd[g], mask, sink_ref, g)
            pairs = _swa_unstack(_dot_tn(vd[g], pr.astype(MXU)), lo)
            for k in range(2):
                p = 2 * g + k
                o_ref[:, p * BLK:(p + 1) * BLK] = pairs[k]
        pl.when(n == t // BLK - 1)(lambda: plan.end(comm))

    outs = pl.pallas_call(
        body, name="swa_fwd", grid=(t // BLK,),
        in_specs=[pl.BlockSpec(memory_space=pltpu.SMEM),
                  pl.BlockSpec((BLK, 512), lambda n: (n, 2)),
                  pl.BlockSpec((t, 256), lambda n: (0, 10))] + plan.specs,
        out_specs=[pl.BlockSpec((BLK, 512), lambda n: (n, 0))] + plan.specs,
        out_shape=[jax.ShapeDtypeStruct((t, 512), F32)] + plan.out_shape,
        scratch_shapes=plan.scratch,
        compiler_params=_cparams(("arbitrary",)),
    )(sinks, p0, p0, *ride)
    return outs[0], outs[1:]


def _conv_window(u_w, w_ref, n_out, first):
    rows = u_w.shape[0]
    acc = None
    for j in range(CONV_W):
        shifted = pltpu.roll(u_w, (rows - (first + j)) % rows, 0)[:n_out]
        term = shifted * w_ref[j:j + 1, :]
        acc = term if acc is None else acc + term
    return acc


def _conv_fwd(p0, conv_w, conv_b, ln_g, ln_b):
    t = p0.shape[0]
    tm = _tile(t, 544)
    hb = tm // HALO

    def body(cur_ref, prev_ref, w_ref, b_ref, g_ref, bb_ref, o_ref):
        i = pl.program_id(0)
        glu = jnp.concatenate([prev_ref[...], cur_ref[...]], axis=0)
        rw = _rows((tm + HALO, 1), i * tm - HALO)
        u_w = jnp.where(rw >= PAD, glu[:, :512] * _sigmoid(glu[:, 512:]), 0.0)
        cv = _conv_window(u_w, w_ref, tm, HALO - (CONV_W - 1)) + b_ref[...]
        xc = cv - jnp.mean(cv, -1, keepdims=True)
        ln = xc * lax.rsqrt(jnp.mean(xc * xc, -1, keepdims=True) + LN_EPS) * g_ref[...] + bb_ref[...]
        o_ref[...] = (ln * _sigmoid(ln)).astype(ACT)

    vec = pl.BlockSpec((1, 512), lambda i: (0, 0))
    return pl.pallas_call(
        body, name="conv_fwd", grid=(t // tm,),
        in_specs=[pl.BlockSpec((tm, D), lambda i: (i, 0)),
                  pl.BlockSpec((HALO, D), lambda i: (jnp.maximum(i * hb - 1, 0), 0)),
                  pl.BlockSpec((CONV_W, 512), lambda i: (0, 0)), vec, vec, vec],
        out_specs=pl.BlockSpec((tm, 512), lambda i: (i, 0)),
        out_shape=jax.ShapeDtypeStruct((t, 512), ACT),
        compiler_params=_cparams(("arbitrary",)),
    )(p0, p0, conv_w, conv_b, ln_g, ln_b)


def _ab_out(h, p0, att, c1, w_pw2, w_out, g_post):
    t = h.shape[0]
    tm = _tile(t, 544)

    def body(h_ref, ga_ref, gb_ref, att_ref, c1_ref, pw_ref, wo_ref, g_ref, h1_ref, y_ref, mix_ref):
        i = pl.program_id(0)
        sga, _ = _silu_and_grad(ga_ref[...])
        sgb, _ = _silu_and_grad(gb_ref[...])
        a = att_ref[...] * sga
        c = _dot(c1_ref[...].astype(MXU), pw_ref[...]) * sgb
        mix = jnp.concatenate([a, c], axis=1).astype(MXU)
        y = _dot(mix, wo_ref[...])
        yn = y * lax.rsqrt(jnp.mean(y * y, -1, keepdims=True) + EPS) * g_ref[...]
        h1_ref[...] = jnp.where(_rows((tm, 1), i * tm) >= PAD, h_ref[...] + yn, 0.0)
        y_ref[...] = y
        mix_ref[...] = mix.astype(ACT)

    row = lambda w, idx: pl.BlockSpec((tm, w), lambda i: (i, idx))
    full = lambda a: pl.BlockSpec(a.shape, lambda i: (0, 0))
    return pl.pallas_call(
        body, name="ab_out", grid=(t // tm,),
        in_specs=[row(D, 0), row(512, 3), row(512, 4), row(512, 0), row(512, 0),
                  full(w_pw2), full(w_out), full(g_post)],
        out_specs=[row(D, 0), row(D, 0), row(D, 0)],
        out_shape=[jax.ShapeDtypeStruct((t, D), F32), jax.ShapeDtypeStruct((t, D), F32),
                   jax.ShapeDtypeStruct((t, D), ACT)],
        compiler_params=_cparams(("arbitrary",)),
    )(h, p0, p0, att, c1, w_pw2, w_out, g_post)


def _sb_in(h, g, w, ride):
    t = h.shape[0]
    tm = _tile(t, 1088)
    plan = _GatherPlan(ride)
    nr = plan.n

    def body(h_ref, g_ref, w_ref, *rest):
        o_ref, hn_ref = rest[nr:nr + 2]
        hn_s = rest[2 * nr + 2]
        comm = (*rest[:nr], *rest[nr + 2:2 * nr + 2], *rest[2 * nr + 3:])
        i, j = pl.program_id(0), pl.program_id(1)
        pl.when((i == 0) & (j == 0))(lambda: plan.begin(comm))

        @pl.when(j == 0)
        def _():
            x = h_ref[...]
            hn = (x * lax.rsqrt(jnp.mean(x * x, -1, keepdims=True) + EPS) * g_ref[...]).astype(MXU)
            hn_s[...] = hn
            hn_ref[...] = hn.astype(ACT)

        o_ref[...] = _dot(hn_s[...], w_ref[...])
        pl.when((i == t // tm - 1) & (j == 7))(lambda: plan.end(comm))

    outs = pl.pallas_call(
        body, name="sb_in", grid=(t // tm, 8),
        in_specs=[pl.BlockSpec((tm, D), lambda i, j: (i, 0)),
                  pl.BlockSpec((1, D), lambda i, j: (0, 0)),
                  pl.BlockSpec((None, D, 512), lambda i, j: (j, 0, 0))] + plan.specs,
        out_specs=[pl.BlockSpec((tm, 512), lambda i, j: (i, j)),
                   pl.BlockSpec((tm, D), lambda i, j: (i, 0))] + plan.specs,
        out_shape=[jax.ShapeDtypeStruct((t, 4096), F32), jax.ShapeDtypeStruct((t, D), ACT)] + plan.out_shape,
        scratch_shapes=[pltpu.VMEM((tm, D), MXU)] + plan.scratch,
        compiler_params=_cparams(("arbitrary", "arbitrary")),
    )(h, g, w, *ride)
    return outs[0], outs[1], outs[2:]


def _split_hi_lo(x):
    hi = x.astype(MXU)
    lo = (x - hi.astype(F32)).astype(MXU)
    return hi, lo


def _scan_matrix(suffix):
    j = lax.broadcasted_iota(jnp.int32, (2 * BLK, 2 * BLK), 0) % BLK
    s = lax.broadcasted_iota(jnp.int32, (2 * BLK, 2 * BLK), 1)
    keep = (s >= BLK) | ((j > s) if suffix else (j < s))
    return jnp.where(keep, 1.0, 0.0).astype(MXU)


def _scan_packed(hi_lo, b, mat):
    cols = slice(b * BLK, (b + 1) * BLK)
    both = _dot(jnp.concatenate([hi_lo[:BLK, cols], hi_lo[BLK:, cols]], axis=1), mat)
    return both[:, :BLK], both[:, BLK:]


KC = 4
GROUPS = 2
GROUPS_FWD = 4


def _sb_logits(qm, kc, valid):
    z = _dot_nt(qm, kc)
    log_beta = jnp.minimum(z, 0.0) - jnp.log(1.0 + jnp.exp(-jnp.abs(z)))
    return log_beta, jnp.where(valid, log_beta - z, 0.0)


def _sb_valid(i, first_key, chunk):
    r = lax.broadcasted_iota(jnp.int32, (BLK, chunk), 0)
    c = lax.broadcasted_iota(jnp.int32, (BLK, chunk), 1)
    kpos = first_key + c
    return (kpos >= PAD) & (kpos < i * BLK + r)


def _sb_walk(i, step_of, init):
    n_full = (i + 1) // KC
    full = step_of(KC)
    carry = lax.fori_loop(0, n_full, lambda s, c: full(s * KC, c), init)
    rest = [lambda c: c] + [functools.partial(step_of(k), n_full * KC) for k in range(1, KC)]
    return lax.switch(i + 1 - n_full * KC, rest, carry)


def _sb_fwd(p1):
    t = p1.shape[0]
    groups = GROUPS_FWD
    w = groups * BLK

    def body(q_ref, k_ref, v_ref, o_ref, lt_ref, k_s, v_s):
        i = pl.program_id(1)

        @pl.when(i == 0)
        def _():
            k_s[...] = k_ref[...].astype(MXU)
            v_s[...] = v_ref[...].astype(MXU)

        lo = lax.broadcasted_iota(jnp.int32, (1, BLK), 1) < HEAD
        qm = []
        for g in range(groups):
            q = q_ref[:, g * BLK:(g + 1) * BLK] * SCALE
            qm += [jnp.where(lo, q, 0.0).astype(MXU), jnp.where(lo, 0.0, q).astype(MXU)]
        mat = _scan_matrix(True)
        heads = range(2 * groups)

        def step_of(kc):
            chunk = kc * BLK

            def step(done, carry):
                start = pl.multiple_of((i + 1 - done - kc) * BLK, BLK)
                valid = _sb_valid(i, start, chunk)
                new, staged = [], []
                for h in heads:
                    lanes = slice((h // 2) * BLK, (h // 2 + 1) * BLK)
                    log_beta, log_1m = _sb_logits(qm[h], k_s[pl.ds(start, chunk), lanes], valid)
                    staged.append((log_beta, jnp.concatenate(_split_hi_lo(log_1m), axis=0)))
                probs = []
                for h in heads:
                    log_beta, hi_lo = staged[h]
                    run = carry[2 * h]
                    parts = [None] * kc
                    for b in reversed(range(kc)):
                        after, total = _scan_packed(hi_lo, b, mat)
                        parts[b] = after + run
                        run = run + total
                    a = jnp.where(valid, jnp.exp(log_beta + jnp.concatenate(parts, axis=1)), 0.0)
                    probs.append((run, a.astype(MXU)))
                for h in heads:
                    lanes = slice((h // 2) * BLK, (h // 2 + 1) * BLK)
                    run, a = probs[h]
                    new += [run, carry[2 * h + 1] + _dot(a, v_s[pl.ds(start, chunk), lanes])]
                return tuple(new)
            return step

        zero = jnp.zeros((BLK, BLK), F32)
        res = _sb_walk(i, step_of, (zero,) * (4 * groups))
        for g in range(groups):
            lanes = slice(g * BLK, (g + 1) * BLK)
            o_ref[:, lanes] = jnp.where(lo, res[4 * g + 1], res[4 * g + 3])
            lt_ref[:, lanes] = jnp.where(lo, res[4 * g], res[4 * g + 2])

    ng = D // w
    blk = pl.BlockSpec((BLK, w), lambda hp, i: (i, hp))
    return pl.pallas_call(
        body, name="sb_fwd", grid=(ng, t // BLK),
        in_specs=[blk,
                  pl.BlockSpec((t, w), lambda hp, i: (0, ng + hp)),
                  pl.BlockSpec((t, w), lambda hp, i: (0, 2 * ng + hp))],
        out_specs=[blk, blk],
        out_shape=[jax.ShapeDtypeStruct((t, D), F32)] * 2,
        scratch_shapes=[pltpu.VMEM((t, w), MXU), pltpu.VMEM((t, w), MXU)],
        compiler_params=_cparams(("arbitrary", "arbitrary")),
    )(p1, p1, p1)


def _sb_out(o, p1, w_out, h1, g_post, tgt):
    t = o.shape[0]
    tm = _tile(t, 544)

    def body(o_ref, g_ref, w_ref, h_ref, gp_ref, t_ref,
             loss_ref, dh_ref, dy_ref, m_ref, do_ref, dg_ref, dgp_ref):
        i = pl.program_id(0)

        @pl.when(i == 0)
        def _():
            loss_ref[...] = jnp.zeros_like(loss_ref)
            dgp_ref[...] = jnp.zeros_like(dgp_ref)

        gate = g_ref[...]
        sg, dsg = _silu_and_grad(gate)
        ov = o_ref[...]
        m = (ov * sg).astype(MXU)
        y = _dot(m, w_ref[...])
        r = lax.rsqrt(jnp.mean(y * y, -1, keepdims=True) + EPS)
        yhat = y * r
        h2 = h_ref[...] + yhat * gp_ref[...]
        diff = jnp.where(_rows((tm, 1), i * tm) >= BLK, h2 - t_ref[...], 0.0)
        loss_ref[...] += jnp.full(loss_ref.shape, 0.5 / D, F32) * jnp.sum(diff * diff)
        dh = diff * (1.0 / D)
        dgp_ref[...] += jnp.sum(dh * yhat, 0, keepdims=True)
        dyn = dh * gp_ref[...]
        dy = (r * (dyn - yhat * jnp.mean(dyn * yhat, -1, keepdims=True))).astype(MXU)
        dm = _dot_nt(dy, w_ref[...])
        dh_ref[...] = dh
        dy_ref[...] = dy.astype(ACT)
        m_ref[...] = m.astype(ACT)
        do_ref[...] = dm * sg
        dg_ref[...] = (dm * ov * dsg).astype(ACT)

    row = lambda idx: pl.BlockSpec((tm, D), lambda i: (i, idx))
    full = lambda a: pl.BlockSpec(a.shape, lambda i: (0, 0))
    acc = lambda s: pl.BlockSpec(s, lambda i: (0, 0))
    return pl.pallas_call(
        body, name="sb_out", grid=(t // tm,),
        in_specs=[row(0), row(3), full(w_out), row(0), full(g_post), row(0)],
        out_specs=[acc((8, BLK)), row(0), row(0), row(0), row(0), row(0), acc((1, D))],
        out_shape=[jax.ShapeDtypeStruct((8, BLK), F32), jax.ShapeDtypeStruct((t, D), F32),
                   jax.ShapeDtypeStruct((t, D), ACT), jax.ShapeDtypeStruct((t, D), ACT),
                   jax.ShapeDtypeStruct((t, D), F32), jax.ShapeDtypeStruct((t, D), ACT),
                   jax.ShapeDtypeStruct((1, D), F32)],
        compiler_params=_cparams(("arbitrary",)),
    )(o, p1, w_out, h1, g_post, tgt)


def _sb_bwd(p1, ltot, do):
    t = p1.shape[0]
    nb = t // BLK

    w = GROUPS * BLK

    def body(q_ref, k_ref, v_ref, lt_ref, do_ref, dq_ref, dk_ref, dv_ref, k_s, v_s, dk_s, dv_s):
        i = pl.program_id(1)
        lo = lax.broadcasted_iota(jnp.int32, (1, BLK), 1) < HEAD

        @pl.when(i == 0)
        def _():
            k_s[...] = k_ref[...].astype(MXU)
            v_s[...] = v_ref[...].astype(MXU)
            dk_s[...] = jnp.zeros_like(dk_s)
            dv_s[...] = jnp.zeros_like(dv_s)

        qm, dom, row_total, q2, do2 = [], [], [], [], []
        for g in range(GROUPS):
            lanes = slice(g * BLK, (g + 1) * BLK)
            q, dout, lt = q_ref[:, lanes] * SCALE, do_ref[:, lanes], lt_ref[:, lanes]
            qm += [jnp.where(lo, q, 0.0).astype(MXU), jnp.where(lo, 0.0, q).astype(MXU)]
            dom += [jnp.where(lo, dout, 0.0).astype(MXU), jnp.where(lo, 0.0, dout).astype(MXU)]
            q2.append(jnp.concatenate(qm[-2:], axis=0))
            do2.append(jnp.concatenate(dom[-2:], axis=0))
            lt_r = pltpu.roll(lt, HEAD, 1)
            row_total += [jnp.where(lo, lt, lt_r), jnp.where(lo, lt_r, lt)]
        mat_l = _scan_matrix(True)
        mat_g = _scan_matrix(False)
        heads = range(2 * GROUPS)

        def step_of(kc):
            chunk = kc * BLK

            def step(done, carry):
                start = pl.multiple_of(done * BLK, BLK)
                keys = lambda ref, h: ref[pl.ds(start, chunk), (h // 2) * BLK:(h // 2 + 1) * BLK]
                valid = _sb_valid(i, start, chunk)
                new, dzs, probs, st1, st2, st3 = [], [], [], [], [], []
                for h in heads:
                    log_beta, log_1m = _sb_logits(qm[h], keys(k_s, h), valid)
                    st1.append((log_beta, jnp.concatenate(_split_hi_lo(log_1m), axis=0),
                                _dot_nt(dom[h], keys(v_s, h))))
                for h in heads:
                    log_beta, hi_lo, da = st1[h]
                    run = carry[3 * h]
                    parts = []
                    for b in range(kc):
                        after, total = _scan_packed(hi_lo, b, mat_l)
                        run = run + total
                        parts.append(after + (row_total[h] - run))
                    a = jnp.where(valid, jnp.exp(log_beta + jnp.concatenate(parts, axis=1)), 0.0)
                    g = da * a
                    probs.append(a.astype(MXU))
                    st2.append((run, g, jnp.concatenate(_split_hi_lo(g), axis=0)))
                for h in heads:
                    run, g, hi_lo = st2[h]
                    run_g = carry[3 * h + 1]
                    parts = []
                    for b in range(kc):
                        before, total_g = _scan_packed(hi_lo, b, mat_g)
                        parts.append(before + run_g)
                        run_g = run_g + total_g
                    sig = jnp.exp(st1[h][0])
                    dz = jnp.where(valid, g - sig * (g + jnp.concatenate(parts, axis=1)), 0.0)
                    dzs.append(dz.astype(MXU))
                    st3.append((run, run_g))
                for h in heads:
                    new += [*st3[h], carry[3 * h + 2] + _dot(dzs[h], keys(k_s, h))]
                for g in range(GROUPS):
                    lanes = slice(g * BLK, (g + 1) * BLK)
                    dk_s[pl.ds(start, chunk), lanes] += _dot_tn(jnp.concatenate(dzs[2 * g:2 * g + 2], axis=0), q2[g])
                    dv_s[pl.ds(start, chunk), lanes] += _dot_tn(jnp.concatenate(probs[2 * g:2 * g + 2], axis=0), do2[g])
                return tuple(new)
            return step

        zero = jnp.zeros((BLK, BLK), F32)
        res = _sb_walk(i, step_of, (zero,) * (6 * GROUPS))
        for g in range(GROUPS):
            dq = jnp.where(lo, res[6 * g + 2], res[6 * g + 5])
            dq_ref[:, g * BLK:(g + 1) * BLK] = (dq * SCALE).astype(ACT)

        @pl.when(i == nb - 1)
        def _():
            dk_ref[...] = dk_s[...].astype(ACT)
            dv_ref[...] = dv_s[...].astype(ACT)

    ng = D // w
    blk = pl.BlockSpec((BLK, w), lambda hp, i: (i, hp))
    col = lambda off: pl.BlockSpec((t, w), lambda hp, i: (0, off + hp))
    return pl.pallas_call(
        body, name="sb_bwd", grid=(ng, nb),
        in_specs=[blk, col(ng), col(2 * ng), blk, blk],
        out_specs=[blk, col(0), col(0)],
        out_shape=[jax.ShapeDtypeStruct((t, D), ACT)] * 3,
        scratch_shapes=[pltpu.VMEM((t, w), MXU), pltpu.VMEM((t, w), MXU),
                        pltpu.VMEM((t, w), F32), pltpu.VMEM((t, w), F32)],
        compiler_params=_cparams(("arbitrary", "arbitrary")),
    )(p1, p1, p1, ltot, do)


def _norm_bwd(x, g, dy, eps):
    r = lax.rsqrt(jnp.mean(x * x, -1, keepdims=True) + eps)
    xhat = x * r
    dxn = dy * g
    return r * (dxn - xhat * jnp.mean(dxn * xhat, -1, keepdims=True)), jnp.sum(dy * xhat, 0, keepdims=True)


def _sb_in_bwd(dparts, w_sb, h1, g_pre1, dh2, ride):
    t = h1.shape[0]
    tm = _tile(t, 544)
    plan = _SiblingPlan(ride)
    nr = plan.n

    def body(dq_ref, dk_ref, dv_ref, dg_ref, w_ref, h_ref, g_ref, dh2_ref, *rest):
        dh1_ref, dgn_ref = rest[nr:nr + 2]
        comm = (*rest[:nr], *rest[nr + 2:])
        i = pl.program_id(0)
        pl.when(i == 0)(lambda: plan.begin(comm))

        @pl.when(i == 0)
        def _():
            dgn_ref[...] = jnp.zeros_like(dgn_ref)

        dhn = None
        for a, ref in enumerate((dq_ref, dk_ref, dv_ref, dg_ref)):
            for b in range(2):
                term = _dot_nt(ref[:, b * 512:(b + 1) * 512].astype(MXU), w_ref[2 * a + b])
                dhn = term if dhn is None else dhn + term
        dx, dg = _norm_bwd(h_ref[...], g_ref[...], dhn, EPS)
        dgn_ref[...] += dg
        dh1_ref[...] = dh2_ref[...] + dx
        pl.when(i == t // tm - 1)(lambda: plan.end(comm))

    row = pl.BlockSpec((tm, D), lambda i: (i, 0))
    vec = pl.BlockSpec((1, D), lambda i: (0, 0))
    outs = pl.pallas_call(
        body, name="sb_in_bwd", grid=(t // tm,),
        in_specs=[row, row, row, row, pl.BlockSpec(w_sb.shape, lambda i: (0, 0, 0)), row, vec, row]
        + plan.specs,
        out_specs=[row, vec] + plan.specs,
        out_shape=[jax.ShapeDtypeStruct((t, D), F32), jax.ShapeDtypeStruct((1, D), F32)] + plan.out_shape,
        scratch_shapes=plan.scratch,
        compiler_params=_cparams(("arbitrary",)),
    )(*dparts, w_sb, h1, g_pre1, dh2, *ride)
    return outs[0], outs[1], outs[2:]


def _ab_out_bwd(dh1, y0, g_post0, w_out, p0, att, c1, w_pw2):
    t = dh1.shape[0]
    tm = _tile(t, 544)

    def body(dh1_ref, y_ref, g0_ref, wo_ref, ga_ref, gb_ref, att_ref, c1_ref, pw_ref,
             dy_ref, dga_ref, dgb_ref, datt_ref, dc1_ref, dc2_ref, dg0_ref):
        @pl.when(pl.program_id(0) == 0)
        def _():
            dg0_ref[...] = jnp.zeros_like(dg0_ref)

        dy, dg = _norm_bwd(y_ref[...], g0_ref[...], dh1_ref[...], EPS)
        dg0_ref[...] += dg
        dy = dy.astype(MXU)
        dy_ref[...] = dy.astype(ACT)
        dmix = _dot_nt(dy, wo_ref[...])
        da, dc = dmix[:, :512], dmix[:, 512:]
        sga, dsga = _silu_and_grad(ga_ref[...])
        sgb, dsgb = _silu_and_grad(gb_ref[...])
        datt_ref[...] = da * sga
        dga_ref[...] = (da * att_ref[...] * dsga).astype(ACT)
        c2 = _dot(c1_ref[...].astype(MXU), pw_ref[...])
        dc2 = (dc * sgb).astype(MXU)
        dgb_ref[...] = (dc * c2 * dsgb).astype(ACT)
        dc2_ref[...] = dc2.astype(ACT)
        dc1_ref[...] = _dot_nt(dc2, pw_ref[...])

    row = lambda w, idx: pl.BlockSpec((tm, w), lambda i: (i, idx))
    full = lambda a: pl.BlockSpec(a.shape, lambda i: (0, 0))
    sd = jax.ShapeDtypeStruct
    return pl.pallas_call(
        body, name="ab_out_bwd", grid=(t // tm,),
        in_specs=[row(D, 0), row(D, 0), full(g_post0), full(w_out), row(512, 3), row(512, 4),
                  row(512, 0), row(512, 0), full(w_pw2)],
        out_specs=[row(D, 0), row(512, 0), row(512, 0), row(512, 0), row(512, 0), row(512, 0),
                   pl.BlockSpec((1, D), lambda i: (0, 0))],
        out_shape=[sd((t, D), ACT), sd((t, 512), ACT), sd((t, 512), ACT), sd((t, 512), F32),
                   sd((t, 512), F32), sd((t, 512), ACT), sd((1, D), F32)],
        compiler_params=_cparams(("arbitrary",)),
    )(dh1, y0, g_post0, w_out, p0, p0, att, c1, w_pw2)


def _conv_bwd(p0, dc1, conv_w, conv_b, ln_g, ln_b, ride):
    t = p0.shape[0]
    tm = _tile(t, 544)
    hb = tm // HALO
    last = t // HALO - 1
    plan = _ChipsPlan(ride)
    nr = plan.n

    def body(cur_ref, prev_ref, next_ref, d_ref, dn_ref, w_ref, b_ref, g_ref, bb_ref, *rest):
        dglu_ref, dw_ref, db_ref, dlg_ref, dlb_ref = rest[nr:nr + 5]
        comm = (*rest[:nr], *rest[nr + 5:])
        i = pl.program_id(0)
        pl.when(i == 0)(lambda: plan.begin(comm))

        @pl.when(i == 0)
        def _():
            for ref in (dw_ref, db_ref, dlg_ref, dlb_ref):
                ref[...] = jnp.zeros_like(ref)

        glu = jnp.concatenate([prev_ref[...], cur_ref[...], next_ref[...]], axis=0)
        rw = _rows((tm + 2 * HALO, 1), i * tm - HALO)
        ga, sg = glu[:, :512], _sigmoid(glu[:, 512:])
        u_w = jnp.where((rw >= PAD) & (rw < t), ga * sg, 0.0)
        n_cv = tm + HALO
        cv = _conv_window(u_w, w_ref, n_cv, HALO - (CONV_W - 1)) + b_ref[...]
        xc = cv - jnp.mean(cv, -1, keepdims=True)
        rstd = lax.rsqrt(jnp.mean(xc * xc, -1, keepdims=True) + LN_EPS)
        cvhat = xc * rstd
        ln = cvhat * g_ref[...] + bb_ref[...]
        _, dsl = _silu_and_grad(ln)
        rc = _rows((n_cv, 1), i * tm)
        dc = jnp.concatenate([d_ref[...], dn_ref[...]], axis=0)
        dln = jnp.where(rc < t, dc * dsl, 0.0)
        dhat = dln * g_ref[...]
        dcv = rstd * (dhat - jnp.mean(dhat, -1, keepdims=True)
                      - cvhat * jnp.mean(dhat * cvhat, -1, keepdims=True))
        own = dcv[:tm]
        dlg_ref[...] += jnp.sum((dln * cvhat)[:tm], 0, keepdims=True)
        dlb_ref[...] += jnp.sum(dln[:tm], 0, keepdims=True)
        db_ref[...] += jnp.sum(own, 0, keepdims=True)
        rows = tm + 2 * HALO
        du = None
        for j in range(CONV_W):
            first = HALO - (CONV_W - 1) + j
            shifted = pltpu.roll(u_w, (rows - first) % rows, 0)[:tm]
            dw_ref[j:j + 1, :] += jnp.sum(own * shifted, 0, keepdims=True)
            back = pltpu.roll(dcv, (n_cv - (CONV_W - 1 - j)) % n_cv, 0)[:tm]
            term = back * w_ref[j:j + 1, :]
            du = term if du is None else du + term
        du = jnp.where(_rows((tm, 1), i * tm) >= PAD, du, 0.0)
        ga_c, sg_c = ga[HALO:HALO + tm], sg[HALO:HALO + tm]
        dglu_ref[:, :512] = (du * sg_c).astype(ACT)
        dglu_ref[:, 512:] = (du * ga_c * sg_c * (1.0 - sg_c)).astype(ACT)
        pl.when(i == t // tm - 1)(lambda: plan.end(comm))

    vec = pl.BlockSpec((1, 512), lambda i: (0, 0))
    nxt = lambda i: (jnp.minimum((i + 1) * hb, last), 0)
    outs = pl.pallas_call(
        body, name="conv_bwd", grid=(t // tm,),
        in_specs=[pl.BlockSpec((tm, D), lambda i: (i, 0)),
                  pl.BlockSpec((HALO, D), lambda i: (jnp.maximum(i * hb - 1, 0), 0)),
                  pl.BlockSpec((HALO, D), nxt),
                  pl.BlockSpec((tm, 512), lambda i: (i, 0)),
                  pl.BlockSpec((HALO, 512), nxt),
                  pl.BlockSpec((CONV_W, 512), lambda i: (0, 0)), vec, vec, vec] + plan.specs,
        out_specs=[pl.BlockSpec((tm, D), lambda i: (i, 0)),
                   pl.BlockSpec((HALO, 512), lambda i: (0, 0)), vec, vec, vec] + plan.specs,
        out_shape=[jax.ShapeDtypeStruct((t, D), ACT), jax.ShapeDtypeStruct((HALO, 512), F32)]
        + [jax.ShapeDtypeStruct((1, 512), F32)] * 3 + plan.out_shape,
        scratch_shapes=plan.scratch,
        compiler_params=_cparams(("arbitrary",)),
    )(p0, p0, p0, dc1, dc1, conv_w, conv_b, ln_g, ln_b, *ride)
    return outs[:5], outs[5:]


def _swa_bwd(p0, datt, sinks, tables, ride):
    t = p0.shape[0]
    nb = t // BLK
    plan = _ChipsPlan(ride)
    nr = plan.n

    def body(sink_ref, q_ref, kv_ref, d_ref, cos_ref, sa_ref, sb_ref, *rest):
        dq_ref, dkv_ref, ds_ref = rest[nr:nr + 3]
        acc = rest[2 * nr + 3]
        comm = (*rest[:nr], *rest[nr + 3:2 * nr + 3], *rest[2 * nr + 4:])
        n = pl.program_id(0)
        pl.when(n == 0)(lambda: plan.begin(comm))

        @pl.when(n == 0)
        def _():
            acc[...] = jnp.zeros_like(acc)
            ds_ref[...] = jnp.zeros_like(ds_ref)

        kd, vd, lo = _swa_keys(kv_ref, n)
        mask = _swa_mask(n)
        row0 = pl.multiple_of(n * BLK, BLK)
        tabs = [r[pl.ds(row0, BLK), :] for r in (cos_ref, sa_ref, sb_ref)]
        dk_g, dv_g, st1, st2 = [], [], [], []
        for g in range(2):
            qs, dos = _swa_stack(q_ref, g, lo), _swa_stack(d_ref, g, lo)
            pr, p_sink = _swa_probs(qs, kd[g], mask, sink_ref, g)
            st1.append((qs, dos, pr, p_sink, _dot_nt(vd[g], dos)))
        for g in range(2):
            qs, dos, pr, p_sink, dpr = st1[g]
            delta = jnp.sum(pr * dpr, 0, keepdims=True)
            st2.append((pr * (dpr - delta) * SCALE).astype(MXU))
            sunk = p_sink * delta
            for h in range(4):
                row = 4 * g + h
                ds_ref[row:row + 1, :] += jnp.full((1, BLK), -1.0, F32) * jnp.sum(sunk[:, h * BLK:(h + 1) * BLK])
        for g in range(2):
            qs, dos, pr, _, _ = st1[g]
            dsc = st2[g]
            pairs = _swa_unstack(_dot_tn(kd[g], dsc), lo)
            for k in range(2):
                p = 2 * g + k
                dq_ref[:, p * BLK:(p + 1) * BLK] = _unrope(pairs[k], *tabs).astype(ACT)
            dk_g.append(_dot(dsc, qs))
            dv_g.append(_dot(pr.astype(MXU), dos))
        fold = lambda a: a + pltpu.roll(a, HEAD, 1)
        dk = jnp.where(lo, fold(dk_g[0]), fold(dk_g[1]))
        dv = jnp.where(lo, fold(dv_g[0]), fold(dv_g[1]))
        dkv = jnp.concatenate([dk, dv], axis=1)
        prev = pl.multiple_of(jnp.maximum(n - 1, 0) * BLK, BLK)
        acc[0:BLK, :] += dkv[0:BLK]
        acc[pl.ds(prev, BLK), :] += dkv[BLK:2 * BLK]
        acc[pl.ds(row0, BLK), :] += dkv[2 * BLK:]

        @pl.when(n == nb - 1)
        def _():
            dkv_ref[:, :BLK] = _unrope(acc[:, :BLK], cos_ref[...], sa_ref[...], sb_ref[...]).astype(ACT)
            dkv_ref[:, BLK:] = acc[:, BLK:].astype(ACT)

        pl.when(n == nb - 1)(lambda: plan.end(comm))

    tab = pl.BlockSpec((t, BLK), lambda n: (0, 0))
    outs = pl.pallas_call(
        body, name="swa_bwd", grid=(nb,),
        in_specs=[pl.BlockSpec(memory_space=pltpu.SMEM),
                  pl.BlockSpec((BLK, 512), lambda n: (n, 2)),
                  pl.BlockSpec((t, 256), lambda n: (0, 10)),
                  pl.BlockSpec((BLK, 512), lambda n: (n, 0)), tab, tab, tab] + plan.specs,
        out_specs=[pl.BlockSpec((BLK, 512), lambda n: (n, 0)),
                   pl.BlockSpec((t, 256), lambda n: (0, 0)),
                   pl.BlockSpec((8, BLK), lambda n: (0, 0))] + plan.specs,
        out_shape=[jax.ShapeDtypeStruct((t, 512), ACT), jax.ShapeDtypeStruct((t, 256), ACT),
                   jax.ShapeDtypeStruct((8, BLK), F32)] + plan.out_shape,
        scratch_shapes=[pltpu.VMEM((t, 256), F32)] + plan.scratch,
        compiler_params=_cparams(("arbitrary",)),
    )(sinks, p0, p0, datt, *tables, *ride)
    return outs[0], outs[1], outs[2], outs[3:]


def _ab_in_bwd(dp0, w_t, h0, g_pre, dh1, ride):
    t = h0.shape[0]
    tm = _tile(t, 544)
    plan = _ChipsPlan(ride)
    nr = plan.n

    pieces = [(p[0], p[1] * dp0.tw, p[2] * dp0.tw) for p in dp0.pieces]

    def body(*refs):
        d_refs = refs[:5]
        w_ref, h_ref, g_ref, dh1_ref = refs[5:9]
        rest = refs[9:]
        dh0_ref, dg_ref = rest[nr:nr + 2]
        comm = (*rest[:nr], *rest[nr + 2:])
        i = pl.program_id(0)
        pl.when(i == 0)(lambda: plan.begin(comm))

        @pl.when(i == 0)
        def _():
            dg_ref[...] = jnp.zeros_like(dg_ref)

        dhn = None
        for ref, (_, first, rows) in zip(d_refs, pieces):
            term = _dot(ref[...].astype(MXU), w_ref[first:first + rows, :])
            dhn = term if dhn is None else dhn + term
        dx, dg = _norm_bwd(h_ref[...], g_ref[...], dhn, EPS)
        dg_ref[...] += dg
        dh0_ref[...] = dh1_ref[...] + dx
        pl.when(i == t // tm - 1)(lambda: plan.end(comm))

    row = pl.BlockSpec((tm, D), lambda i: (i, 0))
    vec = pl.BlockSpec((1, D), lambda i: (0, 0))
    outs = pl.pallas_call(
        body, name="ab_in_bwd", grid=(t // tm,),
        in_specs=[pl.BlockSpec((tm, rows), lambda i: (i, 0)) for _, _, rows in pieces] + [
            pl.BlockSpec(w_t.shape, lambda i: (0, 0)), row, vec, row] + plan.specs,
        out_specs=[row, vec] + plan.specs,
        out_shape=[jax.ShapeDtypeStruct((t, D), F32), jax.ShapeDtypeStruct((1, D), F32)] + plan.out_shape,
        scratch_shapes=plan.scratch,
        compiler_params=_cparams(("arbitrary",)),
    )(*dp0.arrays, w_t, h0, g_pre, dh1, *ride)
    return outs[0], outs[1], outs[2:]


def _dw_plain(a, b, name):
    t, m = a.shape
    n = b.shape[1]
    tm = _tile(t, 1088)
    tn = min(n, 512)
    nk = t // tm

    def body(a_ref, b_ref, o_ref, acc):
        k = pl.program_id(1)

        @pl.when(k == 0)
        def _():
            acc[...] = jnp.zeros_like(acc)

        acc[...] += _dot_tn(a_ref[...].astype(MXU), b_ref[...].astype(MXU))

        @pl.when(k == nk - 1)
        def _():
            o_ref[...] = acc[...].astype(WIRE)

    return pl.pallas_call(
        body, name=name, grid=(n // tn, nk),
        in_specs=[pl.BlockSpec((tm, m), lambda j, k: (k, 0)),
                  pl.BlockSpec((tm, tn), lambda j, k: (k, j))],
        out_specs=pl.BlockSpec((m, tn), lambda j, k: (0, j)),
        out_shape=jax.ShapeDtypeStruct((m, n), WIRE),
        scratch_shapes=[pltpu.VMEM((m, tn), F32)],
        compiler_params=_cparams(("arbitrary", "arbitrary")),
    )(a, b)


def _dw_chunks(hn, dp, name):
    t = hn.shape[0]
    tm = _tile(t, 1088)
    nk = t // tm
    nt, tw = dp.n_tiles, dp.tw
    n_in = len(dp.arrays)

    def body(*refs):
        d_refs = refs[:n_in]
        h_ref, o_ref, acc = refs[n_in:]
        j, k = pl.program_id(0), pl.program_id(1)

        @pl.when(k == 0)
        def _():
            acc[...] = jnp.zeros_like(acc)

        def add(ref):
            acc[...] += _dot_tn(h_ref[...].astype(MXU), ref[...].astype(MXU))
        dp.apply(j, d_refs, add)

        @pl.when(k == nk - 1)
        def _():
            o_ref[...] = acc[...].astype(WIRE)

    return pl.pallas_call(
        body, name=name, grid=(nt, nk),
        in_specs=dp.specs(tm, lambda j, k: k, lambda j, k: j) + [
            pl.BlockSpec((tm, D), lambda j, k: (k, 0))],
        out_specs=pl.BlockSpec((None, D, tw), lambda j, k: (j, 0, 0)),
        out_shape=jax.ShapeDtypeStruct((nt, D, tw), WIRE),
        scratch_shapes=[pltpu.VMEM((D, tw), F32)],
        compiler_params=_cparams(("arbitrary", "arbitrary")),
    )(*dp.arrays, hn)


def _dw_transposed(dp, hn, name):
    t = hn.shape[0]
    tm = _tile(t, 1088)
    nk = t // tm
    nt, tw = dp.n_tiles, dp.tw
    n_in = len(dp.arrays)

    def body(*refs):
        d_refs = refs[:n_in]
        h_ref, o_ref, acc = refs[n_in:]
        j, k = pl.program_id(0), pl.program_id(1)

        @pl.when(k == 0)
        def _():
            acc[...] = jnp.zeros_like(acc)

        def add(ref):
            acc[...] += _dot_tn(ref[...].astype(MXU), h_ref[...].astype(MXU))
        dp.apply(j, d_refs, add)

        @pl.when(k == nk - 1)
        def _():
            o_ref[...] = acc[...].astype(WIRE)

    return pl.pallas_call(
        body, name=name, grid=(nt, nk),
        in_specs=dp.specs(tm, lambda j, k: k, lambda j, k: j) + [
            pl.BlockSpec((tm, D), lambda j, k: (k, 0))],
        out_specs=pl.BlockSpec((tw, D), lambda j, k: (j, 0)),
        out_shape=jax.ShapeDtypeStruct((nt * tw, D), WIRE),
        scratch_shapes=[pltpu.VMEM((tw, D), F32)],
        compiler_params=_cparams(("arbitrary", "arbitrary")),
    )(*dp.arrays, hn)


def kernel(x, meta_tokens, ab_pre_norm, ab_w_in, ab_sinks, ab_conv_w, ab_conv_b, ab_conv_ln_g, ab_conv_ln_b, ab_w_pw2, ab_w_out, ab_post_norm, sb_pre_norm, sb_w_in, sb_w_out, sb_post_norm, loss_target, m_meta_tokens, m_ab_pre_norm, m_ab_w_in, m_ab_sinks, m_ab_conv_w, m_ab_conv_b, m_ab_conv_ln_g, m_ab_conv_ln_b, m_ab_w_pw2, m_ab_w_out, m_ab_post_norm, m_sb_pre_norm, m_sb_w_in, m_sb_w_out, m_sb_post_norm, v_meta_tokens, v_ab_pre_norm, v_ab_w_in, v_ab_sinks, v_ab_conv_w, v_ab_conv_b, v_ab_conv_ln_g, v_ab_conv_ln_b, v_ab_w_pw2, v_ab_w_out, v_ab_post_norm, v_sb_pre_norm, v_sb_w_in, v_sb_w_out, v_sb_post_norm):
    seq = x.shape[1]
    t = seq + BLK
    mx, my, mc = _coords()
    me = 4 * mx + 2 * my + mc
    pos = jnp.stack([mx, my, mc, me]).astype(jnp.int32)

    w_ab_t, *small = _all_gather(
        [ab_w_in[0].T.astype(WIRE), meta_tokens, ab_conv_w[0], sb_pre_norm, sb_post_norm], "gather_first")
    w_ab_t = w_ab_t.reshape(2816, D)
    meta_full = jnp.moveaxis(small[0], 0, 1).reshape(N_META, D)
    conv_w = jnp.moveaxis(small[1], 0, 1).reshape(CONV_W, 512)
    sb_pre = jnp.moveaxis(small[2], 0, 1).reshape(1, D)
    sb_post = jnp.moveaxis(small[3], 0, 1).reshape(1, D)

    h0 = jnp.concatenate([jnp.zeros((PAD, D), F32), meta_full, x[0]], axis=0)
    tgt = jnp.concatenate([jnp.zeros((BLK, D), F32), loss_target[0]], axis=0)
    tables = _rope_tables(t)
    sinks = ab_sinks[0]

    p0, hn0, (w_sb,) = _ab_in(h0, ab_pre_norm, w_ab_t, tables, [sb_w_in[0].astype(WIRE)])
    att, (w_oa, w_pw) = _swa_fwd(p0, sinks, [ab_w_out[0].astype(WIRE), ab_w_pw2[0].astype(WIRE)])
    w_oa, w_pw = w_oa.reshape(D, D), w_pw.reshape(512, 512)
    c1 = _conv_fwd(p0, conv_w, ab_conv_b, ab_conv_ln_g, ab_conv_ln_b)
    h1, y0, mix = _ab_out(h0, p0, att, c1, w_pw, w_oa, ab_post_norm)
    p1, hn1, (w_os,) = _sb_in(h1, sb_pre, w_sb, [sb_w_out[0].astype(WIRE)])
    w_os = w_os.reshape(D, D)
    o, ltot = _sb_fwd(p1)
    loss_part, dh2, dy1, m1, do, dgate, dg_sb_post = _sb_out(o, p1, w_os, h1, sb_post, tgt)

    dq1, dk1, dv1 = _sb_bwd(p1, ltot, do)
    dp1 = _Cols([(dq1, 0, 2), (dk1, 2, 2), (dv1, 4, 2), (dgate, 6, 2)], 512)

    def sibling_stage(parts, names, tag):
        got = _exchange_sibling(parts, "reduce_sibling_" + tag)
        return [_add_sibling(pos, p, r, "add_sibling_" + nm) for p, r, nm in zip(parts, got, names)]

    def finish(sums, got, names):
        return [_sum_chips(pos, s, r, "sum_chips_" + nm) for s, r, nm in zip(sums, got, names)]

    names1 = ["sb_in", "sb_out"]
    parts1 = [_dw_chunks(hn1, dp1, "dw_sb_in").reshape(4, 2, D, 512),
              _dw_plain(m1, dy1, "dw_sb_out").reshape(4, 2, BLK, D)]
    dh1, dg_sb_pre, got = _sb_in_bwd([dq1, dk1, dv1, dgate], w_sb, h1, sb_pre, dh2, parts1)
    sums1 = [_add_sibling(pos, p, r, "add_sibling_" + nm) for p, r, nm in zip(parts1, got, names1)]
    dy0, dga, dgb, datt, dc1, dc2, dg_ab_post = _ab_out_bwd(dh1, y0, ab_post_norm, w_oa, p0, att, c1, w_pw)
    names2 = ["ab_out", "pw2"]
    sums2 = sibling_stage([_dw_plain(mix, dy0, "dw_ab_out").reshape(4, 2, BLK, D),
                           _dw_plain(c1, dc2, "dw_pw2").reshape(4, 2, 64, 512)], names2, "ab_out")
    (dglu, dconv_w, dconv_b, dln_g, dln_b), got1 = _conv_bwd(
        p0, dc1, conv_w, ab_conv_b, ab_conv_ln_g, ab_conv_ln_b, sums1)
    dq0, dkv0, dsinks, got2 = _swa_bwd(p0, datt, sinks, tables, sums2)
    g_sb_w_in, g_sb_w_out = finish(sums1, got1, names1)
    g_ab_w_out, g_ab_w_pw2 = finish(sums2, got2, names2)
    dp0 = _Cols([(dq0, 0, 2), (dkv0, 2, 1), (dga, 3, 2), (dglu, 5, 4), (dgb, 9, 2)], 256)

    sums0 = sibling_stage([_dw_transposed(dp0, hn0, "dw_ab_in").reshape(4, 2, 352, D)], ["ab_in"], "ab_in")
    dh0, dg_ab_pre, got0 = _ab_in_bwd(dp0, w_ab_t, h0, ab_pre_norm, dh1, sums0)
    g_ab_w_in = finish(sums0, got0, ["ab_in"])[0].T

    small_parts = [dh0[PAD:BLK], dg_ab_pre, dsinks, dconv_w, dconv_b, dln_g, dln_b,
                   dg_ab_post, dg_sb_pre, dg_sb_post, loss_part]
    red = _reduce_small(_all_gather(small_parts, "gather_small_grads"), "reduce_small")
    col = lambda a, w: lax.dynamic_slice_in_dim(a, me * w, w, axis=1)
    g_meta = col(red[0], BLK)
    g_ab_pre = red[1]
    g_sinks = red[2][:, 0].reshape(1, 8)
    g_conv_w = col(red[3][:CONV_W], 64)
    g_conv_b, g_ln_g, g_ln_b, g_ab_post = red[4], red[5], red[6], red[7]
    g_sb_pre, g_sb_post = col(red[8], BLK), col(red[9], BLK)

    loss = red[10][0, 0]
    grad_x = dh0[BLK:][None]

    weights = [meta_tokens, ab_pre_norm, ab_w_in[0], ab_sinks, ab_conv_w[0], ab_conv_b, ab_conv_ln_g,
               ab_conv_ln_b, ab_w_pw2[0], ab_w_out[0], ab_post_norm, sb_pre_norm, sb_w_in[0],
               sb_w_out[0], sb_post_norm]
    grads = [g_meta, g_ab_pre, g_ab_w_in, g_sinks, g_conv_w, g_conv_b, g_ln_g, g_ln_b, g_ab_w_pw2,
             g_ab_w_out, g_ab_post, g_sb_pre, g_sb_w_in, g_sb_w_out, g_sb_post]
    ms = [m_meta_tokens, m_ab_pre_norm, m_ab_w_in[0], m_ab_sinks, m_ab_conv_w[0], m_ab_conv_b,
          m_ab_conv_ln_g, m_ab_conv_ln_b, m_ab_w_pw2[0], m_ab_w_out[0], m_ab_post_norm,
          m_sb_pre_norm, m_sb_w_in[0], m_sb_w_out[0], m_sb_post_norm]
    vs = [v_meta_tokens, v_ab_pre_norm, v_ab_w_in[0], v_ab_sinks, v_ab_conv_w[0], v_ab_conv_b,
          v_ab_conv_ln_g, v_ab_conv_ln_b, v_ab_w_pw2[0], v_ab_w_out[0], v_ab_post_norm,
          v_sb_pre_norm, v_sb_w_in[0], v_sb_w_out[0], v_sb_post_norm]
    lead = [w.ndim == 3 for w in (meta_tokens, ab_pre_norm, ab_w_in, ab_sinks, ab_conv_w, ab_conv_b,
                                   ab_conv_ln_g, ab_conv_ln_b, ab_w_pw2, ab_w_out, ab_post_norm,
                                   sb_pre_norm, sb_w_in, sb_w_out, sb_post_norm)]
    big_ids = [2, 8, 9, 12, 13]
    small_ids = [i for i in range(15) if i not in big_ids]
    deltas, new_m, new_v = [None] * 15, [None] * 15, [None] * 15
    for ids, nm in ((small_ids, "adamw_small"), (big_ids, "adamw_big")):
        d_, m_, v_ = _adamw([weights[i] for i in ids], [grads[i] for i in ids],
                            [ms[i] for i in ids], [vs[i] for i in ids], nm)
        for k, i in enumerate(ids):
            deltas[i], new_m[i], new_v[i] = d_[k], m_[k], v_[k]
    fix = lambda arrs: [a[None] if l else a for a, l in zip(arrs, lead)]
    return (loss, grad_x, *fix(grads), *fix(deltas), *fix(new_m), *fix(new_v))
```

```python
import functools

import numpy as np
import jax
import jax.numpy as jnp
from jax import lax
from jax.experimental import pallas as pl
from jax.experimental.pallas import tpu as pltpu

F32 = jnp.float32
MXU = jnp.bfloat16
ACT = jnp.bfloat16
WIRE = jnp.bfloat16

D = 1024
N_META = 16
BLK = 128
PAD = BLK - N_META
HEAD = 64
NEG = -1e30
EPS = 1e-6
LN_EPS = 1e-5
ROPE_THETA = 10000.0
SCALE = HEAD ** -0.5
CONV_W = 31
HALO = 32
LR, B1, B2, ADAM_EPS, WD, STEP = 0.001, 0.9, 0.999, 1e-08, 0.01, 10
VMEM_LIMIT = 56 * 1024 * 1024
MESH = pl.DeviceIdType.MESH

P0_SRC = (5, 6, 7, 8, 0, 1, 3, 4, 9, 10, 2)


def _cparams(sem=None):
    return pltpu.CompilerParams(dimension_semantics=sem, vmem_limit_bytes=VMEM_LIMIT)


def _tile(t, pref):
    for cand in (1088, 544, 272, 128):
        if cand <= pref and t % cand == 0:
            return cand
    raise ValueError(t)


def _sigmoid(x):
    return 1.0 / (1.0 + jnp.exp(-x))


def _silu_and_grad(x):
    s = _sigmoid(x)
    return x * s, s * (1.0 + x * (1.0 - s))


def _dot(a, b):
    return jnp.dot(a, b, preferred_element_type=F32)


def _dot_nt(a, b):
    return lax.dot_general(a, b, (((1,), (1,)), ((), ())), preferred_element_type=F32)


def _dot_tn(a, b):
    return lax.dot_general(a, b, (((0,), (0,)), ((), ())), preferred_element_type=F32)


def _rows(shape, base):
    return base + lax.broadcasted_iota(jnp.int32, shape, 0)


def _rope_tables(t):
    half = HEAD // 2
    inv = ROPE_THETA ** (-np.arange(half, dtype=np.float32) / half)
    pos = (np.arange(t) - PAD).astype(np.float32)
    ang = pos[:, None] * inv[None, :]
    lane = np.arange(BLK)
    cos = np.cos(ang)[:, lane % half].astype(np.float32)
    sin = np.sin(ang)[:, lane % half].astype(np.float32)
    first = (lane % HEAD) < half
    sin_a = np.where(first[None, :], -sin, 0.0).astype(np.float32)
    sin_b = np.where(first[None, :], 0.0, sin).astype(np.float32)
    return jnp.asarray(cos), jnp.asarray(sin_a), jnp.asarray(sin_b)


def _rope(v, cos, sin_a, sin_b):
    return v * cos + pltpu.roll(v, 96, 1) * sin_a + pltpu.roll(v, 32, 1) * sin_b


def _unrope(v, cos, sin_a, sin_b):
    return v * cos - pltpu.roll(v, 96, 1) * sin_a - pltpu.roll(v, 32, 1) * sin_b


def _coords():
    return lax.axis_index("x"), lax.axis_index("y"), lax.axis_index("c")


def _all_gather(arrs, name):
    plan = _GatherPlan(arrs)

    def body(*refs):
        plan.begin(refs)
        plan.end(refs)

    return pl.pallas_call(
        body, name=name, out_shape=plan.out_shape,
        in_specs=plan.specs, out_specs=plan.specs, scratch_shapes=plan.scratch,
    )(*arrs)


class _GatherPlan:
    def __init__(self, arrs):
        n = self.n = len(arrs)
        self.out_shape = [jax.ShapeDtypeStruct((8,) + a.shape, a.dtype) for a in arrs]
        self.specs = [pl.BlockSpec(memory_space=pl.ANY)] * n
        self.scratch = [pltpu.SemaphoreType.DMA((n, 7)), pltpu.SemaphoreType.DMA((n, 7)),
                        pltpu.SemaphoreType.DMA((n,))]

    def _copies(self, refs):
        n = self.n
        ins, outs = refs[:n], refs[n:2 * n]
        send_sems, recv_sems, local_sems = refs[2 * n:]
        x, y, c = _coords()
        me, sibling = (x, y, c), (x, y, 1 - c)
        chips = [(1 - x, y), (x, 1 - y), (1 - x, 1 - y)]

        def copy(a, k, block, to, src=None):
            dst = outs[a].at[4 * block[0] + 2 * block[1] + block[2]]
            return pltpu.make_async_remote_copy(
                src_ref=dst if src is None else src, dst_ref=dst,
                send_sem=send_sems.at[a, k], recv_sem=recv_sems.at[a, k],
                device_id=to, device_id_type=MESH)

        mine = [pltpu.make_async_copy(ins[a], outs[a].at[4 * x + 2 * y + c], local_sems.at[a])
                for a in range(n)]
        first = []
        for a in range(n):
            first.append(copy(a, 0, me, sibling, src=ins[a]))
            for j, chip in enumerate(chips):
                first.append(copy(a, 1 + j, me, (*chip, c), src=ins[a]))
        return copy, mine, first, (me, sibling, chips, c)

    def begin(self, refs):
        _, mine, first, _ = self._copies(refs)
        for cp in mine + first:
            cp.start()

    def end(self, refs):
        copy, mine, first, (me, sibling, chips, c) = self._copies(refs)
        passed = []
        for j, chip in enumerate(chips):
            for a in range(self.n):
                copy(a, 1 + j, (*chip, c), me).wait_recv()
                cp = copy(a, 4 + j, (*chip, c), sibling)
                cp.start()
                passed.append(cp)
        for a in range(self.n):
            copy(a, 0, sibling, me).wait_recv()
            for j, chip in enumerate(chips):
                copy(a, 4 + j, (*chip, 1 - c), me).wait_recv()
        for cp in first + passed:
            cp.wait_send()
        for cp in mine:
            cp.wait()


class _ChipsPlan:
    def __init__(self, sums):
        n = self.n = len(sums)
        self.out_shape = [jax.ShapeDtypeStruct((3,) + s.shape[1:], s.dtype) for s in sums]
        self.specs = [pl.BlockSpec(memory_space=pl.ANY)] * n
        self.scratch = [pltpu.SemaphoreType.DMA((n, 3)), pltpu.SemaphoreType.DMA((n, 3))]

    def _copies(self, refs):
        n = self.n
        ins, outs = refs[:n], refs[n:2 * n]
        send_sems, recv_sems = refs[2 * n:]
        x, y, c = _coords()
        chips = [(1 - x, y), (x, 1 - y), (1 - x, 1 - y)]
        return [pltpu.make_async_remote_copy(
            src_ref=ins[a].at[2 * chip[0] + chip[1]], dst_ref=outs[a].at[k],
            send_sem=send_sems.at[a, k], recv_sem=recv_sems.at[a, k],
            device_id=(*chip, c), device_id_type=MESH)
            for a in range(n) for k, chip in enumerate(chips)]

    def begin(self, refs):
        for cp in self._copies(refs):
            cp.start()

    def end(self, refs):
        for cp in self._copies(refs):
            cp.wait()


def _exchange_sibling(parts, name):
    plan = _SiblingPlan(parts)

    def body(*refs):
        plan.begin(refs)
        plan.end(refs)

    return pl.pallas_call(
        body, name=name, out_shape=plan.out_shape,
        in_specs=plan.specs, out_specs=plan.specs, scratch_shapes=plan.scratch,
    )(*parts)


class _SiblingPlan:
    def __init__(self, parts):
        n = self.n = len(parts)
        self.out_shape = [jax.ShapeDtypeStruct((4,) + p.shape[2:], p.dtype) for p in parts]
        self.specs = [pl.BlockSpec(memory_space=pl.ANY)] * n
        self.scratch = [pltpu.SemaphoreType.DMA((n,)), pltpu.SemaphoreType.DMA((n,))]

    def _copies(self, refs):
        n = self.n
        ins, outs = refs[:n], refs[n:2 * n]
        send_sems, recv_sems = refs[2 * n:]
        x, y, c = _coords()
        return [pltpu.make_async_remote_copy(
            src_ref=ins[a].at[:, 1 - c], dst_ref=outs[a],
            send_sem=send_sems.at[a], recv_sem=recv_sems.at[a],
            device_id=(x, y, 1 - c), device_id_type=MESH) for a in range(n)]

    def begin(self, refs):
        for cp in self._copies(refs):
            cp.start()

    def end(self, refs):
        for cp in self._copies(refs):
            cp.wait()


def _exchange_chips(sums, name):
    plan = _ChipsPlan(sums)

    def body(*refs):
        plan.begin(refs)
        plan.end(refs)

    return pl.pallas_call(
        body, name=name, out_shape=plan.out_shape,
        in_specs=plan.specs, out_specs=plan.specs, scratch_shapes=plan.scratch,
    )(*sums)


def _add_sibling(pos, parts, recvs, name):
    n = len(parts)

    def body(pos_ref, *refs):
        for a in range(n):
            total = refs[a][...].astype(F32) + refs[n + a][...].astype(F32)
            refs[2 * n + a][...] = total.astype(refs[2 * n + a].dtype)

    mine = lambda p: pl.BlockSpec((None, None) + p.shape[2:], lambda q, pos: (q, pos[2], 0, 0))
    chip = lambda p: pl.BlockSpec((None,) + p.shape[2:], lambda q, pos: (q, 0, 0))
    return pl.pallas_call(
        body, name=name,
        grid_spec=pltpu.PrefetchScalarGridSpec(
            num_scalar_prefetch=1, grid=(4,),
            in_specs=[mine(p) for p in parts] + [chip(p) for p in parts],
            out_specs=[chip(p) for p in parts]),
        out_shape=[jax.ShapeDtypeStruct((4,) + p.shape[2:], p.dtype) for p in parts],
        compiler_params=_cparams(("arbitrary",)),
    )(pos, *parts, *recvs)


def _sum_chips(pos, sums, recvs, name):
    n = len(sums)

    def body(pos_ref, *refs):
        for a in range(n):
            g = refs[a][...].astype(F32)
            for k in range(3):
                g = g + refs[n + a][k].astype(F32)
            refs[2 * n + a][...] = g

    own = lambda s: pl.BlockSpec((None,) + s.shape[1:], lambda i, pos: (2 * pos[0] + pos[1], 0, 0))
    got = lambda s: pl.BlockSpec((3,) + s.shape[1:], lambda i, pos: (0, 0, 0))
    return pl.pallas_call(
        body, name=name,
        grid_spec=pltpu.PrefetchScalarGridSpec(
            num_scalar_prefetch=1, grid=(1,),
            in_specs=[own(s) for s in sums] + [got(s) for s in sums],
            out_specs=[pl.BlockSpec(s.shape[1:], lambda i, pos: (0, 0)) for s in sums]),
        out_shape=[jax.ShapeDtypeStruct(s.shape[1:], F32) for s in sums],
        compiler_params=_cparams(("arbitrary",)),
    )(pos, *sums, *recvs)


def _adamw(ws, gs, ms, vs, name):
    n = len(ws)
    c1 = 1.0 / (1.0 - B1 ** STEP)
    c2 = 1.0 / (1.0 - B2 ** STEP)

    def body(*refs):
        w_r, g_r, m_r, v_r = refs[:n], refs[n:2 * n], refs[2 * n:3 * n], refs[3 * n:4 * n]
        d_o, m_o, v_o = refs[4 * n:5 * n], refs[5 * n:6 * n], refs[6 * n:7 * n]
        for a in range(n):
            g = g_r[a][...]
            m = B1 * m_r[a][...] + (1.0 - B1) * g
            v = B2 * v_r[a][...] + (1.0 - B2) * (g * g)
            d_o[a][...] = -LR * ((m * c1) / (jnp.sqrt(v * c2) + ADAM_EPS) + WD * w_r[a][...])
            m_o[a][...] = m
            v_o[a][...] = v

    shapes = [jax.ShapeDtypeStruct(w.shape, F32) for w in ws]
    outs = pl.pallas_call(body, name=name, out_shape=shapes * 3,
                          compiler_params=_cparams())(*ws, *gs, *ms, *vs)
    return outs[:n], outs[n:2 * n], outs[2 * n:]


def _reduce_small(gathered, name):
    n = len(gathered)

    def body(*refs):
        for a in range(n):
            acc = refs[a][0]
            for k in range(1, 8):
                acc = acc + refs[a][k]
            refs[n + a][...] = acc

    return pl.pallas_call(
        body, name=name,
        out_shape=[jax.ShapeDtypeStruct(g.shape[1:], F32) for g in gathered],
        compiler_params=_cparams())(*gathered)


class _Cols:
    def __init__(self, pieces, tw):
        self.pieces, self.tw = pieces, tw
        self.arrays = [p[0] for p in pieces]
        self.n_tiles = sum(p[2] for p in pieces)

    def specs(self, tm, row_of, tile_of):
        out = []
        for _, first, cnt in self.pieces:
            def imap(*g, first=first, cnt=cnt):
                return (row_of(*g), jnp.clip(tile_of(*g) - first, 0, cnt - 1))
            out.append(pl.BlockSpec((tm, self.tw), imap))
        return out

    def apply(self, t, refs, fn):
        for ref, (_, first, cnt) in zip(refs, self.pieces):
            pl.when((t >= first) & (t < first + cnt))(functools.partial(fn, ref))


def _ab_in(h, g, w_t, tables, ride):
    t = h.shape[0]
    tm = _tile(t, 1088)
    src = jnp.asarray(np.array(P0_SRC, np.int32))
    plan = _GatherPlan(ride)
    nr = plan.n

    def body(src_ref, h_ref, g_ref, w_ref, cos_ref, sa_ref, sb_ref, *rest):
        o_ref, hn_ref = rest[nr:nr + 2]
        hn_s = rest[2 * nr + 2]
        comm = (*rest[:nr], *rest[nr + 2:2 * nr + 2], *rest[2 * nr + 3:])
        i, j = pl.program_id(0), pl.program_id(1)
        pl.when((i == 0) & (j == 0))(lambda: plan.begin(comm))

        @pl.when(j == 0)
        def _():
            x = h_ref[...]
            hn = (x * lax.rsqrt(jnp.mean(x * x, -1, keepdims=True) + EPS) * g_ref[...]).astype(MXU)
            hn_s[...] = hn
            hn_ref[...] = hn.astype(ACT)

        acc = _dot_nt(hn_s[...], w_ref[...])
        rope = lambda v: _rope(v, cos_ref[...], sa_ref[...], sb_ref[...])

        @pl.when((j == 4) | (j == 5))
        def _():
            o_ref[:, :BLK] = rope(acc[:, :BLK])
            o_ref[:, BLK:] = rope(acc[:, BLK:])

        @pl.when(j == 10)
        def _():
            o_ref[:, :BLK] = rope(acc[:, :BLK])
            o_ref[:, BLK:] = acc[:, BLK:]

        @pl.when((j < 4) | ((j > 5) & (j < 10)))
        def _():
            o_ref[...] = acc

        pl.when((i == t // tm - 1) & (j == 10))(lambda: plan.end(comm))

    tab = pl.BlockSpec((tm, BLK), lambda i, j, s: (i, 0))
    outs = pl.pallas_call(
        body, name="ab_in",
        grid_spec=pltpu.PrefetchScalarGridSpec(
            num_scalar_prefetch=1, grid=(t // tm, 11),
            in_specs=[pl.BlockSpec((tm, D), lambda i, j, s: (i, 0)),
                      pl.BlockSpec((1, D), lambda i, j, s: (0, 0)),
                      pl.BlockSpec((256, D), lambda i, j, s: (s[j], 0)),
                      tab, tab, tab] + plan.specs,
            out_specs=[pl.BlockSpec((tm, 256), lambda i, j, s: (i, j)),
                       pl.BlockSpec((tm, D), lambda i, j, s: (i, 0))] + plan.specs,
            scratch_shapes=[pltpu.VMEM((tm, D), MXU)] + plan.scratch),
        out_shape=[jax.ShapeDtypeStruct((t, 2816), F32), jax.ShapeDtypeStruct((t, D), ACT)] + plan.out_shape,
        compiler_params=_cparams(("arbitrary", "arbitrary")),
    )(src, h, g, w_t, *tables, *ride)
    return outs[0], outs[1], outs[2:]


def _swa_mask(n):
    c = lax.broadcasted_iota(jnp.int32, (3 * BLK, 4 * BLK), 0)
    r = lax.broadcasted_iota(jnp.int32, (3 * BLK, 4 * BLK), 1) & (BLK - 1)
    qpos = n * BLK + r
    bpos = (n - 2) * BLK + c
    meta_ok = (c >= PAD) & (c < BLK) & (qpos - c >= BLK)
    band_ok = (c >= BLK) & (bpos >= PAD) & (qpos >= bpos) & (qpos - bpos < BLK)
    return meta_ok | band_ok


def _swa_keys(kv_ref, n):
    def blk(b):
        return kv_ref[pl.ds(pl.multiple_of(b * BLK, BLK), BLK), :]
    kv = jnp.concatenate([kv_ref[0:BLK, :], blk(jnp.maximum(n - 1, 0)), blk(n)], axis=0)
    lo = lax.broadcasted_iota(jnp.int32, (1, BLK), 1) < HEAD
    out = []
    for part in (kv[:, :BLK], kv[:, BLK:]):
        rolled = pltpu.roll(part, HEAD, 1)
        out.append((jnp.where(lo, part, rolled).astype(MXU), jnp.where(lo, rolled, part).astype(MXU)))
    return out[0], out[1], lo


def _swa_stack(ref, g, lo):
    parts = []
    for p in (2 * g, 2 * g + 1):
        x = ref[:, p * BLK:(p + 1) * BLK]
        parts += [jnp.where(lo, x, 0.0), jnp.where(lo, 0.0, x)]
    return jnp.concatenate(parts, axis=0).astype(MXU)


def _swa_probs(qs, kd, mask, sink_ref, g):
    sink = jnp.concatenate([jnp.full((1, BLK), sink_ref[4 * g + h], F32) for h in range(4)], axis=1)
    s = jnp.where(mask, _dot_nt(kd, qs) * SCALE, NEG)
    m = jnp.maximum(jnp.max(s, 0, keepdims=True), sink)
    e = jnp.exp(s - m)
    e_sink = jnp.exp(sink - m)
    inv = 1.0 / (jnp.sum(e, 0, keepdims=True) + e_sink)
    return e * inv, e_sink * inv


def _swa_unstack(x_t, lo):
    x = x_t.T
    return [jnp.where(lo, x[0:BLK], x[BLK:2 * BLK]), jnp.where(lo, x[2 * BLK:3 * BLK], x[3 * BLK:])]


def _swa_fwd(p0, sinks, ride):
    t = p0.shape[0]
    plan = _GatherPlan(ride)
    nr = plan.n

    def body(sink_ref, q_ref, kv_ref, *rest):
        o_ref = rest[nr]
        comm = (*rest[:nr], *rest[nr + 1:])
        n = pl.program_id(0)
        pl.when(n == 0)(lambda: plan.begin(comm))
        kd, vd, lo = _swa_keys(kv_ref, n)
        mask = _swa_mask(n)
        for g in range(2):
            pr, _ = _swa_probs(_swa_stack(q_ref, g, lo), kd[g], mask, sink_ref, g)
            pairs = _swa_unstack(_dot_tn(vd[g], pr.astype(MXU)), lo)
            for k in range(2):
                p = 2 * g + k
                o_ref[:, p * BLK:(p + 1) * BLK] = pairs[k]
        pl.when(n == t // BLK - 1)(lambda: plan.end(comm))

    outs = pl.pallas_call(
        body, name="swa_fwd", grid=(t // BLK,),
        in_specs=[pl.BlockSpec(memory_space=pltpu.SMEM),
                  pl.BlockSpec((BLK, 512), lambda n: (n, 2)),
                  pl.BlockSpec((t, 256), lambda n: (0, 10))] + plan.specs,
        out_specs=[pl.BlockSpec((BLK, 512), lambda n: (n, 0))] + plan.specs,
        out_shape=[jax.ShapeDtypeStruct((t, 512), F32)] + plan.out_shape,
        scratch_shapes=plan.scratch,
        compiler_params=_cparams(("arbitrary",)),
    )(sinks, p0, p0, *ride)
    return outs[0], outs[1:]


def _conv_window(u_w, w_ref, n_out, first):
    rows = u_w.shape[0]
    acc = None
    for j in range(CONV_W):
        shifted = pltpu.roll(u_w, (rows - (first + j)) % rows, 0)[:n_out]
        term = shifted * w_ref[j:j + 1, :]
        acc = term if acc is None else acc + term
    return acc


def _conv_fwd(p0, conv_w, conv_b, ln_g, ln_b):
    t = p0.shape[0]
    tm = _tile(t, 544)
    hb = tm // HALO

    def body(cur_ref, prev_ref, w_ref, b_ref, g_ref, bb_ref, o_ref):
        i = pl.program_id(0)
        glu = jnp.concatenate([prev_ref[...], cur_ref[...]], axis=0)
        rw = _rows((tm + HALO, 1), i * tm - HALO)
        u_w = jnp.where(rw >= PAD, glu[:, :512] * _sigmoid(glu[:, 512:]), 0.0)
        cv = _conv_window(u_w, w_ref, tm, HALO - (CONV_W - 1)) + b_ref[...]
        xc = cv - jnp.mean(cv, -1, keepdims=True)
        ln = xc * lax.rsqrt(jnp.mean(xc * xc, -1, keepdims=True) + LN_EPS) * g_ref[...] + bb_ref[...]
        o_ref[...] = (ln * _sigmoid(ln)).astype(ACT)

    vec = pl.BlockSpec((1, 512), lambda i: (0, 0))
    return pl.pallas_call(
        body, name="conv_fwd", grid=(t // tm,),
        in_specs=[pl.BlockSpec((tm, D), lambda i: (i, 0)),
                  pl.BlockSpec((HALO, D), lambda i: (jnp.maximum(i * hb - 1, 0), 0)),
                  pl.BlockSpec((CONV_W, 512), lambda i: (0, 0)), vec, vec, vec],
        out_specs=pl.BlockSpec((tm, 512), lambda i: (i, 0)),
        out_shape=jax.ShapeDtypeStruct((t, 512), ACT),
        compiler_params=_cparams(("arbitrary",)),
    )(p0, p0, conv_w, conv_b, ln_g, ln_b)


def _ab_out(h, p0, att, c1, w_pw2, w_out, g_post):
    t = h.shape[0]
    tm = _tile(t, 544)

    def body(h_ref, ga_ref, gb_ref, att_ref, c1_ref, pw_ref, wo_ref, g_ref, h1_ref, y_ref, mix_ref):
        i = pl.program_id(0)
        sga, _ = _silu_and_grad(ga_ref[...])
        sgb, _ = _silu_and_grad(gb_ref[...])
        a = att_ref[...] * sga
        c = _dot(c1_ref[...].astype(MXU), pw_ref[...]) * sgb
        mix = jnp.concatenate([a, c], axis=1).astype(MXU)
        y = _dot(mix, wo_ref[...])
        yn = y * lax.rsqrt(jnp.mean(y * y, -1, keepdims=True) + EPS) * g_ref[...]
        h1_ref[...] = jnp.where(_rows((tm, 1), i * tm) >= PAD, h_ref[...] + yn, 0.0)
        y_ref[...] = y
        mix_ref[...] = mix.astype(ACT)

    row = lambda w, idx: pl.BlockSpec((tm, w), lambda i: (i, idx))
    full = lambda a: pl.BlockSpec(a.shape, lambda i: (0, 0))
    return pl.pallas_call(
        body, name="ab_out", grid=(t // tm,),
        in_specs=[row(D, 0), row(512, 3), row(512, 4), row(512, 0), row(512, 0),
                  full(w_pw2), full(w_out), full(g_post)],
        out_specs=[row(D, 0), row(D, 0), row(D, 0)],
        out_shape=[jax.ShapeDtypeStruct((t, D), F32), jax.ShapeDtypeStruct((t, D), F32),
                   jax.ShapeDtypeStruct((t, D), ACT)],
        compiler_params=_cparams(("arbitrary",)),
    )(h, p0, p0, att, c1, w_pw2, w_out, g_post)


def _sb_in(h, g, w, ride):
    t = h.shape[0]
    tm = _tile(t, 1088)
    plan = _GatherPlan(ride)
    nr = plan.n

    def body(h_ref, g_ref, w_ref, *rest):
        o_ref, hn_ref = rest[nr:nr + 2]
        hn_s = rest[2 * nr + 2]
        comm = (*rest[:nr], *rest[nr + 2:2 * nr + 2], *rest[2 * nr + 3:])
        i, j = pl.program_id(0), pl.program_id(1)
        pl.when((i == 0) & (j == 0))(lambda: plan.begin(comm))

        @pl.when(j == 0)
        def _():
            x = h_ref[...]
            hn = (x * lax.rsqrt(jnp.mean(x * x, -1, keepdims=True) + EPS) * g_ref[...]).astype(MXU)
            hn_s[...] = hn
            hn_ref[...] = hn.astype(ACT)

        o_ref[...] = _dot(hn_s[...], w_ref[...])
        pl.when((i == t // tm - 1) & (j == 7))(lambda: plan.end(comm))

    outs = pl.pallas_call(
        body, name="sb_in", grid=(t // tm, 8),
        in_specs=[pl.BlockSpec((tm, D), lambda i, j: (i, 0)),
                  pl.BlockSpec((1, D), lambda i, j: (0, 0)),
                  pl.BlockSpec((None, D, 512), lambda i, j: (j, 0, 0))] + plan.specs,
        out_specs=[pl.BlockSpec((tm, 512), lambda i, j: (i, j)),
                   pl.BlockSpec((tm, D), lambda i, j: (i, 0))] + plan.specs,
        out_shape=[jax.ShapeDtypeStruct((t, 4096), F32), jax.ShapeDtypeStruct((t, D), ACT)] + plan.out_shape,
        scratch_shapes=[pltpu.VMEM((tm, D), MXU)] + plan.scratch,
        compiler_params=_cparams(("arbitrary", "arbitrary")),
    )(h, g, w, *ride)
    return outs[0], outs[1], outs[2:]


def _split_hi_lo(x):
    hi = x.astype(MXU)
    lo = (x - hi.astype(F32)).astype(MXU)
    return hi, lo


def _scan_matrix(suffix):
    j = lax.broadcasted_iota(jnp.int32, (2 * BLK, 2 * BLK), 0) % BLK
    s = lax.broadcasted_iota(jnp.int32, (2 * BLK, 2 * BLK), 1)
    keep = (s >= BLK) | ((j > s) if suffix else (j < s))
    return jnp.where(keep, 1.0, 0.0).astype(MXU)


def _scan_packed(hi_lo, b, mat):
    cols = slice(b * BLK, (b + 1) * BLK)
    both = _dot(jnp.concatenate([hi_lo[:BLK, cols], hi_lo[BLK:, cols]], axis=1), mat)
    return both[:, :BLK], both[:, BLK:]


KC = 4
GROUPS = 2
GROUPS_FWD = 4


def _sb_logits(qm, kc, valid):
    z = _dot_nt(qm, kc)
    log_beta = jnp.minimum(z, 0.0) - jnp.log(1.0 + jnp.exp(-jnp.abs(z)))
    return log_beta, jnp.where(valid, log_beta - z, 0.0)


def _sb_valid(i, first_key, chunk):
    r = lax.broadcasted_iota(jnp.int32, (BLK, chunk), 0)
    c = lax.broadcasted_iota(jnp.int32, (BLK, chunk), 1)
    kpos = first_key + c
    return (kpos >= PAD) & (kpos < i * BLK + r)


def _sb_walk(i, step_of, init):
    n_full = (i + 1) // KC
    full = step_of(KC)
    carry = lax.fori_loop(0, n_full, lambda s, c: full(s * KC, c), init)
    rest = [lambda c: c] + [functools.partial(step_of(k), n_full * KC) for k in range(1, KC)]
    return lax.switch(i + 1 - n_full * KC, rest, carry)


def _sb_fwd(p1):
    t = p1.shape[0]
    groups = GROUPS_FWD
    w = groups * BLK

    def body(q_ref, k_ref, v_ref, o_ref, lt_ref, k_s, v_s):
        i = pl.program_id(1)

        @pl.when(i == 0)
        def _():
            k_s[...] = k_ref[...].astype(MXU)
            v_s[...] = v_ref[...].astype(MXU)

        lo = lax.broadcasted_iota(jnp.int32, (1, BLK), 1) < HEAD
        qm = []
        for g in range(groups):
            q = q_ref[:, g * BLK:(g + 1) * BLK] * SCALE
            qm += [jnp.where(lo, q, 0.0).astype(MXU), jnp.where(lo, 0.0, q).astype(MXU)]
        mat = _scan_matrix(True)
        heads = range(2 * groups)

        def step_of(kc):
            chunk = kc * BLK

            def step(done, carry):
                start = pl.multiple_of((i + 1 - done - kc) * BLK, BLK)
                valid = _sb_valid(i, start, chunk)
                new, staged = [], []
                for h in heads:
                    lanes = slice((h // 2) * BLK, (h // 2 + 1) * BLK)
                    log_beta, log_1m = _sb_logits(qm[h], k_s[pl.ds(start, chunk), lanes], valid)
                    staged.append((log_beta, jnp.concatenate(_split_hi_lo(log_1m), axis=0)))
                probs = []
                for h in heads:
                    log_beta, hi_lo = staged[h]
                    run = carry[2 * h]
                    parts = [None] * kc
                    for b in reversed(range(kc)):
                        after, total = _scan_packed(hi_lo, b, mat)
                        parts[b] = after + run
                        run = run + total
                    a = jnp.where(valid, jnp.exp(log_beta + jnp.concatenate(parts, axis=1)), 0.0)
                    probs.append((run, a.astype(MXU)))
                for h in heads:
                    lanes = slice((h // 2) * BLK, (h // 2 + 1) * BLK)
                    run, a = probs[h]
                    new += [run, carry[2 * h + 1] + _dot(a, v_s[pl.ds(start, chunk), lanes])]
                return tuple(new)
            return step

        zero = jnp.zeros((BLK, BLK), F32)
        res = _sb_walk(i, step_of, (zero,) * (4 * groups))
        for g in range(groups):
            lanes = slice(g * BLK, (g + 1) * BLK)
            o_ref[:, lanes] = jnp.where(lo, res[4 * g + 1], res[4 * g + 3])
            lt_ref[:, lanes] = jnp.where(lo, res[4 * g], res[4 * g + 2])

    ng = D // w
    blk = pl.BlockSpec((BLK, w), lambda hp, i: (i, hp))
    return pl.pallas_call(
        body, name="sb_fwd", grid=(ng, t // BLK),
        in_specs=[blk,
                  pl.BlockSpec((t, w), lambda hp, i: (0, ng + hp)),
                  pl.BlockSpec((t, w), lambda hp, i: (0, 2 * ng + hp))],
        out_specs=[blk, blk],
        out_shape=[jax.ShapeDtypeStruct((t, D), F32)] * 2,
        scratch_shapes=[pltpu.VMEM((t, w), MXU), pltpu.VMEM((t, w), MXU)],
        compiler_params=_cparams(("arbitrary", "arbitrary")),
    )(p1, p1, p1)


def _sb_out(o, p1, w_out, h1, g_post, tgt):
    t = o.shape[0]
    tm = _tile(t, 544)

    def body(o_ref, g_ref, w_ref, h_ref, gp_ref, t_ref,
             loss_ref, dh_ref, dy_ref, m_ref, do_ref, dg_ref, dgp_ref):
        i = pl.program_id(0)

        @pl.when(i == 0)
        def _():
            loss_ref[...] = jnp.zeros_like(loss_ref)
            dgp_ref[...] = jnp.zeros_like(dgp_ref)

        gate = g_ref[...]
        sg, dsg = _silu_and_grad(gate)
        ov = o_ref[...]
        m = (ov * sg).astype(MXU)
        y = _dot(m, w_ref[...])
        r = lax.rsqrt(jnp.mean(y * y, -1, keepdims=True) + EPS)
        yhat = y * r
        h2 = h_ref[...] + yhat * gp_ref[...]
        diff = jnp.where(_rows((tm, 1), i * tm) >= BLK, h2 - t_ref[...], 0.0)
        loss_ref[...] += jnp.full(loss_ref.shape, 0.5 / D, F32) * jnp.sum(diff * diff)
        dh = diff * (1.0 / D)
        dgp_ref[...] += jnp.sum(dh * yhat, 0, keepdims=True)
        dyn = dh * gp_ref[...]
        dy = (r * (dyn - yhat * jnp.mean(dyn * yhat, -1, keepdims=True))).astype(MXU)
        dm = _dot_nt(dy, w_ref[...])
        dh_ref[...] = dh
        dy_ref[...] = dy.astype(ACT)
        m_ref[...] = m.astype(ACT)
        do_ref[...] = dm * sg
        dg_ref[...] = (dm * ov * dsg).astype(ACT)

    row = lambda idx: pl.BlockSpec((tm, D), lambda i: (i, idx))
    full = lambda a: pl.BlockSpec(a.shape, lambda i: (0, 0))
    acc = lambda s: pl.BlockSpec(s, lambda i: (0, 0))
    return pl.pallas_call(
        body, name="sb_out", grid=(t // tm,),
        in_specs=[row(0), row(3), full(w_out), row(0), full(g_post), row(0)],
        out_specs=[acc((8, BLK)), row(0), row(0), row(0), row(0), row(0), acc((1, D))],
        out_shape=[jax.ShapeDtypeStruct((8, BLK), F32), jax.ShapeDtypeStruct((t, D), F32),
                   jax.ShapeDtypeStruct((t, D), ACT), jax.ShapeDtypeStruct((t, D), ACT),
                   jax.ShapeDtypeStruct((t, D), F32), jax.ShapeDtypeStruct((t, D), ACT),
                   jax.ShapeDtypeStruct((1, D), F32)],
        compiler_params=_cparams(("arbitrary",)),
    )(o, p1, w_out, h1, g_post, tgt)


def _sb_bwd(p1, ltot, do):
    t = p1.shape[0]
    nb = t // BLK

    w = GROUPS * BLK

    def body(q_ref, k_ref, v_ref, lt_ref, do_ref, dq_ref, dk_ref, dv_ref, k_s, v_s, dk_s, dv_s):
        i = pl.program_id(1)
        lo = lax.broadcasted_iota(jnp.int32, (1, BLK), 1) < HEAD

        @pl.when(i == 0)
        def _():
            k_s[...] = k_ref[...].astype(MXU)
            v_s[...] = v_ref[...].astype(MXU)
            dk_s[...] = jnp.zeros_like(dk_s)
            dv_s[...] = jnp.zeros_like(dv_s)

        qm, dom, row_total, q2, do2 = [], [], [], [], []
        for g in range(GROUPS):
            lanes = slice(g * BLK, (g + 1) * BLK)
            q, dout, lt = q_ref[:, lanes] * SCALE, do_ref[:, lanes], lt_ref[:, lanes]
            qm += [jnp.where(lo, q, 0.0).astype(MXU), jnp.where(lo, 0.0, q).astype(MXU)]
            dom += [jnp.where(lo, dout, 0.0).astype(MXU), jnp.where(lo, 0.0, dout).astype(MXU)]
            q2.append(jnp.concatenate(qm[-2:], axis=0))
            do2.append(jnp.concatenate(dom[-2:], axis=0))
            lt_r = pltpu.roll(lt, HEAD, 1)
            row_total += [jnp.where(lo, lt, lt_r), jnp.where(lo, lt_r, lt)]
        mat_l = _scan_matrix(True)
        mat_g = _scan_matrix(False)
        heads = range(2 * GROUPS)

        def step_of(kc):
            chunk = kc * BLK

            def step(done, carry):
                start = pl.multiple_of(done * BLK, BLK)
                keys = lambda ref, h: ref[pl.ds(start, chunk), (h // 2) * BLK:(h // 2 + 1) * BLK]
                valid = _sb_valid(i, start, chunk)
                new, dzs, probs, st1, st2, st3 = [], [], [], [], [], []
                for h in heads:
                    log_beta, log_1m = _sb_logits(qm[h], keys(k_s, h), valid)
                    st1.append((log_beta, jnp.concatenate(_split_hi_lo(log_1m), axis=0),
                                _dot_nt(dom[h], keys(v_s, h))))
                for h in heads:
                    log_beta, hi_lo, da = st1[h]
                    run = carry[3 * h]
                    parts = []
                    for b in range(kc):
                        after, total = _scan_packed(hi_lo, b, mat_l)
                        run = run + total
                        parts.append(after + (row_total[h] - run))
                    a = jnp.where(valid, jnp.exp(log_beta + jnp.concatenate(parts, axis=1)), 0.0)
                    g = da * a
                    probs.append(a.astype(MXU))
                    st2.append((run, g, jnp.concatenate(_split_hi_lo(g), axis=0)))
                for h in heads:
                    run, g, hi_lo = st2[h]
                    run_g = carry[3 * h + 1]
                    parts = []
                    for b in range(kc):
                        before, total_g = _scan_packed(hi_lo, b, mat_g)
                        parts.append(before + run_g)
                        run_g = run_g + total_g
                    sig = jnp.exp(st1[h][0])
                    dz = jnp.where(valid, g - sig * (g + jnp.concatenate(parts, axis=1)), 0.0)
                    dzs.append(dz.astype(MXU))
                    st3.append((run, run_g))
                for h in heads:
                    new += [*st3[h], carry[3 * h + 2] + _dot(dzs[h], keys(k_s, h))]
                for g in range(GROUPS):
                    lanes = slice(g * BLK, (g + 1) * BLK)
                    dk_s[pl.ds(start, chunk), lanes] += _dot_tn(jnp.concatenate(dzs[2 * g:2 * g + 2], axis=0), q2[g])
                    dv_s[pl.ds(start, chunk), lanes] += _dot_tn(jnp.concatenate(probs[2 * g:2 * g + 2], axis=0), do2[g])
                return tuple(new)
            return step

        zero = jnp.zeros((BLK, BLK), F32)
        res = _sb_walk(i, step_of, (zero,) * (6 * GROUPS))
        for g in range(GROUPS):
            dq = jnp.where(lo, res[6 * g + 2], res[6 * g + 5])
            dq_ref[:, g * BLK:(g + 1) * BLK] = (dq * SCALE).astype(ACT)

        @pl.when(i == nb - 1)
        def _():
            dk_ref[...] = dk_s[...].astype(ACT)
            dv_ref[...] = dv_s[...].astype(ACT)

    ng = D // w
    blk = pl.BlockSpec((BLK, w), lambda hp, i: (i, hp))
    col = lambda off: pl.BlockSpec((t, w), lambda hp, i: (0, off + hp))
    return pl.pallas_call(
        body, name="sb_bwd", grid=(ng, nb),
        in_specs=[blk, col(ng), col(2 * ng), blk, blk],
        out_specs=[blk, col(0), col(0)],
        out_shape=[jax.ShapeDtypeStruct((t, D), ACT)] * 3,
        scratch_shapes=[pltpu.VMEM((t, w), MXU), pltpu.VMEM((t, w), MXU),
                        pltpu.VMEM((t, w), F32), pltpu.VMEM((t, w), F32)],
        compiler_params=_cparams(("arbitrary", "arbitrary")),
    )(p1, p1, p1, ltot, do)


def _norm_bwd(x, g, dy, eps):
    r = lax.rsqrt(jnp.mean(x * x, -1, keepdims=True) + eps)
    xhat = x * r
    dxn = dy * g
    return r * (dxn - xhat * jnp.mean(dxn * xhat, -1, keepdims=True)), jnp.sum(dy * xhat, 0, keepdims=True)


def _sb_in_bwd(dparts, w_sb, h1, g_pre1, dh2, ride):
    t = h1.shape[0]
    tm = _tile(t, 544)
    plan = _SiblingPlan(ride)
    nr = plan.n

    def body(dq_ref, dk_ref, dv_ref, dg_ref, w_ref, h_ref, g_ref, dh2_ref, *rest):
        dh1_ref, dgn_ref = rest[nr:nr + 2]
        comm = (*rest[:nr], *rest[nr + 2:])
        i = pl.program_id(0)
        pl.when(i == 0)(lambda: plan.begin(comm))

        @pl.when(i == 0)
        def _():
            dgn_ref[...] = jnp.zeros_like(dgn_ref)

        dhn = None
        for a, ref in enumerate((dq_ref, dk_ref, dv_ref, dg_ref)):
            for b in range(2):
                term = _dot_nt(ref[:, b * 512:(b + 1) * 512].astype(MXU), w_ref[2 * a + b])
                dhn = term if dhn is None else dhn + term
        dx, dg = _norm_bwd(h_ref[...], g_ref[...], dhn, EPS)
        dgn_ref[...] += dg
        dh1_ref[...] = dh2_ref[...] + dx
        pl.when(i == t // tm - 1)(lambda: plan.end(comm))

    row = pl.BlockSpec((tm, D), lambda i: (i, 0))
    vec = pl.BlockSpec((1, D), lambda i: (0, 0))
    outs = pl.pallas_call(
        body, name="sb_in_bwd", grid=(t // tm,),
        in_specs=[row, row, row, row, pl.BlockSpec(w_sb.shape, lambda i: (0, 0, 0)), row, vec, row]
        + plan.specs,
        out_specs=[row, vec] + plan.specs,
        out_shape=[jax.ShapeDtypeStruct((t, D), F32), jax.ShapeDtypeStruct((1, D), F32)] + plan.out_shape,
        scratch_shapes=plan.scratch,
        compiler_params=_cparams(("arbitrary",)),
    )(*dparts, w_sb, h1, g_pre1, dh2, *ride)
    return outs[0], outs[1], outs[2:]


def _ab_out_bwd(dh1, y0, g_post0, w_out, p0, att, c1, w_pw2):
    t = dh1.shape[0]
    tm = _tile(t, 544)

    def body(dh1_ref, y_ref, g0_ref, wo_ref, ga_ref, gb_ref, att_ref, c1_ref, pw_ref,
             dy_ref, dga_ref, dgb_ref, datt_ref, dc1_ref, dc2_ref, dg0_ref):
        @pl.when(pl.program_id(0) == 0)
        def _():
            dg0_ref[...] = jnp.zeros_like(dg0_ref)

        dy, dg = _norm_bwd(y_ref[...], g0_ref[...], dh1_ref[...], EPS)
        dg0_ref[...] += dg
        dy = dy.astype(MXU)
        dy_ref[...] = dy.astype(ACT)
        dmix = _dot_nt(dy, wo_ref[...])
        da, dc = dmix[:, :512], dmix[:, 512:]
        sga, dsga = _silu_and_grad(ga_ref[...])
        sgb, dsgb = _silu_and_grad(gb_ref[...])
        datt_ref[...] = da * sga
        dga_ref[...] = (da * att_ref[...] * dsga).astype(ACT)
        c2 = _dot(c1_ref[...].astype(MXU), pw_ref[...])
        dc2 = (dc * sgb).astype(MXU)
        dgb_ref[...] = (dc * c2 * dsgb).astype(ACT)
        dc2_ref[...] = dc2.astype(ACT)
        dc1_ref[...] = _dot_nt(dc2, pw_ref[...])

    row = lambda w, idx: pl.BlockSpec((tm, w), lambda i: (i, idx))
    full = lambda a: pl.BlockSpec(a.shape, lambda i: (0, 0))
    sd = jax.ShapeDtypeStruct
    return pl.pallas_call(
        body, name="ab_out_bwd", grid=(t // tm,),
        in_specs=[row(D, 0), row(D, 0), full(g_post0), full(w_out), row(512, 3), row(512, 4),
                  row(512, 0), row(512, 0), full(w_pw2)],
        out_specs=[row(D, 0), row(512, 0), row(512, 0), row(512, 0), row(512, 0), row(512, 0),
                   pl.BlockSpec((1, D), lambda i: (0, 0))],
        out_shape=[sd((t, D), ACT), sd((t, 512), ACT), sd((t, 512), ACT), sd((t, 512), F32),
                   sd((t, 512), F32), sd((t, 512), ACT), sd((1, D), F32)],
        compiler_params=_cparams(("arbitrary",)),
    )(dh1, y0, g_post0, w_out, p0, p0, att, c1, w_pw2)


def _conv_bwd(p0, dc1, conv_w, conv_b, ln_g, ln_b, ride):
    t = p0.shape[0]
    tm = _tile(t, 544)
    hb = tm // HALO
    last = t // HALO - 1
    plan = _ChipsPlan(ride)
    nr = plan.n

    def body(cur_ref, prev_ref, next_ref, d_ref, dn_ref, w_ref, b_ref, g_ref, bb_ref, *rest):
        dglu_ref, dw_ref, db_ref, dlg_ref, dlb_ref = rest[nr:nr + 5]
        comm = (*rest[:nr], *rest[nr + 5:])
        i = pl.program_id(0)
        pl.when(i == 0)(lambda: plan.begin(comm))

        @pl.when(i == 0)
        def _():
            for ref in (dw_ref, db_ref, dlg_ref, dlb_ref):
                ref[...] = jnp.zeros_like(ref)

        glu = jnp.concatenate([prev_ref[...], cur_ref[...], next_ref[...]], axis=0)
        rw = _rows((tm + 2 * HALO, 1), i * tm - HALO)
        ga, sg = glu[:, :512], _sigmoid(glu[:, 512:])
        u_w = jnp.where((rw >= PAD) & (rw < t), ga * sg, 0.0)
        n_cv = tm + HALO
        cv = _conv_window(u_w, w_ref, n_cv, HALO - (CONV_W - 1)) + b_ref[...]
        xc = cv - jnp.mean(cv, -1, keepdims=True)
        rstd = lax.rsqrt(jnp.mean(xc * xc, -1, keepdims=True) + LN_EPS)
        cvhat = xc * rstd
        ln = cvhat * g_ref[...] + bb_ref[...]
        _, dsl = _silu_and_grad(ln)
        rc = _rows((n_cv, 1), i * tm)
        dc = jnp.concatenate([d_ref[...], dn_ref[...]], axis=0)
        dln = jnp.where(rc < t, dc * dsl, 0.0)
        dhat = dln * g_ref[...]
        dcv = rstd * (dhat - jnp.mean(dhat, -1, keepdims=True)
                      - cvhat * jnp.mean(dhat * cvhat, -1, keepdims=True))
        own = dcv[:tm]
        dlg_ref[...] += jnp.sum((dln * cvhat)[:tm], 0, keepdims=True)
        dlb_ref[...] += jnp.sum(dln[:tm], 0, keepdims=True)
        db_ref[...] += jnp.sum(own, 0, keepdims=True)
        rows = tm + 2 * HALO
        du = None
        for j in range(CONV_W):
            first = HALO - (CONV_W - 1) + j
            shifted = pltpu.roll(u_w, (rows - first) % rows, 0)[:tm]
            dw_ref[j:j + 1, :] += jnp.sum(own * shifted, 0, keepdims=True)
            back = pltpu.roll(dcv, (n_cv - (CONV_W - 1 - j)) % n_cv, 0)[:tm]
            term = back * w_ref[j:j + 1, :]
            du = term if du is None else du + term
        du = jnp.where(_rows((tm, 1), i * tm) >= PAD, du, 0.0)
        ga_c, sg_c = ga[HALO:HALO + tm], sg[HALO:HALO + tm]
        dglu_ref[:, :512] = (du * sg_c).astype(ACT)
        dglu_ref[:, 512:] = (du * ga_c * sg_c * (1.0 - sg_c)).astype(ACT)
        pl.when(i == t // tm - 1)(lambda: plan.end(comm))

    vec = pl.BlockSpec((1, 512), lambda i: (0, 0))
    nxt = lambda i: (jnp.minimum((i + 1) * hb, last), 0)
    outs = pl.pallas_call(
        body, name="conv_bwd", grid=(t // tm,),
        in_specs=[pl.BlockSpec((tm, D), lambda i: (i, 0)),
                  pl.BlockSpec((HALO, D), lambda i: (jnp.maximum(i * hb - 1, 0), 0)),
                  pl.BlockSpec((HALO, D), nxt),
                  pl.BlockSpec((tm, 512), lambda i: (i, 0)),
                  pl.BlockSpec((HALO, 512), nxt),
                  pl.BlockSpec((CONV_W, 512), lambda i: (0, 0)), vec, vec, vec] + plan.specs,
        out_specs=[pl.BlockSpec((tm, D), lambda i: (i, 0)),
                   pl.BlockSpec((HALO, 512), lambda i: (0, 0)), vec, vec, vec] + plan.specs,
        out_shape=[jax.ShapeDtypeStruct((t, D), ACT), jax.ShapeDtypeStruct((HALO, 512), F32)]
        + [jax.ShapeDtypeStruct((1, 512), F32)] * 3 + plan.out_shape,
        scratch_shapes=plan.scratch,
        compiler_params=_cparams(("arbitrary",)),
    )(p0, p0, p0, dc1, dc1, conv_w, conv_b, ln_g, ln_b, *ride)
    return outs[:5], outs[5:]


def _swa_bwd(p0, datt, sinks, tables, ride):
    t = p0.shape[0]
    nb = t // BLK
    plan = _ChipsPlan(ride)
    nr = plan.n

    def body(sink_ref, q_ref, kv_ref, d_ref, cos_ref, sa_ref, sb_ref, *rest):
        dq_ref, dkv_ref, ds_ref = rest[nr:nr + 3]
        acc = rest[2 * nr + 3]
        comm = (*rest[:nr], *rest[nr + 3:2 * nr + 3], *rest[2 * nr + 4:])
        n = pl.program_id(0)
        pl.when(n == 0)(lambda: plan.begin(comm))

        @pl.when(n == 0)
        def _():
            acc[...] = jnp.zeros_like(acc)
            ds_ref[...] = jnp.zeros_like(ds_ref)

        kd, vd, lo = _swa_keys(kv_ref, n)
        mask = _swa_mask(n)
        row0 = pl.multiple_of(n * BLK, BLK)
        tabs = [r[pl.ds(row0, BLK), :] for r in (cos_ref, sa_ref, sb_ref)]
        dk_g, dv_g, st1, st2 = [], [], [], []
        for g in range(2):
            qs, dos = _swa_stack(q_ref, g, lo), _swa_stack(d_ref, g, lo)
            pr, p_sink = _swa_probs(qs, kd[g], mask, sink_ref, g)
            st1.append((qs, dos, pr, p_sink, _dot_nt(vd[g], dos)))
        for g in range(2):
            qs, dos, pr, p_sink, dpr = st1[g]
            delta = jnp.sum(pr * dpr, 0, keepdims=True)
            st2.append((pr * (dpr - delta) * SCALE).astype(MXU))
            sunk = p_sink * delta
            for h in range(4):
                row = 4 * g + h
                ds_ref[row:row + 1, :] += jnp.full((1, BLK), -1.0, F32) * jnp.sum(sunk[:, h * BLK:(h + 1) * BLK])
        for g in range(2):
            qs, dos, pr, _, _ = st1[g]
            dsc = st2[g]
            pairs = _swa_unstack(_dot_tn(kd[g], dsc), lo)
            for k in range(2):
                p = 2 * g + k
                dq_ref[:, p * BLK:(p + 1) * BLK] = _unrope(pairs[k], *tabs).astype(ACT)
            dk_g.append(_dot(dsc, qs))
            dv_g.append(_dot(pr.astype(MXU), dos))
        fold = lambda a: a + pltpu.roll(a, HEAD, 1)
        dk = jnp.where(lo, fold(dk_g[0]), fold(dk_g[1]))
        dv = jnp.where(lo, fold(dv_g[0]), fold(dv_g[1]))
        dkv = jnp.concatenate([dk, dv], axis=1)
        prev = pl.multiple_of(jnp.maximum(n - 1, 0) * BLK, BLK)
        acc[0:BLK, :] += dkv[0:BLK]
        acc[pl.ds(prev, BLK), :] += dkv[BLK:2 * BLK]
        acc[pl.ds(row0, BLK), :] += dkv[2 * BLK:]

        @pl.when(n == nb - 1)
        def _():
            dkv_ref[:, :BLK] = _unrope(acc[:, :BLK], cos_ref[...], sa_ref[...], sb_ref[...]).astype(ACT)
            dkv_ref[:, BLK:] = acc[:, BLK:].astype(ACT)

        pl.when(n == nb - 1)(lambda: plan.end(comm))

    tab = pl.BlockSpec((t, BLK), lambda n: (0, 0))
    outs = pl.pallas_call(
        body, name="swa_bwd", grid=(nb,),
        in_specs=[pl.BlockSpec(memory_space=pltpu.SMEM),
                  pl.BlockSpec((BLK, 512), lambda n: (n, 2)),
                  pl.BlockSpec((t, 256), lambda n: (0, 10)),
                  pl.BlockSpec((BLK, 512), lambda n: (n, 0)), tab, tab, tab] + plan.specs,
        out_specs=[pl.BlockSpec((BLK, 512), lambda n: (n, 0)),
                   pl.BlockSpec((t, 256), lambda n: (0, 0)),
                   pl.BlockSpec((8, BLK), lambda n: (0, 0))] + plan.specs,
        out_shape=[jax.ShapeDtypeStruct((t, 512), ACT), jax.ShapeDtypeStruct((t, 256), ACT),
                   jax.ShapeDtypeStruct((8, BLK), F32)] + plan.out_shape,
        scratch_shapes=[pltpu.VMEM((t, 256), F32)] + plan.scratch,
        compiler_params=_cparams(("arbitrary",)),
    )(sinks, p0, p0, datt, *tables, *ride)
    return outs[0], outs[1], outs[2], outs[3:]


def _ab_in_bwd(dp0, w_t, h0, g_pre, dh1, ride):
    t = h0.shape[0]
    tm = _tile(t, 544)
    plan = _ChipsPlan(ride)
    nr = plan.n

    pieces = [(p[0], p[1] * dp0.tw, p[2] * dp0.tw) for p in dp0.pieces]

    def body(*refs):
        d_refs = refs[:5]
        w_ref, h_ref, g_ref, dh1_ref = refs[5:9]
        rest = refs[9:]
        dh0_ref, dg_ref = rest[nr:nr + 2]
        comm = (*rest[:nr], *rest[nr + 2:])
        i = pl.program_id(0)
        pl.when(i == 0)(lambda: plan.begin(comm))

        @pl.when(i == 0)
        def _():
            dg_ref[...] = jnp.zeros_like(dg_ref)

        dhn = None
        for ref, (_, first, rows) in zip(d_refs, pieces):
            term = _dot(ref[...].astype(MXU), w_ref[first:first + rows, :])
            dhn = term if dhn is None else dhn + term
        dx, dg = _norm_bwd(h_ref[...], g_ref[...], dhn, EPS)
        dg_ref[...] += dg
        dh0_ref[...] = dh1_ref[...] + dx
        pl.when(i == t // tm - 1)(lambda: plan.end(comm))

    row = pl.BlockSpec((tm, D), lambda i: (i, 0))
    vec = pl.BlockSpec((1, D), lambda i: (0, 0))
    outs = pl.pallas_call(
        body, name="ab_in_bwd", grid=(t // tm,),
        in_specs=[pl.BlockSpec((tm, rows), lambda i: (i, 0)) for _, _, rows in pieces] + [
            pl.BlockSpec(w_t.shape, lambda i: (0, 0)), row, vec, row] + plan.specs,
        out_specs=[row, vec] + plan.specs,
        out_shape=[jax.ShapeDtypeStruct((t, D), F32), jax.ShapeDtypeStruct((1, D), F32)] + plan.out_shape,
        scratch_shapes=plan.scratch,
        compiler_params=_cparams(("arbitrary",)),
    )(*dp0.arrays, w_t, h0, g_pre, dh1, *ride)
    return outs[0], outs[1], outs[2:]


def _dw_plain(a, b, name):
    t, m = a.shape
    n = b.shape[1]
    tm = _tile(t, 1088)
    tn = min(n, 512)
    nk = t // tm

    def body(a_ref, b_ref, o_ref, acc):
        k = pl.program_id(1)

        @pl.when(k == 0)
        def _():
            acc[...] = jnp.zeros_like(acc)

        acc[...] += _dot_tn(a_ref[...].astype(MXU), b_ref[...].astype(MXU))

        @pl.when(k == nk - 1)
        def _():
            o_ref[...] = acc[...].astype(WIRE)

    return pl.pallas_call(
        body, name=name, grid=(n // tn, nk),
        in_specs=[pl.BlockSpec((tm, m), lambda j, k: (k, 0)),
                  pl.BlockSpec((tm, tn), lambda j, k: (k, j))],
        out_specs=pl.BlockSpec((m, tn), lambda j, k: (0, j)),
        out_shape=jax.ShapeDtypeStruct((m, n), WIRE),
        scratch_shapes=[pltpu.VMEM((m, tn), F32)],
        compiler_params=_cparams(("arbitrary", "arbitrary")),
    )(a, b)


def _dw_chunks(hn, dp, name):
    t = hn.shape[0]
    tm = _tile(t, 1088)
    nk = t // tm
    nt, tw = dp.n_tiles, dp.tw
    n_in = len(dp.arrays)

    def body(*refs):
        d_refs = refs[:n_in]
        h_ref, o_ref, acc = refs[n_in:]
        j, k = pl.program_id(0), pl.program_id(1)

        @pl.when(k == 0)
        def _():
            acc[...] = jnp.zeros_like(acc)

        def add(ref):
            acc[...] += _dot_tn(h_ref[...].astype(MXU), ref[...].astype(MXU))
        dp.apply(j, d_refs, add)

        @pl.when(k == nk - 1)
        def _():
            o_ref[...] = acc[...].astype(WIRE)

    return pl.pallas_call(
        body, name=name, grid=(nt, nk),
        in_specs=dp.specs(tm, lambda j, k: k, lambda j, k: j) + [
            pl.BlockSpec((tm, D), lambda j, k: (k, 0))],
        out_specs=pl.BlockSpec((None, D, tw), lambda j, k: (j, 0, 0)),
        out_shape=jax.ShapeDtypeStruct((nt, D, tw), WIRE),
        scratch_shapes=[pltpu.VMEM((D, tw), F32)],
        compiler_params=_cparams(("arbitrary", "arbitrary")),
    )(*dp.arrays, hn)


def _dw_transposed(dp, hn, name):
    t = hn.shape[0]
    tm = _tile(t, 1088)
    nk = t // tm
    nt, tw = dp.n_tiles, dp.tw
    n_in = len(dp.arrays)

    def body(*refs):
        d_refs = refs[:n_in]
        h_ref, o_ref, acc = refs[n_in:]
        j, k = pl.program_id(0), pl.program_id(1)

        @pl.when(k == 0)
        def _():
            acc[...] = jnp.zeros_like(acc)

        def add(ref):
            acc[...] += _dot_tn(ref[...].astype(MXU), h_ref[...].astype(MXU))
        dp.apply(j, d_refs, add)

        @pl.when(k == nk - 1)
        def _():
            o_ref[...] = acc[...].astype(WIRE)

    return pl.pallas_call(
        body, name=name, grid=(nt, nk),
        in_specs=dp.specs(tm, lambda j, k: k, lambda j, k: j) + [
            pl.BlockSpec((tm, D), lambda j, k: (k, 0))],
        out_specs=pl.BlockSpec((tw, D), lambda j, k: (j, 0)),
        out_shape=jax.ShapeDtypeStruct((nt * tw, D), WIRE),
        scratch_shapes=[pltpu.VMEM((tw, D), F32)],
        compiler_params=_cparams(("arbitrary", "arbitrary")),
    )(*dp.arrays, hn)


def kernel(x, meta_tokens, ab_pre_norm, ab_w_in, ab_sinks, ab_conv_w, ab_conv_b, ab_conv_ln_g, ab_conv_ln_b, ab_w_pw2, ab_w_out, ab_post_norm, sb_pre_norm, sb_w_in, sb_w_out, sb_post_norm, loss_target, m_meta_tokens, m_ab_pre_norm, m_ab_w_in, m_ab_sinks, m_ab_conv_w, m_ab_conv_b, m_ab_conv_ln_g, m_ab_conv_ln_b, m_ab_w_pw2, m_ab_w_out, m_ab_post_norm, m_sb_pre_norm, m_sb_w_in, m_sb_w_out, m_sb_post_norm, v_meta_tokens, v_ab_pre_norm, v_ab_w_in, v_ab_sinks, v_ab_conv_w, v_ab_conv_b, v_ab_conv_ln_g, v_ab_conv_ln_b, v_ab_w_pw2, v_ab_w_out, v_ab_post_norm, v_sb_pre_norm, v_sb_w_in, v_sb_w_out, v_sb_post_norm):
    seq = x.shape[1]
    t = seq + BLK
    mx, my, mc = _coords()
    me = 4 * mx + 2 * my + mc
    pos = jnp.stack([mx, my, mc, me]).astype(jnp.int32)

    w_ab_t, *small = _all_gather(
        [ab_w_in[0].T.astype(WIRE), meta_tokens, ab_conv_w[0], sb_pre_norm, sb_post_norm], "gather_first")
    w_ab_t = w_ab_t.reshape(2816, D)
    meta_full = jnp.moveaxis(small[0], 0, 1).reshape(N_META, D)
    conv_w = jnp.moveaxis(small[1], 0, 1).reshape(CONV_W, 512)
    sb_pre = jnp.moveaxis(small[2], 0, 1).reshape(1, D)
    sb_post = jnp.moveaxis(small[3], 0, 1).reshape(1, D)

    h0 = jnp.concatenate([jnp.zeros((PAD, D), F32), meta_full, x[0]], axis=0)
    tgt = jnp.concatenate([jnp.zeros((BLK, D), F32), loss_target[0]], axis=0)
    tables = _rope_tables(t)
    sinks = ab_sinks[0]

    p0, hn0, (w_sb,) = _ab_in(h0, ab_pre_norm, w_ab_t, tables, [sb_w_in[0].astype(WIRE)])
    att, (w_oa, w_pw) = _swa_fwd(p0, sinks, [ab_w_out[0].astype(WIRE), ab_w_pw2[0].astype(WIRE)])
    w_oa, w_pw = w_oa.reshape(D, D), w_pw.reshape(512, 512)
    c1 = _conv_fwd(p0, conv_w, ab_conv_b, ab_conv_ln_g, ab_conv_ln_b)
    h1, y0, mix = _ab_out(h0, p0, att, c1, w_pw, w_oa, ab_post_norm)
    p1, hn1, (w_os,) = _sb_in(h1, sb_pre, w_sb, [sb_w_out[0].astype(WIRE)])
    w_os = w_os.reshape(D, D)
    o, ltot = _sb_fwd(p1)
    loss_part, dh2, dy1, m1, do, dgate, dg_sb_post = _sb_out(o, p1, w_os, h1, sb_post, tgt)

    dq1, dk1, dv1 = _sb_bwd(p1, ltot, do)
    dp1 = _Cols([(dq1, 0, 2), (dk1, 2, 2), (dv1, 4, 2), (dgate, 6, 2)], 512)

    def sibling_stage(parts, tag):
        got = _exchange_sibling(parts, "reduce_sibling_" + tag)
        return _add_sibling(pos, parts, got, "add_sibling_" + tag)

    parts1 = [_dw_chunks(hn1, dp1, "dw_sb_in").reshape(4, 2, D, 512),
              _dw_plain(m1, dy1, "dw_sb_out").reshape(4, 2, BLK, D)]
    dh1, dg_sb_pre, got = _sb_in_bwd([dq1, dk1, dv1, dgate], w_sb, h1, sb_pre, dh2, parts1)
    sums1 = _add_sibling(pos, parts1, got, "add_sibling_sb")
    dy0, dga, dgb, datt, dc1, dc2, dg_ab_post = _ab_out_bwd(dh1, y0, ab_post_norm, w_oa, p0, att, c1, w_pw)
    sums2 = sibling_stage([_dw_plain(mix, dy0, "dw_ab_out").reshape(4, 2, BLK, D),
                           _dw_plain(c1, dc2, "dw_pw2").reshape(4, 2, 64, 512)], "ab_out")
    (dglu, dconv_w, dconv_b, dln_g, dln_b), got1 = _conv_bwd(
        p0, dc1, conv_w, ab_conv_b, ab_conv_ln_g, ab_conv_ln_b, sums1)
    dq0, dkv0, dsinks, got2 = _swa_bwd(p0, datt, sinks, tables, sums2)
    dp0 = _Cols([(dq0, 0, 2), (dkv0, 2, 1), (dga, 3, 2), (dglu, 5, 4), (dgb, 9, 2)], 256)

    sums0 = sibling_stage([_dw_transposed(dp0, hn0, "dw_ab_in").reshape(4, 2, 352, D)], "ab_in")
    dh0, dg_ab_pre, got0 = _ab_in_bwd(dp0, w_ab_t, h0, ab_pre_norm, dh1, sums0)
    g_sb_w_in, g_sb_w_out, g_ab_w_out, g_ab_w_pw2, g_ab_w_in_t = _sum_chips(
        pos, sums1 + sums2 + sums0, list(got1) + list(got2) + list(got0), "sum_chips")
    g_ab_w_in = g_ab_w_in_t.T

    small_parts = [dh0[PAD:BLK], dg_ab_pre, dsinks, dconv_w, dconv_b, dln_g, dln_b,
                   dg_ab_post, dg_sb_pre, dg_sb_post, loss_part]
    red = _reduce_small(_all_gather(small_parts, "gather_small_grads"), "reduce_small")
    col = lambda a, w: lax.dynamic_slice_in_dim(a, me * w, w, axis=1)
    g_meta = col(red[0], BLK)
    g_ab_pre = red[1]
    g_sinks = red[2][:, 0].reshape(1, 8)
    g_conv_w = col(red[3][:CONV_W], 64)
    g_conv_b, g_ln_g, g_ln_b, g_ab_post = red[4], red[5], red[6], red[7]
    g_sb_pre, g_sb_post = col(red[8], BLK), col(red[9], BLK)

    loss = red[10][0, 0]
    grad_x = dh0[BLK:][None]

    weights = [meta_tokens, ab_pre_norm, ab_w_in[0], ab_sinks, ab_conv_w[0], ab_conv_b, ab_conv_ln_g,
               ab_conv_ln_b, ab_w_pw2[0], ab_w_out[0], ab_post_norm, sb_pre_norm, sb_w_in[0],
               sb_w_out[0], sb_post_norm]
    grads = [g_meta, g_ab_pre, g_ab_w_in, g_sinks, g_conv_w, g_conv_b, g_ln_g, g_ln_b, g_ab_w_pw2,
             g_ab_w_out, g_ab_post, g_sb_pre, g_sb_w_in, g_sb_w_out, g_sb_post]
    ms = [m_meta_tokens, m_ab_pre_norm, m_ab_w_in[0], m_ab_sinks, m_ab_conv_w[0], m_ab_conv_b,
          m_ab_conv_ln_g, m_ab_conv_ln_b, m_ab_w_pw2[0], m_ab_w_out[0], m_ab_post_norm,
          m_sb_pre_norm, m_sb_w_in[0], m_sb_w_out[0], m_sb_post_norm]
    vs = [v_meta_tokens, v_ab_pre_norm, v_ab_w_in[0], v_ab_sinks, v_ab_conv_w[0], v_ab_conv_b,
          v_ab_conv_ln_g, v_ab_conv_ln_b, v_ab_w_pw2[0], v_ab_w_out[0], v_ab_post_norm,
          v_sb_pre_norm, v_sb_w_in[0], v_sb_w_out[0], v_sb_post_norm]
    lead = [w.ndim == 3 for w in (meta_tokens, ab_pre_norm, ab_w_in, ab_sinks, ab_conv_w, ab_conv_b,
                                   ab_conv_ln_g, ab_conv_ln_b, ab_w_pw2, ab_w_out, ab_post_norm,
                                   sb_pre_norm, sb_w_in, sb_w_out, sb_post_norm)]
    big_ids = [2, 8, 9, 12, 13]
    small_ids = [i for i in range(15) if i not in big_ids]
    deltas, new_m, new_v = [None] * 15, [None] * 15, [None] * 15
    for ids, nm in ((small_ids, "adamw_small"), (big_ids, "adamw_big")):
        d_, m_, v_ = _adamw([weights[i] for i in ids], [grads[i] for i in ids],
                            [ms[i] for i in ids], [vs[i] for i in ids], nm)
        for k, i in enumerate(ids):
            deltas[i], new_m[i], new_v[i] = d_[k], m_[k], v_[k]
    fix = lambda arrs: [a[None] if l else a for a, l in zip(arrs, lead)]
    return (loss, grad_x, *fix(grads), *fix(deltas), *fix(new_m), *fix(new_v))
```
